```python
import math
import jax, jax.numpy as jnp
from jax import lax
import numpy as np

D_MODEL = 1024
BATCH = 4
SEQ = 4096
DEPTH = 4

GRID_W = 64
CTX_LEN = 256
N_EVEN = (DEPTH + 1) // 2
N_ODD = DEPTH // 2

NA_HEADS = 8
HEAD_DIM = D_MODEL // (2 * NA_HEADS)
NA_WIDTH = NA_HEADS * HEAD_DIM
NA_WIN_H = 8
NA_WIN_W = 16
SC_WIDTH = D_MODEL - NA_WIDTH
SC_CONV = 3
HY_WIDTH = D_MODEL // 2
HY_ORDER = 2
HY_SHORT = 3
HY_BANDS = 16
HY_EMB = 1 + 2 * HY_BANDS
HY_FFN = 64
HY_TARGET = 1e-2
HY_FAST = 0.3
HY_SLOW = 1.5
HY_MAX_DECAY = math.log(HY_TARGET) / HY_FAST
HY_MIN_DECAY = math.log(HY_TARGET) / HY_SLOW
CF_WIDTH = D_MODEL - HY_WIDTH
CF_CONV_WIDTH = 31
EVEN_IN = 3 * NA_WIDTH + 3 * SC_WIDTH
ODD_IN = 3 * HY_WIDTH + 2 * CF_WIDTH
N_GROUPS = 4
EXPERTS_PER_GROUP = 4
N_EXPERTS = N_GROUPS * EXPERTS_PER_GROUP
TOP_K = 2
D_EXPERT = D_MODEL // 4

RMS_EPS = 1e-6
LN_EPS = 1e-5
NEG_INF = -1e30

kernel_name = 'hybrid_natten_hyena_conformer_hmoe_dit'


def rmsnorm(x, g):
    xf = x.astype(jnp.float32)
    y = xf * lax.rsqrt(jnp.mean(xf * xf, axis=-1, keepdims=True) + RMS_EPS)
    return (y * g).astype(x.dtype)


def layernorm(x, g, b):
    xf = x.astype(jnp.float32)
    mu = jnp.mean(xf, axis=-1, keepdims=True)
    var = jnp.mean(jnp.square(xf - mu), axis=-1, keepdims=True)
    return ((xf - mu) * lax.rsqrt(var + LN_EPS) * g + b).astype(x.dtype)


def split_cols(p, sizes):
    return jnp.split(p, [int(i) for i in np.cumsum(sizes)[:-1]], axis=-1)


def dwconv(x, w):
    k = w.shape[0]
    return lax.conv_general_dilated(
        x, w[:, None, :], window_strides=(1,), padding=[(k // 2, k - 1 - k // 2)],
        dimension_numbers=('NWC', 'WIO', 'NWC'), feature_group_count=x.shape[-1])


def heads(a):
    return a.reshape(a.shape[0], a.shape[1], NA_HEADS, HEAD_DIM)


def neighbourhood_attention(q, k, v, k_ctx, v_ctx, rpb):
    b, n, h, dh = q.shape
    rows = n // GRID_W
    kh = min(NA_WIN_H, rows)
    kw = NA_WIN_W
    scale = dh ** -0.5
    qg = q.reshape(b, rows, GRID_W, h, dh)
    kg = k.reshape(b, rows, GRID_W, h, dh)
    vg = v.reshape(b, rows, GRID_W, h, dh)
    cols = jnp.arange(GRID_W)
    col_start = jnp.clip(cols - kw // 2, 0, GRID_W - kw)
    col_valid = (cols[None, :] >= col_start[:, None]) & (cols[None, :] < col_start[:, None] + kw)
    col_idx = jnp.clip(cols[None, :] - cols[:, None], -(kw - 1), kw - 1) + (NA_WIN_W - 1)

    def row_block(args):
        q_row, r = args
        start = jnp.clip(r - kh // 2, 0, rows - kh)
        k_band = lax.dynamic_slice_in_dim(kg, start, kh, axis=1)
        v_band = lax.dynamic_slice_in_dim(vg, start, kh, axis=1)
        row_idx = start + jnp.arange(kh) - r + (NA_WIN_H - 1)
        bias = rpb[:, row_idx[None, :, None], col_idx[:, None, :]]
        s_win = jnp.einsum('bqhd,brkhd->bhqrk', q_row, k_band).astype(jnp.float32) * scale + bias
        s_win = jnp.where(col_valid[:, None, :], s_win, NEG_INF)
        s_ctx = jnp.einsum('bqhd,bchd->bhqc', q_row, k_ctx).astype(jnp.float32) * scale
        s = jnp.concatenate([s_win.reshape(b, h, GRID_W, kh * GRID_W), s_ctx], axis=-1)
        p = jax.nn.softmax(s, axis=-1).astype(v.dtype)
        p_win = p[..., :kh * GRID_W].reshape(b, h, GRID_W, kh, GRID_W)
        p_ctx = p[..., kh * GRID_W:]
        return (jnp.einsum('bhqrk,brkhd->bqhd', p_win, v_band)
                + jnp.einsum('bhqc,bchd->bqhd', p_ctx, v_ctx))

    out = lax.map(row_block, (jnp.moveaxis(qg, 1, 0), jnp.arange(rows)))
    return jnp.moveaxis(out, 0, 1).reshape(b, n, h * dh)


def context_attention(q, k, v):
    b, lc, h, dh = q.shape
    s = jnp.einsum('bqhd,bkhd->bhqk', q, k).astype(jnp.float32) * dh ** -0.5
    p = jax.nn.softmax(s, axis=-1).astype(v.dtype)
    return jnp.einsum('bhqk,bkhd->bqhd', p, v).reshape(b, lc, h * dh)


def short_gated_conv(gb, gc, hv, w):
    return gb * dwconv(gc * hv, w)


def even_mixer(u_lat, u_ctx, w_in, qn_g, kn_g, rpb, sc_w, ctx_needed):
    sizes = [NA_WIDTH] * 3 + [SC_WIDTH] * 3

    def project(u):
        q, k, v, gb, gc, hv = split_cols(u @ w_in, sizes)
        return rmsnorm(heads(q), qn_g), rmsnorm(heads(k), kn_g), heads(v), gb, gc, hv

    ql, kl, vl, bl, cl, hl = project(u_lat)
    if ctx_needed:
        qc, kc, vc, bc, cc, hc = project(u_ctx)
    else:
        kc, vc = split_cols(u_ctx @ w_in[:, NA_WIDTH:3 * NA_WIDTH], [NA_WIDTH, NA_WIDTH])
        kc, vc = rmsnorm(heads(kc), kn_g), heads(vc)
    y_lat = jnp.concatenate([neighbourhood_attention(ql, kl, vl, kc, vc, rpb),
                             short_gated_conv(bl, cl, hl, sc_w)], axis=-1)
    if not ctx_needed:
        return y_lat, None
    y_ctx = jnp.concatenate([context_attention(qc, kc, vc),
                             short_gated_conv(bc, cc, hc, sc_w)], axis=-1)
    return y_lat, y_ctx


def hyena_filter_bank(length, w1, b1, w2, b2, w3, freq):
    f32 = jnp.float32
    t = jnp.linspace(0.0, 1.0, length, dtype=f32)[:, None]
    w = 2.0 * math.pi * jnp.arange(length, dtype=f32)[:, None] / length
    bands = jnp.linspace(1e-4, HY_BANDS - 1, HY_BANDS, dtype=f32)[None, :]
    z = jnp.concatenate([t, jnp.cos(bands * w), -jnp.sin(bands * w)], axis=-1)
    hid = jnp.sin(freq[0] * (z @ w1.astype(f32) + b1))
    hid = jnp.sin(freq[1] * (hid @ w2.astype(f32) + b2))
    filt = (hid @ w3.astype(f32)).reshape(length, HY_ORDER, 2, HY_WIDTH)
    deltas = jnp.abs(jnp.linspace(HY_MIN_DECAY, HY_MAX_DECAY, HY_WIDTH, dtype=f32))
    filt = filt * jnp.exp(-t * deltas)[:, None, None, :]
    filt = filt / jnp.sum(jnp.abs(filt), axis=(0, 2), keepdims=True)
    fwd, bwd = filt[:, :, 0], filt[:, :, 1]
    taps = jnp.concatenate([fwd, jnp.zeros_like(fwd[:1]), bwd[:0:-1]], axis=0)
    return jnp.fft.rfft(taps, axis=0)


def long_conv(u, taps_f, bias):
    n = u.shape[1]
    uf = u.astype(jnp.float32)
    y = jnp.fft.irfft(jnp.fft.rfft(uf, n=2 * n, axis=1) * taps_f, n=2 * n, axis=1)[:, :n]
    return (y + uf * bias).astype(u.dtype)


def odd_mixer(u, w_in, short_w, w1, b1, w2, b2, w3, freq, hy_bias, cf_w, cf_b, ln_g, ln_b):
    hy, a, g = split_cols(u @ w_in, [3 * HY_WIDTH, CF_WIDTH, CF_WIDTH])
    v, x1, x2 = split_cols(dwconv(hy, short_w), [HY_WIDTH] * 3)
    taps_f = hyena_filter_bank(u.shape[1], w1, b1, w2, b2, w3, freq)
    z = v
    for o, gate in enumerate((x1, x2)):
        z = gate * long_conv(z, taps_f[:, o], hy_bias[o])
    cf = dwconv(a * jax.nn.sigmoid(g), cf_w) + cf_b
    cf = jax.nn.silu(layernorm(cf, ln_g, ln_b))
    return jnp.concatenate([z, cf], axis=-1)


def hier_moe(h, w_group, b_group, w_router, b_router, w_gate, w_up, w_down):
    b, n, d = h.shape
    t = h.reshape(b * n, d)
    tf = t.astype(jnp.float32)
    g_logits = tf @ w_group.astype(jnp.float32) + b_group
    g_prob = jax.nn.softmax(g_logits, axis=-1)
    g_idx = jnp.argmax(g_logits, axis=-1)
    g_w = jnp.take_along_axis(g_prob, g_idx[:, None], axis=-1)
    e_logits = (tf @ w_router.astype(jnp.float32) + b_router).reshape(-1, N_GROUPS, EXPERTS_PER_GROUP)
    e_logits = jnp.take_along_axis(e_logits, g_idx[:, None, None], axis=1)[:, 0]
    top_v, top_i = lax.top_k(e_logits, TOP_K)
    top_w = jax.nn.softmax(top_v, axis=-1) * g_w
    expert_id = g_idx[:, None] * EXPERTS_PER_GROUP + top_i
    combine = jnp.sum(jax.nn.one_hot(expert_id, N_EXPERTS, dtype=jnp.float32) * top_w[..., None],
                      axis=1).astype(t.dtype)
    act = (jax.nn.silu(jnp.einsum('td,edf->tef', t, w_gate))
           * jnp.einsum('td,edf->tef', t, w_up) * combine[..., None])
    return jnp.einsum('tef,efd->td', act, w_down).reshape(b, n, d)


def setup_inputs(seed: int = 0) -> dict:
    key = jax.random.key(seed)
    ks = iter(jax.random.split(key, 48))
    f32 = jnp.float32
    D = D_MODEL

    def nrm(shape, scale):
        return jax.random.normal(next(ks), shape, f32) * scale

    def gain(shape):
        return 1.0 + nrm(shape, 0.02)

    return {
        'x': nrm((BATCH, SEQ, D), 1.0),
        'c': nrm((BATCH, D), 1.0),
        'ctx': nrm((BATCH, CTX_LEN, D), 1.0),
        'c_ctx': nrm((D,), 1.0),
        'ada_w': nrm((DEPTH, D, 6 * D), 0.5 * D ** -0.5),
        'ada_b': nrm((DEPTH, 6 * D), 0.02),
        'norm1_g': gain((DEPTH, D)),
        'norm2_g': gain((DEPTH, D)),
        'w_in_even': nrm((N_EVEN, D, EVEN_IN), D ** -0.5),
        'qn_g': gain((N_EVEN, HEAD_DIM)),
        'kn_g': gain((N_EVEN, HEAD_DIM)),
        'na_rpb': nrm((N_EVEN, NA_HEADS, 2 * NA_WIN_H - 1, 2 * NA_WIN_W - 1), 0.1),
        'sc_conv_w': nrm((N_EVEN, SC_CONV, SC_WIDTH), SC_CONV ** -0.5),
        'w_in_odd': nrm((N_ODD, D, ODD_IN), D ** -0.5),
        'hy_short_w': nrm((N_ODD, HY_SHORT, 3 * HY_WIDTH), HY_SHORT ** -0.5),
        'hy_w1': nrm((N_ODD, HY_EMB, HY_FFN), HY_EMB ** -0.5),
        'hy_b1': nrm((N_ODD, HY_FFN), 0.1),
        'hy_w2': nrm((N_ODD, HY_FFN, HY_FFN), HY_FFN ** -0.5),
        'hy_b2': nrm((N_ODD, HY_FFN), 0.1),
        'hy_w3': nrm((N_ODD, HY_FFN, HY_ORDER * 2 * HY_WIDTH), HY_FFN ** -0.5),
        'hy_freq': 1.0 + nrm((N_ODD, 2, HY_FFN), 0.1),
        'hy_bias': nrm((N_ODD, HY_ORDER, HY_WIDTH), 0.5),
        'cf_conv_w': nrm((N_ODD, CF_CONV_WIDTH, CF_WIDTH), CF_CONV_WIDTH ** -0.5),
        'cf_conv_b': nrm((N_ODD, CF_WIDTH), 0.02),
        'cf_ln_g': gain((N_ODD, CF_WIDTH)),
        'cf_ln_b': nrm((N_ODD, CF_WIDTH), 0.02),
        'w_out': nrm((DEPTH, D, D), D ** -0.5),
        'moe_w_group': nrm((DEPTH, D, N_GROUPS), D ** -0.5),
        'moe_b_group': nrm((DEPTH, N_GROUPS), 0.01),
        'moe_w_router': nrm((DEPTH, D, N_EXPERTS), D ** -0.5),
        'moe_b_router': nrm((DEPTH, N_EXPERTS), 0.01),
        'moe_w_gate': nrm((DEPTH, N_EXPERTS, D, D_EXPERT), D ** -0.5),
        'moe_w_up': nrm((DEPTH, N_EXPERTS, D, D_EXPERT), D ** -0.5),
        'moe_w_down': nrm((DEPTH, N_EXPERTS, D_EXPERT, D), D_EXPERT ** -0.5),
    }


def reference(x, c, ctx, c_ctx, ada_w, ada_b, norm1_g, norm2_g,
              w_in_even, qn_g, kn_g, na_rpb, sc_conv_w,
              w_in_odd, hy_short_w, hy_w1, hy_b1, hy_w2, hy_b2, hy_w3, hy_freq, hy_bias,
              cf_conv_w, cf_conv_b, cf_ln_g, cf_ln_b, w_out,
              moe_w_group, moe_b_group, moe_w_router, moe_b_router,
              moe_w_gate, moe_w_up, moe_w_down):
    for l in range(DEPTH):
        ctx_needed = any(j % 2 == 0 for j in range(l + 1, DEPTH))
        sh1, sc1, g1, sh2, sc2, g2 = [m[:, None, :] for m in
                                      jnp.split(jax.nn.silu(c) @ ada_w[l] + ada_b[l], 6, axis=-1)]
        csh1, csc1, cg1, csh2, csc2, cg2 = jnp.split(jax.nn.silu(c_ctx) @ ada_w[l] + ada_b[l], 6, axis=-1)
        moe_p = (moe_w_group[l], moe_b_group[l], moe_w_router[l], moe_b_router[l],
                 moe_w_gate[l], moe_w_up[l], moe_w_down[l])
        u_lat = rmsnorm(x, norm1_g[l]) * (1.0 + sc1) + sh1
        if l % 2 == 0 or ctx_needed:
            u_ctx = rmsnorm(ctx, norm1_g[l]) * (1.0 + csc1) + csh1
        if l % 2 == 0:
            e = l // 2
            y_lat, y_ctx = even_mixer(u_lat, u_ctx, w_in_even[e], qn_g[e], kn_g[e], na_rpb[e],
                                      sc_conv_w[e], ctx_needed)
        else:
            o = l // 2
            odd_p = (w_in_odd[o], hy_short_w[o], hy_w1[o], hy_b1[o], hy_w2[o], hy_b2[o], hy_w3[o],
                     hy_freq[o], hy_bias[o], cf_conv_w[o], cf_conv_b[o], cf_ln_g[o], cf_ln_b[o])
            y_lat = odd_mixer(u_lat, *odd_p)
            y_ctx = odd_mixer(u_ctx, *odd_p) if ctx_needed else None
        x = x + g1 * (y_lat @ w_out[l])
        x = x + g2 * hier_moe(rmsnorm(x, norm2_g[l]) * (1.0 + sc2) + sh2, *moe_p)
        if ctx_needed:
            ctx = ctx + cg1 * (y_ctx @ w_out[l])
            ctx = ctx + cg2 * hier_moe(rmsnorm(ctx, norm2_g[l]) * (1.0 + csc2) + csh2, *moe_p)
    return x
```

```python
import functools
import math

import numpy as np
import jax
import jax.numpy as jnp
from jax import lax
from jax.experimental import pallas as pl
from jax.experimental.pallas import tpu as pltpu

F32 = jnp.float32
BF16 = jnp.bfloat16

D_MODEL = 1024
GRID_W = 64
NA_HEADS = 8
HEAD_DIM = 64
NA_WIDTH = 512
NA_WIN_H = 8
NA_WIN_W = 16
SC_WIDTH = 512
HY_WIDTH = 512
HY_BANDS = 16
HY_EMB = 1 + 2 * HY_BANDS
HY_FFN = 64
HY_MAX_DECAY = math.log(1e-2) / 0.3
HY_MIN_DECAY = math.log(1e-2) / 1.5
CF_WIDTH = 512
CF_TAPS = 31
N_GROUPS = 4
EXPERTS_PER_GROUP = 4
N_EXPERTS = 16
D_EXPERT = 256
RMS_EPS = 1e-6
LN_EPS = 1e-5
NEG_INF = -1e30

VMEM_LIMIT_BYTES = 56 * 1024 * 1024
LANES = 128

FFT_NA = 64
FFT_NB = 128
FFT_N = FFT_NA * FFT_NB


def _params(*sem):
    return pltpu.CompilerParams(dimension_semantics=tuple(sem), vmem_limit_bytes=VMEM_LIMIT_BYTES)


def _dot(a, b):
    return jnp.dot(a, b, preferred_element_type=F32)


def _dot_nt(a, b):
    return lax.dot_general(a, b, (((1,), (1,)), ((), ())), preferred_element_type=F32)


def _dot_tn(a, b):
    return lax.dot_general(a, b, (((0,), (0,)), ((), ())), preferred_element_type=F32)


def _dot_f32(a, b):
    return jnp.dot(a, b, preferred_element_type=F32, precision=lax.Precision.HIGHEST)


def _silu(x):
    return x * jax.nn.sigmoid(x)


def _ada_kernel(ct_ref, w_ref, b_ref, o_ref, *, n_cond):
    ct = ct_ref[...]
    s = _silu(ct)
    w = w_ref[0]
    rows = [jnp.sum(w * s[:, r:r + 1], axis=0, keepdims=True) for r in range(n_cond)]
    rows.append(jnp.zeros((8 - n_cond, w.shape[1]), F32))
    o_ref[0] = jnp.concatenate(rows, axis=0) + b_ref[0]


def _ada_modulation(cond, ada_w, ada_b):
    n_cond, d = cond.shape
    depth, _, n6 = ada_w.shape
    tn = 768
    ct = jnp.zeros((d, 8), F32).at[:, :n_cond].set(cond.T)
    return pl.pallas_call(
        functools.partial(_ada_kernel, n_cond=n_cond),
        out_shape=jax.ShapeDtypeStruct((depth, 8, n6), F32),
        grid=(depth, n6 // tn),
        in_specs=[pl.BlockSpec((d, 8), lambda l, j: (0, 0)),
                  pl.BlockSpec((1, d, tn), lambda l, j: (l, 0, j)),
                  pl.BlockSpec((1, 1, tn), lambda l, j: (l, 0, j))],
        out_specs=pl.BlockSpec((1, 8, tn), lambda l, j: (l, 0, j)),
        compiler_params=_params("parallel", "parallel"),
        name="ada_modulation",
    )(ct, ada_w, ada_b.reshape(depth, 1, n6))


def _norm_modulate(x, g, sc, sh):
    ms = jnp.mean(x * x, axis=-1, keepdims=True)
    return x * lax.rsqrt(ms + RMS_EPS) * g * (1.0 + sc) + sh


def _head_rmsnorm(t, seg, gain):
    sq = t * t
    hi = sq.astype(BF16)
    lo = (sq - hi.astype(F32)).astype(BF16)
    ss = _dot(hi, seg) + _dot(lo, seg)
    return t * lax.rsqrt(ss * (1.0 / HEAD_DIM) + RMS_EPS) * gain


def _inproj_even_kernel(x_ref, g_ref, sc_ref, sh_ref, w_ref, qg_ref, kg_ref, seg_ref,
                        q_ref, k_ref, v_ref, gb_ref, p_ref):
    u = _norm_modulate(x_ref[0], g_ref[...], sc_ref[0], sh_ref[0]).astype(BF16)
    seg = seg_ref[...]
    w = NA_WIDTH
    q = _dot(u, w_ref[:, 0 * w:1 * w])
    q_ref[0] = (_head_rmsnorm(q, seg, qg_ref[...]) * (HEAD_DIM ** -0.5)).astype(BF16)
    k = _dot(u, w_ref[:, 1 * w:2 * w])
    k_ref[0] = _head_rmsnorm(k, seg, kg_ref[...]).astype(BF16)
    v_ref[0] = _dot(u, w_ref[:, 2 * w:3 * w]).astype(BF16)
    gb_ref[0] = _dot(u, w_ref[:, 3 * w:4 * w]).astype(BF16)
    gc = _dot(u, w_ref[:, 4 * w:5 * w])
    hv = _dot(u, w_ref[:, 5 * w:6 * w])
    p_ref[0] = (gc * hv).astype(BF16)


def _inproj_even(x, g, sc, sh, w_bf, qg, kg, seg):
    b, n, d = x.shape
    tm = min(512, n)
    tok = lambda i, j: (i, j, 0)
    mod = lambda i, j: (i, 0, 0)
    const = lambda i, j: (0, 0)
    out = jax.ShapeDtypeStruct((b, n, NA_WIDTH), BF16)
    return pl.pallas_call(
        _inproj_even_kernel,
        out_shape=(out,) * 5,
        grid=(b, n // tm),
        in_specs=[pl.BlockSpec((1, tm, d), tok),
                  pl.BlockSpec((1, d), const),
                  pl.BlockSpec((1, 1, d), mod),
                  pl.BlockSpec((1, 1, d), mod),
                  pl.BlockSpec(w_bf.shape, const),
                  pl.BlockSpec((1, NA_WIDTH), const),
                  pl.BlockSpec((1, NA_WIDTH), const),
                  pl.BlockSpec((NA_WIDTH, NA_WIDTH), const)],
        out_specs=(pl.BlockSpec((1, tm, NA_WIDTH), tok),) * 5,
        compiler_params=_params("parallel", "parallel"),
        name="inproj_even",
    )(x, g, sc, sh, w_bf, qg, kg, seg)


def _bias_kernel(rpb_ref, o_ref):
    h = pl.program_id(0)
    qi = lax.broadcasted_iota(jnp.int32, (GRID_W, GRID_W), 0)
    ki = lax.broadcasted_iota(jnp.int32, (GRID_W, GRID_W), 1)
    start = jnp.clip(qi - NA_WIN_W // 2, 0, GRID_W - NA_WIN_W)
    valid = jnp.logical_and(ki >= start, ki < start + NA_WIN_W)
    cidx = jnp.clip(ki - qi, -(NA_WIN_W - 1), NA_WIN_W - 1) + (NA_WIN_W - 1)
    n_dr = 2 * NA_WIN_H - 1
    n_dc = 2 * NA_WIN_W - 1

    def body(j, accs):
        m = cidx == j
        return tuple(jnp.where(m, rpb_ref[(h * n_dr + d) * n_dc + j], a) for d, a in enumerate(accs))

    accs = lax.fori_loop(0, n_dc, body, tuple(jnp.zeros((GRID_W, GRID_W), F32) for _ in range(n_dr)))
    tiles = [jnp.where(valid, a, NEG_INF) for a in accs]
    for d0 in range(NA_WIN_H):
        o_ref[0, d0] = jnp.concatenate(tiles[d0:d0 + NA_WIN_H], axis=1)


def _bias_table(rpb):
    return pl.pallas_call(
        _bias_kernel,
        out_shape=jax.ShapeDtypeStruct((NA_HEADS, NA_WIN_H, GRID_W, NA_WIN_H * GRID_W), F32),
        grid=(NA_HEADS,),
        in_specs=[pl.BlockSpec(memory_space=pltpu.SMEM)],
        out_specs=pl.BlockSpec((1, NA_WIN_H, GRID_W, NA_WIN_H * GRID_W), lambda h: (h, 0, 0, 0)),
        compiler_params=_params("arbitrary"),
        name="rpb_bias_table",
    )(rpb.reshape(-1))


def _pair_attention(q2, kw, vw, bias, kc, vc, first_half):
    s_c = _dot_nt(q2, kc)
    m = jnp.max(s_c, axis=-1, keepdims=True)
    if kw is not None:
        s_w = _dot_nt(q2, kw) + bias
        m = jnp.maximum(m, jnp.max(s_w, axis=-1, keepdims=True))
        p_w = jnp.exp(s_w - m)
    p_c = jnp.exp(s_c - m)
    den = jnp.sum(p_c, axis=-1, keepdims=True)
    o = _dot(p_c.astype(BF16), vc)
    if kw is not None:
        den = den + jnp.sum(p_w, axis=-1, keepdims=True)
        o = o + _dot(p_w.astype(BF16), vw)
    o = o / den
    half = o.shape[0] // 2
    return jnp.where(first_half, o[:half], o[half:])


def _stack_heads(qp, first_half):
    zero = jnp.zeros_like(qp)
    return jnp.concatenate([jnp.where(first_half, qp, zero), jnp.where(first_half, zero, qp)], axis=0)


def _natten_kernel(q_ref, k_ref, v_ref, kc_ref, vc_ref, bias_ref, o_ref, *, rows_per_step, n_rows):
    blk = pl.program_id(1)
    lane = lax.broadcasted_iota(jnp.int32, (GRID_W, LANES), 1)
    first_half = lane < HEAD_DIM
    band = NA_WIN_H * GRID_W

    def row_body(j, carry):
        r = blk * rows_per_step + j
        start = jnp.clip(r - NA_WIN_H // 2, 0, n_rows - NA_WIN_H)
        d0 = start - r + (NA_WIN_H - 1)
        koff = pl.multiple_of(start * GRID_W, GRID_W)
        qoff = pl.multiple_of(j * GRID_W, GRID_W)
        outs = []
        for hp in range(NA_HEADS // 2):
            cs = slice(hp * LANES, (hp + 1) * LANES)
            q2 = _stack_heads(q_ref[0, pl.ds(qoff, GRID_W), cs], first_half)
            kw = k_ref[0, pl.ds(koff, band), cs]
            vw = v_ref[0, pl.ds(koff, band), cs]
            bias = jnp.concatenate([bias_ref[2 * hp, d0], bias_ref[2 * hp + 1, d0]], axis=0)
            outs.append(_pair_attention(q2, kw, vw, bias, kc_ref[0, :, cs], vc_ref[0, :, cs], first_half))
        o_ref[0, pl.ds(qoff, GRID_W), :] = jnp.concatenate(outs, axis=1).astype(o_ref.dtype)
        return carry

    lax.fori_loop(0, rows_per_step, row_body, 0)


def _natten(q, k, v, kc, vc, bias):
    b, n, w = q.shape
    n_rows = n // GRID_W
    rows_per_step = 8
    tq = rows_per_step * GRID_W
    lc = kc.shape[1]
    return pl.pallas_call(
        functools.partial(_natten_kernel, rows_per_step=rows_per_step, n_rows=n_rows),
        out_shape=jax.ShapeDtypeStruct((b, n, w), BF16),
        grid=(b, n_rows // rows_per_step),
        in_specs=[pl.BlockSpec((1, tq, w), lambda i, j: (i, j, 0)),
                  pl.BlockSpec((1, n, w), lambda i, j: (i, 0, 0)),
                  pl.BlockSpec((1, n, w), lambda i, j: (i, 0, 0)),
                  pl.BlockSpec((1, lc, w), lambda i, j: (i, 0, 0)),
                  pl.BlockSpec((1, lc, w), lambda i, j: (i, 0, 0)),
                  pl.BlockSpec(bias.shape, lambda i, j: (0, 0, 0, 0))],
        out_specs=pl.BlockSpec((1, tq, w), lambda i, j: (i, j, 0)),
        compiler_params=_params("parallel", "arbitrary"),
        name="neighbourhood_attention",
    )(q, k, v, kc, vc, bias)


def _ctx_attn_kernel(q_ref, k_ref, v_ref, o_ref):
    lc = q_ref.shape[1]
    lane = lax.broadcasted_iota(jnp.int32, (lc, LANES), 1)
    first_half = lane < HEAD_DIM
    outs = []
    for hp in range(NA_HEADS // 2):
        cs = slice(hp * LANES, (hp + 1) * LANES)
        q2 = _stack_heads(q_ref[0, :, cs], first_half)
        outs.append(_pair_attention(q2, None, None, None, k_ref[0, :, cs], v_ref[0, :, cs], first_half))
    o_ref[0] = jnp.concatenate(outs, axis=1).astype(o_ref.dtype)


def _ctx_attention(q, k, v):
    b, lc, w = q.shape
    spec = pl.BlockSpec((1, lc, w), lambda i: (i, 0, 0))
    return pl.pallas_call(
        _ctx_attn_kernel,
        out_shape=jax.ShapeDtypeStruct((b, lc, w), BF16),
        grid=(b,),
        in_specs=[spec, spec, spec],
        out_specs=spec,
        compiler_params=_params("parallel"),
        name="context_attention",
    )(q, k, v)


def _sgconv_kernel(gb_ref, p_ref, w_ref, o_ref):
    p = p_ref[0].astype(F32)
    n = p.shape[0]
    row = lax.broadcasted_iota(jnp.int32, p.shape, 0)
    prev = jnp.where(row == 0, 0.0, pltpu.roll(p, 1, axis=0))
    nxt = jnp.where(row == n - 1, 0.0, pltpu.roll(p, n - 1, axis=0))
    y = w_ref[0:1, :] * prev + w_ref[1:2, :] * p + w_ref[2:3, :] * nxt
    o_ref[0] = (gb_ref[0].astype(F32) * y).astype(o_ref.dtype)


def _short_gated_conv(gb, p, w):
    b, n, c = p.shape
    spec = pl.BlockSpec((1, n, LANES), lambda i, j: (i, 0, j))
    return pl.pallas_call(
        _sgconv_kernel,
        out_shape=jax.ShapeDtypeStruct((b, n, c), BF16),
        grid=(b, c // LANES),
        in_specs=[spec, spec, pl.BlockSpec((3, LANES), lambda i, j: (0, j))],
        out_specs=spec,
        compiler_params=_params("parallel", "parallel"),
        name="short_gated_conv",
    )(gb, p, w)


def _inproj_odd_kernel(x_ref, g_ref, sc_ref, sh_ref, wht_ref, wag_ref, hy_ref, ag_ref):
    u = _norm_modulate(x_ref[0], g_ref[...], sc_ref[0], sh_ref[0]).astype(BF16)
    hy_ref[0] = _dot_nt(wht_ref[...], u).astype(BF16)
    a = _dot(u, wag_ref[:, :CF_WIDTH])
    g = _dot(u, wag_ref[:, CF_WIDTH:])
    ag_ref[0] = (a * jax.nn.sigmoid(g)).astype(BF16)


def _inproj_odd(x, g, sc, sh, wht_bf, wag_bf):
    b, n, d = x.shape
    tm = min(512, n)
    hw = wht_bf.shape[0]
    tok = lambda i, j: (i, j, 0)
    mod = lambda i, j: (i, 0, 0)
    const = lambda i, j: (0, 0)
    return pl.pallas_call(
        _inproj_odd_kernel,
        out_shape=(jax.ShapeDtypeStruct((b, hw, n), BF16), jax.ShapeDtypeStruct((b, n, CF_WIDTH), BF16)),
        grid=(b, n // tm),
        in_specs=[pl.BlockSpec((1, tm, d), tok),
                  pl.BlockSpec((1, d), const),
                  pl.BlockSpec((1, 1, d), mod),
                  pl.BlockSpec((1, 1, d), mod),
                  pl.BlockSpec(wht_bf.shape, const),
                  pl.BlockSpec(wag_bf.shape, const)],
        out_specs=(pl.BlockSpec((1, hw, tm), lambda i, j: (i, 0, j)),
                   pl.BlockSpec((1, tm, CF_WIDTH), tok)),
        compiler_params=_params("parallel", "parallel"),
        name="inproj_odd",
    )(x, g, sc, sh, wht_bf, wag_bf)


CF_PAD = 16


def _conformer_kernel(ag_ref, w_ref, cb_ref, lg_ref, lb_ref, o_ref, pad_ref, *, seq, chunk):
    zeros = jnp.zeros((CF_PAD, CF_WIDTH), F32)
    pad_ref[0:CF_PAD, :] = zeros
    pad_ref[CF_PAD + seq:2 * CF_PAD + seq, :] = zeros
    pad_ref[CF_PAD:CF_PAD + seq, :] = ag_ref[0].astype(F32)
    half = CF_TAPS // 2
    for c in range(seq // chunk):
        base = c * chunk
        acc = jnp.zeros((chunk, CF_WIDTH), F32)
        for j in range(CF_TAPS):
            off = base + j - half + CF_PAD
            acc = acc + w_ref[j:j + 1, :] * pad_ref[off:off + chunk, :]
        y = acc + cb_ref[...]
        mu = jnp.mean(y, axis=-1, keepdims=True)
        yc = y - mu
        var = jnp.mean(yc * yc, axis=-1, keepdims=True)
        z = yc * lax.rsqrt(var + LN_EPS) * lg_ref[...] + lb_ref[...]
        o_ref[0, base:base + chunk, :] = _silu(z).astype(o_ref.dtype)


def _conformer(ag, w, cb, lg, lb):
    b, n, c = ag.shape
    chunk = 64
    spec = pl.BlockSpec((1, n, c), lambda i: (i, 0, 0))
    vec = pl.BlockSpec((1, c), lambda i: (0, 0))
    return pl.pallas_call(
        functools.partial(_conformer_kernel, seq=n, chunk=chunk),
        out_shape=jax.ShapeDtypeStruct((b, n, c), BF16),
        grid=(b,),
        in_specs=[spec, pl.BlockSpec((CF_TAPS, c), lambda i: (0, 0)), vec, vec, vec],
        out_specs=spec,
        scratch_shapes=[pltpu.VMEM((n + 2 * CF_PAD, c), F32)],
        compiler_params=_params("parallel"),
        name="conformer_conv",
    )(ag, w, cb, lg, lb)


def _hyena_features(length):
    t = np.linspace(0.0, 1.0, length, dtype=np.float32)
    w = (2.0 * math.pi * np.arange(length, dtype=np.float32) / length).astype(np.float32)
    bands = np.linspace(1e-4, HY_BANDS - 1, HY_BANDS, dtype=np.float32)
    ang = (bands[:, None] * w[None, :]).astype(np.float32)
    zt = np.concatenate([t[None, :], np.cos(ang), -np.sin(ang)], axis=0).astype(np.float32)
    deltas = np.abs(np.linspace(HY_MIN_DECAY, HY_MAX_DECAY, HY_WIDTH, dtype=np.float32))
    return zt, t[None, :], deltas[:, None]


def _taps_kernel(zt_ref, t_ref, dl_ref, w1t_ref, b1_ref, f0_ref, w2t_ref, b2_ref, f1_ref, w3t_ref,
                 o_ref, hid_ref):
    first = jnp.logical_and(pl.program_id(0) == 0, pl.program_id(1) == 0)

    @pl.when(first)
    def _():
        h1 = jnp.sin(f0_ref[...] * (_dot_f32(w1t_ref[...], zt_ref[...]) + b1_ref[...]))
        hid_ref[...] = jnp.sin(f1_ref[...] * (_dot_f32(w2t_ref[...], h1) + b2_ref[...]))

    hid = hid_ref[...]
    decay = jnp.exp(-(dl_ref[...] * t_ref[...]))
    fwd = _dot_f32(w3t_ref[0, 0], hid) * decay
    bwd = _dot_f32(w3t_ref[0, 1], hid) * decay
    nrm = jnp.sum(jnp.abs(fwd), axis=-1, keepdims=True) + jnp.sum(jnp.abs(bwd), axis=-1, keepdims=True)
    lane = lax.broadcasted_iota(jnp.int32, fwd.shape, 1)
    o_ref[0, 0] = fwd / nrm
    o_ref[0, 1] = jnp.where(lane == 0, 0.0, bwd / nrm)


def _hyena_taps(length, w1, b1, w2, b2, w3, freq):
    zt, t, deltas = _hyena_features(length)
    cb = 128
    w3t = w3.T.reshape(2, 2, HY_WIDTH, HY_FFN)
    col = lambda v: v.reshape(HY_FFN, 1)
    const = lambda o, j: (0, 0)
    return pl.pallas_call(
        _taps_kernel,
        out_shape=jax.ShapeDtypeStruct((2, 2, HY_WIDTH, length), F32),
        grid=(2, HY_WIDTH // cb),
        in_specs=[pl.BlockSpec((HY_EMB, length), const),
                  pl.BlockSpec((1, length), const),
                  pl.BlockSpec((cb, 1), lambda o, j: (j, 0)),
                  pl.BlockSpec((HY_FFN, HY_EMB), const),
                  pl.BlockSpec((HY_FFN, 1), const),
                  pl.BlockSpec((HY_FFN, 1), const),
                  pl.BlockSpec((HY_FFN, HY_FFN), const),
                  pl.BlockSpec((HY_FFN, 1), const),
                  pl.BlockSpec((HY_FFN, 1), const),
                  pl.BlockSpec((1, 2, cb, HY_FFN), lambda o, j: (o, 0, j, 0))],
        out_specs=pl.BlockSpec((1, 2, cb, length), lambda o, j: (o, 0, j, 0)),
        scratch_shapes=[pltpu.VMEM((HY_FFN, length), F32)],
        compiler_params=_params("arbitrary", "arbitrary"),
        name="hyena_filter_taps",
    )(jnp.asarray(zt), jnp.asarray(t), jnp.asarray(deltas), w1.T, col(b1), col(freq[0]),
      w2.T, col(b2), col(freq[1]), w3t)


def _dft_constants():
    na, nb, n = FFT_NA, FFT_NB, FFT_N
    half = na // 2
    ka = np.arange(na)
    ang_a = 2.0 * np.pi * np.outer(ka, ka) / na
    ca, sa = np.cos(ang_a), np.sin(ang_a)
    fa = np.block([[ca[:half], -sa[:half]], [sa[:half], ca[:half]]])
    fai = np.block([[ca[:, :half], sa[:, :half]], [-sa[:, :half], ca[:, :half]]])
    kb = np.arange(nb)
    ang_b = 2.0 * np.pi * np.outer(kb, kb) / nb
    cbm, sbm = np.cos(ang_b), np.sin(ang_b)
    fb = np.block([[cbm, -sbm], [sbm, cbm]])
    fbi = np.block([[cbm, sbm], [-sbm, cbm]])
    ang_t = 2.0 * np.pi * np.outer(kb, ka) / n
    ct, st = np.cos(ang_t), np.sin(ang_t)
    tw_fc = np.concatenate([ct, ct], axis=1)
    tw_fs = np.concatenate([st, -st], axis=1)
    tw_ic, tw_is = ct.T.copy(), st.T.copy()
    bf = lambda a: jnp.asarray(a, dtype=F32).astype(BF16)
    f32 = lambda a: jnp.asarray(a, dtype=F32)
    return dict(fa=bf(fa), fai=bf(fai), fb=bf(fb), fbi=bf(fbi),
                tw_fc=f32(tw_fc), tw_fs=f32(tw_fs), tw_ic=f32(tw_ic), tw_is=f32(tw_is))


def _fft_forward(zr, zi, fa, tw_fc, tw_fs, fb):
    c, half, nb = zr.shape
    tr = jnp.swapaxes(zr, 1, 2)
    if zi is None:
        lhs, mat = tr, fa[:half]
    else:
        lhs, mat = jnp.concatenate([tr, jnp.swapaxes(zi, 1, 2)], axis=2), fa
    a = _dot(lhs.reshape(c * nb, lhs.shape[2]).astype(BF16), mat).reshape(c, nb, 2 * FFT_NA)
    a = a * tw_fc + pltpu.roll(a, FFT_NA, axis=2) * tw_fs
    t = jnp.swapaxes(a, 1, 2)
    lhs2 = jnp.concatenate([t[:, :FFT_NA, :], t[:, FFT_NA:, :]], axis=2)
    x = _dot(lhs2.reshape(c * FFT_NA, 2 * nb).astype(BF16), fb)
    return x.reshape(c, FFT_NA, 2 * nb)


def _fft_inverse(y, fbi, tw_ic, tw_is, fai):
    c = y.shape[0]
    nb = FFT_NB
    b = _dot(y.reshape(c * FFT_NA, 2 * nb).astype(BF16), fbi).reshape(c, FFT_NA, 2 * nb)
    br, bi = b[:, :, :nb], b[:, :, nb:]
    rr = br * tw_ic - bi * tw_is
    ii = bi * tw_ic + br * tw_is
    t = jnp.swapaxes(jnp.concatenate([rr, ii], axis=1), 1, 2)
    o = _dot(t.reshape(c * nb, 2 * FFT_NA).astype(BF16), fai).reshape(c, nb, FFT_NA)
    o = jnp.swapaxes(o, 1, 2)
    return o[:, :FFT_NA // 2, :], o[:, FFT_NA // 2:, :]


def _filter_fft_kernel(taps_ref, fa_ref, twc_ref, tws_ref, fb_ref, o_ref):
    args = (fa_ref[...], twc_ref[...], tws_ref[...], fb_ref[...])
    f = _fft_forward(taps_ref[0, 0], None, *args)
    g = _fft_forward(taps_ref[0, 1], None, *args)
    nb = FFT_NB
    hr = f[:, :, :nb] + g[:, :, :nb]
    hi = f[:, :, nb:] - g[:, :, nb:]
    o_ref[0] = jnp.concatenate([hr, hi], axis=2) * (1.0 / FFT_N)


def _filter_fft(taps, consts):
    _, _, c, length = taps.shape
    cb = 32
    taps5 = taps.reshape(2, 2, c, length // FFT_NB, FFT_NB)
    cm = lambda o, j: (0, 0)
    return pl.pallas_call(
        _filter_fft_kernel,
        out_shape=jax.ShapeDtypeStruct((2, c, FFT_NA, 2 * FFT_NB), F32),
        grid=(2, c // cb),
        in_specs=[pl.BlockSpec((1, 2, cb, length // FFT_NB, FFT_NB), lambda o, j: (o, 0, j, 0, 0)),
                  pl.BlockSpec(consts["fa"].shape, cm),
                  pl.BlockSpec(consts["tw_fc"].shape, cm),
                  pl.BlockSpec(consts["tw_fs"].shape, cm),
                  pl.BlockSpec(consts["fb"].shape, cm)],
        out_specs=pl.BlockSpec((1, cb, FFT_NA, 2 * FFT_NB), lambda o, j: (o, j, 0, 0)),
        compiler_params=_params("parallel", "parallel"),
        name="hyena_filter_fft",
    )(taps5, consts["fa"], consts["tw_fc"], consts["tw_fs"], consts["fb"])


def _shift_tokens(a, direction):
    rows = a.shape[-2]
    lane = lax.broadcasted_iota(jnp.int32, a.shape, a.ndim - 1)
    row = lax.broadcasted_iota(jnp.int32, a.shape, a.ndim - 2)
    if direction == 1:
        l = pltpu.roll(a, 1, axis=a.ndim - 1)
        ls = pltpu.roll(l, 1, axis=a.ndim - 2)
        out = jnp.where(lane == 0, ls, l)
        edge = jnp.logical_and(lane == 0, row == 0)
    else:
        l = pltpu.roll(a, LANES - 1, axis=a.ndim - 1)
        ls = pltpu.roll(l, rows - 1, axis=a.ndim - 2)
        out = jnp.where(lane == LANES - 1, ls, l)
        edge = jnp.logical_and(lane == LANES - 1, row == rows - 1)
    return jnp.where(edge, 0.0, out)


def _short_conv3(a, w_ref):
    return w_ref[0] * _shift_tokens(a, 1) + w_ref[1] * a + w_ref[2] * _shift_tokens(a, -1)


def _hyena_kernel(v_ref, x1_ref, x2_ref, wv_ref, w1_ref, w2_ref, h_ref, hb_ref,
                  fa_ref, twfc_ref, twfs_ref, fb_ref, fbi_ref, twic_ref, twis_ref, fai_ref, o_ref):
    fwd_c = (fa_ref[...], twfc_ref[...], twfs_ref[...], fb_ref[...])
    inv_c = (fbi_ref[...], twic_ref[...], twis_ref[...], fai_ref[...])
    z = _short_conv3(v_ref[...].astype(F32), wv_ref)
    zr, zi = z[0], z[1]
    nb = FFT_NB
    for o, (g_ref, gw_ref) in enumerate(((x1_ref, w1_ref), (x2_ref, w2_ref))):
        x = _fft_forward(zr, zi, *fwd_c)
        h = h_ref[o]
        xr, xi, hr, hi = x[:, :, :nb], x[:, :, nb:], h[:, :, :nb], h[:, :, nb:]
        y = jnp.concatenate([xr * hr - xi * hi, xr * hi + xi * hr], axis=2)
        yr, yi = _fft_inverse(y, *inv_c)
        gate = _short_conv3(g_ref[...].astype(F32), gw_ref)
        bias = hb_ref[o]
        zr = gate[0] * (yr + zr * bias)
        zi = gate[1] * (yi + zi * bias)
    o_ref[0] = zr.astype(o_ref.dtype)
    o_ref[1] = zi.astype(o_ref.dtype)


def _hyena_latent(hyt, short_w, h_spec, hy_bias, consts):
    b, c3, length = hyt.shape
    c = c3 // 3
    rows = length // FFT_NB
    cb = 32
    nblk = c // cb
    hy4 = hyt.reshape(b, c3, rows, FFT_NB)
    w4 = jnp.broadcast_to(short_w.reshape(3, c3, 1, 1), (3, c3, 1, FFT_NB))
    hb4 = jnp.broadcast_to(hy_bias.reshape(2, c, 1, 1), (2, c, 1, FFT_NB))
    sig = lambda g: pl.BlockSpec((2, cb, rows, FFT_NB), lambda j, p: (p, g * nblk + j, 0, 0))
    wsp = lambda g: pl.BlockSpec((3, cb, 1, FFT_NB), lambda j, p: (0, g * nblk + j, 0, 0))
    cm = lambda j, p: (0, 0)
    names = ("fa", "tw_fc", "tw_fs", "fb", "fbi", "tw_ic", "tw_is", "fai")
    out = pl.pallas_call(
        _hyena_kernel,
        out_shape=jax.ShapeDtypeStruct((b, c, rows, FFT_NB), BF16),
        grid=(nblk, b // 2),
        in_specs=[sig(0), sig(1), sig(2), wsp(0), wsp(1), wsp(2),
                  pl.BlockSpec((2, cb, FFT_NA, 2 * FFT_NB), lambda j, p: (0, j, 0, 0)),
                  pl.BlockSpec((2, cb, 1, FFT_NB), lambda j, p: (0, j, 0, 0))]
                 + [pl.BlockSpec(consts[k].shape, cm) for k in names],
        out_specs=pl.BlockSpec((2, cb, rows, FFT_NB), lambda j, p: (p, j, 0, 0)),
        compiler_params=_params("parallel", "arbitrary"),
        name="hyena_long_conv",
    )(hy4, hy4, hy4, w4, w4, w4, h_spec, hb4, *[consts[k] for k in names])
    return out.reshape(b, c, length)


def _dense_dft_constants(length):
    n = 2 * length
    k = np.arange(n)
    ang = 2.0 * np.pi * np.outer(np.arange(length), k) / n
    fwd = np.concatenate([np.cos(ang), -np.sin(ang)], axis=1)
    inv = np.concatenate([np.cos(ang).T, -np.sin(ang).T], axis=0) / n
    bf = lambda a: jnp.asarray(a, dtype=F32).astype(BF16)
    return bf(fwd), bf(inv)


def _shift_lanes(a, direction):
    n = a.shape[-1]
    lane = lax.broadcasted_iota(jnp.int32, a.shape, a.ndim - 1)
    if direction == 1:
        return jnp.where(lane == 0, 0.0, pltpu.roll(a, 1, axis=a.ndim - 1))
    return jnp.where(lane == n - 1, 0.0, pltpu.roll(a, n - 1, axis=a.ndim - 1))


def _hyena_ctx_kernel(v_ref, x1_ref, x2_ref, wv_ref, w1_ref, w2_ref, taps_ref, hb_ref, fwd_ref, inv_ref, o_ref):
    bsz, cb, length = v_ref.shape
    n = 2 * length
    fwd_m, inv_m = fwd_ref[...], inv_ref[...]

    def conv3(ref, w_ref):
        a = ref[...].astype(F32)
        return w_ref[0] * _shift_lanes(a, 1) + w_ref[1] * a + w_ref[2] * _shift_lanes(a, -1)

    z = conv3(v_ref, wv_ref)
    for o, (g_ref, gw_ref) in enumerate(((x1_ref, w1_ref), (x2_ref, w2_ref))):
        f = _dot(taps_ref[o, 0].astype(BF16), fwd_m)
        g = _dot(taps_ref[o, 1].astype(BF16), fwd_m)
        hr = f[:, :n] + g[:, :n]
        hi = f[:, n:] - g[:, n:]
        x = _dot(z.reshape(bsz * cb, length).astype(BF16), fwd_m).reshape(bsz, cb, 2 * n)
        xr, xi = x[:, :, :n], x[:, :, n:]
        y = jnp.concatenate([xr * hr - xi * hi, xr * hi + xi * hr], axis=2)
        yt = _dot(y.reshape(bsz * cb, 2 * n).astype(BF16), inv_m).reshape(bsz, cb, length)
        z = conv3(g_ref, gw_ref) * (yt + z * hb_ref[o])
    o_ref[...] = z.astype(o_ref.dtype)


def _hyena_context(hyt, short_w, taps, hy_bias):
    b, c3, length = hyt.shape
    c = c3 // 3
    cb = 128
    nblk = c // cb
    fwd_m, inv_m = _dense_dft_constants(length)
    w4 = jnp.broadcast_to(short_w.reshape(3, c3, 1), (3, c3, length))
    hb = jnp.broadcast_to(hy_bias.reshape(2, c, 1), (2, c, length))
    sig = lambda g: pl.BlockSpec((b, cb, length), lambda j: (0, g * nblk + j, 0))
    wsp = lambda g: pl.BlockSpec((3, cb, length), lambda j: (0, g * nblk + j, 0))
    return pl.pallas_call(
        _hyena_ctx_kernel,
        out_shape=jax.ShapeDtypeStruct((b, c, length), BF16),
        grid=(nblk,),
        in_specs=[sig(0), sig(1), sig(2), wsp(0), wsp(1), wsp(2),
                  pl.BlockSpec((2, 2, cb, length), lambda j: (0, 0, j, 0)),
                  pl.BlockSpec((2, cb, length), lambda j: (0, j, 0)),
                  pl.BlockSpec(fwd_m.shape, lambda j: (0, 0)),
                  pl.BlockSpec(inv_m.shape, lambda j: (0, 0))],
        out_specs=pl.BlockSpec((b, cb, length), lambda j: (0, j, 0)),
        compiler_params=_params("parallel"),
        name="hyena_context_conv",
    )(hyt, hyt, hyt, w4, w4, w4, taps, hb, fwd_m, inv_m)


def _route(logits):
    gl = logits[:, :N_EXPERTS]
    el = logits[:, N_EXPERTS:]
    lane = lax.broadcasted_iota(jnp.int32, gl.shape, 1).astype(F32)
    grp = jnp.floor(lane * (1.0 / EXPERTS_PER_GROUP))
    big = float(N_EXPERTS)
    gmax = jnp.max(gl, axis=-1, keepdims=True)
    gidx = jnp.min(jnp.where(gl == gmax, grp, big), axis=-1, keepdims=True)
    gsum = jnp.sum(jnp.exp(gl - gmax), axis=-1, keepdims=True) * (1.0 / EXPERTS_PER_GROUP)
    g_w = 1.0 / gsum
    em = jnp.where(grp == gidx, el, NEG_INF)
    t1 = jnp.max(em, axis=-1, keepdims=True)
    i1 = jnp.min(jnp.where(em == t1, lane, big), axis=-1, keepdims=True)
    em2 = jnp.where(lane == i1, 2.0 * NEG_INF, em)
    t2 = jnp.max(em2, axis=-1, keepdims=True)
    i2 = jnp.min(jnp.where(em2 == t2, lane, big), axis=-1, keepdims=True)
    e2 = jnp.exp(t2 - t1)
    den = 1.0 + e2
    w1 = g_w / den
    w2 = g_w * e2 / den
    return jnp.where(lane == i1, w1, 0.0) + jnp.where(lane == i2, w2, 0.0)


def _outproj_kernel(y1_ref, y2_ref, w_ref, x_ref, g1_ref, n2_ref, sc2_ref, sh2_ref, wr_ref, br_ref,
                    xo_ref, h_ref, cmb_ref, *, y1_channel_major):
    half = w_ref.shape[0] // 2
    if y1_channel_major:
        acc = _dot_tn(y1_ref[0], w_ref[:half, :])
    else:
        acc = _dot(y1_ref[0], w_ref[:half, :])
    acc = acc + _dot(y2_ref[0], w_ref[half:, :])
    xn = x_ref[0] + g1_ref[0] * acc
    xo_ref[0] = xn
    h = _norm_modulate(xn, n2_ref[...], sc2_ref[0], sh2_ref[0])
    h_ref[0] = h.astype(BF16)
    cmb = _route(_dot_f32(h, wr_ref[...]) + br_ref[...])
    for g in range(N_GROUPS):
        cmb_ref[g, 0] = cmb[:, g * EXPERTS_PER_GROUP:(g + 1) * EXPERTS_PER_GROUP]


def _outproj(y1, y2, w_bf, x, g1, n2g, sc2, sh2, wr, br, y1_channel_major):
    b, n, d = x.shape
    tm = min(512, n)
    half = d // 2
    tok = lambda i, j: (i, j, 0)
    mod = lambda i, j: (i, 0, 0)
    const = lambda i, j: (0, 0)
    y1_spec = (pl.BlockSpec((1, half, tm), lambda i, j: (i, 0, j)) if y1_channel_major
               else pl.BlockSpec((1, tm, half), tok))
    return pl.pallas_call(
        functools.partial(_outproj_kernel, y1_channel_major=y1_channel_major),
        out_shape=(jax.ShapeDtypeStruct((b, n, d), F32), jax.ShapeDtypeStruct((b, n, d), BF16),
                   jax.ShapeDtypeStruct((N_GROUPS, b, n, EXPERTS_PER_GROUP), F32)),
        grid=(b, n // tm),
        in_specs=[y1_spec,
                  pl.BlockSpec((1, tm, half), tok),
                  pl.BlockSpec((d, d), const),
                  pl.BlockSpec((1, tm, d), tok),
                  pl.BlockSpec((1, 1, d), mod),
                  pl.BlockSpec((1, d), const),
                  pl.BlockSpec((1, 1, d), mod),
                  pl.BlockSpec((1, 1, d), mod),
                  pl.BlockSpec(wr.shape, const),
                  pl.BlockSpec(br.shape, const)],
        out_specs=(pl.BlockSpec((1, tm, d), tok), pl.BlockSpec((1, tm, d), tok),
                   pl.BlockSpec((N_GROUPS, 1, tm, EXPERTS_PER_GROUP), lambda i, j: (0, i, j, 0))),
        compiler_params=_params("parallel", "parallel"),
        name="outproj_router",
    )(y1, y2, w_bf, x, g1, n2g, sc2, sh2, wr, br)


def _moe_kernel(h_ref, cmb_ref, x_ref, g2_ref, wg_ref, wu_ref, wd_ref, o_ref, acc_ref):
    j = pl.program_id(2)
    h = h_ref[0]
    a = _dot(h, wg_ref[...])
    u = _dot(h, wu_ref[...])
    act = _silu(a) * u
    cmb = cmb_ref[0, 0]
    parts = [act[:, e * D_EXPERT:(e + 1) * D_EXPERT] * cmb[:, e:e + 1] for e in range(EXPERTS_PER_GROUP)]
    contrib = _dot(jnp.concatenate(parts, axis=1).astype(BF16), wd_ref[...])

    @pl.when(j == 0)
    def _():
        acc_ref[...] = contrib

    @pl.when(j > 0)
    def _():
        acc_ref[...] += contrib

    @pl.when(j == N_GROUPS - 1)
    def _():
        o_ref[0] = x_ref[0] + g2_ref[0] * acc_ref[...]


def _moe(h, cmb, x, g2, wg, wu, wd):
    b, n, d = x.shape
    tm = min(512, n)
    gw = EXPERTS_PER_GROUP * D_EXPERT
    tok = lambda i, t, j: (i, t, 0)
    return pl.pallas_call(
        _moe_kernel,
        out_shape=jax.ShapeDtypeStruct((b, n, d), F32),
        grid=(b, n // tm, N_GROUPS),
        in_specs=[pl.BlockSpec((1, tm, d), tok),
                  pl.BlockSpec((1, 1, tm, EXPERTS_PER_GROUP), lambda i, t, j: (j, i, t, 0)),
                  pl.BlockSpec((1, tm, d), tok),
                  pl.BlockSpec((1, 1, d), lambda i, t, j: (i, 0, 0)),
                  pl.BlockSpec((d, gw), lambda i, t, j: (0, j)),
                  pl.BlockSpec((d, gw), lambda i, t, j: (0, j)),
                  pl.BlockSpec((gw, d), lambda i, t, j: (j, 0))],
        out_specs=pl.BlockSpec((1, tm, d), tok),
        scratch_shapes=[pltpu.VMEM((tm, d), F32)],
        compiler_params=_params("parallel", "parallel", "arbitrary"),
        name="moe_experts",
    )(h, cmb, x, g2, wg, wu, wd)


def kernel(x, c, ctx, c_ctx, ada_w, ada_b, norm1_g, norm2_g, w_in_even, qn_g, kn_g, na_rpb, sc_conv_w, w_in_odd, hy_short_w, hy_w1, hy_b1, hy_w2, hy_b2, hy_w3, hy_freq, hy_bias, cf_conv_w, cf_conv_b, cf_ln_g, cf_ln_b, w_out, moe_w_group, moe_b_group, moe_w_router, moe_b_router, moe_w_gate, moe_w_up, moe_w_down):
    depth = ada_w.shape[0]
    bsz, seq, d = x.shape
    lc = ctx.shape[1]
    assert 2 * seq == FFT_N and d == D_MODEL and bsz % 2 == 0

    mods = _ada_modulation(jnp.concatenate([c, c_ctx[None, :]], axis=0), ada_w, ada_b)
    seg = jnp.asarray(np.kron(np.eye(NA_HEADS), np.ones((HEAD_DIM, HEAD_DIM))), dtype=BF16)
    consts = _dft_constants()

    for l in range(depth):
        ctx_needed = any(j % 2 == 0 for j in range(l + 1, depth))
        lat_mod = [m[:, None, :] for m in jnp.split(mods[l, :bsz], 6, axis=-1)]
        ctx_mod = [jnp.broadcast_to(m[None, :, :], (bsz, 1, d)) for m in jnp.split(mods[l, bsz:bsz + 1], 6, axis=-1)]
        sh1, sc1, g1, sh2, sc2, g2 = lat_mod
        csh1, csc1, cg1, csh2, csc2, cg2 = ctx_mod
        n1g = norm1_g[l][None, :]
        n2g = norm2_g[l][None, :]
        w_out_bf = w_out[l].astype(BF16)
        wr = jnp.concatenate([jnp.repeat(moe_w_group[l], EXPERTS_PER_GROUP, axis=1), moe_w_router[l]], axis=1)
        br = jnp.concatenate([jnp.repeat(moe_b_group[l], EXPERTS_PER_GROUP), moe_b_router[l]])[None, :]
        wg = jnp.transpose(moe_w_gate[l], (1, 0, 2)).reshape(d, N_EXPERTS * D_EXPERT).astype(BF16)
        wu = jnp.transpose(moe_w_up[l], (1, 0, 2)).reshape(d, N_EXPERTS * D_EXPERT).astype(BF16)
        wd = moe_w_down[l].reshape(N_EXPERTS * D_EXPERT, d).astype(BF16)
        use_ctx = (l % 2 == 0) or ctx_needed

        if l % 2 == 0:
            e = l // 2
            w_in = w_in_even[e].astype(BF16)
            qg = jnp.tile(qn_g[e], NA_HEADS)[None, :]
            kg = jnp.tile(kn_g[e], NA_HEADS)[None, :]
            ql, kl, vl, gbl, pl_ = _inproj_even(x, n1g, sc1, sh1, w_in, qg, kg, seg)
            qc, kc, vc, gbc, pc = _inproj_even(ctx, n1g, csc1, csh1, w_in, qg, kg, seg)
            bias = _bias_table(na_rpb[e])
            y1 = _natten(ql, kl, vl, kc, vc, bias)
            y2 = _short_gated_conv(gbl, pl_, sc_conv_w[e])
            lat_cm = False
            if ctx_needed:
                y1c = _ctx_attention(qc, kc, vc)
                y2c = _short_gated_conv(gbc, pc, sc_conv_w[e])
        else:
            o = l // 2
            wht = w_in_odd[o][:, :3 * HY_WIDTH].T.astype(BF16)
            wag = w_in_odd[o][:, 3 * HY_WIDTH:].astype(BF16)
            hyt, ag = _inproj_odd(x, n1g, sc1, sh1, wht, wag)
            taps = _hyena_taps(seq, hy_w1[o], hy_b1[o], hy_w2[o], hy_b2[o], hy_w3[o], hy_freq[o])
            h_spec = _filter_fft(taps, consts)
            y1 = _hyena_latent(hyt, hy_short_w[o], h_spec, hy_bias[o], consts)
            cf_args = (cf_conv_w[o], cf_conv_b[o][None, :], cf_ln_g[o][None, :], cf_ln_b[o][None, :])
            y2 = _conformer(ag, *cf_args)
            lat_cm = True
            if ctx_needed:
                hytc, agc = _inproj_odd(ctx, n1g, csc1, csh1, wht, wag)
                taps_c = _hyena_taps(lc, hy_w1[o], hy_b1[o], hy_w2[o], hy_b2[o], hy_w3[o], hy_freq[o])
                y1c = _hyena_context(hytc, hy_short_w[o], taps_c, hy_bias[o])
                y2c = _conformer(agc, *cf_args)

        x1, h2, cmb = _outproj(y1, y2, w_out_bf, x, g1, n2g, sc2, sh2, wr, br, lat_cm)
        x = _moe(h2, cmb, x1, g2, wg, wu, wd)
        if ctx_needed:
            c1, hc2, cmbc = _outproj(y1c, y2c, w_out_bf, ctx, cg1, n2g, csc2, csh2, wr, br, lat_cm)
            ctx = _moe(hc2, cmbc, c1, cg2, wg, wu, wd)
    return x
```

```python
import functools
import math

import numpy as np
import jax
import jax.numpy as jnp
from jax import lax
from jax.experimental import pallas as pl
from jax.experimental.pallas import tpu as pltpu

F32 = jnp.float32
BF16 = jnp.bfloat16

D_MODEL = 1024
GRID_W = 64
NA_HEADS = 8
HEAD_DIM = 64
NA_WIDTH = 512
NA_WIN_H = 8
NA_WIN_W = 16
SC_WIDTH = 512
HY_WIDTH = 512
HY_BANDS = 16
HY_EMB = 1 + 2 * HY_BANDS
HY_FFN = 64
HY_MAX_DECAY = math.log(1e-2) / 0.3
HY_MIN_DECAY = math.log(1e-2) / 1.5
CF_WIDTH = 512
CF_TAPS = 31
N_GROUPS = 4
EXPERTS_PER_GROUP = 4
N_EXPERTS = 16
D_EXPERT = 256
RMS_EPS = 1e-6
LN_EPS = 1e-5
NEG_INF = -1e30

VMEM_LIMIT_BYTES = 56 * 1024 * 1024
LANES = 128

FFT_NA = 64
FFT_NB = 128
FFT_N = FFT_NA * FFT_NB


def _params(*sem):
    return pltpu.CompilerParams(dimension_semantics=tuple(sem), vmem_limit_bytes=VMEM_LIMIT_BYTES)


def _dot(a, b):
    return jnp.dot(a, b, preferred_element_type=F32)


def _dot_nt(a, b):
    return lax.dot_general(a, b, (((1,), (1,)), ((), ())), preferred_element_type=F32)


def _dot_tn(a, b):
    return lax.dot_general(a, b, (((0,), (0,)), ((), ())), preferred_element_type=F32)


def _dot_f32(a, b):
    return jnp.dot(a, b, preferred_element_type=F32, precision=lax.Precision.HIGHEST)


def _silu(x):
    return x * jax.nn.sigmoid(x)


def _ada_kernel(ct_ref, w_ref, b_ref, o_ref, *, n_cond):
    ct = ct_ref[...]
    s = _silu(ct)
    w = w_ref[0]
    rows = [jnp.sum(w * s[:, r:r + 1], axis=0, keepdims=True) for r in range(n_cond)]
    rows.append(jnp.zeros((8 - n_cond, w.shape[1]), F32))
    o_ref[0] = jnp.concatenate(rows, axis=0) + b_ref[0]


def _ada_modulation(cond, ada_w, ada_b):
    n_cond, d = cond.shape
    depth, _, n6 = ada_w.shape
    tn = 768
    ct = jnp.zeros((d, 8), F32).at[:, :n_cond].set(cond.T)
    return pl.pallas_call(
        functools.partial(_ada_kernel, n_cond=n_cond),
        out_shape=jax.ShapeDtypeStruct((depth, 8, n6), F32),
        grid=(depth, n6 // tn),
        in_specs=[pl.BlockSpec((d, 8), lambda l, j: (0, 0)),
                  pl.BlockSpec((1, d, tn), lambda l, j: (l, 0, j)),
                  pl.BlockSpec((1, 1, tn), lambda l, j: (l, 0, j))],
        out_specs=pl.BlockSpec((1, 8, tn), lambda l, j: (l, 0, j)),
        compiler_params=_params("parallel", "parallel"),
        name="ada_modulation",
    )(ct, ada_w, ada_b.reshape(depth, 1, n6))


def _norm_modulate(x, g, sc, sh):
    ms = jnp.mean(x * x, axis=-1, keepdims=True)
    return x * lax.rsqrt(ms + RMS_EPS) * g * (1.0 + sc) + sh


def _head_rmsnorm(t, seg, gain):
    sq = t * t
    hi = sq.astype(BF16)
    lo = (sq - hi.astype(F32)).astype(BF16)
    ss = _dot(hi, seg) + _dot(lo, seg)
    return t * lax.rsqrt(ss * (1.0 / HEAD_DIM) + RMS_EPS) * gain


def _inproj_even_kernel(x_ref, g_ref, sc_ref, sh_ref, w_ref, qg_ref, kg_ref, seg_ref,
                        q_ref, k_ref, v_ref, gb_ref, p_ref):
    u = _norm_modulate(x_ref[0], g_ref[...], sc_ref[0], sh_ref[0]).astype(BF16)
    seg = seg_ref[...]
    w = NA_WIDTH
    q = _dot(u, w_ref[:, 0 * w:1 * w])
    q_ref[0] = (_head_rmsnorm(q, seg, qg_ref[...]) * (HEAD_DIM ** -0.5)).astype(BF16)
    k = _dot(u, w_ref[:, 1 * w:2 * w])
    k_ref[0] = _head_rmsnorm(k, seg, kg_ref[...]).astype(BF16)
    v_ref[0] = _dot(u, w_ref[:, 2 * w:3 * w]).astype(BF16)
    gb_ref[0] = _dot(u, w_ref[:, 3 * w:4 * w]).astype(BF16)
    gc = _dot(u, w_ref[:, 4 * w:5 * w])
    hv = _dot(u, w_ref[:, 5 * w:6 * w])
    p_ref[0] = (gc * hv).astype(BF16)


def _inproj_even(x, g, sc, sh, w_bf, qg, kg, seg):
    b, n, d = x.shape
    tm = min(512, n)
    tok = lambda i, j: (i, j, 0)
    mod = lambda i, j: (i, 0, 0)
    const = lambda i, j: (0, 0)
    out = jax.ShapeDtypeStruct((b, n, NA_WIDTH), BF16)
    return pl.pallas_call(
        _inproj_even_kernel,
        out_shape=(out,) * 5,
        grid=(b, n // tm),
        in_specs=[pl.BlockSpec((1, tm, d), tok),
                  pl.BlockSpec((1, d), const),
                  pl.BlockSpec((1, 1, d), mod),
                  pl.BlockSpec((1, 1, d), mod),
                  pl.BlockSpec(w_bf.shape, const),
                  pl.BlockSpec((1, NA_WIDTH), const),
                  pl.BlockSpec((1, NA_WIDTH), const),
                  pl.BlockSpec((NA_WIDTH, NA_WIDTH), const)],
        out_specs=(pl.BlockSpec((1, tm, NA_WIDTH), tok),) * 5,
        compiler_params=_params("parallel", "parallel"),
        name="inproj_even",
    )(x, g, sc, sh, w_bf, qg, kg, seg)


def _bias_kernel(rpb_ref, o_ref):
    h = pl.program_id(0)
    qi = lax.broadcasted_iota(jnp.int32, (GRID_W, GRID_W), 0)
    ki = lax.broadcasted_iota(jnp.int32, (GRID_W, GRID_W), 1)
    start = jnp.clip(qi - NA_WIN_W // 2, 0, GRID_W - NA_WIN_W)
    valid = jnp.logical_and(ki >= start, ki < start + NA_WIN_W)
    cidx = jnp.clip(ki - qi, -(NA_WIN_W - 1), NA_WIN_W - 1) + (NA_WIN_W - 1)
    n_dr = 2 * NA_WIN_H - 1
    n_dc = 2 * NA_WIN_W - 1

    def body(j, accs):
        m = cidx == j
        return tuple(jnp.where(m, rpb_ref[(h * n_dr + d) * n_dc + j], a) for d, a in enumerate(accs))

    accs = lax.fori_loop(0, n_dc, body, tuple(jnp.zeros((GRID_W, GRID_W), F32) for _ in range(n_dr)))
    tiles = [jnp.where(valid, a, NEG_INF) for a in accs]
    for d0 in range(NA_WIN_H):
        o_ref[0, d0] = jnp.concatenate(tiles[d0:d0 + NA_WIN_H], axis=1)


def _bias_table(rpb):
    return pl.pallas_call(
        _bias_kernel,
        out_shape=jax.ShapeDtypeStruct((NA_HEADS, NA_WIN_H, GRID_W, NA_WIN_H * GRID_W), F32),
        grid=(NA_HEADS,),
        in_specs=[pl.BlockSpec(memory_space=pltpu.SMEM)],
        out_specs=pl.BlockSpec((1, NA_WIN_H, GRID_W, NA_WIN_H * GRID_W), lambda h: (h, 0, 0, 0)),
        compiler_params=_params("arbitrary"),
        name="rpb_bias_table",
    )(rpb.reshape(-1))


def _pair_attention(q2, kw, vw, bias, kc, vc, first_half):
    s_c = _dot_nt(q2, kc)
    m = jnp.max(s_c, axis=-1, keepdims=True)
    if kw is not None:
        s_w = _dot_nt(q2, kw) + bias
        m = jnp.maximum(m, jnp.max(s_w, axis=-1, keepdims=True))
        p_w = jnp.exp(s_w - m)
    p_c = jnp.exp(s_c - m)
    den = jnp.sum(p_c, axis=-1, keepdims=True)
    o = _dot(p_c.astype(BF16), vc)
    if kw is not None:
        den = den + jnp.sum(p_w, axis=-1, keepdims=True)
        o = o + _dot(p_w.astype(BF16), vw)
    o = o / den
    half = o.shape[0] // 2
    return jnp.where(first_half, o[:half], o[half:])


def _stack_heads(qp, first_half):
    zero = jnp.zeros_like(qp)
    return jnp.concatenate([jnp.where(first_half, qp, zero), jnp.where(first_half, zero, qp)], axis=0)


def _natten_kernel(q_ref, k_ref, v_ref, kc_ref, vc_ref, bias_ref, o_ref, *, rows_per_step, n_rows):
    blk = pl.program_id(1)
    lane = lax.broadcasted_iota(jnp.int32, (GRID_W, LANES), 1)
    first_half = lane < HEAD_DIM
    band = NA_WIN_H * GRID_W

    def row_body(j, carry):
        r = blk * rows_per_step + j
        start = jnp.clip(r - NA_WIN_H // 2, 0, n_rows - NA_WIN_H)
        d0 = start - r + (NA_WIN_H - 1)
        koff = pl.multiple_of(start * GRID_W, GRID_W)
        qoff = pl.multiple_of(j * GRID_W, GRID_W)
        outs = []
        for hp in range(NA_HEADS // 2):
            cs = slice(hp * LANES, (hp + 1) * LANES)
            q2 = _stack_heads(q_ref[0, pl.ds(qoff, GRID_W), cs], first_half)
            kw = k_ref[0, pl.ds(koff, band), cs]
            vw = v_ref[0, pl.ds(koff, band), cs]
            bias = jnp.concatenate([bias_ref[2 * hp, d0], bias_ref[2 * hp + 1, d0]], axis=0)
            outs.append(_pair_attention(q2, kw, vw, bias, kc_ref[0, :, cs], vc_ref[0, :, cs], first_half))
        o_ref[0, pl.ds(qoff, GRID_W), :] = jnp.concatenate(outs, axis=1).astype(o_ref.dtype)
        return carry

    lax.fori_loop(0, rows_per_step, row_body, 0, unroll=2)


def _natten(q, k, v, kc, vc, bias):
    b, n, w = q.shape
    n_rows = n // GRID_W
    rows_per_step = 8
    tq = rows_per_step * GRID_W
    lc = kc.shape[1]
    return pl.pallas_call(
        functools.partial(_natten_kernel, rows_per_step=rows_per_step, n_rows=n_rows),
        out_shape=jax.ShapeDtypeStruct((b, n, w), BF16),
        grid=(b, n_rows // rows_per_step),
        in_specs=[pl.BlockSpec((1, tq, w), lambda i, j: (i, j, 0)),
                  pl.BlockSpec((1, n, w), lambda i, j: (i, 0, 0)),
                  pl.BlockSpec((1, n, w), lambda i, j: (i, 0, 0)),
                  pl.BlockSpec((1, lc, w), lambda i, j: (i, 0, 0)),
                  pl.BlockSpec((1, lc, w), lambda i, j: (i, 0, 0)),
                  pl.BlockSpec(bias.shape, lambda i, j: (0, 0, 0, 0))],
        out_specs=pl.BlockSpec((1, tq, w), lambda i, j: (i, j, 0)),
        compiler_params=_params("parallel", "arbitrary"),
        name="neighbourhood_attention",
    )(q, k, v, kc, vc, bias)


def _ctx_attn_kernel(q_ref, k_ref, v_ref, o_ref):
    lc = q_ref.shape[1]
    lane = lax.broadcasted_iota(jnp.int32, (lc, LANES), 1)
    first_half = lane < HEAD_DIM
    outs = []
    for hp in range(NA_HEADS // 2):
        cs = slice(hp * LANES, (hp + 1) * LANES)
        q2 = _stack_heads(q_ref[0, :, cs], first_half)
        outs.append(_pair_attention(q2, None, None, None, k_ref[0, :, cs], v_ref[0, :, cs], first_half))
    o_ref[0] = jnp.concatenate(outs, axis=1).astype(o_ref.dtype)


def _ctx_attention(q, k, v):
    b, lc, w = q.shape
    spec = pl.BlockSpec((1, lc, w), lambda i: (i, 0, 0))
    return pl.pallas_call(
        _ctx_attn_kernel,
        out_shape=jax.ShapeDtypeStruct((b, lc, w), BF16),
        grid=(b,),
        in_specs=[spec, spec, spec],
        out_specs=spec,
        compiler_params=_params("parallel"),
        name="context_attention",
    )(q, k, v)


def _sgconv_kernel(gb_ref, p_ref, w_ref, o_ref):
    p = p_ref[0].astype(F32)
    n = p.shape[0]
    row = lax.broadcasted_iota(jnp.int32, p.shape, 0)
    prev = jnp.where(row == 0, 0.0, pltpu.roll(p, 1, axis=0))
    nxt = jnp.where(row == n - 1, 0.0, pltpu.roll(p, n - 1, axis=0))
    y = w_ref[0:1, :] * prev + w_ref[1:2, :] * p + w_ref[2:3, :] * nxt
    o_ref[0] = (gb_ref[0].astype(F32) * y).astype(o_ref.dtype)


def _short_gated_conv(gb, p, w):
    b, n, c = p.shape
    spec = pl.BlockSpec((1, n, LANES), lambda i, j: (i, 0, j))
    return pl.pallas_call(
        _sgconv_kernel,
        out_shape=jax.ShapeDtypeStruct((b, n, c), BF16),
        grid=(b, c // LANES),
        in_specs=[spec, spec, pl.BlockSpec((3, LANES), lambda i, j: (0, j))],
        out_specs=spec,
        compiler_params=_params("parallel", "parallel"),
        name="short_gated_conv",
    )(gb, p, w)


def _inproj_odd_kernel(x_ref, g_ref, sc_ref, sh_ref, wt_ref, hy_ref, ag_ref):
    u = _norm_modulate(x_ref[0], g_ref[...], sc_ref[0], sh_ref[0]).astype(BF16)
    hw = hy_ref.shape[1]
    hy_ref[0] = _dot_nt(wt_ref[0:hw, :], u).astype(BF16)
    a = _dot_nt(wt_ref[hw:hw + CF_WIDTH, :], u)
    g = _dot_nt(wt_ref[hw + CF_WIDTH:hw + 2 * CF_WIDTH, :], u)
    ag_ref[0] = (a * jax.nn.sigmoid(g)).astype(BF16)


def _inproj_odd(x, g, sc, sh, wt_bf):
    b, n, d = x.shape
    tm = min(512, n)
    hw = 3 * HY_WIDTH
    tok = lambda i, j: (i, j, 0)
    mod = lambda i, j: (i, 0, 0)
    const = lambda i, j: (0, 0)
    cm = lambda i, j: (i, 0, j)
    return pl.pallas_call(
        _inproj_odd_kernel,
        out_shape=(jax.ShapeDtypeStruct((b, hw, n), BF16), jax.ShapeDtypeStruct((b, CF_WIDTH, n), BF16)),
        grid=(b, n // tm),
        in_specs=[pl.BlockSpec((1, tm, d), tok),
                  pl.BlockSpec((1, d), const),
                  pl.BlockSpec((1, 1, d), mod),
                  pl.BlockSpec((1, 1, d), mod),
                  pl.BlockSpec(wt_bf.shape, const)],
        out_specs=(pl.BlockSpec((1, hw, tm), cm), pl.BlockSpec((1, CF_WIDTH, tm), cm)),
        compiler_params=_params("parallel", "parallel"),
        name="inproj_odd",
    )(x, g, sc, sh, wt_bf)


CF_HALO = LANES
CF_ROWS = 8


def _conformer_kernel(ag_ref, wt_ref, cb_ref, lg_ref, lb_ref, o_ref, pad_ref, y_ref, *, seq):
    width = seq + 2 * CF_HALO
    halo = jnp.zeros((CF_WIDTH, CF_HALO), F32)
    pad_ref[:, 0:CF_HALO] = halo
    pad_ref[:, CF_HALO + seq:width] = halo
    pad_ref[:, CF_HALO:CF_HALO + seq] = ag_ref[0].astype(F32)
    half = CF_TAPS // 2

    def conv_rows(i, carry):
        r0 = pl.multiple_of(i * CF_ROWS, CF_ROWS)
        win = pad_ref[pl.ds(r0, CF_ROWS), :]
        w = wt_ref[pl.ds(r0, CF_ROWS), :]
        acc = jnp.zeros((CF_ROWS, seq), F32)
        for j in range(CF_TAPS):
            off = CF_HALO - half + j
            acc = acc + w[:, j:j + 1] * pltpu.roll(win, width - off, axis=1)[:, :seq]
        y_ref[pl.ds(r0, CF_ROWS), :] = acc + cb_ref[pl.ds(r0, CF_ROWS), :]
        return carry

    lax.fori_loop(0, CF_WIDTH // CF_ROWS, conv_rows, 0)
    lt = min(512, seq)
    for k in range(seq // lt):
        y = y_ref[:, k * lt:(k + 1) * lt]
        mu = jnp.mean(y, axis=0, keepdims=True)
        yc = y - mu
        var = jnp.mean(yc * yc, axis=0, keepdims=True)
        z = yc * lax.rsqrt(var + LN_EPS) * lg_ref[...] + lb_ref[...]
        o_ref[0, :, k * lt:(k + 1) * lt] = _silu(z).astype(o_ref.dtype)


def _conformer(ag, w, cb, lg, lb):
    b, c, n = ag.shape
    spec = pl.BlockSpec((1, c, n), lambda i: (i, 0, 0))
    col = pl.BlockSpec((c, 1), lambda i: (0, 0))
    return pl.pallas_call(
        functools.partial(_conformer_kernel, seq=n),
        out_shape=jax.ShapeDtypeStruct((b, c, n), BF16),
        grid=(b,),
        in_specs=[spec, pl.BlockSpec((c, CF_TAPS), lambda i: (0, 0)), col, col, col],
        out_specs=spec,
        scratch_shapes=[pltpu.VMEM((c, n + 2 * CF_HALO), F32), pltpu.VMEM((c, n), F32)],
        compiler_params=_params("parallel"),
        name="conformer_conv",
    )(ag, w.T, cb[:, None], lg[:, None], lb[:, None])


def _hyena_features(length):
    t = np.linspace(0.0, 1.0, length, dtype=np.float32)
    w = (2.0 * math.pi * np.arange(length, dtype=np.float32) / length).astype(np.float32)
    bands = np.linspace(1e-4, HY_BANDS - 1, HY_BANDS, dtype=np.float32)
    ang = (bands[:, None] * w[None, :]).astype(np.float32)
    zt = np.concatenate([t[None, :], np.cos(ang), -np.sin(ang)], axis=0).astype(np.float32)
    deltas = np.abs(np.linspace(HY_MIN_DECAY, HY_MAX_DECAY, HY_WIDTH, dtype=np.float32))
    rev = (length - np.arange(length)) % length
    zt2 = np.concatenate([zt, zt[:, rev]], axis=1)
    t2 = np.concatenate([t, t[rev]])[None, :]
    return zt2, t2, deltas[:, None]


def _taps_kernel(zt_ref, t_ref, dl_ref, w1t_ref, b1_ref, f0_ref, w2t_ref, b2_ref, f1_ref, w3t_ref,
                 o_ref, hid_ref):
    first = jnp.logical_and(pl.program_id(0) == 0, pl.program_id(1) == 0)
    length = t_ref.shape[1] // 2

    @pl.when(first)
    def _():
        h1 = jnp.sin(f0_ref[...] * (_dot_f32(w1t_ref[...], zt_ref[...]) + b1_ref[...]))
        hid_ref[...] = jnp.sin(f1_ref[...] * (_dot_f32(w2t_ref[...], h1) + b2_ref[...]))

    decay = jnp.exp(-(dl_ref[...] * t_ref[...]))
    fwd = _dot_f32(w3t_ref[0, 0], hid_ref[:, :length])
    bwd = _dot_f32(w3t_ref[0, 1], hid_ref[:, length:])
    taps = jnp.concatenate([fwd, bwd], axis=1) * decay
    nrm = jnp.sum(jnp.abs(taps), axis=-1, keepdims=True)
    lane = lax.broadcasted_iota(jnp.int32, taps.shape, 1)
    o_ref[0] = jnp.where(lane == length, 0.0, taps / nrm)


def _hyena_taps(length, w1, b1, w2, b2, w3, freq):
    zt2, t2, deltas = _hyena_features(length)
    cb = 128
    w3t = w3.T.reshape(2, 2, HY_WIDTH, HY_FFN)
    col = lambda v: v.reshape(HY_FFN, 1)
    const = lambda o, j: (0, 0)
    return pl.pallas_call(
        _taps_kernel,
        out_shape=jax.ShapeDtypeStruct((2, HY_WIDTH, 2 * length), F32),
        grid=(2, HY_WIDTH // cb),
        in_specs=[pl.BlockSpec((HY_EMB, 2 * length), const),
                  pl.BlockSpec((1, 2 * length), const),
                  pl.BlockSpec((cb, 1), lambda o, j: (j, 0)),
                  pl.BlockSpec((HY_FFN, HY_EMB), const),
                  pl.BlockSpec((HY_FFN, 1), const),
                  pl.BlockSpec((HY_FFN, 1), const),
                  pl.BlockSpec((HY_FFN, HY_FFN), const),
                  pl.BlockSpec((HY_FFN, 1), const),
                  pl.BlockSpec((HY_FFN, 1), const),
                  pl.BlockSpec((1, 2, cb, HY_FFN), lambda o, j: (o, 0, j, 0))],
        out_specs=pl.BlockSpec((1, cb, 2 * length), lambda o, j: (o, j, 0)),
        scratch_shapes=[pltpu.VMEM((HY_FFN, 2 * length), F32)],
        compiler_params=_params("arbitrary", "arbitrary"),
        name="hyena_filter_taps",
    )(jnp.asarray(zt2), jnp.asarray(t2), jnp.asarray(deltas), w1.T, col(b1), col(freq[0]),
      w2.T, col(b2), col(freq[1]), w3t)


def _dft_constants():
    na, nb, n = FFT_NA, FFT_NB, FFT_N
    half = na // 2
    ka = np.arange(na)
    ang_a = 2.0 * np.pi * np.outer(ka, ka) / na
    ca, sa = np.cos(ang_a), np.sin(ang_a)
    fa = np.block([[ca[:half], -sa[:half]], [sa[:half], ca[:half]]])
    fai = np.block([[ca[:, :half], sa[:, :half]], [-sa[:, :half], ca[:, :half]]])
    kb = np.arange(nb)
    ang_b = 2.0 * np.pi * np.outer(kb, kb) / nb
    cbm, sbm = np.cos(ang_b), np.sin(ang_b)
    fb = np.block([[cbm, -sbm], [sbm, cbm]])
    fbi = np.block([[cbm, sbm], [-sbm, cbm]])
    ang_t = 2.0 * np.pi * np.outer(kb, ka) / n
    ct, st = np.cos(ang_t), np.sin(ang_t)
    tw_fc = np.concatenate([ct, ct], axis=1)
    tw_fs = np.concatenate([st, -st], axis=1)
    tw_ic, tw_is = ct.T.copy(), st.T.copy()
    bf = lambda a: jnp.asarray(a, dtype=F32).astype(BF16)
    f32 = lambda a: jnp.asarray(a, dtype=F32)
    fa_real = np.concatenate([ca, -sa], axis=1)
    return dict(fa=bf(fa), fa_real=bf(fa_real), fai=bf(fai), fb=bf(fb), fbi=bf(fbi),
                tw_fc=f32(tw_fc), tw_fs=f32(tw_fs), tw_ic=f32(tw_ic), tw_is=f32(tw_is))


def _fft_forward(zr, zi, fa, tw_fc, tw_fs, fb):
    c, _, nb = zr.shape
    tr = jnp.swapaxes(zr, 1, 2)
    lhs = tr if zi is None else jnp.concatenate([tr, jnp.swapaxes(zi, 1, 2)], axis=2)
    a = _dot(lhs.reshape(c * nb, lhs.shape[2]).astype(BF16), fa).reshape(c, nb, 2 * FFT_NA)
    a = a * tw_fc + pltpu.roll(a, FFT_NA, axis=2) * tw_fs
    t = jnp.swapaxes(a, 1, 2)
    lhs2 = jnp.concatenate([t[:, :FFT_NA, :], t[:, FFT_NA:, :]], axis=2)
    x = _dot(lhs2.reshape(c * FFT_NA, 2 * nb).astype(BF16), fb)
    return x.reshape(c, FFT_NA, 2 * nb)


def _fft_inverse(y, fbi, tw_ic, tw_is, fai):
    c = y.shape[0]
    nb = FFT_NB
    b = _dot(y.reshape(c * FFT_NA, 2 * nb).astype(BF16), fbi).reshape(c, FFT_NA, 2 * nb)
    br, bi = b[:, :, :nb], b[:, :, nb:]
    rr = br * tw_ic - bi * tw_is
    ii = bi * tw_ic + br * tw_is
    t = jnp.swapaxes(jnp.concatenate([rr, ii], axis=1), 1, 2)
    o = _dot(t.reshape(c * nb, 2 * FFT_NA).astype(BF16), fai).reshape(c, nb, FFT_NA)
    o = jnp.swapaxes(o, 1, 2)
    return o[:, :FFT_NA // 2, :], o[:, FFT_NA // 2:, :]


def _filter_fft_kernel(taps_ref, fa_ref, twc_ref, tws_ref, fb_ref, o_ref):
    h = _fft_forward(taps_ref[0], None, fa_ref[...], twc_ref[...], tws_ref[...], fb_ref[...])
    o_ref[0] = h * (1.0 / FFT_N)


def _filter_fft(taps, consts):
    _, c, n = taps.shape
    cb = 32
    taps4 = taps.reshape(2, c, FFT_NA, FFT_NB)
    cm = lambda o, j: (0, 0)
    return pl.pallas_call(
        _filter_fft_kernel,
        out_shape=jax.ShapeDtypeStruct((2, c, FFT_NA, 2 * FFT_NB), F32),
        grid=(2, c // cb),
        in_specs=[pl.BlockSpec((1, cb, FFT_NA, FFT_NB), lambda o, j: (o, j, 0, 0)),
                  pl.BlockSpec(consts["fa_real"].shape, cm),
                  pl.BlockSpec(consts["tw_fc"].shape, cm),
                  pl.BlockSpec(consts["tw_fs"].shape, cm),
                  pl.BlockSpec(consts["fb"].shape, cm)],
        out_specs=pl.BlockSpec((1, cb, FFT_NA, 2 * FFT_NB), lambda o, j: (o, j, 0, 0)),
        compiler_params=_params("parallel", "parallel"),
        name="hyena_filter_fft",
    )(taps4, consts["fa_real"], consts["tw_fc"], consts["tw_fs"], consts["fb"])


def _shift_tokens(a, direction):
    rows = a.shape[-2]
    lane = lax.broadcasted_iota(jnp.int32, a.shape, a.ndim - 1)
    row = lax.broadcasted_iota(jnp.int32, a.shape, a.ndim - 2)
    if direction == 1:
        l = pltpu.roll(a, 1, axis=a.ndim - 1)
        ls = pltpu.roll(l, 1, axis=a.ndim - 2)
        out = jnp.where(lane == 0, ls, l)
        edge = jnp.logical_and(lane == 0, row == 0)
    else:
        l = pltpu.roll(a, LANES - 1, axis=a.ndim - 1)
        ls = pltpu.roll(l, rows - 1, axis=a.ndim - 2)
        out = jnp.where(lane == LANES - 1, ls, l)
        edge = jnp.logical_and(lane == LANES - 1, row == rows - 1)
    return jnp.where(edge, 0.0, out)


def _short_conv3(a, w_ref):
    return w_ref[0] * _shift_tokens(a, 1) + w_ref[1] * a + w_ref[2] * _shift_tokens(a, -1)


def _hyena_kernel(v_ref, x1_ref, x2_ref, wv_ref, w1_ref, w2_ref, h_ref, hb_ref,
                  fa_ref, twfc_ref, twfs_ref, fb_ref, fbi_ref, twic_ref, twis_ref, fai_ref, o_ref):
    fwd_c = (fa_ref[...], twfc_ref[...], twfs_ref[...], fb_ref[...])
    inv_c = (fbi_ref[...], twic_ref[...], twis_ref[...], fai_ref[...])
    z = _short_conv3(v_ref[...].astype(F32), wv_ref)
    zr, zi = z[0], z[1]
    nb = FFT_NB
    for o, (g_ref, gw_ref) in enumerate(((x1_ref, w1_ref), (x2_ref, w2_ref))):
        x = _fft_forward(zr, zi, *fwd_c)
        h = h_ref[o]
        xr, xi, hr, hi = x[:, :, :nb], x[:, :, nb:], h[:, :, :nb], h[:, :, nb:]
        y = jnp.concatenate([xr * hr - xi * hi, xr * hi + xi * hr], axis=2)
        yr, yi = _fft_inverse(y, *inv_c)
        gate = _short_conv3(g_ref[...].astype(F32), gw_ref)
        bias = hb_ref[o]
        zr = gate[0] * (yr + zr * bias)
        zi = gate[1] * (yi + zi * bias)
    o_ref[0] = zr.astype(o_ref.dtype)
    o_ref[1] = zi.astype(o_ref.dtype)


def _hyena_latent(hyt, short_w, h_spec, hy_bias, consts):
    b, c3, length = hyt.shape
    c = c3 // 3
    rows = length // FFT_NB
    cb = 32
    nblk = c // cb
    hy4 = hyt.reshape(b, c3, rows, FFT_NB)
    w4 = jnp.broadcast_to(short_w.reshape(3, c3, 1, 1), (3, c3, 1, FFT_NB))
    hb4 = jnp.broadcast_to(hy_bias.reshape(2, c, 1, 1), (2, c, 1, FFT_NB))
    sig = lambda g: pl.BlockSpec((2, cb, rows, FFT_NB), lambda j, p: (p, g * nblk + j, 0, 0))
    wsp = lambda g: pl.BlockSpec((3, cb, 1, FFT_NB), lambda j, p: (0, g * nblk + j, 0, 0))
    cm = lambda j, p: (0, 0)
    names = ("fa", "tw_fc", "tw_fs", "fb", "fbi", "tw_ic", "tw_is", "fai")
    out = pl.pallas_call(
        _hyena_kernel,
        out_shape=jax.ShapeDtypeStruct((b, c, rows, FFT_NB), BF16),
        grid=(nblk, b // 2),
        in_specs=[sig(0), sig(1), sig(2), wsp(0), wsp(1), wsp(2),
                  pl.BlockSpec((2, cb, FFT_NA, 2 * FFT_NB), lambda j, p: (0, j, 0, 0)),
                  pl.BlockSpec((2, cb, 1, FFT_NB), lambda j, p: (0, j, 0, 0))]
                 + [pl.BlockSpec(consts[k].shape, cm) for k in names],
        out_specs=pl.BlockSpec((2, cb, rows, FFT_NB), lambda j, p: (p, j, 0, 0)),
        compiler_params=_params("parallel", "arbitrary"),
        name="hyena_long_conv",
    )(hy4, hy4, hy4, w4, w4, w4, h_spec, hb4, *[consts[k] for k in names])
    return out.reshape(b, c, length)


def _dense_dft_constants(length):
    n = 2 * length
    k = np.arange(n)
    ang = 2.0 * np.pi * np.outer(k, k) / n
    fwd = np.concatenate([np.cos(ang), -np.sin(ang)], axis=1)
    inv = np.concatenate([np.cos(ang[:length]).T, -np.sin(ang[:length]).T], axis=0) / n
    bf = lambda a: jnp.asarray(a, dtype=F32).astype(BF16)
    return bf(fwd), bf(inv)


def _shift_lanes(a, direction):
    n = a.shape[-1]
    lane = lax.broadcasted_iota(jnp.int32, a.shape, a.ndim - 1)
    if direction == 1:
        return jnp.where(lane == 0, 0.0, pltpu.roll(a, 1, axis=a.ndim - 1))
    return jnp.where(lane == n - 1, 0.0, pltpu.roll(a, n - 1, axis=a.ndim - 1))


def _hyena_ctx_kernel(v_ref, x1_ref, x2_ref, wv_ref, w1_ref, w2_ref, taps_ref, hb_ref, fwd_ref, inv_ref, o_ref):
    bsz, cb, length = v_ref.shape
    n = 2 * length
    fwd_m, inv_m = fwd_ref[0:length, :], inv_ref[...]

    def conv3(ref, w_ref):
        a = ref[...].astype(F32)
        return w_ref[0] * _shift_lanes(a, 1) + w_ref[1] * a + w_ref[2] * _shift_lanes(a, -1)

    z = conv3(v_ref, wv_ref)
    for o, (g_ref, gw_ref) in enumerate(((x1_ref, w1_ref), (x2_ref, w2_ref))):
        h = _dot(taps_ref[o].astype(BF16), fwd_ref[...])
        hr, hi = h[:, :n], h[:, n:]
        x = _dot(z.reshape(bsz * cb, length).astype(BF16), fwd_m).reshape(bsz, cb, 2 * n)
        xr, xi = x[:, :, :n], x[:, :, n:]
        y = jnp.concatenate([xr * hr - xi * hi, xr * hi + xi * hr], axis=2)
        yt = _dot(y.reshape(bsz * cb, 2 * n).astype(BF16), inv_m).reshape(bsz, cb, length)
        z = conv3(g_ref, gw_ref) * (yt + z * hb_ref[o])
    o_ref[...] = z.astype(o_ref.dtype)


def _hyena_context(hyt, short_w, taps, hy_bias):
    b, c3, length = hyt.shape
    c = c3 // 3
    cb = 128
    nblk = c // cb
    fwd_m, inv_m = _dense_dft_constants(length)
    w4 = jnp.broadcast_to(short_w.reshape(3, c3, 1), (3, c3, length))
    hb = jnp.broadcast_to(hy_bias.reshape(2, c, 1), (2, c, length))
    sig = lambda g: pl.BlockSpec((b, cb, length), lambda j: (0, g * nblk + j, 0))
    wsp = lambda g: pl.BlockSpec((3, cb, length), lambda j: (0, g * nblk + j, 0))
    return pl.pallas_call(
        _hyena_ctx_kernel,
        out_shape=jax.ShapeDtypeStruct((b, c, length), BF16),
        grid=(nblk,),
        in_specs=[sig(0), sig(1), sig(2), wsp(0), wsp(1), wsp(2),
                  pl.BlockSpec((2, cb, 2 * length), lambda j: (0, j, 0)),
                  pl.BlockSpec((2, cb, length), lambda j: (0, j, 0)),
                  pl.BlockSpec(fwd_m.shape, lambda j: (0, 0)),
                  pl.BlockSpec(inv_m.shape, lambda j: (0, 0))],
        out_specs=pl.BlockSpec((b, cb, length), lambda j: (0, j, 0)),
        compiler_params=_params("parallel"),
        name="hyena_context_conv",
    )(hyt, hyt, hyt, w4, w4, w4, taps, hb, fwd_m, inv_m)


def _route(gl, el):
    row = lax.broadcasted_iota(jnp.int32, gl.shape, 0).astype(F32)
    grp = jnp.floor(row * (1.0 / EXPERTS_PER_GROUP))
    big = float(N_EXPERTS)
    gmax = jnp.max(gl, axis=0, keepdims=True)
    gidx = jnp.min(jnp.where(gl == gmax, grp, big), axis=0, keepdims=True)
    gsum = jnp.sum(jnp.exp(gl - gmax), axis=0, keepdims=True) * (1.0 / EXPERTS_PER_GROUP)
    g_w = 1.0 / gsum
    em = jnp.where(grp == gidx, el, NEG_INF)
    t1 = jnp.max(em, axis=0, keepdims=True)
    i1 = jnp.min(jnp.where(em == t1, row, big), axis=0, keepdims=True)
    em2 = jnp.where(row == i1, 2.0 * NEG_INF, em)
    t2 = jnp.max(em2, axis=0, keepdims=True)
    i2 = jnp.min(jnp.where(em2 == t2, row, big), axis=0, keepdims=True)
    e2 = jnp.exp(t2 - t1)
    den = 1.0 + e2
    w1 = g_w / den
    w2 = g_w * e2 / den
    return jnp.where(row == i1, w1, 0.0) + jnp.where(row == i2, w2, 0.0)


def _outproj_kernel(y1_ref, y2_ref, w_ref, x_ref, g1_ref, n2_ref, sc2_ref, sh2_ref, wrh_ref, wrl_ref, br_ref,
                    xo_ref, h_ref, cmb_ref, *, channel_major):
    half = w_ref.shape[0] // 2
    dot_in = _dot_tn if channel_major else _dot
    acc = dot_in(y1_ref[0], w_ref[:half, :]) + dot_in(y2_ref[0], w_ref[half:, :])
    xn = x_ref[0] + g1_ref[0] * acc
    xo_ref[0] = xn
    h = _norm_modulate(xn, n2_ref[...], sc2_ref[0], sh2_ref[0])
    hi = h.astype(BF16)
    lo = (h - hi.astype(F32)).astype(BF16)
    h_ref[0] = hi
    logits = _dot(hi, wrh_ref[...]) + _dot(lo, wrh_ref[...]) + _dot(hi, wrl_ref[...]) + br_ref[...]
    lt = logits.T
    cmb = _route(lt[0:N_EXPERTS], lt[N_EXPERTS:2 * N_EXPERTS])
    pad = jnp.zeros((LANES - N_EXPERTS, cmb.shape[1]), F32)
    cmb_ref[0] = jnp.concatenate([cmb, pad], axis=0).T


def _outproj(y1, y2, w_bf, x, g1, n2g, sc2, sh2, wrh, wrl, br, channel_major):
    b, n, d = x.shape
    tm = min(512, n)
    half = d // 2
    tok = lambda i, j: (i, j, 0)
    mod = lambda i, j: (i, 0, 0)
    const = lambda i, j: (0, 0)
    y_spec = (pl.BlockSpec((1, half, tm), lambda i, j: (i, 0, j)) if channel_major
              else pl.BlockSpec((1, tm, half), tok))
    return pl.pallas_call(
        functools.partial(_outproj_kernel, channel_major=channel_major),
        out_shape=(jax.ShapeDtypeStruct((b, n, d), F32), jax.ShapeDtypeStruct((b, n, d), BF16),
                   jax.ShapeDtypeStruct((b, n, LANES), F32)),
        grid=(b, n // tm),
        in_specs=[y_spec, y_spec,
                  pl.BlockSpec((d, d), const),
                  pl.BlockSpec((1, tm, d), tok),
                  pl.BlockSpec((1, 1, d), mod),
                  pl.BlockSpec((1, d), const),
                  pl.BlockSpec((1, 1, d), mod),
                  pl.BlockSpec((1, 1, d), mod),
                  pl.BlockSpec(wrh.shape, const),
                  pl.BlockSpec(wrl.shape, const),
                  pl.BlockSpec(br.shape, const)],
        out_specs=(pl.BlockSpec((1, tm, d), tok), pl.BlockSpec((1, tm, d), tok),
                   pl.BlockSpec((1, tm, LANES), tok)),
        compiler_params=_params("parallel", "parallel"),
        name="outproj_router",
    )(y1, y2, w_bf, x, g1, n2g, sc2, sh2, wrh, wrl, br)


def _moe_kernel(h_ref, cmb_ref, x_ref, g2_ref, wg_ref, wu_ref, wd_ref, o_ref, acc_ref):
    j = pl.program_id(2)
    h = h_ref[0]
    cmb = cmb_ref[0]
    lane = lax.broadcasted_iota(jnp.int32, cmb.shape, 1)
    contrib = None
    for e in range(EXPERTS_PER_GROUP):
        a = _dot(h, wg_ref[e])
        u = _dot(h, wu_ref[e])
        w_e = jnp.sum(jnp.where(lane == j * EXPERTS_PER_GROUP + e, cmb, 0.0), axis=1, keepdims=True)
        part = _dot((_silu(a) * u * w_e).astype(BF16), wd_ref[e])
        contrib = part if contrib is None else contrib + part

    @pl.when(j == 0)
    def _():
        acc_ref[...] = contrib

    @pl.when(j > 0)
    def _():
        acc_ref[...] += contrib

    @pl.when(j == N_GROUPS - 1)
    def _():
        o_ref[0] = x_ref[0] + g2_ref[0] * acc_ref[...]


def _moe(h, cmb, x, g2, wg, wu, wd):
    b, n, d = x.shape
    tm = min(512, n)
    tok = lambda i, t, j: (i, t, 0)
    epg = EXPERTS_PER_GROUP
    return pl.pallas_call(
        _moe_kernel,
        out_shape=jax.ShapeDtypeStruct((b, n, d), F32),
        grid=(b, n // tm, N_GROUPS),
        in_specs=[pl.BlockSpec((1, tm, d), tok),
                  pl.BlockSpec((1, tm, LANES), tok),
                  pl.BlockSpec((1, tm, d), tok),
                  pl.BlockSpec((1, 1, d), lambda i, t, j: (i, 0, 0)),
                  pl.BlockSpec((epg, d, D_EXPERT), lambda i, t, j: (j, 0, 0)),
                  pl.BlockSpec((epg, d, D_EXPERT), lambda i, t, j: (j, 0, 0)),
                  pl.BlockSpec((epg, D_EXPERT, d), lambda i, t, j: (j, 0, 0))],
        out_specs=pl.BlockSpec((1, tm, d), tok),
        scratch_shapes=[pltpu.VMEM((tm, d), F32)],
        compiler_params=_params("parallel", "parallel", "arbitrary"),
        name="moe_experts",
    )(h, cmb, x, g2, wg, wu, wd)


def kernel(x, c, ctx, c_ctx, ada_w, ada_b, norm1_g, norm2_g, w_in_even, qn_g, kn_g, na_rpb, sc_conv_w, w_in_odd, hy_short_w, hy_w1, hy_b1, hy_w2, hy_b2, hy_w3, hy_freq, hy_bias, cf_conv_w, cf_conv_b, cf_ln_g, cf_ln_b, w_out, moe_w_group, moe_b_group, moe_w_router, moe_b_router, moe_w_gate, moe_w_up, moe_w_down):
    depth = ada_w.shape[0]
    bsz, seq, d = x.shape
    lc = ctx.shape[1]
    assert 2 * seq == FFT_N and d == D_MODEL and bsz % 2 == 0

    mods = _ada_modulation(jnp.concatenate([c, c_ctx[None, :]], axis=0), ada_w, ada_b)
    seg = jnp.asarray(np.kron(np.eye(NA_HEADS), np.ones((HEAD_DIM, HEAD_DIM))), dtype=BF16)
    consts = _dft_constants()

    for l in range(depth):
        ctx_needed = any(j % 2 == 0 for j in range(l + 1, depth))
        lat_mod = [m[:, None, :] for m in jnp.split(mods[l, :bsz], 6, axis=-1)]
        ctx_mod = [jnp.broadcast_to(m[None, :, :], (bsz, 1, d)) for m in jnp.split(mods[l, bsz:bsz + 1], 6, axis=-1)]
        sh1, sc1, g1, sh2, sc2, g2 = lat_mod
        csh1, csc1, cg1, csh2, csc2, cg2 = ctx_mod
        n1g = norm1_g[l][None, :]
        n2g = norm2_g[l][None, :]
        w_out_bf = w_out[l].astype(BF16)
        wr = jnp.concatenate([jnp.repeat(moe_w_group[l], EXPERTS_PER_GROUP, axis=1), moe_w_router[l],
                              jnp.zeros((d, LANES - 2 * N_EXPERTS), F32)], axis=1)
        br = jnp.concatenate([jnp.repeat(moe_b_group[l], EXPERTS_PER_GROUP), moe_b_router[l],
                              jnp.zeros((LANES - 2 * N_EXPERTS,), F32)])[None, :]
        wrh = wr.astype(BF16)
        wrl = (wr - wrh.astype(F32)).astype(BF16)
        wg = moe_w_gate[l].astype(BF16)
        wu = moe_w_up[l].astype(BF16)
        wd = moe_w_down[l].astype(BF16)

        if l % 2 == 0:
            e = l // 2
            w_in = w_in_even[e].astype(BF16)
            qg = jnp.tile(qn_g[e], NA_HEADS)[None, :]
            kg = jnp.tile(kn_g[e], NA_HEADS)[None, :]
            ql, kl, vl, gbl, pl_ = _inproj_even(x, n1g, sc1, sh1, w_in, qg, kg, seg)
            qc, kc, vc, gbc, pc = _inproj_even(ctx, n1g, csc1, csh1, w_in, qg, kg, seg)
            bias = _bias_table(na_rpb[e])
            y1 = _natten(ql, kl, vl, kc, vc, bias)
            y2 = _short_gated_conv(gbl, pl_, sc_conv_w[e])
            lat_cm = False
            if ctx_needed:
                y1c = _ctx_attention(qc, kc, vc)
                y2c = _short_gated_conv(gbc, pc, sc_conv_w[e])
        else:
            o = l // 2
            w_in_t = w_in_odd[o].T.astype(BF16)
            hyt, ag = _inproj_odd(x, n1g, sc1, sh1, w_in_t)
            taps = _hyena_taps(seq, hy_w1[o], hy_b1[o], hy_w2[o], hy_b2[o], hy_w3[o], hy_freq[o])
            h_spec = _filter_fft(taps, consts)
            y1 = _hyena_latent(hyt, hy_short_w[o], h_spec, hy_bias[o], consts)
            cf_args = (cf_conv_w[o], cf_conv_b[o], cf_ln_g[o], cf_ln_b[o])
            y2 = _conformer(ag, *cf_args)
            lat_cm = True
            if ctx_needed:
                hytc, agc = _inproj_odd(ctx, n1g, csc1, csh1, w_in_t)
                taps_c = _hyena_taps(lc, hy_w1[o], hy_b1[o], hy_w2[o], hy_b2[o], hy_w3[o], hy_freq[o])
                y1c = _hyena_context(hytc, hy_short_w[o], taps_c, hy_bias[o])
                y2c = _conformer(agc, *cf_args)

        x1, h2, cmb = _outproj(y1, y2, w_out_bf, x, g1, n2g, sc2, sh2, wrh, wrl, br, lat_cm)
        x = _moe(h2, cmb, x1, g2, wg, wu, wd)
        if ctx_needed:
            c1, hc2, cmbc = _outproj(y1c, y2c, w_out_bf, ctx, cg1, n2g, csc2, csh2, wrh, wrl, br, lat_cm)
            ctx = _moe(hc2, cmbc, c1, cg2, wg, wu, wd)
    return x
```

```python
import functools
import math

import numpy as np
import jax
import jax.numpy as jnp
from jax import lax
from jax.experimental import pallas as pl
from jax.experimental.pallas import tpu as pltpu

F32 = jnp.float32
BF16 = jnp.bfloat16

D_MODEL = 1024
GRID_W = 64
NA_HEADS = 8
HEAD_DIM = 64
NA_WIDTH = 512
NA_WIN_H = 8
NA_WIN_W = 16
SC_WIDTH = 512
HY_WIDTH = 512
HY_BANDS = 16
HY_EMB = 1 + 2 * HY_BANDS
HY_FFN = 64
HY_MAX_DECAY = math.log(1e-2) / 0.3
HY_MIN_DECAY = math.log(1e-2) / 1.5
CF_WIDTH = 512
CF_TAPS = 31
N_GROUPS = 4
EXPERTS_PER_GROUP = 4
N_EXPERTS = 16
D_EXPERT = 256
RMS_EPS = 1e-6
LN_EPS = 1e-5
NEG_INF = -1e30

VMEM_LIMIT_BYTES = 56 * 1024 * 1024
LANES = 128

FFT_NA = 64
FFT_NB = 128
FFT_N = FFT_NA * FFT_NB


def _params(*sem):
    return pltpu.CompilerParams(dimension_semantics=tuple(sem), vmem_limit_bytes=VMEM_LIMIT_BYTES)


def _dot(a, b):
    return jnp.dot(a, b, preferred_element_type=F32)


def _dot_nt(a, b):
    return lax.dot_general(a, b, (((1,), (1,)), ((), ())), preferred_element_type=F32)


def _dot_tn(a, b):
    return lax.dot_general(a, b, (((0,), (0,)), ((), ())), preferred_element_type=F32)


def _dot_f32(a, b):
    return jnp.dot(a, b, preferred_element_type=F32, precision=lax.Precision.HIGHEST)


def _silu(x):
    return x * jax.nn.sigmoid(x)


def _ada_kernel(ct_ref, w_ref, b_ref, o_ref, *, n_cond):
    ct = ct_ref[...]
    s = _silu(ct)
    w = w_ref[0]
    rows = [jnp.sum(w * s[:, r:r + 1], axis=0, keepdims=True) for r in range(n_cond)]
    rows.append(jnp.zeros((8 - n_cond, w.shape[1]), F32))
    o_ref[0] = jnp.concatenate(rows, axis=0) + b_ref[0]


def _ada_modulation(cond, ada_w, ada_b):
    n_cond, d = cond.shape
    depth, _, n6 = ada_w.shape
    tn = 768
    ct = jnp.zeros((d, 8), F32).at[:, :n_cond].set(cond.T)
    return pl.pallas_call(
        functools.partial(_ada_kernel, n_cond=n_cond),
        out_shape=jax.ShapeDtypeStruct((depth, 8, n6), F32),
        grid=(depth, n6 // tn),
        in_specs=[pl.BlockSpec((d, 8), lambda l, j: (0, 0)),
                  pl.BlockSpec((1, d, tn), lambda l, j: (l, 0, j)),
                  pl.BlockSpec((1, 1, tn), lambda l, j: (l, 0, j))],
        out_specs=pl.BlockSpec((1, 8, tn), lambda l, j: (l, 0, j)),
        compiler_params=_params("parallel", "parallel"),
        name="ada_modulation",
    )(ct, ada_w, ada_b.reshape(depth, 1, n6))


def _norm_modulate(x, g, sc, sh):
    ms = jnp.mean(x * x, axis=-1, keepdims=True)
    return x * lax.rsqrt(ms + RMS_EPS) * g * (1.0 + sc) + sh


def _head_rmsnorm(t, seg, gain):
    ss = _dot((t * t).astype(BF16), seg)
    return t * lax.rsqrt(ss * (1.0 / HEAD_DIM) + RMS_EPS) * gain


def _inproj_even_kernel(x_ref, g_ref, sc_ref, sh_ref, w_ref, qg_ref, kg_ref, seg_ref,
                        q_ref, k_ref, v_ref, gb_ref, p_ref):
    u = _norm_modulate(x_ref[0], g_ref[...], sc_ref[0], sh_ref[0]).astype(BF16)
    seg = seg_ref[...]
    w = NA_WIDTH
    q = _dot(u, w_ref[:, 0 * w:1 * w])
    q_ref[0] = (_head_rmsnorm(q, seg, qg_ref[...]) * (HEAD_DIM ** -0.5)).astype(BF16)
    k = _dot(u, w_ref[:, 1 * w:2 * w])
    k_ref[0] = _head_rmsnorm(k, seg, kg_ref[...]).astype(BF16)
    v_ref[0] = _dot(u, w_ref[:, 2 * w:3 * w]).astype(BF16)
    gb_ref[0] = _dot(u, w_ref[:, 3 * w:4 * w]).astype(BF16)
    gc = _dot(u, w_ref[:, 4 * w:5 * w])
    hv = _dot(u, w_ref[:, 5 * w:6 * w])
    p_ref[0] = (gc * hv).astype(BF16)


def _inproj_even(x, g, sc, sh, w_bf, qg, kg, seg):
    b, n, d = x.shape
    tm = min(512, n)
    tok = lambda i, j: (i, j, 0)
    mod = lambda i, j: (i, 0, 0)
    const = lambda i, j: (0, 0)
    out = jax.ShapeDtypeStruct((b, n, NA_WIDTH), BF16)
    return pl.pallas_call(
        _inproj_even_kernel,
        out_shape=(out,) * 5,
        grid=(b, n // tm),
        in_specs=[pl.BlockSpec((1, tm, d), tok),
                  pl.BlockSpec((1, d), const),
                  pl.BlockSpec((1, 1, d), mod),
                  pl.BlockSpec((1, 1, d), mod),
                  pl.BlockSpec(w_bf.shape, const),
                  pl.BlockSpec((1, NA_WIDTH), const),
                  pl.BlockSpec((1, NA_WIDTH), const),
                  pl.BlockSpec((NA_WIDTH, NA_WIDTH), const)],
        out_specs=(pl.BlockSpec((1, tm, NA_WIDTH), tok),) * 5,
        compiler_params=_params("parallel", "parallel"),
        name="inproj_even",
    )(x, g, sc, sh, w_bf, qg, kg, seg)


def _bias_kernel(rpb_ref, o_ref):
    h = pl.program_id(0)
    qi = lax.broadcasted_iota(jnp.int32, (GRID_W, GRID_W), 0)
    ki = lax.broadcasted_iota(jnp.int32, (GRID_W, GRID_W), 1)
    start = jnp.clip(qi - NA_WIN_W // 2, 0, GRID_W - NA_WIN_W)
    valid = jnp.logical_and(ki >= start, ki < start + NA_WIN_W)
    cidx = jnp.clip(ki - qi, -(NA_WIN_W - 1), NA_WIN_W - 1) + (NA_WIN_W - 1)
    n_dr = 2 * NA_WIN_H - 1
    n_dc = 2 * NA_WIN_W - 1

    def body(j, accs):
        m = cidx == j
        return tuple(jnp.where(m, rpb_ref[(h * n_dr + d) * n_dc + j], a) for d, a in enumerate(accs))

    accs = lax.fori_loop(0, n_dc, body, tuple(jnp.zeros((GRID_W, GRID_W), F32) for _ in range(n_dr)))
    tiles = [jnp.where(valid, a, NEG_INF) for a in accs]
    for d0 in range(NA_WIN_H):
        o_ref[0, d0] = jnp.concatenate(tiles[d0:d0 + NA_WIN_H], axis=1)


def _bias_table(rpb):
    return pl.pallas_call(
        _bias_kernel,
        out_shape=jax.ShapeDtypeStruct((NA_HEADS, NA_WIN_H, GRID_W, NA_WIN_H * GRID_W), F32),
        grid=(NA_HEADS,),
        in_specs=[pl.BlockSpec(memory_space=pltpu.SMEM)],
        out_specs=pl.BlockSpec((1, NA_WIN_H, GRID_W, NA_WIN_H * GRID_W), lambda h: (h, 0, 0, 0)),
        compiler_params=_params("arbitrary"),
        name="rpb_bias_table",
    )(rpb.reshape(-1))


def _pair_attention(q2, kw, vw, bias, kc, vc, first_half):
    s_c = _dot_nt(q2, kc)
    m = jnp.max(s_c, axis=-1, keepdims=True)
    if kw is not None:
        s_w = _dot_nt(q2, kw) + bias
        m = jnp.maximum(m, jnp.max(s_w, axis=-1, keepdims=True))
        p_w = jnp.exp(s_w - m)
    p_c = jnp.exp(s_c - m)
    den = jnp.sum(p_c, axis=-1, keepdims=True)
    o = _dot(p_c.astype(BF16), vc)
    if kw is not None:
        den = den + jnp.sum(p_w, axis=-1, keepdims=True)
        o = o + _dot(p_w.astype(BF16), vw)
    o = o / den
    half = o.shape[0] // 2
    return jnp.where(first_half, o[:half], o[half:])


def _stack_heads(qp, first_half):
    zero = jnp.zeros_like(qp)
    return jnp.concatenate([jnp.where(first_half, qp, zero), jnp.where(first_half, zero, qp)], axis=0)


def _natten_kernel(q_ref, k_ref, v_ref, kc_ref, vc_ref, bias_ref, o_ref, *, rows_per_step, n_rows):
    blk = pl.program_id(1)
    lane = lax.broadcasted_iota(jnp.int32, (GRID_W, LANES), 1)
    first_half = lane < HEAD_DIM
    band = NA_WIN_H * GRID_W

    n_pairs = NA_HEADS // 2

    def row_body(j, carry):
        r = blk * rows_per_step + j
        start = jnp.clip(r - NA_WIN_H // 2, 0, n_rows - NA_WIN_H)
        d0 = start - r + (NA_WIN_H - 1)
        koff = pl.multiple_of(start * GRID_W, GRID_W)
        qoff = pl.multiple_of(j * GRID_W, GRID_W)
        cols = [slice(hp * LANES, (hp + 1) * LANES) for hp in range(n_pairs)]
        q2 = [_stack_heads(q_ref[0, pl.ds(qoff, GRID_W), cs], first_half) for cs in cols]
        s_w = [_dot_nt(q2[hp], k_ref[0, pl.ds(koff, band), cols[hp]])
               + jnp.concatenate([bias_ref[2 * hp, d0], bias_ref[2 * hp + 1, d0]], axis=0) for hp in range(n_pairs)]
        s_c = [_dot_nt(q2[hp], kc_ref[0, :, cols[hp]]) for hp in range(n_pairs)]
        m = [jnp.maximum(jnp.max(s_w[hp], axis=-1, keepdims=True), jnp.max(s_c[hp], axis=-1, keepdims=True))
             for hp in range(n_pairs)]
        p_w = [jnp.exp(s_w[hp] - m[hp]) for hp in range(n_pairs)]
        p_c = [jnp.exp(s_c[hp] - m[hp]) for hp in range(n_pairs)]
        den = [jnp.sum(p_w[hp], axis=-1, keepdims=True) + jnp.sum(p_c[hp], axis=-1, keepdims=True)
               for hp in range(n_pairs)]
        outs = []
        for hp in range(n_pairs):
            o = (_dot(p_w[hp].astype(BF16), v_ref[0, pl.ds(koff, band), cols[hp]])
                 + _dot(p_c[hp].astype(BF16), vc_ref[0, :, cols[hp]])) / den[hp]
            outs.append(jnp.where(first_half, o[:GRID_W], o[GRID_W:]))
        o_ref[0, pl.ds(qoff, GRID_W), :] = jnp.concatenate(outs, axis=1).astype(o_ref.dtype)
        return carry

    lax.fori_loop(0, rows_per_step, row_body, 0, unroll=2)


def _natten(q, k, v, kc, vc, bias):
    b, n, w = q.shape
    n_rows = n // GRID_W
    rows_per_step = 8
    tq = rows_per_step * GRID_W
    lc = kc.shape[1]
    return pl.pallas_call(
        functools.partial(_natten_kernel, rows_per_step=rows_per_step, n_rows=n_rows),
        out_shape=jax.ShapeDtypeStruct((b, n, w), BF16),
        grid=(b, n_rows // rows_per_step),
        in_specs=[pl.BlockSpec((1, tq, w), lambda i, j: (i, j, 0)),
                  pl.BlockSpec((1, n, w), lambda i, j: (i, 0, 0)),
                  pl.BlockSpec((1, n, w), lambda i, j: (i, 0, 0)),
                  pl.BlockSpec((1, lc, w), lambda i, j: (i, 0, 0)),
                  pl.BlockSpec((1, lc, w), lambda i, j: (i, 0, 0)),
                  pl.BlockSpec(bias.shape, lambda i, j: (0, 0, 0, 0))],
        out_specs=pl.BlockSpec((1, tq, w), lambda i, j: (i, j, 0)),
        compiler_params=_params("parallel", "arbitrary"),
        name="neighbourhood_attention",
    )(q, k, v, kc, vc, bias)


def _ctx_attn_kernel(q_ref, k_ref, v_ref, o_ref):
    lc = q_ref.shape[1]
    lane = lax.broadcasted_iota(jnp.int32, (lc, LANES), 1)
    first_half = lane < HEAD_DIM
    outs = []
    for hp in range(NA_HEADS // 2):
        cs = slice(hp * LANES, (hp + 1) * LANES)
        q2 = _stack_heads(q_ref[0, :, cs], first_half)
        outs.append(_pair_attention(q2, None, None, None, k_ref[0, :, cs], v_ref[0, :, cs], first_half))
    o_ref[0] = jnp.concatenate(outs, axis=1).astype(o_ref.dtype)


def _ctx_attention(q, k, v):
    b, lc, w = q.shape
    spec = pl.BlockSpec((1, lc, w), lambda i: (i, 0, 0))
    return pl.pallas_call(
        _ctx_attn_kernel,
        out_shape=jax.ShapeDtypeStruct((b, lc, w), BF16),
        grid=(b,),
        in_specs=[spec, spec, spec],
        out_specs=spec,
        compiler_params=_params("parallel"),
        name="context_attention",
    )(q, k, v)


def _sgconv_kernel(gb_ref, p_ref, w_ref, o_ref):
    p = p_ref[0].astype(F32)
    n = p.shape[0]
    row = lax.broadcasted_iota(jnp.int32, p.shape, 0)
    prev = jnp.where(row == 0, 0.0, pltpu.roll(p, 1, axis=0))
    nxt = jnp.where(row == n - 1, 0.0, pltpu.roll(p, n - 1, axis=0))
    y = w_ref[0:1, :] * prev + w_ref[1:2, :] * p + w_ref[2:3, :] * nxt
    o_ref[0] = (gb_ref[0].astype(F32) * y).astype(o_ref.dtype)


def _short_gated_conv(gb, p, w):
    b, n, c = p.shape
    spec = pl.BlockSpec((1, n, LANES), lambda i, j: (i, 0, j))
    return pl.pallas_call(
        _sgconv_kernel,
        out_shape=jax.ShapeDtypeStruct((b, n, c), BF16),
        grid=(b, c // LANES),
        in_specs=[spec, spec, pl.BlockSpec((3, LANES), lambda i, j: (0, j))],
        out_specs=spec,
        compiler_params=_params("parallel", "parallel"),
        name="short_gated_conv",
    )(gb, p, w)


def _inproj_odd_kernel(x_ref, g_ref, sc_ref, sh_ref, wht_ref, wag_ref, hy_ref, ag_ref):
    u = _norm_modulate(x_ref[0], g_ref[...], sc_ref[0], sh_ref[0]).astype(BF16)
    hy_ref[0] = _dot_nt(wht_ref[...], u).astype(BF16)
    a = _dot(u, wag_ref[:, :CF_WIDTH])
    g = _dot(u, wag_ref[:, CF_WIDTH:])
    ag_ref[0] = (a * jax.nn.sigmoid(g)).astype(BF16)


def _inproj_odd(x, g, sc, sh, wht_bf, wag_bf):
    b, n, d = x.shape
    tm = min(512, n)
    hw = wht_bf.shape[0]
    tok = lambda i, j: (i, j, 0)
    mod = lambda i, j: (i, 0, 0)
    const = lambda i, j: (0, 0)
    return pl.pallas_call(
        _inproj_odd_kernel,
        out_shape=(jax.ShapeDtypeStruct((b, hw, n), BF16), jax.ShapeDtypeStruct((b, n, CF_WIDTH), BF16)),
        grid=(b, n // tm),
        in_specs=[pl.BlockSpec((1, tm, d), tok),
                  pl.BlockSpec((1, d), const),
                  pl.BlockSpec((1, 1, d), mod),
                  pl.BlockSpec((1, 1, d), mod),
                  pl.BlockSpec(wht_bf.shape, const),
                  pl.BlockSpec(wag_bf.shape, const)],
        out_specs=(pl.BlockSpec((1, hw, tm), lambda i, j: (i, 0, j)), pl.BlockSpec((1, tm, CF_WIDTH), tok)),
        compiler_params=_params("parallel", "parallel"),
        name="inproj_odd",
    )(x, g, sc, sh, wht_bf, wag_bf)


SUBLANES = 8
CF_PAD = 2 * SUBLANES
CF_ROWS = 128


def _conformer_kernel(ag_ref, w_ref, cb_ref, lg_ref, lb_ref, o_ref, pad_ref, *, seq):
    zeros = jnp.zeros((CF_PAD, CF_WIDTH), F32)
    pad_ref[0:CF_PAD, :] = zeros
    pad_ref[CF_PAD + seq:2 * CF_PAD + seq, :] = zeros
    pad_ref[CF_PAD:CF_PAD + seq, :] = ag_ref[0].astype(F32)
    shift0 = CF_PAD - CF_TAPS // 2
    n_groups = (shift0 + CF_TAPS - 1) // SUBLANES + 1

    def conv_rows(i, carry):
        n0 = pl.multiple_of(i * CF_ROWS, CF_ROWS)
        wins = [pad_ref[pl.ds(n0 + SUBLANES * a, CF_ROWS + SUBLANES), :] for a in range(n_groups)]
        acc = None
        for b in range(SUBLANES):
            part = None
            for a in range(n_groups):
                j = SUBLANES * a + b - shift0
                if 0 <= j < CF_TAPS:
                    term = w_ref[j:j + 1, :] * wins[a]
                    part = term if part is None else part + term
            part = part[b:b + CF_ROWS, :]
            acc = part if acc is None else acc + part
        y = acc + cb_ref[...]
        mu = jnp.mean(y, axis=-1, keepdims=True)
        yc = y - mu
        var = jnp.mean(yc * yc, axis=-1, keepdims=True)
        z = yc * lax.rsqrt(var + LN_EPS) * lg_ref[...] + lb_ref[...]
        o_ref[0, pl.ds(n0, CF_ROWS), :] = _silu(z).astype(o_ref.dtype)
        return carry

    lax.fori_loop(0, seq // CF_ROWS, conv_rows, 0)


def _conformer(ag, w, cb, lg, lb):
    b, n, c = ag.shape
    spec = pl.BlockSpec((1, n, c), lambda i: (i, 0, 0))
    vec = pl.BlockSpec((1, c), lambda i: (0, 0))
    return pl.pallas_call(
        functools.partial(_conformer_kernel, seq=n),
        out_shape=jax.ShapeDtypeStruct((b, n, c), BF16),
        grid=(b,),
        in_specs=[spec, pl.BlockSpec((CF_TAPS, c), lambda i: (0, 0)), vec, vec, vec],
        out_specs=spec,
        scratch_shapes=[pltpu.VMEM((n + 2 * CF_PAD, c), F32)],
        compiler_params=_params("parallel"),
        name="conformer_conv",
    )(ag, w, cb[None, :], lg[None, :], lb[None, :])


def _hyena_features(length):
    t = np.linspace(0.0, 1.0, length, dtype=np.float32)
    w = (2.0 * math.pi * np.arange(length, dtype=np.float32) / length).astype(np.float32)
    bands = np.linspace(1e-4, HY_BANDS - 1, HY_BANDS, dtype=np.float32)
    ang = (bands[:, None] * w[None, :]).astype(np.float32)
    zt = np.concatenate([t[None, :], np.cos(ang), -np.sin(ang)], axis=0).astype(np.float32)
    deltas = np.abs(np.linspace(HY_MIN_DECAY, HY_MAX_DECAY, HY_WIDTH, dtype=np.float32))
    rev = (length - np.arange(length)) % length
    zt2 = np.concatenate([zt, zt[:, rev]], axis=1)
    t2 = np.concatenate([t, t[rev]])[None, :]
    return zt2, t2, deltas[:, None]


def _taps_kernel(zt_ref, t_ref, dl_ref, w1t_ref, b1_ref, f0_ref, w2t_ref, b2_ref, f1_ref, w3t_ref,
                 o_ref, hid_ref):
    first = jnp.logical_and(pl.program_id(0) == 0, pl.program_id(1) == 0)
    length = t_ref.shape[1] // 2

    @pl.when(first)
    def _():
        h1 = jnp.sin(f0_ref[...] * (_dot_f32(w1t_ref[...], zt_ref[...]) + b1_ref[...]))
        hid_ref[...] = jnp.sin(f1_ref[...] * (_dot_f32(w2t_ref[...], h1) + b2_ref[...]))

    decay = jnp.exp(-(dl_ref[...] * t_ref[...]))
    fwd = _dot_f32(w3t_ref[0, 0], hid_ref[:, :length])
    bwd = _dot_f32(w3t_ref[0, 1], hid_ref[:, length:])
    taps = jnp.concatenate([fwd, bwd], axis=1) * decay
    nrm = jnp.sum(jnp.abs(taps), axis=-1, keepdims=True)
    lane = lax.broadcasted_iota(jnp.int32, taps.shape, 1)
    o_ref[0] = jnp.where(lane == length, 0.0, taps / nrm)


def _hyena_taps(length, w1, b1, w2, b2, w3, freq):
    zt2, t2, deltas = _hyena_features(length)
    cb = 128
    w3t = w3.T.reshape(2, 2, HY_WIDTH, HY_FFN)
    col = lambda v: v.reshape(HY_FFN, 1)
    const = lambda o, j: (0, 0)
    return pl.pallas_call(
        _taps_kernel,
        out_shape=jax.ShapeDtypeStruct((2, HY_WIDTH, 2 * length), F32),
        grid=(2, HY_WIDTH // cb),
        in_specs=[pl.BlockSpec((HY_EMB, 2 * length), const),
                  pl.BlockSpec((1, 2 * length), const),
                  pl.BlockSpec((cb, 1), lambda o, j: (j, 0)),
                  pl.BlockSpec((HY_FFN, HY_EMB), const),
                  pl.BlockSpec((HY_FFN, 1), const),
                  pl.BlockSpec((HY_FFN, 1), const),
                  pl.BlockSpec((HY_FFN, HY_FFN), const),
                  pl.BlockSpec((HY_FFN, 1), const),
                  pl.BlockSpec((HY_FFN, 1), const),
                  pl.BlockSpec((1, 2, cb, HY_FFN), lambda o, j: (o, 0, j, 0))],
        out_specs=pl.BlockSpec((1, cb, 2 * length), lambda o, j: (o, j, 0)),
        scratch_shapes=[pltpu.VMEM((HY_FFN, 2 * length), F32)],
        compiler_params=_params("arbitrary", "arbitrary"),
        name="hyena_filter_taps",
    )(jnp.asarray(zt2), jnp.asarray(t2), jnp.asarray(deltas), w1.T, col(b1), col(freq[0]),
      w2.T, col(b2), col(freq[1]), w3t)


def _dft_constants():
    na, nb, n = FFT_NA, FFT_NB, FFT_N
    half = na // 2
    ka = np.arange(na)
    ang_a = 2.0 * np.pi * np.outer(ka, ka) / na
    ca, sa = np.cos(ang_a), np.sin(ang_a)
    fa = np.block([[ca[:half], -sa[:half]], [sa[:half], ca[:half]]])
    fai = np.block([[ca[:, :half], sa[:, :half]], [-sa[:, :half], ca[:, :half]]])
    kb = np.arange(nb)
    ang_b = 2.0 * np.pi * np.outer(kb, kb) / nb
    cbm, sbm = np.cos(ang_b), np.sin(ang_b)
    fb = np.block([[cbm, -sbm], [sbm, cbm]])
    fbi = np.block([[cbm, sbm], [-sbm, cbm]])
    ang_t = 2.0 * np.pi * np.outer(kb, ka) / n
    ct, st = np.cos(ang_t), np.sin(ang_t)
    tw_fc = np.concatenate([ct, ct], axis=1)
    tw_fs = np.concatenate([st, -st], axis=1)
    tw_ic, tw_is = ct.T.copy(), st.T.copy()
    bf = lambda a: jnp.asarray(a, dtype=F32).astype(BF16)
    f32 = lambda a: jnp.asarray(a, dtype=F32)
    fa_real = np.concatenate([ca, -sa], axis=1)
    return dict(fa=bf(fa), fa_real=bf(fa_real), fai=bf(fai), fb=bf(fb), fbi=bf(fbi),
                tw_fc=f32(tw_fc), tw_fs=f32(tw_fs), tw_ic=f32(tw_ic), tw_is=f32(tw_is))


def _fft_forward(zr, zi, fa, tw_fc, tw_fs, fb):
    c, _, nb = zr.shape
    tr = jnp.swapaxes(zr, 1, 2)
    lhs = tr if zi is None else jnp.concatenate([tr, jnp.swapaxes(zi, 1, 2)], axis=2)
    a = _dot(lhs.reshape(c * nb, lhs.shape[2]).astype(BF16), fa).reshape(c, nb, 2 * FFT_NA)
    a = a * tw_fc + pltpu.roll(a, FFT_NA, axis=2) * tw_fs
    t = jnp.swapaxes(a, 1, 2)
    lhs2 = jnp.concatenate([t[:, :FFT_NA, :], t[:, FFT_NA:, :]], axis=2)
    x = _dot(lhs2.reshape(c * FFT_NA, 2 * nb).astype(BF16), fb)
    return x.reshape(c, FFT_NA, 2 * nb)


def _fft_inverse(y, fbi, tw_ic, tw_is, fai):
    c = y.shape[0]
    nb = FFT_NB
    b = _dot(y.reshape(c * FFT_NA, 2 * nb).astype(BF16), fbi).reshape(c, FFT_NA, 2 * nb)
    br, bi = b[:, :, :nb], b[:, :, nb:]
    rr = br * tw_ic - bi * tw_is
    ii = bi * tw_ic + br * tw_is
    t = jnp.swapaxes(jnp.concatenate([rr, ii], axis=1), 1, 2)
    o = _dot(t.reshape(c * nb, 2 * FFT_NA).astype(BF16), fai).reshape(c, nb, FFT_NA)
    o = jnp.swapaxes(o, 1, 2)
    return o[:, :FFT_NA // 2, :], o[:, FFT_NA // 2:, :]


def _filter_fft_kernel(taps_ref, fa_ref, twc_ref, tws_ref, fb_ref, o_ref):
    h = _fft_forward(taps_ref[0], None, fa_ref[...], twc_ref[...], tws_ref[...], fb_ref[...])
    o_ref[0] = h * (1.0 / FFT_N)


def _filter_fft(taps, consts):
    _, c, n = taps.shape
    cb = 32
    taps4 = taps.reshape(2, c, FFT_NA, FFT_NB)
    cm = lambda o, j: (0, 0)
    return pl.pallas_call(
        _filter_fft_kernel,
        out_shape=jax.ShapeDtypeStruct((2, c, FFT_NA, 2 * FFT_NB), F32),
        grid=(2, c // cb),
        in_specs=[pl.BlockSpec((1, cb, FFT_NA, FFT_NB), lambda o, j: (o, j, 0, 0)),
                  pl.BlockSpec(consts["fa_real"].shape, cm),
                  pl.BlockSpec(consts["tw_fc"].shape, cm),
                  pl.BlockSpec(consts["tw_fs"].shape, cm),
                  pl.BlockSpec(consts["fb"].shape, cm)],
        out_specs=pl.BlockSpec((1, cb, FFT_NA, 2 * FFT_NB), lambda o, j: (o, j, 0, 0)),
        compiler_params=_params("parallel", "parallel"),
        name="hyena_filter_fft",
    )(taps4, consts["fa_real"], consts["tw_fc"], consts["tw_fs"], consts["fb"])


def _shift_tokens(a, direction):
    rows = a.shape[-2]
    lane = lax.broadcasted_iota(jnp.int32, a.shape, a.ndim - 1)
    row = lax.broadcasted_iota(jnp.int32, a.shape, a.ndim - 2)
    if direction == 1:
        l = pltpu.roll(a, 1, axis=a.ndim - 1)
        ls = pltpu.roll(l, 1, axis=a.ndim - 2)
        out = jnp.where(lane == 0, ls, l)
        edge = jnp.logical_and(lane == 0, row == 0)
    else:
        l = pltpu.roll(a, LANES - 1, axis=a.ndim - 1)
        ls = pltpu.roll(l, rows - 1, axis=a.ndim - 2)
        out = jnp.where(lane == LANES - 1, ls, l)
        edge = jnp.logical_and(lane == LANES - 1, row == rows - 1)
    return jnp.where(edge, 0.0, out)


def _short_conv3(a, w_ref):
    return w_ref[0] * _shift_tokens(a, 1) + w_ref[1] * a + w_ref[2] * _shift_tokens(a, -1)


def _hyena_kernel(v_ref, x1_ref, x2_ref, wv_ref, w1_ref, w2_ref, h_ref, hb_ref,
                  fa_ref, twfc_ref, twfs_ref, fb_ref, fbi_ref, twic_ref, twis_ref, fai_ref, o_ref):
    fwd_c = (fa_ref[...], twfc_ref[...], twfs_ref[...], fb_ref[...])
    inv_c = (fbi_ref[...], twic_ref[...], twis_ref[...], fai_ref[...])
    z = _short_conv3(v_ref[...].astype(F32), wv_ref)
    zr, zi = z[0], z[1]
    nb = FFT_NB
    for o, (g_ref, gw_ref) in enumerate(((x1_ref, w1_ref), (x2_ref, w2_ref))):
        x = _fft_forward(zr, zi, *fwd_c)
        h = h_ref[o]
        xr, xi, hr, hi = x[:, :, :nb], x[:, :, nb:], h[:, :, :nb], h[:, :, nb:]
        y = jnp.concatenate([xr * hr - xi * hi, xr * hi + xi * hr], axis=2)
        yr, yi = _fft_inverse(y, *inv_c)
        gate = _short_conv3(g_ref[...].astype(F32), gw_ref)
        bias = hb_ref[o]
        zr = gate[0] * (yr + zr * bias)
        zi = gate[1] * (yi + zi * bias)
    o_ref[0] = zr.astype(o_ref.dtype)
    o_ref[1] = zi.astype(o_ref.dtype)


def _hyena_latent(hyt, short_w, h_spec, hy_bias, consts):
    b, c3, length = hyt.shape
    c = c3 // 3
    rows = length // FFT_NB
    cb = 32
    nblk = c // cb
    hy4 = hyt.reshape(b, c3, rows, FFT_NB)
    w4 = jnp.broadcast_to(short_w.reshape(3, c3, 1, 1), (3, c3, 1, FFT_NB))
    hb4 = jnp.broadcast_to(hy_bias.reshape(2, c, 1, 1), (2, c, 1, FFT_NB))
    sig = lambda g: pl.BlockSpec((2, cb, rows, FFT_NB), lambda j, p: (p, g * nblk + j, 0, 0))
    wsp = lambda g: pl.BlockSpec((3, cb, 1, FFT_NB), lambda j, p: (0, g * nblk + j, 0, 0))
    cm = lambda j, p: (0, 0)
    names = ("fa", "tw_fc", "tw_fs", "fb", "fbi", "tw_ic", "tw_is", "fai")
    out = pl.pallas_call(
        _hyena_kernel,
        out_shape=jax.ShapeDtypeStruct((b, c, rows, FFT_NB), BF16),
        grid=(nblk, b // 2),
        in_specs=[sig(0), sig(1), sig(2), wsp(0), wsp(1), wsp(2),
                  pl.BlockSpec((2, cb, FFT_NA, 2 * FFT_NB), lambda j, p: (0, j, 0, 0)),
                  pl.BlockSpec((2, cb, 1, FFT_NB), lambda j, p: (0, j, 0, 0))]
                 + [pl.BlockSpec(consts[k].shape, cm) for k in names],
        out_specs=pl.BlockSpec((2, cb, rows, FFT_NB), lambda j, p: (p, j, 0, 0)),
        compiler_params=_params("parallel", "arbitrary"),
        name="hyena_long_conv",
    )(hy4, hy4, hy4, w4, w4, w4, h_spec, hb4, *[consts[k] for k in names])
    return out.reshape(b, c, length)


def _dense_dft_constants(length):
    n = 2 * length
    k = np.arange(n)
    ang = 2.0 * np.pi * np.outer(k, k) / n
    fwd = np.concatenate([np.cos(ang), -np.sin(ang)], axis=1)
    inv = np.concatenate([np.cos(ang[:length]).T, -np.sin(ang[:length]).T], axis=0) / n
    bf = lambda a: jnp.asarray(a, dtype=F32).astype(BF16)
    return bf(fwd), bf(inv)


def _shift_lanes(a, direction):
    n = a.shape[-1]
    lane = lax.broadcasted_iota(jnp.int32, a.shape, a.ndim - 1)
    if direction == 1:
        return jnp.where(lane == 0, 0.0, pltpu.roll(a, 1, axis=a.ndim - 1))
    return jnp.where(lane == n - 1, 0.0, pltpu.roll(a, n - 1, axis=a.ndim - 1))


def _hyena_ctx_kernel(v_ref, x1_ref, x2_ref, wv_ref, w1_ref, w2_ref, taps_ref, hb_ref, fwd_ref, inv_ref, o_ref):
    bsz, cb, length = v_ref.shape
    n = 2 * length
    fwd_m, inv_m = fwd_ref[0:length, :], inv_ref[...]

    def conv3(ref, w_ref):
        a = ref[...].astype(F32)
        return w_ref[0] * _shift_lanes(a, 1) + w_ref[1] * a + w_ref[2] * _shift_lanes(a, -1)

    z = conv3(v_ref, wv_ref)
    for o, (g_ref, gw_ref) in enumerate(((x1_ref, w1_ref), (x2_ref, w2_ref))):
        h = _dot(taps_ref[o].astype(BF16), fwd_ref[...])
        hr, hi = h[:, :n], h[:, n:]
        x = _dot(z.reshape(bsz * cb, length).astype(BF16), fwd_m).reshape(bsz, cb, 2 * n)
        xr, xi = x[:, :, :n], x[:, :, n:]
        y = jnp.concatenate([xr * hr - xi * hi, xr * hi + xi * hr], axis=2)
        yt = _dot(y.reshape(bsz * cb, 2 * n).astype(BF16), inv_m).reshape(bsz, cb, length)
        z = conv3(g_ref, gw_ref) * (yt + z * hb_ref[o])
    o_ref[...] = z.astype(o_ref.dtype)


def _hyena_context(hyt, short_w, taps, hy_bias):
    b, c3, length = hyt.shape
    c = c3 // 3
    cb = 128
    nblk = c // cb
    fwd_m, inv_m = _dense_dft_constants(length)
    w4 = jnp.broadcast_to(short_w.reshape(3, c3, 1), (3, c3, length))
    hb = jnp.broadcast_to(hy_bias.reshape(2, c, 1), (2, c, length))
    sig = lambda g: pl.BlockSpec((b, cb, length), lambda j: (0, g * nblk + j, 0))
    wsp = lambda g: pl.BlockSpec((3, cb, length), lambda j: (0, g * nblk + j, 0))
    return pl.pallas_call(
        _hyena_ctx_kernel,
        out_shape=jax.ShapeDtypeStruct((b, c, length), BF16),
        grid=(nblk,),
        in_specs=[sig(0), sig(1), sig(2), wsp(0), wsp(1), wsp(2),
                  pl.BlockSpec((2, cb, 2 * length), lambda j: (0, j, 0)),
                  pl.BlockSpec((2, cb, length), lambda j: (0, j, 0)),
                  pl.BlockSpec(fwd_m.shape, lambda j: (0, 0)),
                  pl.BlockSpec(inv_m.shape, lambda j: (0, 0))],
        out_specs=pl.BlockSpec((b, cb, length), lambda j: (0, j, 0)),
        compiler_params=_params("parallel"),
        name="hyena_context_conv",
    )(hyt, hyt, hyt, w4, w4, w4, taps, hb, fwd_m, inv_m)


def _route(gl, el):
    row = lax.broadcasted_iota(jnp.int32, gl.shape, 0).astype(F32)
    grp = jnp.floor(row * (1.0 / EXPERTS_PER_GROUP))
    big = float(N_EXPERTS)
    gmax = jnp.max(gl, axis=0, keepdims=True)
    gidx = jnp.min(jnp.where(gl == gmax, grp, big), axis=0, keepdims=True)
    gsum = jnp.sum(jnp.exp(gl - gmax), axis=0, keepdims=True) * (1.0 / EXPERTS_PER_GROUP)
    g_w = 1.0 / gsum
    em = jnp.where(grp == gidx, el, NEG_INF)
    t1 = jnp.max(em, axis=0, keepdims=True)
    i1 = jnp.min(jnp.where(em == t1, row, big), axis=0, keepdims=True)
    em2 = jnp.where(row == i1, 2.0 * NEG_INF, em)
    t2 = jnp.max(em2, axis=0, keepdims=True)
    i2 = jnp.min(jnp.where(em2 == t2, row, big), axis=0, keepdims=True)
    e2 = jnp.exp(t2 - t1)
    den = 1.0 + e2
    w1 = g_w / den
    w2 = g_w * e2 / den
    return jnp.where(row == i1, w1, 0.0) + jnp.where(row == i2, w2, 0.0)


def _outproj_kernel(y1_ref, y2_ref, w_ref, x_ref, g1_ref, n2_ref, sc2_ref, sh2_ref, wrh_ref, wrl_ref, br_ref,
                    xo_ref, h_ref, cmb_ref, *, channel_major):
    half = w_ref.shape[0] // 2
    dots = [_dot_tn if cm else _dot for cm in channel_major]
    acc = dots[0](y1_ref[0], w_ref[:half, :]) + dots[1](y2_ref[0], w_ref[half:, :])
    xn = x_ref[0] + g1_ref[0] * acc
    xo_ref[0] = xn
    h = _norm_modulate(xn, n2_ref[...], sc2_ref[0], sh2_ref[0])
    hi = h.astype(BF16)
    lo = (h - hi.astype(F32)).astype(BF16)
    h_ref[0] = hi
    logits = _dot(hi, wrh_ref[...]) + _dot(lo, wrh_ref[...]) + _dot(hi, wrl_ref[...]) + br_ref[...]
    lt = logits.T
    cmb = _route(lt[0:N_EXPERTS], lt[N_EXPERTS:2 * N_EXPERTS])
    pad = jnp.zeros((LANES - N_EXPERTS, cmb.shape[1]), F32)
    cmb_ref[0] = jnp.concatenate([cmb, pad], axis=0).T


def _outproj(y1, y2, w_bf, x, g1, n2g, sc2, sh2, wrh, wrl, br, channel_major):
    b, n, d = x.shape
    tm = min(512, n)
    half = d // 2
    tok = lambda i, j: (i, j, 0)
    mod = lambda i, j: (i, 0, 0)
    const = lambda i, j: (0, 0)
    y_specs = [pl.BlockSpec((1, half, tm), lambda i, j: (i, 0, j)) if cm else pl.BlockSpec((1, tm, half), tok)
               for cm in channel_major]
    return pl.pallas_call(
        functools.partial(_outproj_kernel, channel_major=tuple(channel_major)),
        out_shape=(jax.ShapeDtypeStruct((b, n, d), F32), jax.ShapeDtypeStruct((b, n, d), BF16),
                   jax.ShapeDtypeStruct((b, n, LANES), F32)),
        grid=(b, n // tm),
        in_specs=[y_specs[0], y_specs[1],
                  pl.BlockSpec((d, d), const),
                  pl.BlockSpec((1, tm, d), tok),
                  pl.BlockSpec((1, 1, d), mod),
                  pl.BlockSpec((1, d), const),
                  pl.BlockSpec((1, 1, d), mod),
                  pl.BlockSpec((1, 1, d), mod),
                  pl.BlockSpec(wrh.shape, const),
                  pl.BlockSpec(wrl.shape, const),
                  pl.BlockSpec(br.shape, const)],
        out_specs=(pl.BlockSpec((1, tm, d), tok), pl.BlockSpec((1, tm, d), tok),
                   pl.BlockSpec((1, tm, LANES), tok)),
        compiler_params=_params("parallel", "parallel"),
        name="outproj_router",
    )(y1, y2, w_bf, x, g1, n2g, sc2, sh2, wrh, wrl, br)


def _cast_kernel(w_ref, o_ref):
    o_ref[...] = w_ref[...].astype(o_ref.dtype)


def _to_bf16(w):
    e, r, c = w.shape
    spec = pl.BlockSpec((2, r, c), lambda i: (i, 0, 0))
    return pl.pallas_call(
        _cast_kernel,
        out_shape=jax.ShapeDtypeStruct(w.shape, BF16),
        grid=(e // 2,),
        in_specs=[spec],
        out_specs=spec,
        compiler_params=_params("parallel"),
        name="expert_weights_bf16",
    )(w)


def _moe_kernel(h_ref, cmb_ref, x_ref, g2_ref, wg_ref, wu_ref, wd_ref, o_ref, acc_ref):
    j = pl.program_id(2)
    h = h_ref[0]
    cmb = cmb_ref[0]
    lane = lax.broadcasted_iota(jnp.int32, cmb.shape, 1)
    experts = range(EXPERTS_PER_GROUP)
    w_e = [jnp.sum(jnp.where(lane == j * EXPERTS_PER_GROUP + e, cmb, 0.0), axis=1, keepdims=True) for e in experts]
    a = [_dot(h, wg_ref[e]) for e in experts]
    u = [_dot(h, wu_ref[e]) for e in experts]
    act = [(_silu(a[e]) * u[e] * w_e[e]).astype(BF16) for e in experts]
    contrib = _dot(act[0], wd_ref[0])
    for e in experts[1:]:
        contrib = contrib + _dot(act[e], wd_ref[e])

    @pl.when(j == 0)
    def _():
        acc_ref[...] = contrib

    @pl.when(j > 0)
    def _():
        acc_ref[...] += contrib

    @pl.when(j == N_GROUPS - 1)
    def _():
        o_ref[0] = x_ref[0] + g2_ref[0] * acc_ref[...]


def _moe(h, cmb, x, g2, wg, wu, wd):
    b, n, d = x.shape
    tm = min(512, n)
    tok = lambda i, t, j: (i, t, 0)
    epg = EXPERTS_PER_GROUP
    return pl.pallas_call(
        _moe_kernel,
        out_shape=jax.ShapeDtypeStruct((b, n, d), F32),
        grid=(b, n // tm, N_GROUPS),
        in_specs=[pl.BlockSpec((1, tm, d), tok),
                  pl.BlockSpec((1, tm, LANES), tok),
                  pl.BlockSpec((1, tm, d), tok),
                  pl.BlockSpec((1, 1, d), lambda i, t, j: (i, 0, 0)),
                  pl.BlockSpec((epg, d, D_EXPERT), lambda i, t, j: (j, 0, 0)),
                  pl.BlockSpec((epg, d, D_EXPERT), lambda i, t, j: (j, 0, 0)),
                  pl.BlockSpec((epg, D_EXPERT, d), lambda i, t, j: (j, 0, 0))],
        out_specs=pl.BlockSpec((1, tm, d), tok),
        scratch_shapes=[pltpu.VMEM((tm, d), F32)],
        compiler_params=_params("parallel", "parallel", "arbitrary"),
        name="moe_experts",
    )(h, cmb, x, g2, wg, wu, wd)


def kernel(x, c, ctx, c_ctx, ada_w, ada_b, norm1_g, norm2_g, w_in_even, qn_g, kn_g, na_rpb, sc_conv_w, w_in_odd, hy_short_w, hy_w1, hy_b1, hy_w2, hy_b2, hy_w3, hy_freq, hy_bias, cf_conv_w, cf_conv_b, cf_ln_g, cf_ln_b, w_out, moe_w_group, moe_b_group, moe_w_router, moe_b_router, moe_w_gate, moe_w_up, moe_w_down):
    depth = ada_w.shape[0]
    bsz, seq, d = x.shape
    lc = ctx.shape[1]
    assert 2 * seq == FFT_N and d == D_MODEL and bsz % 2 == 0

    mods = _ada_modulation(jnp.concatenate([c, c_ctx[None, :]], axis=0), ada_w, ada_b)
    seg = jnp.asarray(np.kron(np.eye(NA_HEADS), np.ones((HEAD_DIM, HEAD_DIM))), dtype=BF16)
    consts = _dft_constants()

    for l in range(depth):
        ctx_needed = any(j % 2 == 0 for j in range(l + 1, depth))
        lat_mod = [m[:, None, :] for m in jnp.split(mods[l, :bsz], 6, axis=-1)]
        ctx_mod = [jnp.broadcast_to(m[None, :, :], (bsz, 1, d)) for m in jnp.split(mods[l, bsz:bsz + 1], 6, axis=-1)]
        sh1, sc1, g1, sh2, sc2, g2 = lat_mod
        csh1, csc1, cg1, csh2, csc2, cg2 = ctx_mod
        n1g = norm1_g[l][None, :]
        n2g = norm2_g[l][None, :]
        w_out_bf = w_out[l].astype(BF16)
        wr = jnp.concatenate([jnp.repeat(moe_w_group[l], EXPERTS_PER_GROUP, axis=1), moe_w_router[l],
                              jnp.zeros((d, LANES - 2 * N_EXPERTS), F32)], axis=1)
        br = jnp.concatenate([jnp.repeat(moe_b_group[l], EXPERTS_PER_GROUP), moe_b_router[l],
                              jnp.zeros((LANES - 2 * N_EXPERTS,), F32)])[None, :]
        wrh = wr.astype(BF16)
        wrl = (wr - wrh.astype(F32)).astype(BF16)
        wg = _to_bf16(moe_w_gate[l])
        wu = _to_bf16(moe_w_up[l])
        wd = _to_bf16(moe_w_down[l])

        if l % 2 == 0:
            e = l // 2
            w_in = w_in_even[e].astype(BF16)
            qg = jnp.tile(qn_g[e], NA_HEADS)[None, :]
            kg = jnp.tile(kn_g[e], NA_HEADS)[None, :]
            ql, kl, vl, gbl, pl_ = _inproj_even(x, n1g, sc1, sh1, w_in, qg, kg, seg)
            qc, kc, vc, gbc, pc = _inproj_even(ctx, n1g, csc1, csh1, w_in, qg, kg, seg)
            bias = _bias_table(na_rpb[e])
            y1 = _natten(ql, kl, vl, kc, vc, bias)
            y2 = _short_gated_conv(gbl, pl_, sc_conv_w[e])
            lat_cm = (False, False)
            if ctx_needed:
                y1c = _ctx_attention(qc, kc, vc)
                y2c = _short_gated_conv(gbc, pc, sc_conv_w[e])
        else:
            o = l // 2
            wht = w_in_odd[o][:, :3 * HY_WIDTH].T.astype(BF16)
            wag = w_in_odd[o][:, 3 * HY_WIDTH:].astype(BF16)
            hyt, ag = _inproj_odd(x, n1g, sc1, sh1, wht, wag)
            taps = _hyena_taps(seq, hy_w1[o], hy_b1[o], hy_w2[o], hy_b2[o], hy_w3[o], hy_freq[o])
            h_spec = _filter_fft(taps, consts)
            y1 = _hyena_latent(hyt, hy_short_w[o], h_spec, hy_bias[o], consts)
            cf_args = (cf_conv_w[o], cf_conv_b[o], cf_ln_g[o], cf_ln_b[o])
            y2 = _conformer(ag, *cf_args)
            lat_cm = (True, False)
            if ctx_needed:
                hytc, agc = _inproj_odd(ctx, n1g, csc1, csh1, wht, wag)
                taps_c = _hyena_taps(lc, hy_w1[o], hy_b1[o], hy_w2[o], hy_b2[o], hy_w3[o], hy_freq[o])
                y1c = _hyena_context(hytc, hy_short_w[o], taps_c, hy_bias[o])
                y2c = _conformer(agc, *cf_args)

        x1, h2, cmb = _outproj(y1, y2, w_out_bf, x, g1, n2g, sc2, sh2, wrh, wrl, br, lat_cm)
        x = _moe(h2, cmb, x1, g2, wg, wu, wd)
        if ctx_needed:
            c1, hc2, cmbc = _outproj(y1c, y2c, w_out_bf, ctx, cg1, n2g, csc2, csh2, wrh, wrl, br, lat_cm)
            ctx = _moe(hc2, cmbc, c1, cg2, wg, wu, wd)
    return x
```

```python
import functools
import math

import numpy as np
import jax
import jax.numpy as jnp
from jax import lax
from jax.experimental import pallas as pl
from jax.experimental.pallas import tpu as pltpu

F32 = jnp.float32
BF16 = jnp.bfloat16

D_MODEL = 1024
GRID_W = 64
NA_HEADS = 8
HEAD_DIM = 64
NA_WIDTH = 512
NA_WIN_H = 8
NA_WIN_W = 16
SC_WIDTH = 512
HY_WIDTH = 512
HY_BANDS = 16
HY_EMB = 1 + 2 * HY_BANDS
HY_FFN = 64
HY_MAX_DECAY = math.log(1e-2) / 0.3
HY_MIN_DECAY = math.log(1e-2) / 1.5
CF_WIDTH = 512
CF_TAPS = 31
N_GROUPS = 4
EXPERTS_PER_GROUP = 4
N_EXPERTS = 16
D_EXPERT = 256
RMS_EPS = 1e-6
LN_EPS = 1e-5
NEG_INF = -1e30

VMEM_LIMIT_BYTES = 56 * 1024 * 1024
LANES = 128

FFT_NA = 64
FFT_NB = 128
FFT_N = FFT_NA * FFT_NB


def _params(*sem):
    return pltpu.CompilerParams(dimension_semantics=tuple(sem), vmem_limit_bytes=VMEM_LIMIT_BYTES)


def _dot(a, b):
    return jnp.dot(a, b, preferred_element_type=F32)


def _dot_nt(a, b):
    return lax.dot_general(a, b, (((1,), (1,)), ((), ())), preferred_element_type=F32)


def _dot_tn(a, b):
    return lax.dot_general(a, b, (((0,), (0,)), ((), ())), preferred_element_type=F32)


def _dot_f32(a, b):
    return jnp.dot(a, b, preferred_element_type=F32, precision=lax.Precision.HIGHEST)


def _silu(x):
    return x * jax.nn.sigmoid(x)


def _ada_kernel(ct_ref, w_ref, b_ref, o_ref, *, n_cond):
    ct = ct_ref[...]
    s = _silu(ct)
    w = w_ref[0]
    rows = [jnp.sum(w * s[:, r:r + 1], axis=0, keepdims=True) for r in range(n_cond)]
    rows.append(jnp.zeros((8 - n_cond, w.shape[1]), F32))
    o_ref[0] = jnp.concatenate(rows, axis=0) + b_ref[0]


def _ada_modulation(cond, ada_w, ada_b):
    n_cond, d = cond.shape
    depth, _, n6 = ada_w.shape
    tn = 1536
    ct = jnp.zeros((d, 8), F32).at[:, :n_cond].set(cond.T)
    return pl.pallas_call(
        functools.partial(_ada_kernel, n_cond=n_cond),
        out_shape=jax.ShapeDtypeStruct((depth, 8, n6), F32),
        grid=(depth, n6 // tn),
        in_specs=[pl.BlockSpec((d, 8), lambda l, j: (0, 0)),
                  pl.BlockSpec((1, d, tn), lambda l, j: (l, 0, j)),
                  pl.BlockSpec((1, 1, tn), lambda l, j: (l, 0, j))],
        out_specs=pl.BlockSpec((1, 8, tn), lambda l, j: (l, 0, j)),
        compiler_params=_params("parallel", "parallel"),
        name="ada_modulation",
    )(ct, ada_w, ada_b.reshape(depth, 1, n6))


def _norm_modulate(x, g, sc, sh):
    ms = jnp.mean(x * x, axis=-1, keepdims=True)
    return x * lax.rsqrt(ms + RMS_EPS) * g * (1.0 + sc) + sh


def _head_rmsnorm(t, seg, gain):
    ss = _dot((t * t).astype(BF16), seg)
    return t * lax.rsqrt(ss * (1.0 / HEAD_DIM) + RMS_EPS) * gain


def _inproj_even_kernel(x_ref, g_ref, sc_ref, sh_ref, w_ref, qg_ref, kg_ref, seg_ref,
                        q_ref, k_ref, v_ref, gb_ref, p_ref):
    u = _norm_modulate(x_ref[0], g_ref[...], sc_ref[0], sh_ref[0]).astype(BF16)
    seg = seg_ref[...]
    w = NA_WIDTH
    q = _dot(u, w_ref[:, 0 * w:1 * w])
    q_ref[0] = (_head_rmsnorm(q, seg, qg_ref[...]) * (HEAD_DIM ** -0.5)).astype(BF16)
    k = _dot(u, w_ref[:, 1 * w:2 * w])
    k_ref[0] = _head_rmsnorm(k, seg, kg_ref[...]).astype(BF16)
    v_ref[0] = _dot(u, w_ref[:, 2 * w:3 * w]).astype(BF16)
    gb_ref[0] = _dot(u, w_ref[:, 3 * w:4 * w]).astype(BF16)
    gc = _dot(u, w_ref[:, 4 * w:5 * w])
    hv = _dot(u, w_ref[:, 5 * w:6 * w])
    p_ref[0] = (gc * hv).astype(BF16)


def _inproj_even(x, g, sc, sh, w_bf, qg, kg, seg):
    b, n, d = x.shape
    tm = min(512, n)
    tok = lambda i, j: (i, j, 0)
    mod = lambda i, j: (i, 0, 0)
    const = lambda i, j: (0, 0)
    out = jax.ShapeDtypeStruct((b, n, NA_WIDTH), BF16)
    return pl.pallas_call(
        _inproj_even_kernel,
        out_shape=(out,) * 5,
        grid=(b, n // tm),
        in_specs=[pl.BlockSpec((1, tm, d), tok),
                  pl.BlockSpec((1, d), const),
                  pl.BlockSpec((1, 1, d), mod),
                  pl.BlockSpec((1, 1, d), mod),
                  pl.BlockSpec(w_bf.shape, const),
                  pl.BlockSpec((1, NA_WIDTH), const),
                  pl.BlockSpec((1, NA_WIDTH), const),
                  pl.BlockSpec((NA_WIDTH, NA_WIDTH), const)],
        out_specs=(pl.BlockSpec((1, tm, NA_WIDTH), tok),) * 5,
        compiler_params=_params("parallel", "parallel"),
        name="inproj_even",
    )(x, g, sc, sh, w_bf, qg, kg, seg)


def _bias_kernel(rpb_ref, o_ref):
    h = pl.program_id(0)
    qi = lax.broadcasted_iota(jnp.int32, (GRID_W, GRID_W), 0)
    ki = lax.broadcasted_iota(jnp.int32, (GRID_W, GRID_W), 1)
    start = jnp.clip(qi - NA_WIN_W // 2, 0, GRID_W - NA_WIN_W)
    valid = jnp.logical_and(ki >= start, ki < start + NA_WIN_W)
    cidx = jnp.clip(ki - qi, -(NA_WIN_W - 1), NA_WIN_W - 1) + (NA_WIN_W - 1)
    n_dr = 2 * NA_WIN_H - 1
    n_dc = 2 * NA_WIN_W - 1

    def body(j, accs):
        m = cidx == j
        return tuple(jnp.where(m, rpb_ref[(h * n_dr + d) * n_dc + j], a) for d, a in enumerate(accs))

    accs = lax.fori_loop(0, n_dc, body, tuple(jnp.zeros((GRID_W, GRID_W), F32) for _ in range(n_dr)))
    tiles = [jnp.where(valid, a, NEG_INF) for a in accs]
    for d0 in range(NA_WIN_H):
        o_ref[0, d0] = jnp.concatenate(tiles[d0:d0 + NA_WIN_H], axis=1)


def _bias_table(rpb):
    return pl.pallas_call(
        _bias_kernel,
        out_shape=jax.ShapeDtypeStruct((NA_HEADS, NA_WIN_H, GRID_W, NA_WIN_H * GRID_W), F32),
        grid=(NA_HEADS,),
        in_specs=[pl.BlockSpec(memory_space=pltpu.SMEM)],
        out_specs=pl.BlockSpec((1, NA_WIN_H, GRID_W, NA_WIN_H * GRID_W), lambda h: (h, 0, 0, 0)),
        compiler_params=_params("arbitrary"),
        name="rpb_bias_table",
    )(rpb.reshape(-1))


def _pair_attention(q2, kw, vw, bias, kc, vc, first_half):
    s_c = _dot_nt(q2, kc)
    m = jnp.max(s_c, axis=-1, keepdims=True)
    if kw is not None:
        s_w = _dot_nt(q2, kw) + bias
        m = jnp.maximum(m, jnp.max(s_w, axis=-1, keepdims=True))
        p_w = jnp.exp(s_w - m)
    p_c = jnp.exp(s_c - m)
    den = jnp.sum(p_c, axis=-1, keepdims=True)
    o = _dot(p_c.astype(BF16), vc)
    if kw is not None:
        den = den + jnp.sum(p_w, axis=-1, keepdims=True)
        o = o + _dot(p_w.astype(BF16), vw)
    o = o / den
    half = o.shape[0] // 2
    return jnp.where(first_half, o[:half], o[half:])


def _stack_heads(qp, first_half):
    zero = jnp.zeros_like(qp)
    return jnp.concatenate([jnp.where(first_half, qp, zero), jnp.where(first_half, zero, qp)], axis=0)


def _natten_kernel(q_ref, k_ref, v_ref, kc_ref, vc_ref, bias_ref, o_ref, *, rows_per_step, n_rows):
    blk = pl.program_id(1)
    lane = lax.broadcasted_iota(jnp.int32, (GRID_W, LANES), 1)
    first_half = lane < HEAD_DIM
    band = NA_WIN_H * GRID_W

    n_pairs = NA_HEADS // 2

    def row_body(j, carry):
        r = blk * rows_per_step + j
        start = jnp.clip(r - NA_WIN_H // 2, 0, n_rows - NA_WIN_H)
        d0 = start - r + (NA_WIN_H - 1)
        koff = pl.multiple_of(start * GRID_W, GRID_W)
        qoff = pl.multiple_of(j * GRID_W, GRID_W)
        cols = [slice(hp * LANES, (hp + 1) * LANES) for hp in range(n_pairs)]
        q2 = [_stack_heads(q_ref[0, pl.ds(qoff, GRID_W), cs], first_half) for cs in cols]
        s_w = [_dot_nt(q2[hp], k_ref[0, pl.ds(koff, band), cols[hp]])
               + jnp.concatenate([bias_ref[2 * hp, d0], bias_ref[2 * hp + 1, d0]], axis=0) for hp in range(n_pairs)]
        s_c = [_dot_nt(q2[hp], kc_ref[0, :, cols[hp]]) for hp in range(n_pairs)]
        m = [jnp.maximum(jnp.max(s_w[hp], axis=-1, keepdims=True), jnp.max(s_c[hp], axis=-1, keepdims=True))
             for hp in range(n_pairs)]
        p_w = [jnp.exp(s_w[hp] - m[hp]) for hp in range(n_pairs)]
        p_c = [jnp.exp(s_c[hp] - m[hp]) for hp in range(n_pairs)]
        den = [jnp.sum(p_w[hp], axis=-1, keepdims=True) + jnp.sum(p_c[hp], axis=-1, keepdims=True)
               for hp in range(n_pairs)]
        outs = []
        for hp in range(n_pairs):
            o = (_dot(p_w[hp].astype(BF16), v_ref[0, pl.ds(koff, band), cols[hp]])
                 + _dot(p_c[hp].astype(BF16), vc_ref[0, :, cols[hp]])) / den[hp]
            outs.append(jnp.where(first_half, o[:GRID_W], o[GRID_W:]))
        o_ref[0, pl.ds(qoff, GRID_W), :] = jnp.concatenate(outs, axis=1).astype(o_ref.dtype)
        return carry

    lax.fori_loop(0, rows_per_step, row_body, 0, unroll=2)


def _natten(q, k, v, kc, vc, bias):
    b, n, w = q.shape
    n_rows = n // GRID_W
    rows_per_step = 8
    tq = rows_per_step * GRID_W
    lc = kc.shape[1]
    return pl.pallas_call(
        functools.partial(_natten_kernel, rows_per_step=rows_per_step, n_rows=n_rows),
        out_shape=jax.ShapeDtypeStruct((b, n, w), BF16),
        grid=(b, n_rows // rows_per_step),
        in_specs=[pl.BlockSpec((1, tq, w), lambda i, j: (i, j, 0)),
                  pl.BlockSpec((1, n, w), lambda i, j: (i, 0, 0)),
                  pl.BlockSpec((1, n, w), lambda i, j: (i, 0, 0)),
                  pl.BlockSpec((1, lc, w), lambda i, j: (i, 0, 0)),
                  pl.BlockSpec((1, lc, w), lambda i, j: (i, 0, 0)),
                  pl.BlockSpec(bias.shape, lambda i, j: (0, 0, 0, 0))],
        out_specs=pl.BlockSpec((1, tq, w), lambda i, j: (i, j, 0)),
        compiler_params=_params("parallel", "arbitrary"),
        name="neighbourhood_attention",
    )(q, k, v, kc, vc, bias)


def _ctx_attn_kernel(q_ref, k_ref, v_ref, o_ref):
    lc = q_ref.shape[1]
    lane = lax.broadcasted_iota(jnp.int32, (lc, LANES), 1)
    first_half = lane < HEAD_DIM
    outs = []
    for hp in range(NA_HEADS // 2):
        cs = slice(hp * LANES, (hp + 1) * LANES)
        q2 = _stack_heads(q_ref[0, :, cs], first_half)
        outs.append(_pair_attention(q2, None, None, None, k_ref[0, :, cs], v_ref[0, :, cs], first_half))
    o_ref[0] = jnp.concatenate(outs, axis=1).astype(o_ref.dtype)


def _ctx_attention(q, k, v):
    b, lc, w = q.shape
    spec = pl.BlockSpec((1, lc, w), lambda i: (i, 0, 0))
    return pl.pallas_call(
        _ctx_attn_kernel,
        out_shape=jax.ShapeDtypeStruct((b, lc, w), BF16),
        grid=(b,),
        in_specs=[spec, spec, spec],
        out_specs=spec,
        compiler_params=_params("parallel"),
        name="context_attention",
    )(q, k, v)


def _sgconv_kernel(gb_ref, p_ref, w_ref, o_ref):
    p = p_ref[0].astype(F32)
    n = p.shape[0]
    row = lax.broadcasted_iota(jnp.int32, p.shape, 0)
    prev = jnp.where(row == 0, 0.0, pltpu.roll(p, 1, axis=0))
    nxt = jnp.where(row == n - 1, 0.0, pltpu.roll(p, n - 1, axis=0))
    y = w_ref[0:1, :] * prev + w_ref[1:2, :] * p + w_ref[2:3, :] * nxt
    o_ref[0] = (gb_ref[0].astype(F32) * y).astype(o_ref.dtype)


def _short_gated_conv(gb, p, w):
    b, n, c = p.shape
    spec = pl.BlockSpec((1, n, LANES), lambda i, j: (i, 0, j))
    return pl.pallas_call(
        _sgconv_kernel,
        out_shape=jax.ShapeDtypeStruct((b, n, c), BF16),
        grid=(b, c // LANES),
        in_specs=[spec, spec, pl.BlockSpec((3, LANES), lambda i, j: (0, j))],
        out_specs=spec,
        compiler_params=_params("parallel", "parallel"),
        name="short_gated_conv",
    )(gb, p, w)


def _inproj_odd_kernel(x_ref, g_ref, sc_ref, sh_ref, wht_ref, wag_ref, hy_ref, ag_ref):
    u = _norm_modulate(x_ref[0], g_ref[...], sc_ref[0], sh_ref[0]).astype(BF16)
    hy_ref[0] = _dot_nt(wht_ref[...], u).astype(BF16)
    a = _dot(u, wag_ref[:, :CF_WIDTH])
    g = _dot(u, wag_ref[:, CF_WIDTH:])
    ag_ref[0] = (a * jax.nn.sigmoid(g)).astype(BF16)


def _inproj_odd(x, g, sc, sh, wht_bf, wag_bf):
    b, n, d = x.shape
    tm = min(512, n)
    hw = wht_bf.shape[0]
    tok = lambda i, j: (i, j, 0)
    mod = lambda i, j: (i, 0, 0)
    const = lambda i, j: (0, 0)
    return pl.pallas_call(
        _inproj_odd_kernel,
        out_shape=(jax.ShapeDtypeStruct((b, hw, n), BF16), jax.ShapeDtypeStruct((b, n, CF_WIDTH), BF16)),
        grid=(b, n // tm),
        in_specs=[pl.BlockSpec((1, tm, d), tok),
                  pl.BlockSpec((1, d), const),
                  pl.BlockSpec((1, 1, d), mod),
                  pl.BlockSpec((1, 1, d), mod),
                  pl.BlockSpec(wht_bf.shape, const),
                  pl.BlockSpec(wag_bf.shape, const)],
        out_specs=(pl.BlockSpec((1, hw, tm), lambda i, j: (i, 0, j)), pl.BlockSpec((1, tm, CF_WIDTH), tok)),
        compiler_params=_params("parallel", "parallel"),
        name="inproj_odd",
    )(x, g, sc, sh, wht_bf, wag_bf)


SUBLANES = 8
CF_PAD = 2 * SUBLANES
CF_ROWS = 128


def _conformer_kernel(ag_ref, w_ref, cb_ref, lg_ref, lb_ref, o_ref, pad_ref, *, seq):
    zeros = jnp.zeros((CF_PAD, CF_WIDTH), F32)
    pad_ref[0:CF_PAD, :] = zeros
    pad_ref[CF_PAD + seq:2 * CF_PAD + seq, :] = zeros
    pad_ref[CF_PAD:CF_PAD + seq, :] = ag_ref[0].astype(F32)
    shift0 = CF_PAD - CF_TAPS // 2
    n_groups = (shift0 + CF_TAPS - 1) // SUBLANES + 1

    def conv_rows(i, carry):
        n0 = pl.multiple_of(i * CF_ROWS, CF_ROWS)
        wins = [pad_ref[pl.ds(n0 + SUBLANES * a, CF_ROWS + SUBLANES), :] for a in range(n_groups)]
        acc = None
        for b in range(SUBLANES):
            part = None
            for a in range(n_groups):
                j = SUBLANES * a + b - shift0
                if 0 <= j < CF_TAPS:
                    term = w_ref[j:j + 1, :] * wins[a]
                    part = term if part is None else part + term
            part = part[b:b + CF_ROWS, :]
            acc = part if acc is None else acc + part
        y = acc + cb_ref[...]
        mu = jnp.mean(y, axis=-1, keepdims=True)
        yc = y - mu
        var = jnp.mean(yc * yc, axis=-1, keepdims=True)
        z = yc * lax.rsqrt(var + LN_EPS) * lg_ref[...] + lb_ref[...]
        o_ref[0, pl.ds(n0, CF_ROWS), :] = _silu(z).astype(o_ref.dtype)
        return carry

    lax.fori_loop(0, seq // CF_ROWS, conv_rows, 0)


def _conformer(ag, w, cb, lg, lb):
    b, n, c = ag.shape
    spec = pl.BlockSpec((1, n, c), lambda i: (i, 0, 0))
    vec = pl.BlockSpec((1, c), lambda i: (0, 0))
    return pl.pallas_call(
        functools.partial(_conformer_kernel, seq=n),
        out_shape=jax.ShapeDtypeStruct((b, n, c), BF16),
        grid=(b,),
        in_specs=[spec, pl.BlockSpec((CF_TAPS, c), lambda i: (0, 0)), vec, vec, vec],
        out_specs=spec,
        scratch_shapes=[pltpu.VMEM((n + 2 * CF_PAD, c), F32)],
        compiler_params=_params("parallel"),
        name="conformer_conv",
    )(ag, w, cb[None, :], lg[None, :], lb[None, :])


def _hyena_features(length):
    t = np.linspace(0.0, 1.0, length, dtype=np.float32)
    w = (2.0 * math.pi * np.arange(length, dtype=np.float32) / length).astype(np.float32)
    bands = np.linspace(1e-4, HY_BANDS - 1, HY_BANDS, dtype=np.float32)
    ang = (bands[:, None] * w[None, :]).astype(np.float32)
    zt = np.concatenate([t[None, :], np.cos(ang), -np.sin(ang)], axis=0).astype(np.float32)
    deltas = np.abs(np.linspace(HY_MIN_DECAY, HY_MAX_DECAY, HY_WIDTH, dtype=np.float32))
    rev = (length - np.arange(length)) % length
    zt2 = np.concatenate([zt, zt[:, rev]], axis=1)
    t2 = np.concatenate([t, t[rev]])[None, :]
    return zt2, t2, deltas[:, None]


def _taps_kernel(zt_ref, t_ref, dl_ref, w1t_ref, b1_ref, f0_ref, w2t_ref, b2_ref, f1_ref, w3t_ref,
                 o_ref, hid_ref):
    first = jnp.logical_and(pl.program_id(0) == 0, pl.program_id(1) == 0)
    length = t_ref.shape[1] // 2

    @pl.when(first)
    def _():
        h1 = jnp.sin(f0_ref[...] * (_dot_f32(w1t_ref[...], zt_ref[...]) + b1_ref[...]))
        hid_ref[...] = jnp.sin(f1_ref[...] * (_dot_f32(w2t_ref[...], h1) + b2_ref[...]))

    decay = jnp.exp(-(dl_ref[...] * t_ref[...]))
    fwd = _dot_f32(w3t_ref[0, 0], hid_ref[:, :length])
    bwd = _dot_f32(w3t_ref[0, 1], hid_ref[:, length:])
    taps = jnp.concatenate([fwd, bwd], axis=1) * decay
    nrm = jnp.sum(jnp.abs(taps), axis=-1, keepdims=True)
    lane = lax.broadcasted_iota(jnp.int32, taps.shape, 1)
    o_ref[0] = jnp.where(lane == length, 0.0, taps / nrm)


def _hyena_taps(length, w1, b1, w2, b2, w3, freq):
    zt2, t2, deltas = _hyena_features(length)
    cb = 128
    w3t = w3.T.reshape(2, 2, HY_WIDTH, HY_FFN)
    col = lambda v: v.reshape(HY_FFN, 1)
    const = lambda o, j: (0, 0)
    return pl.pallas_call(
        _taps_kernel,
        out_shape=jax.ShapeDtypeStruct((2, HY_WIDTH, 2 * length), F32),
        grid=(2, HY_WIDTH // cb),
        in_specs=[pl.BlockSpec((HY_EMB, 2 * length), const),
                  pl.BlockSpec((1, 2 * length), const),
                  pl.BlockSpec((cb, 1), lambda o, j: (j, 0)),
                  pl.BlockSpec((HY_FFN, HY_EMB), const),
                  pl.BlockSpec((HY_FFN, 1), const),
                  pl.BlockSpec((HY_FFN, 1), const),
                  pl.BlockSpec((HY_FFN, HY_FFN), const),
                  pl.BlockSpec((HY_FFN, 1), const),
                  pl.BlockSpec((HY_FFN, 1), const),
                  pl.BlockSpec((1, 2, cb, HY_FFN), lambda o, j: (o, 0, j, 0))],
        out_specs=pl.BlockSpec((1, cb, 2 * length), lambda o, j: (o, j, 0)),
        scratch_shapes=[pltpu.VMEM((HY_FFN, 2 * length), F32)],
        compiler_params=_params("arbitrary", "arbitrary"),
        name="hyena_filter_taps",
    )(jnp.asarray(zt2), jnp.asarray(t2), jnp.asarray(deltas), w1.T, col(b1), col(freq[0]),
      w2.T, col(b2), col(freq[1]), w3t)


def _dft_constants():
    na, nb, n = FFT_NA, FFT_NB, FFT_N
    half = na // 2
    ka = np.arange(na)
    ang_a = 2.0 * np.pi * np.outer(ka, ka) / na
    ca, sa = np.cos(ang_a), np.sin(ang_a)
    fa = np.block([[ca[:half], -sa[:half]], [sa[:half], ca[:half]]])
    fai = np.block([[ca[:, :half], sa[:, :half]], [-sa[:, :half], ca[:, :half]]])
    kb = np.arange(nb)
    ang_b = 2.0 * np.pi * np.outer(kb, kb) / nb
    cbm, sbm = np.cos(ang_b), np.sin(ang_b)
    fb = np.block([[cbm, -sbm], [sbm, cbm]])
    fbi = np.block([[cbm, sbm], [-sbm, cbm]])
    ang_t = 2.0 * np.pi * np.outer(kb, ka) / n
    ct, st = np.cos(ang_t), np.sin(ang_t)
    tw_fc = np.concatenate([ct, ct], axis=1)
    tw_fs = np.concatenate([st, -st], axis=1)
    tw_ic, tw_is = ct.T.copy(), st.T.copy()
    bf = lambda a: jnp.asarray(a, dtype=F32).astype(BF16)
    f32 = lambda a: jnp.asarray(a, dtype=F32)
    fa_real = np.concatenate([ca, -sa], axis=1)
    return dict(fa=bf(fa), fa_real=bf(fa_real), fai=bf(fai), fb=bf(fb), fbi=bf(fbi),
                tw_fc=f32(tw_fc), tw_fs=f32(tw_fs), tw_ic=f32(tw_ic), tw_is=f32(tw_is))


def _fft_forward(zr, zi, fa, tw_fc, tw_fs, fb):
    c, _, nb = zr.shape
    tr = jnp.swapaxes(zr, 1, 2)
    lhs = tr if zi is None else jnp.concatenate([tr, jnp.swapaxes(zi, 1, 2)], axis=2)
    a = _dot(lhs.reshape(c * nb, lhs.shape[2]).astype(BF16), fa).reshape(c, nb, 2 * FFT_NA)
    a = a * tw_fc + pltpu.roll(a, FFT_NA, axis=2) * tw_fs
    t = jnp.swapaxes(a, 1, 2)
    lhs2 = jnp.concatenate([t[:, :FFT_NA, :], t[:, FFT_NA:, :]], axis=2)
    x = _dot(lhs2.reshape(c * FFT_NA, 2 * nb).astype(BF16), fb)
    return x.reshape(c, FFT_NA, 2 * nb)


def _fft_inverse(y, fbi, tw_ic, tw_is, fai):
    c = y.shape[0]
    nb = FFT_NB
    b = _dot(y.reshape(c * FFT_NA, 2 * nb).astype(BF16), fbi).reshape(c, FFT_NA, 2 * nb)
    br, bi = b[:, :, :nb], b[:, :, nb:]
    rr = br * tw_ic - bi * tw_is
    ii = bi * tw_ic + br * tw_is
    t = jnp.swapaxes(jnp.concatenate([rr, ii], axis=1), 1, 2)
    o = _dot(t.reshape(c * nb, 2 * FFT_NA).astype(BF16), fai).reshape(c, nb, FFT_NA)
    o = jnp.swapaxes(o, 1, 2)
    return o[:, :FFT_NA // 2, :], o[:, FFT_NA // 2:, :]


def _filter_fft_kernel(taps_ref, fa_ref, twc_ref, tws_ref, fb_ref, o_ref):
    h = _fft_forward(taps_ref[0], None, fa_ref[...], twc_ref[...], tws_ref[...], fb_ref[...])
    o_ref[0] = h * (1.0 / FFT_N)


def _filter_fft(taps, consts):
    _, c, n = taps.shape
    cb = 32
    taps4 = taps.reshape(2, c, FFT_NA, FFT_NB)
    cm = lambda o, j: (0, 0)
    return pl.pallas_call(
        _filter_fft_kernel,
        out_shape=jax.ShapeDtypeStruct((2, c, FFT_NA, 2 * FFT_NB), F32),
        grid=(2, c // cb),
        in_specs=[pl.BlockSpec((1, cb, FFT_NA, FFT_NB), lambda o, j: (o, j, 0, 0)),
                  pl.BlockSpec(consts["fa_real"].shape, cm),
                  pl.BlockSpec(consts["tw_fc"].shape, cm),
                  pl.BlockSpec(consts["tw_fs"].shape, cm),
                  pl.BlockSpec(consts["fb"].shape, cm)],
        out_specs=pl.BlockSpec((1, cb, FFT_NA, 2 * FFT_NB), lambda o, j: (o, j, 0, 0)),
        compiler_params=_params("parallel", "parallel"),
        name="hyena_filter_fft",
    )(taps4, consts["fa_real"], consts["tw_fc"], consts["tw_fs"], consts["fb"])


def _shift_tokens(a, direction):
    rows = a.shape[-2]
    lane = lax.broadcasted_iota(jnp.int32, a.shape, a.ndim - 1)
    row = lax.broadcasted_iota(jnp.int32, a.shape, a.ndim - 2)
    if direction == 1:
        l = pltpu.roll(a, 1, axis=a.ndim - 1)
        ls = pltpu.roll(l, 1, axis=a.ndim - 2)
        out = jnp.where(lane == 0, ls, l)
        edge = jnp.logical_and(lane == 0, row == 0)
    else:
        l = pltpu.roll(a, LANES - 1, axis=a.ndim - 1)
        ls = pltpu.roll(l, rows - 1, axis=a.ndim - 2)
        out = jnp.where(lane == LANES - 1, ls, l)
        edge = jnp.logical_and(lane == LANES - 1, row == rows - 1)
    return jnp.where(edge, 0.0, out)


def _short_conv3(a, w_ref):
    return w_ref[0] * _shift_tokens(a, 1) + w_ref[1] * a + w_ref[2] * _shift_tokens(a, -1)


def _hyena_kernel(v_ref, x1_ref, x2_ref, wv_ref, w1_ref, w2_ref, h_ref, hb_ref,
                  fa_ref, twfc_ref, twfs_ref, fb_ref, fbi_ref, twic_ref, twis_ref, fai_ref, o_ref):
    fwd_c = (fa_ref[...], twfc_ref[...], twfs_ref[...], fb_ref[...])
    inv_c = (fbi_ref[...], twic_ref[...], twis_ref[...], fai_ref[...])
    z = _short_conv3(v_ref[...].astype(F32), wv_ref)
    zr, zi = z[0], z[1]
    nb = FFT_NB
    for o, (g_ref, gw_ref) in enumerate(((x1_ref, w1_ref), (x2_ref, w2_ref))):
        x = _fft_forward(zr, zi, *fwd_c)
        h = h_ref[o]
        xr, xi, hr, hi = x[:, :, :nb], x[:, :, nb:], h[:, :, :nb], h[:, :, nb:]
        y = jnp.concatenate([xr * hr - xi * hi, xr * hi + xi * hr], axis=2)
        yr, yi = _fft_inverse(y, *inv_c)
        gate = _short_conv3(g_ref[...].astype(F32), gw_ref)
        bias = hb_ref[o]
        zr = gate[0] * (yr + zr * bias)
        zi = gate[1] * (yi + zi * bias)
    o_ref[0] = zr.astype(o_ref.dtype)
    o_ref[1] = zi.astype(o_ref.dtype)


def _hyena_latent(hyt, short_w, h_spec, hy_bias, consts):
    b, c3, length = hyt.shape
    c = c3 // 3
    rows = length // FFT_NB
    cb = 32
    nblk = c // cb
    hy4 = hyt.reshape(b, c3, rows, FFT_NB)
    w4 = jnp.broadcast_to(short_w.reshape(3, c3, 1, 1), (3, c3, 1, FFT_NB))
    hb4 = jnp.broadcast_to(hy_bias.reshape(2, c, 1, 1), (2, c, 1, FFT_NB))
    sig = lambda g: pl.BlockSpec((2, cb, rows, FFT_NB), lambda j, p: (p, g * nblk + j, 0, 0))
    wsp = lambda g: pl.BlockSpec((3, cb, 1, FFT_NB), lambda j, p: (0, g * nblk + j, 0, 0))
    cm = lambda j, p: (0, 0)
    names = ("fa", "tw_fc", "tw_fs", "fb", "fbi", "tw_ic", "tw_is", "fai")
    out = pl.pallas_call(
        _hyena_kernel,
        out_shape=jax.ShapeDtypeStruct((b, c, rows, FFT_NB), BF16),
        grid=(nblk, b // 2),
        in_specs=[sig(0), sig(1), sig(2), wsp(0), wsp(1), wsp(2),
                  pl.BlockSpec((2, cb, FFT_NA, 2 * FFT_NB), lambda j, p: (0, j, 0, 0)),
                  pl.BlockSpec((2, cb, 1, FFT_NB), lambda j, p: (0, j, 0, 0))]
                 + [pl.BlockSpec(consts[k].shape, cm) for k in names],
        out_specs=pl.BlockSpec((2, cb, rows, FFT_NB), lambda j, p: (p, j, 0, 0)),
        compiler_params=_params("parallel", "arbitrary"),
        name="hyena_long_conv",
    )(hy4, hy4, hy4, w4, w4, w4, h_spec, hb4, *[consts[k] for k in names])
    return out.reshape(b, c, length)


def _dense_dft_constants(length):
    n = 2 * length
    k = np.arange(n)
    ang = 2.0 * np.pi * np.outer(k, k) / n
    fwd = np.concatenate([np.cos(ang), -np.sin(ang)], axis=1)
    inv = np.concatenate([np.cos(ang[:length]).T, -np.sin(ang[:length]).T], axis=0) / n
    bf = lambda a: jnp.asarray(a, dtype=F32).astype(BF16)
    return bf(fwd), bf(inv)


def _shift_lanes(a, direction):
    n = a.shape[-1]
    lane = lax.broadcasted_iota(jnp.int32, a.shape, a.ndim - 1)
    if direction == 1:
        return jnp.where(lane == 0, 0.0, pltpu.roll(a, 1, axis=a.ndim - 1))
    return jnp.where(lane == n - 1, 0.0, pltpu.roll(a, n - 1, axis=a.ndim - 1))


def _hyena_ctx_kernel(v_ref, x1_ref, x2_ref, wv_ref, w1_ref, w2_ref, taps_ref, hb_ref, fwd_ref, inv_ref, o_ref):
    bsz, cb, length = v_ref.shape
    n = 2 * length
    fwd_m, inv_m = fwd_ref[0:length, :], inv_ref[...]

    def conv3(ref, w_ref):
        a = ref[...].astype(F32)
        return w_ref[0] * _shift_lanes(a, 1) + w_ref[1] * a + w_ref[2] * _shift_lanes(a, -1)

    z = conv3(v_ref, wv_ref)
    for o, (g_ref, gw_ref) in enumerate(((x1_ref, w1_ref), (x2_ref, w2_ref))):
        h = _dot(taps_ref[o].astype(BF16), fwd_ref[...])
        hr, hi = h[:, :n], h[:, n:]
        x = _dot(z.reshape(bsz * cb, length).astype(BF16), fwd_m).reshape(bsz, cb, 2 * n)
        xr, xi = x[:, :, :n], x[:, :, n:]
        y = jnp.concatenate([xr * hr - xi * hi, xr * hi + xi * hr], axis=2)
        yt = _dot(y.reshape(bsz * cb, 2 * n).astype(BF16), inv_m).reshape(bsz, cb, length)
        z = conv3(g_ref, gw_ref) * (yt + z * hb_ref[o])
    o_ref[...] = z.astype(o_ref.dtype)


def _hyena_context(hyt, short_w, taps, hy_bias):
    b, c3, length = hyt.shape
    c = c3 // 3
    cb = 128
    nblk = c // cb
    fwd_m, inv_m = _dense_dft_constants(length)
    w4 = jnp.broadcast_to(short_w.reshape(3, c3, 1), (3, c3, length))
    hb = jnp.broadcast_to(hy_bias.reshape(2, c, 1), (2, c, length))
    sig = lambda g: pl.BlockSpec((b, cb, length), lambda j: (0, g * nblk + j, 0))
    wsp = lambda g: pl.BlockSpec((3, cb, length), lambda j: (0, g * nblk + j, 0))
    return pl.pallas_call(
        _hyena_ctx_kernel,
        out_shape=jax.ShapeDtypeStruct((b, c, length), BF16),
        grid=(nblk,),
        in_specs=[sig(0), sig(1), sig(2), wsp(0), wsp(1), wsp(2),
                  pl.BlockSpec((2, cb, 2 * length), lambda j: (0, j, 0)),
                  pl.BlockSpec((2, cb, length), lambda j: (0, j, 0)),
                  pl.BlockSpec(fwd_m.shape, lambda j: (0, 0)),
                  pl.BlockSpec(inv_m.shape, lambda j: (0, 0))],
        out_specs=pl.BlockSpec((b, cb, length), lambda j: (0, j, 0)),
        compiler_params=_params("parallel"),
        name="hyena_context_conv",
    )(hyt, hyt, hyt, w4, w4, w4, taps, hb, fwd_m, inv_m)


def _route(gl, el):
    row = lax.broadcasted_iota(jnp.int32, gl.shape, 0).astype(F32)
    grp = jnp.floor(row * (1.0 / EXPERTS_PER_GROUP))
    big = float(N_EXPERTS)
    gmax = jnp.max(gl, axis=0, keepdims=True)
    gidx = jnp.min(jnp.where(gl == gmax, grp, big), axis=0, keepdims=True)
    gsum = jnp.sum(jnp.exp(gl - gmax), axis=0, keepdims=True) * (1.0 / EXPERTS_PER_GROUP)
    g_w = 1.0 / gsum
    em = jnp.where(grp == gidx, el, NEG_INF)
    t1 = jnp.max(em, axis=0, keepdims=True)
    i1 = jnp.min(jnp.where(em == t1, row, big), axis=0, keepdims=True)
    em2 = jnp.where(row == i1, 2.0 * NEG_INF, em)
    t2 = jnp.max(em2, axis=0, keepdims=True)
    i2 = jnp.min(jnp.where(em2 == t2, row, big), axis=0, keepdims=True)
    e2 = jnp.exp(t2 - t1)
    den = 1.0 + e2
    w1 = g_w / den
    w2 = g_w * e2 / den
    return jnp.where(row == i1, w1, 0.0) + jnp.where(row == i2, w2, 0.0)


def _outproj_kernel(y1_ref, y2_ref, w_ref, x_ref, g1_ref, n2_ref, sc2_ref, sh2_ref, wr_ref, br_ref,
                    xo_ref, h_ref, cmb_ref, *, channel_major):
    half = w_ref.shape[0] // 2
    dots = [_dot_tn if cm else _dot for cm in channel_major]
    acc = dots[0](y1_ref[0], w_ref[:half, :]) + dots[1](y2_ref[0], w_ref[half:, :])
    xn = x_ref[0] + g1_ref[0] * acc
    xo_ref[0] = xn
    h = _norm_modulate(xn, n2_ref[...], sc2_ref[0], sh2_ref[0])
    hi = h.astype(BF16)
    lo = (h - hi.astype(F32)).astype(BF16)
    h_ref[0] = hi
    p = _dot(hi, wr_ref[...])
    logits = p[:, :LANES] + p[:, LANES:] + _dot(lo, wr_ref[:, :LANES]) + br_ref[...]
    lt = logits.T
    cmb = _route(lt[0:N_EXPERTS], lt[N_EXPERTS:2 * N_EXPERTS])
    pad = jnp.zeros((LANES - N_EXPERTS, cmb.shape[1]), F32)
    cmb_ref[0] = jnp.concatenate([cmb, pad], axis=0).T


def _outproj(y1, y2, w_bf, x, g1, n2g, sc2, sh2, wr2, br, channel_major):
    b, n, d = x.shape
    tm = min(512, n)
    half = d // 2
    tok = lambda i, j: (i, j, 0)
    mod = lambda i, j: (i, 0, 0)
    const = lambda i, j: (0, 0)
    y_specs = [pl.BlockSpec((1, half, tm), lambda i, j: (i, 0, j)) if cm else pl.BlockSpec((1, tm, half), tok)
               for cm in channel_major]
    return pl.pallas_call(
        functools.partial(_outproj_kernel, channel_major=tuple(channel_major)),
        out_shape=(jax.ShapeDtypeStruct((b, n, d), F32), jax.ShapeDtypeStruct((b, n, d), BF16),
                   jax.ShapeDtypeStruct((b, n, LANES), F32)),
        grid=(b, n // tm),
        in_specs=[y_specs[0], y_specs[1],
                  pl.BlockSpec((d, d), const),
                  pl.BlockSpec((1, tm, d), tok),
                  pl.BlockSpec((1, 1, d), mod),
                  pl.BlockSpec((1, d), const),
                  pl.BlockSpec((1, 1, d), mod),
                  pl.BlockSpec((1, 1, d), mod),
                  pl.BlockSpec(wr2.shape, const),
                  pl.BlockSpec(br.shape, const)],
        out_specs=(pl.BlockSpec((1, tm, d), tok), pl.BlockSpec((1, tm, d), tok),
                   pl.BlockSpec((1, tm, LANES), tok)),
        compiler_params=_params("parallel", "parallel"),
        name="outproj_router",
    )(y1, y2, w_bf, x, g1, n2g, sc2, sh2, wr2, br)


def _cast_kernel(w_ref, o_ref):
    o_ref[...] = w_ref[...].astype(o_ref.dtype)


def _to_bf16(w):
    e, r, c = w.shape
    spec = pl.BlockSpec((2, r, c), lambda i: (i, 0, 0))
    return pl.pallas_call(
        _cast_kernel,
        out_shape=jax.ShapeDtypeStruct(w.shape, BF16),
        grid=(e // 2,),
        in_specs=[spec],
        out_specs=spec,
        compiler_params=_params("parallel"),
        name="expert_weights_bf16",
    )(w)


MOE_TILE = 512


def _swiglu_group(h, w_rows, j, wg_ref, wu_ref, wd_ref):
    lane = lax.broadcasted_iota(jnp.int32, w_rows.shape, 1)
    experts = range(EXPERTS_PER_GROUP)
    w_e = [jnp.sum(jnp.where(lane == j * EXPERTS_PER_GROUP + e, w_rows, 0.0), axis=1, keepdims=True) for e in experts]
    a = [_dot(h, wg_ref[e]) for e in experts]
    u = [_dot(h, wu_ref[e]) for e in experts]
    act = [(_silu(a[e]) * u[e] * w_e[e]).astype(BF16) for e in experts]
    out = _dot(act[0], wd_ref[0])
    for e in experts[1:]:
        out = out + _dot(act[e], wd_ref[e])
    return out


def _moe_kernel(h_ref, cmb_ref, x_ref, g2_ref, wg_ref, wu_ref, wd_ref, o_ref):
    j = pl.program_id(2)
    contrib = _swiglu_group(h_ref[0], cmb_ref[0], j, wg_ref, wu_ref, wd_ref)

    @pl.when(j == 0)
    def _():
        o_ref[0] = contrib

    @pl.when(jnp.logical_and(j > 0, j < N_GROUPS - 1))
    def _():
        o_ref[0] += contrib

    @pl.when(j == N_GROUPS - 1)
    def _():
        o_ref[0] = x_ref[0] + g2_ref[0] * (o_ref[0] + contrib)


def _moe(h, cmb, x, g2, wg, wu, wd, layer):
    b, n, d = x.shape
    tm = min(MOE_TILE, n)
    tok = lambda i, t, j: (i, t, 0)
    epg = EXPERTS_PER_GROUP
    wmap = lambda i, t, j: (layer * N_GROUPS + j, 0, 0)
    return pl.pallas_call(
        _moe_kernel,
        out_shape=jax.ShapeDtypeStruct((b, n, d), F32),
        grid=(b, n // tm, N_GROUPS),
        in_specs=[pl.BlockSpec((1, tm, d), tok),
                  pl.BlockSpec((1, tm, LANES), tok),
                  pl.BlockSpec((1, tm, d), tok),
                  pl.BlockSpec((1, 1, d), lambda i, t, j: (i, 0, 0)),
                  pl.BlockSpec((epg, d, D_EXPERT), wmap),
                  pl.BlockSpec((epg, d, D_EXPERT), wmap),
                  pl.BlockSpec((epg, D_EXPERT, d), wmap)],
        out_specs=pl.BlockSpec((1, tm, d), tok),
        compiler_params=_params("parallel", "parallel", "arbitrary"),
        name="moe_experts",
    )(h, cmb, x, g2, wg, wu, wd)


def kernel(x, c, ctx, c_ctx, ada_w, ada_b, norm1_g, norm2_g, w_in_even, qn_g, kn_g, na_rpb, sc_conv_w, w_in_odd, hy_short_w, hy_w1, hy_b1, hy_w2, hy_b2, hy_w3, hy_freq, hy_bias, cf_conv_w, cf_conv_b, cf_ln_g, cf_ln_b, w_out, moe_w_group, moe_b_group, moe_w_router, moe_b_router, moe_w_gate, moe_w_up, moe_w_down):
    depth = ada_w.shape[0]
    bsz, seq, d = x.shape
    lc = ctx.shape[1]
    assert 2 * seq == FFT_N and d == D_MODEL and bsz % 2 == 0

    mods = _ada_modulation(jnp.concatenate([c, c_ctx[None, :]], axis=0), ada_w, ada_b)
    seg = jnp.asarray(np.kron(np.eye(NA_HEADS), np.ones((HEAD_DIM, HEAD_DIM))), dtype=BF16)
    consts = _dft_constants()
    wg = _to_bf16(moe_w_gate.reshape((-1,) + moe_w_gate.shape[2:]))
    wu = _to_bf16(moe_w_up.reshape((-1,) + moe_w_up.shape[2:]))
    wd = _to_bf16(moe_w_down.reshape((-1,) + moe_w_down.shape[2:]))

    for l in range(depth):
        ctx_needed = any(j % 2 == 0 for j in range(l + 1, depth))
        lat_mod = [m[:, None, :] for m in jnp.split(mods[l, :bsz], 6, axis=-1)]
        ctx_mod = [jnp.broadcast_to(m[None, :, :], (bsz, 1, d)) for m in jnp.split(mods[l, bsz:bsz + 1], 6, axis=-1)]
        sh1, sc1, g1, sh2, sc2, g2 = lat_mod
        csh1, csc1, cg1, csh2, csc2, cg2 = ctx_mod
        n1g = norm1_g[l][None, :]
        n2g = norm2_g[l][None, :]
        w_out_bf = w_out[l].astype(BF16)
        wr = jnp.concatenate([jnp.repeat(moe_w_group[l], EXPERTS_PER_GROUP, axis=1), moe_w_router[l],
                              jnp.zeros((d, LANES - 2 * N_EXPERTS), F32)], axis=1)
        br = jnp.concatenate([jnp.repeat(moe_b_group[l], EXPERTS_PER_GROUP), moe_b_router[l],
                              jnp.zeros((LANES - 2 * N_EXPERTS,), F32)])[None, :]
        wrh = wr.astype(BF16)
        wr2 = jnp.concatenate([wrh, (wr - wrh.astype(F32)).astype(BF16)], axis=1)

        if l % 2 == 0:
            e = l // 2
            w_in = w_in_even[e].astype(BF16)
            qg = jnp.tile(qn_g[e], NA_HEADS)[None, :]
            kg = jnp.tile(kn_g[e], NA_HEADS)[None, :]
            ql, kl, vl, gbl, pl_ = _inproj_even(x, n1g, sc1, sh1, w_in, qg, kg, seg)
            qc, kc, vc, gbc, pc = _inproj_even(ctx, n1g, csc1, csh1, w_in, qg, kg, seg)
            bias = _bias_table(na_rpb[e])
            y1 = _natten(ql, kl, vl, kc, vc, bias)
            y2 = _short_gated_conv(gbl, pl_, sc_conv_w[e])
            lat_cm = (False, False)
            if ctx_needed:
                y1c = _ctx_attention(qc, kc, vc)
                y2c = _short_gated_conv(gbc, pc, sc_conv_w[e])
        else:
            o = l // 2
            wht = w_in_odd[o][:, :3 * HY_WIDTH].T.astype(BF16)
            wag = w_in_odd[o][:, 3 * HY_WIDTH:].astype(BF16)
            hyt, ag = _inproj_odd(x, n1g, sc1, sh1, wht, wag)
            taps = _hyena_taps(seq, hy_w1[o], hy_b1[o], hy_w2[o], hy_b2[o], hy_w3[o], hy_freq[o])
            h_spec = _filter_fft(taps, consts)
            y1 = _hyena_latent(hyt, hy_short_w[o], h_spec, hy_bias[o], consts)
            cf_args = (cf_conv_w[o], cf_conv_b[o], cf_ln_g[o], cf_ln_b[o])
            y2 = _conformer(ag, *cf_args)
            lat_cm = (True, False)
            if ctx_needed:
                hytc, agc = _inproj_odd(ctx, n1g, csc1, csh1, wht, wag)
                taps_c = _hyena_taps(lc, hy_w1[o], hy_b1[o], hy_w2[o], hy_b2[o], hy_w3[o], hy_freq[o])
                y1c = _hyena_context(hytc, hy_short_w[o], taps_c, hy_bias[o])
                y2c = _conformer(agc, *cf_args)

        x1, h2, cmb = _outproj(y1, y2, w_out_bf, x, g1, n2g, sc2, sh2, wr2, br, lat_cm)
        x = _moe(h2, cmb, x1, g2, wg, wu, wd, l)
        if ctx_needed:
            c1, hc2, cmbc = _outproj(y1c, y2c, w_out_bf, ctx, cg1, n2g, csc2, csh2, wr2, br, lat_cm)
            ctx = _moe(hc2, cmbc, c1, cg2, wg, wu, wd, l)
    return x
```

```python
import functools
import math

import numpy as np
import jax
import jax.numpy as jnp
from jax import lax
from jax.experimental import pallas as pl
from jax.experimental.pallas import tpu as pltpu

F32 = jnp.float32
BF16 = jnp.bfloat16

D_MODEL = 1024
GRID_W = 64
NA_HEADS = 8
HEAD_DIM = 64
NA_WIDTH = 512
NA_WIN_H = 8
NA_WIN_W = 16
SC_WIDTH = 512
HY_WIDTH = 512
HY_BANDS = 16
HY_EMB = 1 + 2 * HY_BANDS
HY_FFN = 64
HY_MAX_DECAY = math.log(1e-2) / 0.3
HY_MIN_DECAY = math.log(1e-2) / 1.5
CF_WIDTH = 512
CF_TAPS = 31
N_GROUPS = 4
EXPERTS_PER_GROUP = 4
N_EXPERTS = 16
D_EXPERT = 256
RMS_EPS = 1e-6
LN_EPS = 1e-5
NEG_INF = -1e30

VMEM_LIMIT_BYTES = 56 * 1024 * 1024
LANES = 128

FFT_NA = 64
FFT_NB = 128
FFT_N = FFT_NA * FFT_NB


def _params(*sem):
    return pltpu.CompilerParams(dimension_semantics=tuple(sem), vmem_limit_bytes=VMEM_LIMIT_BYTES)


def _dot(a, b):
    return jnp.dot(a, b, preferred_element_type=F32)


def _dot_nt(a, b):
    return lax.dot_general(a, b, (((1,), (1,)), ((), ())), preferred_element_type=F32)


def _dot_tn(a, b):
    return lax.dot_general(a, b, (((0,), (0,)), ((), ())), preferred_element_type=F32)


def _dot_f32(a, b):
    return jnp.dot(a, b, preferred_element_type=F32, precision=lax.Precision.HIGHEST)


def _silu(x):
    return x * jax.nn.sigmoid(x)


def _ada_kernel(ct_ref, w_ref, b_ref, o_ref, *, n_cond):
    ct = ct_ref[...]
    s = _silu(ct)
    w = w_ref[0]
    rows = [jnp.sum(w * s[:, r:r + 1], axis=0, keepdims=True) for r in range(n_cond)]
    rows.append(jnp.zeros((8 - n_cond, w.shape[1]), F32))
    o_ref[0] = jnp.concatenate(rows, axis=0) + b_ref[0]


def _ada_modulation(cond, ada_w, ada_b):
    n_cond, d = cond.shape
    depth, _, n6 = ada_w.shape
    tn = 1536
    ct = jnp.zeros((d, 8), F32).at[:, :n_cond].set(cond.T)
    return pl.pallas_call(
        functools.partial(_ada_kernel, n_cond=n_cond),
        out_shape=jax.ShapeDtypeStruct((depth, 8, n6), F32),
        grid=(depth, n6 // tn),
        in_specs=[pl.BlockSpec((d, 8), lambda l, j: (0, 0)),
                  pl.BlockSpec((1, d, tn), lambda l, j: (l, 0, j)),
                  pl.BlockSpec((1, 1, tn), lambda l, j: (l, 0, j))],
        out_specs=pl.BlockSpec((1, 8, tn), lambda l, j: (l, 0, j)),
        compiler_params=_params("parallel", "parallel"),
        name="ada_modulation",
    )(ct, ada_w, ada_b.reshape(depth, 1, n6))


def _norm_modulate(x, g, sc, sh):
    ms = jnp.mean(x * x, axis=-1, keepdims=True)
    return x * lax.rsqrt(ms + RMS_EPS) * g * (1.0 + sc) + sh


def _head_rmsnorm(t, seg, gain):
    ss = _dot((t * t).astype(BF16), seg)
    return t * lax.rsqrt(ss * (1.0 / HEAD_DIM) + RMS_EPS) * gain


def _inproj_even_kernel(x_ref, g_ref, sc_ref, sh_ref, w_ref, qg_ref, kg_ref, seg_ref,
                        q_ref, k_ref, v_ref, gb_ref, p_ref):
    u = _norm_modulate(x_ref[0], g_ref[...], sc_ref[0], sh_ref[0]).astype(BF16)
    seg = seg_ref[...]
    w = NA_WIDTH
    q = _dot(u, w_ref[:, 0 * w:1 * w])
    q_ref[0] = (_head_rmsnorm(q, seg, qg_ref[...]) * (HEAD_DIM ** -0.5)).astype(BF16)
    k = _dot(u, w_ref[:, 1 * w:2 * w])
    k_ref[0] = _head_rmsnorm(k, seg, kg_ref[...]).astype(BF16)
    v_ref[0] = _dot(u, w_ref[:, 2 * w:3 * w]).astype(BF16)
    gb_ref[0] = _dot(u, w_ref[:, 3 * w:4 * w]).astype(BF16)
    gc = _dot(u, w_ref[:, 4 * w:5 * w])
    hv = _dot(u, w_ref[:, 5 * w:6 * w])
    p_ref[0] = (gc * hv).astype(BF16)


def _inproj_even(x, g, sc, sh, w_bf, qg, kg, seg):
    b, n, d = x.shape
    tm = min(512, n)
    tok = lambda i, j: (i, j, 0)
    mod = lambda i, j: (i, 0, 0)
    const = lambda i, j: (0, 0)
    out = jax.ShapeDtypeStruct((b, n, NA_WIDTH), BF16)
    return pl.pallas_call(
        _inproj_even_kernel,
        out_shape=(out,) * 5,
        grid=(b, n // tm),
        in_specs=[pl.BlockSpec((1, tm, d), tok),
                  pl.BlockSpec((1, d), const),
                  pl.BlockSpec((1, 1, d), mod),
                  pl.BlockSpec((1, 1, d), mod),
                  pl.BlockSpec(w_bf.shape, const),
                  pl.BlockSpec((1, NA_WIDTH), const),
                  pl.BlockSpec((1, NA_WIDTH), const),
                  pl.BlockSpec((NA_WIDTH, NA_WIDTH), const)],
        out_specs=(pl.BlockSpec((1, tm, NA_WIDTH), tok),) * 5,
        compiler_params=_params("parallel", "parallel"),
        name="inproj_even",
    )(x, g, sc, sh, w_bf, qg, kg, seg)


def _bias_kernel(rpb_ref, o_ref):
    h = pl.program_id(0)
    qi = lax.broadcasted_iota(jnp.int32, (GRID_W, GRID_W), 0)
    ki = lax.broadcasted_iota(jnp.int32, (GRID_W, GRID_W), 1)
    start = jnp.clip(qi - NA_WIN_W // 2, 0, GRID_W - NA_WIN_W)
    valid = jnp.logical_and(ki >= start, ki < start + NA_WIN_W)
    cidx = jnp.clip(ki - qi, -(NA_WIN_W - 1), NA_WIN_W - 1) + (NA_WIN_W - 1)
    n_dr = 2 * NA_WIN_H - 1
    n_dc = 2 * NA_WIN_W - 1

    def body(j, accs):
        m = cidx == j
        return tuple(jnp.where(m, rpb_ref[(h * n_dr + d) * n_dc + j], a) for d, a in enumerate(accs))

    accs = lax.fori_loop(0, n_dc, body, tuple(jnp.zeros((GRID_W, GRID_W), F32) for _ in range(n_dr)))
    tiles = [jnp.where(valid, a, NEG_INF) for a in accs]
    for d0 in range(NA_WIN_H):
        o_ref[0, d0] = jnp.concatenate(tiles[d0:d0 + NA_WIN_H], axis=1)


def _bias_table(rpb):
    return pl.pallas_call(
        _bias_kernel,
        out_shape=jax.ShapeDtypeStruct((NA_HEADS, NA_WIN_H, GRID_W, NA_WIN_H * GRID_W), F32),
        grid=(NA_HEADS,),
        in_specs=[pl.BlockSpec(memory_space=pltpu.SMEM)],
        out_specs=pl.BlockSpec((1, NA_WIN_H, GRID_W, NA_WIN_H * GRID_W), lambda h: (h, 0, 0, 0)),
        compiler_params=_params("arbitrary"),
        name="rpb_bias_table",
    )(rpb.reshape(-1))


def _pair_attention(q2, kw, vw, bias, kc, vc, first_half):
    s_c = _dot_nt(q2, kc)
    m = jnp.max(s_c, axis=-1, keepdims=True)
    if kw is not None:
        s_w = _dot_nt(q2, kw) + bias
        m = jnp.maximum(m, jnp.max(s_w, axis=-1, keepdims=True))
        p_w = jnp.exp(s_w - m)
    p_c = jnp.exp(s_c - m)
    den = jnp.sum(p_c, axis=-1, keepdims=True)
    o = _dot(p_c.astype(BF16), vc)
    if kw is not None:
        den = den + jnp.sum(p_w, axis=-1, keepdims=True)
        o = o + _dot(p_w.astype(BF16), vw)
    o = o / den
    half = o.shape[0] // 2
    return jnp.where(first_half, o[:half], o[half:])


def _stack_heads(qp, first_half):
    zero = jnp.zeros_like(qp)
    return jnp.concatenate([jnp.where(first_half, qp, zero), jnp.where(first_half, zero, qp)], axis=0)


def _natten_kernel(q_ref, k_ref, v_ref, kc_ref, vc_ref, bias_ref, o_ref, *, rows_per_step, n_rows):
    blk = pl.program_id(1)
    lane = lax.broadcasted_iota(jnp.int32, (GRID_W, LANES), 1)
    first_half = lane < HEAD_DIM
    band = NA_WIN_H * GRID_W

    n_pairs = NA_HEADS // 2

    def row_body(j, carry):
        r = blk * rows_per_step + j
        start = jnp.clip(r - NA_WIN_H // 2, 0, n_rows - NA_WIN_H)
        d0 = start - r + (NA_WIN_H - 1)
        koff = pl.multiple_of(start * GRID_W, GRID_W)
        qoff = pl.multiple_of(j * GRID_W, GRID_W)
        cols = [slice(hp * LANES, (hp + 1) * LANES) for hp in range(n_pairs)]
        q2 = [_stack_heads(q_ref[0, pl.ds(qoff, GRID_W), cs], first_half) for cs in cols]
        s_w = [_dot_nt(q2[hp], k_ref[0, pl.ds(koff, band), cols[hp]])
               + jnp.concatenate([bias_ref[2 * hp, d0], bias_ref[2 * hp + 1, d0]], axis=0) for hp in range(n_pairs)]
        s_c = [_dot_nt(q2[hp], kc_ref[0, :, cols[hp]]) for hp in range(n_pairs)]
        m = [jnp.maximum(jnp.max(s_w[hp], axis=-1, keepdims=True), jnp.max(s_c[hp], axis=-1, keepdims=True))
             for hp in range(n_pairs)]
        p_w = [jnp.exp(s_w[hp] - m[hp]) for hp in range(n_pairs)]
        p_c = [jnp.exp(s_c[hp] - m[hp]) for hp in range(n_pairs)]
        den = [jnp.sum(p_w[hp], axis=-1, keepdims=True) + jnp.sum(p_c[hp], axis=-1, keepdims=True)
               for hp in range(n_pairs)]
        outs = []
        for hp in range(n_pairs):
            o = (_dot(p_w[hp].astype(BF16), v_ref[0, pl.ds(koff, band), cols[hp]])
                 + _dot(p_c[hp].astype(BF16), vc_ref[0, :, cols[hp]])) / den[hp]
            outs.append(jnp.where(first_half, o[:GRID_W], o[GRID_W:]))
        o_ref[0, pl.ds(qoff, GRID_W), :] = jnp.concatenate(outs, axis=1).astype(o_ref.dtype)
        return carry

    lax.fori_loop(0, rows_per_step, row_body, 0, unroll=2)


def _natten(q, k, v, kc, vc, bias):
    b, n, w = q.shape
    n_rows = n // GRID_W
    rows_per_step = 8
    tq = rows_per_step * GRID_W
    lc = kc.shape[1]
    return pl.pallas_call(
        functools.partial(_natten_kernel, rows_per_step=rows_per_step, n_rows=n_rows),
        out_shape=jax.ShapeDtypeStruct((b, n, w), BF16),
        grid=(b, n_rows // rows_per_step),
        in_specs=[pl.BlockSpec((1, tq, w), lambda i, j: (i, j, 0)),
                  pl.BlockSpec((1, n, w), lambda i, j: (i, 0, 0)),
                  pl.BlockSpec((1, n, w), lambda i, j: (i, 0, 0)),
                  pl.BlockSpec((1, lc, w), lambda i, j: (i, 0, 0)),
                  pl.BlockSpec((1, lc, w), lambda i, j: (i, 0, 0)),
                  pl.BlockSpec(bias.shape, lambda i, j: (0, 0, 0, 0))],
        out_specs=pl.BlockSpec((1, tq, w), lambda i, j: (i, j, 0)),
        compiler_params=_params("parallel", "arbitrary"),
        name="neighbourhood_attention",
    )(q, k, v, kc, vc, bias)


def _ctx_attn_kernel(q_ref, k_ref, v_ref, o_ref):
    lc = q_ref.shape[1]
    lane = lax.broadcasted_iota(jnp.int32, (lc, LANES), 1)
    first_half = lane < HEAD_DIM
    outs = []
    for hp in range(NA_HEADS // 2):
        cs = slice(hp * LANES, (hp + 1) * LANES)
        q2 = _stack_heads(q_ref[0, :, cs], first_half)
        outs.append(_pair_attention(q2, None, None, None, k_ref[0, :, cs], v_ref[0, :, cs], first_half))
    o_ref[0] = jnp.concatenate(outs, axis=1).astype(o_ref.dtype)


def _ctx_attention(q, k, v):
    b, lc, w = q.shape
    spec = pl.BlockSpec((1, lc, w), lambda i: (i, 0, 0))
    return pl.pallas_call(
        _ctx_attn_kernel,
        out_shape=jax.ShapeDtypeStruct((b, lc, w), BF16),
        grid=(b,),
        in_specs=[spec, spec, spec],
        out_specs=spec,
        compiler_params=_params("parallel"),
        name="context_attention",
    )(q, k, v)


def _sgconv_kernel(gb_ref, p_ref, w_ref, o_ref):
    p = p_ref[0].astype(F32)
    n = p.shape[0]
    row = lax.broadcasted_iota(jnp.int32, p.shape, 0)
    prev = jnp.where(row == 0, 0.0, pltpu.roll(p, 1, axis=0))
    nxt = jnp.where(row == n - 1, 0.0, pltpu.roll(p, n - 1, axis=0))
    y = w_ref[0:1, :] * prev + w_ref[1:2, :] * p + w_ref[2:3, :] * nxt
    o_ref[0] = (gb_ref[0].astype(F32) * y).astype(o_ref.dtype)


def _short_gated_conv(gb, p, w):
    b, n, c = p.shape
    spec = pl.BlockSpec((1, n, LANES), lambda i, j: (i, 0, j))
    return pl.pallas_call(
        _sgconv_kernel,
        out_shape=jax.ShapeDtypeStruct((b, n, c), BF16),
        grid=(b, c // LANES),
        in_specs=[spec, spec, pl.BlockSpec((3, LANES), lambda i, j: (0, j))],
        out_specs=spec,
        compiler_params=_params("parallel", "parallel"),
        name="short_gated_conv",
    )(gb, p, w)


def _inproj_odd_kernel(x_ref, g_ref, sc_ref, sh_ref, wht_ref, wag_ref, hy_ref, ag_ref):
    u = _norm_modulate(x_ref[0], g_ref[...], sc_ref[0], sh_ref[0]).astype(BF16)
    hy_ref[0] = _dot_nt(wht_ref[...], u).astype(BF16)
    a = _dot(u, wag_ref[:, :CF_WIDTH])
    g = _dot(u, wag_ref[:, CF_WIDTH:])
    ag_ref[0] = (a * jax.nn.sigmoid(g)).astype(BF16)


def _inproj_odd(x, g, sc, sh, wht_bf, wag_bf):
    b, n, d = x.shape
    tm = min(512, n)
    hw = wht_bf.shape[0]
    tok = lambda i, j: (i, j, 0)
    mod = lambda i, j: (i, 0, 0)
    const = lambda i, j: (0, 0)
    return pl.pallas_call(
        _inproj_odd_kernel,
        out_shape=(jax.ShapeDtypeStruct((b, hw, n), BF16), jax.ShapeDtypeStruct((b, n, CF_WIDTH), BF16)),
        grid=(b, n // tm),
        in_specs=[pl.BlockSpec((1, tm, d), tok),
                  pl.BlockSpec((1, d), const),
                  pl.BlockSpec((1, 1, d), mod),
                  pl.BlockSpec((1, 1, d), mod),
                  pl.BlockSpec(wht_bf.shape, const),
                  pl.BlockSpec(wag_bf.shape, const)],
        out_specs=(pl.BlockSpec((1, hw, tm), lambda i, j: (i, 0, j)), pl.BlockSpec((1, tm, CF_WIDTH), tok)),
        compiler_params=_params("parallel", "parallel"),
        name="inproj_odd",
    )(x, g, sc, sh, wht_bf, wag_bf)


SUBLANES = 8
CF_PAD = 2 * SUBLANES
CF_ROWS = 128


def _conformer_kernel(ag_ref, w_ref, cb_ref, lg_ref, lb_ref, o_ref, pad_ref, *, seq):
    zeros = jnp.zeros((CF_PAD, CF_WIDTH), F32)
    pad_ref[0:CF_PAD, :] = zeros
    pad_ref[CF_PAD + seq:2 * CF_PAD + seq, :] = zeros
    pad_ref[CF_PAD:CF_PAD + seq, :] = ag_ref[0].astype(F32)
    shift0 = CF_PAD - CF_TAPS // 2
    n_groups = (shift0 + CF_TAPS - 1) // SUBLANES + 1

    def conv_rows(i, carry):
        n0 = pl.multiple_of(i * CF_ROWS, CF_ROWS)
        wins = [pad_ref[pl.ds(n0 + SUBLANES * a, CF_ROWS + SUBLANES), :] for a in range(n_groups)]
        acc = None
        for b in range(SUBLANES):
            part = None
            for a in range(n_groups):
                j = SUBLANES * a + b - shift0
                if 0 <= j < CF_TAPS:
                    term = w_ref[j:j + 1, :] * wins[a]
                    part = term if part is None else part + term
            part = part[b:b + CF_ROWS, :]
            acc = part if acc is None else acc + part
        y = acc + cb_ref[...]
        mu = jnp.mean(y, axis=-1, keepdims=True)
        yc = y - mu
        var = jnp.mean(yc * yc, axis=-1, keepdims=True)
        z = yc * lax.rsqrt(var + LN_EPS) * lg_ref[...] + lb_ref[...]
        o_ref[0, pl.ds(n0, CF_ROWS), :] = _silu(z).astype(o_ref.dtype)
        return carry

    lax.fori_loop(0, seq // CF_ROWS, conv_rows, 0)


def _conformer(ag, w, cb, lg, lb):
    b, n, c = ag.shape
    spec = pl.BlockSpec((1, n, c), lambda i: (i, 0, 0))
    vec = pl.BlockSpec((1, c), lambda i: (0, 0))
    return pl.pallas_call(
        functools.partial(_conformer_kernel, seq=n),
        out_shape=jax.ShapeDtypeStruct((b, n, c), BF16),
        grid=(b,),
        in_specs=[spec, pl.BlockSpec((CF_TAPS, c), lambda i: (0, 0)), vec, vec, vec],
        out_specs=spec,
        scratch_shapes=[pltpu.VMEM((n + 2 * CF_PAD, c), F32)],
        compiler_params=_params("parallel"),
        name="conformer_conv",
    )(ag, w, cb[None, :], lg[None, :], lb[None, :])


def _hyena_features(length):
    t = np.linspace(0.0, 1.0, length, dtype=np.float32)
    w = (2.0 * math.pi * np.arange(length, dtype=np.float32) / length).astype(np.float32)
    bands = np.linspace(1e-4, HY_BANDS - 1, HY_BANDS, dtype=np.float32)
    ang = (bands[:, None] * w[None, :]).astype(np.float32)
    zt = np.concatenate([t[None, :], np.cos(ang), -np.sin(ang)], axis=0).astype(np.float32)
    deltas = np.abs(np.linspace(HY_MIN_DECAY, HY_MAX_DECAY, HY_WIDTH, dtype=np.float32))
    rev = (length - np.arange(length)) % length
    zt2 = np.concatenate([zt, zt[:, rev]], axis=1)
    t2 = np.concatenate([t, t[rev]])[None, :]
    return zt2, t2, deltas[:, None]


def _taps_kernel(zt_ref, t_ref, dl_ref, w1t_ref, b1_ref, f0_ref, w2t_ref, b2_ref, f1_ref, w3t_ref,
                 o_ref, hid_ref):
    first = jnp.logical_and(pl.program_id(0) == 0, pl.program_id(1) == 0)
    length = t_ref.shape[1] // 2

    @pl.when(first)
    def _():
        h1 = jnp.sin(f0_ref[...] * (_dot_f32(w1t_ref[...], zt_ref[...]) + b1_ref[...]))
        hid_ref[...] = jnp.sin(f1_ref[...] * (_dot_f32(w2t_ref[...], h1) + b2_ref[...]))

    decay = jnp.exp(-(dl_ref[...] * t_ref[...]))
    fwd = _dot_f32(w3t_ref[0, 0], hid_ref[:, :length])
    bwd = _dot_f32(w3t_ref[0, 1], hid_ref[:, length:])
    taps = jnp.concatenate([fwd, bwd], axis=1) * decay
    nrm = jnp.sum(jnp.abs(taps), axis=-1, keepdims=True)
    lane = lax.broadcasted_iota(jnp.int32, taps.shape, 1)
    o_ref[0] = jnp.where(lane == length, 0.0, taps / nrm)


def _hyena_taps(length, w1, b1, w2, b2, w3, freq):
    zt2, t2, deltas = _hyena_features(length)
    cb = 128
    w3t = w3.T.reshape(2, 2, HY_WIDTH, HY_FFN)
    col = lambda v: v.reshape(HY_FFN, 1)
    const = lambda o, j: (0, 0)
    return pl.pallas_call(
        _taps_kernel,
        out_shape=jax.ShapeDtypeStruct((2, HY_WIDTH, 2 * length), F32),
        grid=(2, HY_WIDTH // cb),
        in_specs=[pl.BlockSpec((HY_EMB, 2 * length), const),
                  pl.BlockSpec((1, 2 * length), const),
                  pl.BlockSpec((cb, 1), lambda o, j: (j, 0)),
                  pl.BlockSpec((HY_FFN, HY_EMB), const),
                  pl.BlockSpec((HY_FFN, 1), const),
                  pl.BlockSpec((HY_FFN, 1), const),
                  pl.BlockSpec((HY_FFN, HY_FFN), const),
                  pl.BlockSpec((HY_FFN, 1), const),
                  pl.BlockSpec((HY_FFN, 1), const),
                  pl.BlockSpec((1, 2, cb, HY_FFN), lambda o, j: (o, 0, j, 0))],
        out_specs=pl.BlockSpec((1, cb, 2 * length), lambda o, j: (o, j, 0)),
        scratch_shapes=[pltpu.VMEM((HY_FFN, 2 * length), F32)],
        compiler_params=_params("arbitrary", "arbitrary"),
        name="hyena_filter_taps",
    )(jnp.asarray(zt2), jnp.asarray(t2), jnp.asarray(deltas), w1.T, col(b1), col(freq[0]),
      w2.T, col(b2), col(freq[1]), w3t)


def _dft_constants():
    na, nb, n = FFT_NA, FFT_NB, FFT_N
    half = na // 2
    ka = np.arange(na)
    ang_a = 2.0 * np.pi * np.outer(ka, ka) / na
    ca, sa = np.cos(ang_a), np.sin(ang_a)
    fa = np.block([[ca[:half], -sa[:half]], [sa[:half], ca[:half]]])
    fai = np.block([[ca[:, :half], sa[:, :half]], [-sa[:, :half], ca[:, :half]]])
    kb = np.arange(nb)
    ang_b = 2.0 * np.pi * np.outer(kb, kb) / nb
    cbm, sbm = np.cos(ang_b), np.sin(ang_b)
    fb = np.block([[cbm, -sbm], [sbm, cbm]])
    fbi = np.block([[cbm, sbm], [-sbm, cbm]])
    ang_t = 2.0 * np.pi * np.outer(kb, ka) / n
    ct, st = np.cos(ang_t), np.sin(ang_t)
    tw_fc = np.concatenate([ct, ct], axis=1)
    tw_fs = np.concatenate([st, -st], axis=1)
    tw_ic, tw_is = ct.T.copy(), st.T.copy()
    bf = lambda a: jnp.asarray(a, dtype=F32).astype(BF16)
    f32 = lambda a: jnp.asarray(a, dtype=F32)
    fa_real = np.concatenate([ca, -sa], axis=1)
    return dict(fa=bf(fa), fa_real=bf(fa_real), fai=bf(fai), fb=bf(fb), fbi=bf(fbi),
                tw_fc=f32(tw_fc), tw_fs=f32(tw_fs), tw_ic=f32(tw_ic), tw_is=f32(tw_is))


def _fft_forward(zr, zi, fa, tw_fc, tw_fs, fb):
    c, _, nb = zr.shape
    tr = jnp.swapaxes(zr, 1, 2)
    lhs = tr if zi is None else jnp.concatenate([tr, jnp.swapaxes(zi, 1, 2)], axis=2)
    a = _dot(lhs.reshape(c * nb, lhs.shape[2]).astype(BF16), fa).reshape(c, nb, 2 * FFT_NA)
    a = a * tw_fc + pltpu.roll(a, FFT_NA, axis=2) * tw_fs
    t = jnp.swapaxes(a, 1, 2)
    lhs2 = jnp.concatenate([t[:, :FFT_NA, :], t[:, FFT_NA:, :]], axis=2)
    x = _dot(lhs2.reshape(c * FFT_NA, 2 * nb).astype(BF16), fb)
    return x.reshape(c, FFT_NA, 2 * nb)


def _fft_inverse(y, fbi, tw_ic, tw_is, fai):
    c = y.shape[0]
    nb = FFT_NB
    b = _dot(y.reshape(c * FFT_NA, 2 * nb).astype(BF16), fbi).reshape(c, FFT_NA, 2 * nb)
    br, bi = b[:, :, :nb], b[:, :, nb:]
    rr = br * tw_ic - bi * tw_is
    ii = bi * tw_ic + br * tw_is
    t = jnp.swapaxes(jnp.concatenate([rr, ii], axis=1), 1, 2)
    o = _dot(t.reshape(c * nb, 2 * FFT_NA).astype(BF16), fai).reshape(c, nb, FFT_NA)
    o = jnp.swapaxes(o, 1, 2)
    return o[:, :FFT_NA // 2, :], o[:, FFT_NA // 2:, :]


def _filter_fft_kernel(taps_ref, fa_ref, twc_ref, tws_ref, fb_ref, o_ref):
    h = _fft_forward(taps_ref[0], None, fa_ref[...], twc_ref[...], tws_ref[...], fb_ref[...])
    o_ref[0] = h * (1.0 / FFT_N)


def _filter_fft(taps, consts):
    _, c, n = taps.shape
    cb = 32
    taps4 = taps.reshape(2, c, FFT_NA, FFT_NB)
    cm = lambda o, j: (0, 0)
    return pl.pallas_call(
        _filter_fft_kernel,
        out_shape=jax.ShapeDtypeStruct((2, c, FFT_NA, 2 * FFT_NB), F32),
        grid=(2, c // cb),
        in_specs=[pl.BlockSpec((1, cb, FFT_NA, FFT_NB), lambda o, j: (o, j, 0, 0)),
                  pl.BlockSpec(consts["fa_real"].shape, cm),
                  pl.BlockSpec(consts["tw_fc"].shape, cm),
                  pl.BlockSpec(consts["tw_fs"].shape, cm),
                  pl.BlockSpec(consts["fb"].shape, cm)],
        out_specs=pl.BlockSpec((1, cb, FFT_NA, 2 * FFT_NB), lambda o, j: (o, j, 0, 0)),
        compiler_params=_params("parallel", "parallel"),
        name="hyena_filter_fft",
    )(taps4, consts["fa_real"], consts["tw_fc"], consts["tw_fs"], consts["fb"])


def _shift_tokens(a, direction):
    rows = a.shape[-2]
    lane = lax.broadcasted_iota(jnp.int32, a.shape, a.ndim - 1)
    row = lax.broadcasted_iota(jnp.int32, a.shape, a.ndim - 2)
    if direction == 1:
        l = pltpu.roll(a, 1, axis=a.ndim - 1)
        ls = pltpu.roll(l, 1, axis=a.ndim - 2)
        out = jnp.where(lane == 0, ls, l)
        edge = jnp.logical_and(lane == 0, row == 0)
    else:
        l = pltpu.roll(a, LANES - 1, axis=a.ndim - 1)
        ls = pltpu.roll(l, rows - 1, axis=a.ndim - 2)
        out = jnp.where(lane == LANES - 1, ls, l)
        edge = jnp.logical_and(lane == LANES - 1, row == rows - 1)
    return jnp.where(edge, 0.0, out)


def _short_conv3(a, w_ref):
    return w_ref[0] * _shift_tokens(a, 1) + w_ref[1] * a + w_ref[2] * _shift_tokens(a, -1)


def _hyena_kernel(v_ref, x1_ref, x2_ref, wv_ref, w1_ref, w2_ref, h_ref, hb_ref,
                  fa_ref, twfc_ref, twfs_ref, fb_ref, fbi_ref, twic_ref, twis_ref, fai_ref, o_ref):
    fwd_c = (fa_ref[...], twfc_ref[...], twfs_ref[...], fb_ref[...])
    inv_c = (fbi_ref[...], twic_ref[...], twis_ref[...], fai_ref[...])
    z = _short_conv3(v_ref[...].astype(F32), wv_ref)
    zr, zi = z[0], z[1]
    nb = FFT_NB
    for o, (g_ref, gw_ref) in enumerate(((x1_ref, w1_ref), (x2_ref, w2_ref))):
        x = _fft_forward(zr, zi, *fwd_c)
        h = h_ref[o]
        xr, xi, hr, hi = x[:, :, :nb], x[:, :, nb:], h[:, :, :nb], h[:, :, nb:]
        y = jnp.concatenate([xr * hr - xi * hi, xr * hi + xi * hr], axis=2)
        yr, yi = _fft_inverse(y, *inv_c)
        gate = _short_conv3(g_ref[...].astype(F32), gw_ref)
        bias = hb_ref[o]
        zr = gate[0] * (yr + zr * bias)
        zi = gate[1] * (yi + zi * bias)
    o_ref[0] = zr.astype(o_ref.dtype)
    o_ref[1] = zi.astype(o_ref.dtype)


def _hyena_latent(hyt, short_w, h_spec, hy_bias, consts):
    b, c3, length = hyt.shape
    c = c3 // 3
    rows = length // FFT_NB
    cb = 32
    nblk = c // cb
    hy4 = hyt.reshape(b, c3, rows, FFT_NB)
    w4 = jnp.broadcast_to(short_w.reshape(3, c3, 1, 1), (3, c3, 1, FFT_NB))
    hb4 = jnp.broadcast_to(hy_bias.reshape(2, c, 1, 1), (2, c, 1, FFT_NB))
    sig = lambda g: pl.BlockSpec((2, cb, rows, FFT_NB), lambda j, p: (p, g * nblk + j, 0, 0))
    wsp = lambda g: pl.BlockSpec((3, cb, 1, FFT_NB), lambda j, p: (0, g * nblk + j, 0, 0))
    cm = lambda j, p: (0, 0)
    names = ("fa", "tw_fc", "tw_fs", "fb", "fbi", "tw_ic", "tw_is", "fai")
    out = pl.pallas_call(
        _hyena_kernel,
        out_shape=jax.ShapeDtypeStruct((b, c, rows, FFT_NB), BF16),
        grid=(nblk, b // 2),
        in_specs=[sig(0), sig(1), sig(2), wsp(0), wsp(1), wsp(2),
                  pl.BlockSpec((2, cb, FFT_NA, 2 * FFT_NB), lambda j, p: (0, j, 0, 0)),
                  pl.BlockSpec((2, cb, 1, FFT_NB), lambda j, p: (0, j, 0, 0))]
                 + [pl.BlockSpec(consts[k].shape, cm) for k in names],
        out_specs=pl.BlockSpec((2, cb, rows, FFT_NB), lambda j, p: (p, j, 0, 0)),
        compiler_params=_params("parallel", "arbitrary"),
        name="hyena_long_conv",
    )(hy4, hy4, hy4, w4, w4, w4, h_spec, hb4, *[consts[k] for k in names])
    return out.reshape(b, c, length)


def _dense_dft_constants(length):
    n = 2 * length
    k = np.arange(n)
    ang = 2.0 * np.pi * np.outer(k, k) / n
    fwd = np.concatenate([np.cos(ang), -np.sin(ang)], axis=1)
    inv = np.concatenate([np.cos(ang[:length]).T, -np.sin(ang[:length]).T], axis=0) / n
    bf = lambda a: jnp.asarray(a, dtype=F32).astype(BF16)
    return bf(fwd), bf(inv)


def _shift_lanes(a, direction):
    n = a.shape[-1]
    lane = lax.broadcasted_iota(jnp.int32, a.shape, a.ndim - 1)
    if direction == 1:
        return jnp.where(lane == 0, 0.0, pltpu.roll(a, 1, axis=a.ndim - 1))
    return jnp.where(lane == n - 1, 0.0, pltpu.roll(a, n - 1, axis=a.ndim - 1))


def _hyena_ctx_kernel(v_ref, x1_ref, x2_ref, wv_ref, w1_ref, w2_ref, taps_ref, hb_ref, fwd_ref, inv_ref, o_ref):
    bsz, cb, length = v_ref.shape
    n = 2 * length
    fwd_m, inv_m = fwd_ref[0:length, :], inv_ref[...]

    def conv3(ref, w_ref):
        a = ref[...].astype(F32)
        return w_ref[0] * _shift_lanes(a, 1) + w_ref[1] * a + w_ref[2] * _shift_lanes(a, -1)

    z = conv3(v_ref, wv_ref)
    for o, (g_ref, gw_ref) in enumerate(((x1_ref, w1_ref), (x2_ref, w2_ref))):
        h = _dot(taps_ref[o].astype(BF16), fwd_ref[...])
        hr, hi = h[:, :n], h[:, n:]
        x = _dot(z.reshape(bsz * cb, length).astype(BF16), fwd_m).reshape(bsz, cb, 2 * n)
        xr, xi = x[:, :, :n], x[:, :, n:]
        y = jnp.concatenate([xr * hr - xi * hi, xr * hi + xi * hr], axis=2)
        yt = _dot(y.reshape(bsz * cb, 2 * n).astype(BF16), inv_m).reshape(bsz, cb, length)
        z = conv3(g_ref, gw_ref) * (yt + z * hb_ref[o])
    o_ref[...] = z.astype(o_ref.dtype)


def _hyena_context(hyt, short_w, taps, hy_bias):
    b, c3, length = hyt.shape
    c = c3 // 3
    cb = 128
    nblk = c // cb
    fwd_m, inv_m = _dense_dft_constants(length)
    w4 = jnp.broadcast_to(short_w.reshape(3, c3, 1), (3, c3, length))
    hb = jnp.broadcast_to(hy_bias.reshape(2, c, 1), (2, c, length))
    sig = lambda g: pl.BlockSpec((b, cb, length), lambda j: (0, g * nblk + j, 0))
    wsp = lambda g: pl.BlockSpec((3, cb, length), lambda j: (0, g * nblk + j, 0))
    return pl.pallas_call(
        _hyena_ctx_kernel,
        out_shape=jax.ShapeDtypeStruct((b, c, length), BF16),
        grid=(nblk,),
        in_specs=[sig(0), sig(1), sig(2), wsp(0), wsp(1), wsp(2),
                  pl.BlockSpec((2, cb, 2 * length), lambda j: (0, j, 0)),
                  pl.BlockSpec((2, cb, length), lambda j: (0, j, 0)),
                  pl.BlockSpec(fwd_m.shape, lambda j: (0, 0)),
                  pl.BlockSpec(inv_m.shape, lambda j: (0, 0))],
        out_specs=pl.BlockSpec((b, cb, length), lambda j: (0, j, 0)),
        compiler_params=_params("parallel"),
        name="hyena_context_conv",
    )(hyt, hyt, hyt, w4, w4, w4, taps, hb, fwd_m, inv_m)


def _route(gl, el):
    row = lax.broadcasted_iota(jnp.int32, gl.shape, 0).astype(F32)
    grp = jnp.floor(row * (1.0 / EXPERTS_PER_GROUP))
    big = float(N_EXPERTS)
    gmax = jnp.max(gl, axis=0, keepdims=True)
    gidx = jnp.min(jnp.where(gl == gmax, grp, big), axis=0, keepdims=True)
    gsum = jnp.sum(jnp.exp(gl - gmax), axis=0, keepdims=True) * (1.0 / EXPERTS_PER_GROUP)
    g_w = 1.0 / gsum
    em = jnp.where(grp == gidx, el, NEG_INF)
    t1 = jnp.max(em, axis=0, keepdims=True)
    i1 = jnp.min(jnp.where(em == t1, row, big), axis=0, keepdims=True)
    em2 = jnp.where(row == i1, 2.0 * NEG_INF, em)
    t2 = jnp.max(em2, axis=0, keepdims=True)
    i2 = jnp.min(jnp.where(em2 == t2, row, big), axis=0, keepdims=True)
    e2 = jnp.exp(t2 - t1)
    den = 1.0 + e2
    w1 = g_w / den
    w2 = g_w * e2 / den
    return jnp.where(row == i1, w1, 0.0) + jnp.where(row == i2, w2, 0.0), gidx


SLOT_LANE = N_EXPERTS
CHUNKS_LANE = N_EXPERTS + 1
MOE_TILE = 512
MOE_CHUNK = 64
MOE_TILE_CHUNKS = MOE_TILE // MOE_CHUNK + N_GROUPS
MOE_SORTED_ROWS = MOE_TILE_CHUNKS * MOE_CHUNK


def _dispatch_slots(gidx, tri):
    tm = gidx.shape[1]
    sub = lax.broadcasted_iota(jnp.int32, (SUBLANES, tm), 0)
    grp = sub.astype(F32)
    member = jnp.where(grp == gidx, 1.0, 0.0)
    rank = _dot(member.astype(BF16), tri)
    count = jnp.sum(member, axis=1, keepdims=True)
    chunks = jnp.floor((count + float(MOE_CHUNK - 1)) * (1.0 / MOE_CHUNK))
    first = jnp.zeros_like(chunks)
    for g in range(1, N_GROUPS):
        first = first + jnp.where(sub[:, 0:1] >= g, chunks[g - 1:g, :], 0.0)
    slot = jnp.sum(member * (first * float(MOE_CHUNK) + rank - 1.0), axis=0, keepdims=True)
    out = jnp.where(sub == 0, slot, 0.0)
    for g in range(N_GROUPS):
        out = out + jnp.where(sub == g + 1, chunks[g:g + 1, :], 0.0)
    return out


def _outproj_kernel(y1_ref, y2_ref, w_ref, x_ref, g1_ref, n2_ref, sc2_ref, sh2_ref, wr_ref, br_ref, tri_ref,
                    xo_ref, h_ref, cmb_ref, *, channel_major):
    half = w_ref.shape[0] // 2
    dots = [_dot_tn if cm else _dot for cm in channel_major]
    acc = dots[0](y1_ref[0], w_ref[:half, :]) + dots[1](y2_ref[0], w_ref[half:, :])
    xn = x_ref[0] + g1_ref[0] * acc
    xo_ref[0] = xn
    h = _norm_modulate(xn, n2_ref[...], sc2_ref[0], sh2_ref[0])
    hi = h.astype(BF16)
    lo = (h - hi.astype(F32)).astype(BF16)
    h_ref[0] = hi
    p = _dot(hi, wr_ref[...])
    logits = p[:, :LANES] + p[:, LANES:] + _dot(lo, wr_ref[:, :LANES]) + br_ref[...]
    lt = logits.T
    cmb, gidx = _route(lt[0:N_EXPERTS], lt[N_EXPERTS:2 * N_EXPERTS])
    pad = jnp.zeros((LANES - N_EXPERTS - SUBLANES, cmb.shape[1]), F32)
    cmb_ref[0] = jnp.concatenate([cmb, _dispatch_slots(gidx, tri_ref[...]), pad], axis=0).T


def _outproj(y1, y2, w_bf, x, g1, n2g, sc2, sh2, wr2, br, channel_major):
    b, n, d = x.shape
    tm = min(512, n)
    half = d // 2
    tok = lambda i, j: (i, j, 0)
    mod = lambda i, j: (i, 0, 0)
    const = lambda i, j: (0, 0)
    y_specs = [pl.BlockSpec((1, half, tm), lambda i, j: (i, 0, j)) if cm else pl.BlockSpec((1, tm, half), tok)
               for cm in channel_major]
    return pl.pallas_call(
        functools.partial(_outproj_kernel, channel_major=tuple(channel_major)),
        out_shape=(jax.ShapeDtypeStruct((b, n, d), F32), jax.ShapeDtypeStruct((b, n, d), BF16),
                   jax.ShapeDtypeStruct((b, n, LANES), F32)),
        grid=(b, n // tm),
        in_specs=[y_specs[0], y_specs[1],
                  pl.BlockSpec((d, d), const),
                  pl.BlockSpec((1, tm, d), tok),
                  pl.BlockSpec((1, 1, d), mod),
                  pl.BlockSpec((1, d), const),
                  pl.BlockSpec((1, 1, d), mod),
                  pl.BlockSpec((1, 1, d), mod),
                  pl.BlockSpec(wr2.shape, const),
                  pl.BlockSpec(br.shape, const),
                  pl.BlockSpec((tm, tm), const)],
        out_specs=(pl.BlockSpec((1, tm, d), tok), pl.BlockSpec((1, tm, d), tok),
                   pl.BlockSpec((1, tm, LANES), tok)),
        compiler_params=_params("parallel", "parallel"),
        name="outproj_router",
    )(y1, y2, w_bf, x, g1, n2g, sc2, sh2, wr2, br, jnp.asarray(np.triu(np.ones((tm, tm), np.float32)), dtype=BF16))


def _cast_kernel(w_ref, o_ref):
    o_ref[...] = w_ref[...].astype(o_ref.dtype)


def _to_bf16(w):
    e, r, c = w.shape
    spec = pl.BlockSpec((2, r, c), lambda i: (i, 0, 0))
    return pl.pallas_call(
        _cast_kernel,
        out_shape=jax.ShapeDtypeStruct(w.shape, BF16),
        grid=(e // 2,),
        in_specs=[spec],
        out_specs=spec,
        compiler_params=_params("parallel"),
        name="expert_weights_bf16",
    )(w)


def _swiglu_group(h, w_rows, j, wg_ref, wu_ref, wd_ref):
    lane = lax.broadcasted_iota(jnp.int32, w_rows.shape, 1)
    experts = range(EXPERTS_PER_GROUP)
    w_e = [jnp.sum(jnp.where(lane == j * EXPERTS_PER_GROUP + e, w_rows, 0.0), axis=1, keepdims=True) for e in experts]
    a = [_dot(h, wg_ref[e]) for e in experts]
    u = [_dot(h, wu_ref[e]) for e in experts]
    act = [(_silu(a[e]) * u[e] * w_e[e]).astype(BF16) for e in experts]
    out = _dot(act[0], wd_ref[0])
    for e in experts[1:]:
        out = out + _dot(act[e], wd_ref[e])
    return out


def _moe_kernel(h_ref, cmb_ref, x_ref, g2_ref, wg_ref, wu_ref, wd_ref, o_ref):
    j = pl.program_id(2)
    contrib = _swiglu_group(h_ref[0], cmb_ref[0], j, wg_ref, wu_ref, wd_ref)

    @pl.when(j == 0)
    def _():
        o_ref[0] = contrib

    @pl.when(jnp.logical_and(j > 0, j < N_GROUPS - 1))
    def _():
        o_ref[0] += contrib

    @pl.when(j == N_GROUPS - 1)
    def _():
        o_ref[0] = x_ref[0] + g2_ref[0] * (o_ref[0] + contrib)


def _moe(h, cmb, x, g2, wg, wu, wd, layer):
    b, n, d = x.shape
    tm = min(MOE_TILE, n)
    tok = lambda i, t, j: (i, t, 0)
    epg = EXPERTS_PER_GROUP
    wmap = lambda i, t, j: (layer * N_GROUPS + j, 0, 0)
    return pl.pallas_call(
        _moe_kernel,
        out_shape=jax.ShapeDtypeStruct((b, n, d), F32),
        grid=(b, n // tm, N_GROUPS),
        in_specs=[pl.BlockSpec((1, tm, d), tok),
                  pl.BlockSpec((1, tm, LANES), tok),
                  pl.BlockSpec((1, tm, d), tok),
                  pl.BlockSpec((1, 1, d), lambda i, t, j: (i, 0, 0)),
                  pl.BlockSpec((epg, d, D_EXPERT), wmap),
                  pl.BlockSpec((epg, d, D_EXPERT), wmap),
                  pl.BlockSpec((epg, D_EXPERT, d), wmap)],
        out_specs=pl.BlockSpec((1, tm, d), tok),
        compiler_params=_params("parallel", "parallel", "arbitrary"),
        name="moe_experts",
    )(h, cmb, x, g2, wg, wu, wd)


MOE_STEP_CHUNKS = 8


def _slot_onehot(cmb):
    lane = lax.broadcasted_iota(jnp.int32, cmb.shape, 1)
    slot = jnp.sum(jnp.where(lane == SLOT_LANE, cmb, 0.0), axis=1, keepdims=True)
    rows = lax.broadcasted_iota(jnp.int32, (cmb.shape[0], MOE_SORTED_ROWS), 1).astype(F32)
    return jnp.where(slot == rows, 1.0, 0.0)


def _moe_dispatch_kernel(h_ref, cmb_ref, hs_ref, cs_ref):
    cmb = cmb_ref[0]
    gather = _slot_onehot(cmb).T.astype(BF16)
    hs = _dot(gather, h_ref[0]).astype(BF16)
    c_hi = cmb.astype(BF16)
    c_lo = (cmb - c_hi.astype(F32)).astype(BF16)
    cs = _dot(gather, c_hi) + _dot(gather, c_lo)
    for k in range(MOE_TILE_CHUNKS):
        rows = slice(k * MOE_CHUNK, (k + 1) * MOE_CHUNK)
        hs_ref[k] = hs[rows]
        cs_ref[k] = cs[rows]


def _moe_dispatch(h, cmb):
    b, n, d = h.shape
    nt = n // MOE_TILE
    tok = lambda i, t: (i, t, 0)
    srt = lambda i, t: (i * nt + t, 0, 0)
    nchunks = b * nt * MOE_TILE_CHUNKS
    return pl.pallas_call(
        _moe_dispatch_kernel,
        out_shape=(jax.ShapeDtypeStruct((nchunks, MOE_CHUNK, d), BF16),
                   jax.ShapeDtypeStruct((nchunks, MOE_CHUNK, LANES), F32)),
        grid=(b, nt),
        in_specs=[pl.BlockSpec((1, MOE_TILE, d), tok), pl.BlockSpec((1, MOE_TILE, LANES), tok)],
        out_specs=(pl.BlockSpec((MOE_TILE_CHUNKS, MOE_CHUNK, d), srt),
                   pl.BlockSpec((MOE_TILE_CHUNKS, MOE_CHUNK, LANES), srt)),
        compiler_params=_params("parallel", "parallel"),
        name="moe_dispatch",
    )(h, cmb)


def _moe_sorted_kernel(group_ref, used_ref, src_ref, *refs):
    n = MOE_STEP_CHUNKS
    hs_refs, cs_refs = refs[:n], refs[n:2 * n]
    wg_ref, wu_ref, wd_ref, ys_ref = refs[2 * n:]
    s = pl.program_id(0)

    @pl.when(used_ref[s] > 0)
    def _():
        h = jnp.concatenate([r[0] for r in hs_refs], axis=0)
        w_rows = jnp.concatenate([r[0] for r in cs_refs], axis=0)
        y = _swiglu_group(h, w_rows, group_ref[s], wg_ref, wu_ref, wd_ref).astype(BF16)
        for k in range(n):
            ys_ref[k] = y[k * MOE_CHUNK:(k + 1) * MOE_CHUNK]

    @pl.when(used_ref[s] == 0)
    def _():
        ys_ref[...] = jnp.zeros(ys_ref.shape, ys_ref.dtype)


def _moe_sorted(hs, cs, step_group, step_used, chunk_src, wg, wu, wd, layer):
    _, _, d = hs.shape
    n = MOE_STEP_CHUNKS
    steps = step_group.shape[0]
    epg = EXPERTS_PER_GROUP
    chunk = lambda k, width: pl.BlockSpec((1, MOE_CHUNK, width), lambda s, grp, used, src: (src[s * n + k], 0, 0))
    wmap = lambda s, grp, used, src: (layer * N_GROUPS + grp[s], 0, 0)
    return pl.pallas_call(
        _moe_sorted_kernel,
        out_shape=jax.ShapeDtypeStruct((steps * n, MOE_CHUNK, d), BF16),
        grid_spec=pltpu.PrefetchScalarGridSpec(
            num_scalar_prefetch=3,
            grid=(steps,),
            in_specs=[chunk(k, d) for k in range(n)] + [chunk(k, LANES) for k in range(n)]
                     + [pl.BlockSpec((epg, d, D_EXPERT), wmap),
                        pl.BlockSpec((epg, d, D_EXPERT), wmap),
                        pl.BlockSpec((epg, D_EXPERT, d), wmap)],
            out_specs=pl.BlockSpec((n, MOE_CHUNK, d), lambda s, grp, used, src: (s, 0, 0))),
        compiler_params=_params("arbitrary"),
        name="moe_sorted_experts",
    )(step_group, step_used, chunk_src, *([hs] * n), *([cs] * n), wg, wu, wd)


def _moe_combine_kernel(pos_ref, *refs):
    ys_refs = refs[:MOE_TILE_CHUNKS]
    cmb_ref, x_ref, g2_ref, o_ref = refs[MOE_TILE_CHUNKS:]
    scatter = _slot_onehot(cmb_ref[0]).astype(BF16)
    ys = jnp.concatenate([r[0] for r in ys_refs], axis=0)
    o_ref[0] = x_ref[0] + g2_ref[0] * _dot(scatter, ys)


def _moe_combine(ys, chunk_pos, cmb, x, g2):
    b, n, d = x.shape
    nt = n // MOE_TILE
    ntc = MOE_TILE_CHUNKS
    tok = lambda i, t, pos: (i, t, 0)
    chunk = lambda k: pl.BlockSpec((1, MOE_CHUNK, d), lambda i, t, pos: (pos[(i * nt + t) * ntc + k], 0, 0))
    return pl.pallas_call(
        _moe_combine_kernel,
        out_shape=jax.ShapeDtypeStruct((b, n, d), F32),
        grid_spec=pltpu.PrefetchScalarGridSpec(
            num_scalar_prefetch=1,
            grid=(b, nt),
            in_specs=[chunk(k) for k in range(ntc)]
                     + [pl.BlockSpec((1, MOE_TILE, LANES), tok),
                        pl.BlockSpec((1, MOE_TILE, d), tok),
                        pl.BlockSpec((1, 1, d), lambda i, t, pos: (i, 0, 0))],
            out_specs=pl.BlockSpec((1, MOE_TILE, d), tok)),
        compiler_params=_params("parallel", "parallel"),
        name="moe_combine",
    )(chunk_pos, *([ys] * ntc), cmb, x, g2)


def _moe_chunk_schedule(cmb):
    b, n, _ = cmb.shape
    nt = b * (n // MOE_TILE)
    ntc, nsc = MOE_TILE_CHUNKS, MOE_STEP_CHUNKS
    steps = nt * ntc // nsc + N_GROUPS
    cnt = cmb[:, ::MOE_TILE, CHUNKS_LANE:CHUNKS_LANE + N_GROUPS].reshape(nt, N_GROUPS).astype(jnp.int32)
    in_tile = jnp.cumsum(cnt, axis=1) - cnt
    before = jnp.cumsum(cnt, axis=0) - cnt
    total = jnp.sum(cnt, axis=0)
    padded = (total + nsc - 1) // nsc * nsc
    gstart = jnp.cumsum(padded) - padded
    k = jnp.arange(ntc, dtype=jnp.int32)
    c = jnp.arange(MOE_TILE // MOE_CHUNK, dtype=jnp.int32)
    dst = gstart[None, :, None] + before[:, :, None] + c[None, None, :]
    src = (jnp.arange(nt, dtype=jnp.int32) * ntc)[:, None, None] + in_tile[:, :, None] + c[None, None, :]
    valid = c[None, None, :] < cnt[:, :, None]
    chunk_src = jnp.zeros((steps * nsc,), jnp.int32).at[jnp.where(valid, dst, steps * nsc).reshape(-1)].set(
        src.reshape(-1), mode="drop")
    first_chunk = jnp.arange(steps, dtype=jnp.int32) * nsc
    step_group = jnp.sum((first_chunk[:, None] >= (gstart + padded)[None, :-1]).astype(jnp.int32), axis=1)
    step_used = (first_chunk < gstart[-1] + padded[-1]).astype(jnp.int32)
    ends = in_tile + cnt
    grp_k = jnp.minimum(jnp.sum((k[None, :, None] >= ends[:, None, :]).astype(jnp.int32), axis=2), N_GROUPS - 1)
    take = lambda a: jnp.take_along_axis(a, grp_k, axis=1)
    pos = gstart[grp_k] + take(before) + (k[None, :] - take(in_tile))
    chunk_pos = jnp.where(k[None, :] < ends[:, -1:], pos, 0).reshape(-1).astype(jnp.int32)
    return step_group, step_used, chunk_src, chunk_pos


def _moe_latent(h, cmb, x, g2, wg, wu, wd, layer):
    step_group, step_used, chunk_src, chunk_pos = _moe_chunk_schedule(cmb)
    hs, cs = _moe_dispatch(h, cmb)
    ys = _moe_sorted(hs, cs, step_group, step_used, chunk_src, wg, wu, wd, layer)
    return _moe_combine(ys, chunk_pos, cmb, x, g2)


def kernel(x, c, ctx, c_ctx, ada_w, ada_b, norm1_g, norm2_g, w_in_even, qn_g, kn_g, na_rpb, sc_conv_w, w_in_odd, hy_short_w, hy_w1, hy_b1, hy_w2, hy_b2, hy_w3, hy_freq, hy_bias, cf_conv_w, cf_conv_b, cf_ln_g, cf_ln_b, w_out, moe_w_group, moe_b_group, moe_w_router, moe_b_router, moe_w_gate, moe_w_up, moe_w_down):
    depth = ada_w.shape[0]
    bsz, seq, d = x.shape
    lc = ctx.shape[1]
    assert 2 * seq == FFT_N and d == D_MODEL and bsz % 2 == 0

    mods = _ada_modulation(jnp.concatenate([c, c_ctx[None, :]], axis=0), ada_w, ada_b)
    seg = jnp.asarray(np.kron(np.eye(NA_HEADS), np.ones((HEAD_DIM, HEAD_DIM))), dtype=BF16)
    consts = _dft_constants()
    wg = _to_bf16(moe_w_gate.reshape((-1,) + moe_w_gate.shape[2:]))
    wu = _to_bf16(moe_w_up.reshape((-1,) + moe_w_up.shape[2:]))
    wd = _to_bf16(moe_w_down.reshape((-1,) + moe_w_down.shape[2:]))

    for l in range(depth):
        ctx_needed = any(j % 2 == 0 for j in range(l + 1, depth))
        lat_mod = [m[:, None, :] for m in jnp.split(mods[l, :bsz], 6, axis=-1)]
        ctx_mod = [jnp.broadcast_to(m[None, :, :], (bsz, 1, d)) for m in jnp.split(mods[l, bsz:bsz + 1], 6, axis=-1)]
        sh1, sc1, g1, sh2, sc2, g2 = lat_mod
        csh1, csc1, cg1, csh2, csc2, cg2 = ctx_mod
        n1g = norm1_g[l][None, :]
        n2g = norm2_g[l][None, :]
        w_out_bf = w_out[l].astype(BF16)
        wr = jnp.concatenate([jnp.repeat(moe_w_group[l], EXPERTS_PER_GROUP, axis=1), moe_w_router[l],
                              jnp.zeros((d, LANES - 2 * N_EXPERTS), F32)], axis=1)
        br = jnp.concatenate([jnp.repeat(moe_b_group[l], EXPERTS_PER_GROUP), moe_b_router[l],
                              jnp.zeros((LANES - 2 * N_EXPERTS,), F32)])[None, :]
        wrh = wr.astype(BF16)
        wr2 = jnp.concatenate([wrh, (wr - wrh.astype(F32)).astype(BF16)], axis=1)

        if l % 2 == 0:
            e = l // 2
            w_in = w_in_even[e].astype(BF16)
            qg = jnp.tile(qn_g[e], NA_HEADS)[None, :]
            kg = jnp.tile(kn_g[e], NA_HEADS)[None, :]
            ql, kl, vl, gbl, pl_ = _inproj_even(x, n1g, sc1, sh1, w_in, qg, kg, seg)
            qc, kc, vc, gbc, pc = _inproj_even(ctx, n1g, csc1, csh1, w_in, qg, kg, seg)
            bias = _bias_table(na_rpb[e])
            y1 = _natten(ql, kl, vl, kc, vc, bias)
            y2 = _short_gated_conv(gbl, pl_, sc_conv_w[e])
            lat_cm = (False, False)
            if ctx_needed:
                y1c = _ctx_attention(qc, kc, vc)
                y2c = _short_gated_conv(gbc, pc, sc_conv_w[e])
        else:
            o = l // 2
            wht = w_in_odd[o][:, :3 * HY_WIDTH].T.astype(BF16)
            wag = w_in_odd[o][:, 3 * HY_WIDTH:].astype(BF16)
            hyt, ag = _inproj_odd(x, n1g, sc1, sh1, wht, wag)
            taps = _hyena_taps(seq, hy_w1[o], hy_b1[o], hy_w2[o], hy_b2[o], hy_w3[o], hy_freq[o])
            h_spec = _filter_fft(taps, consts)
            y1 = _hyena_latent(hyt, hy_short_w[o], h_spec, hy_bias[o], consts)
            cf_args = (cf_conv_w[o], cf_conv_b[o], cf_ln_g[o], cf_ln_b[o])
            y2 = _conformer(ag, *cf_args)
            lat_cm = (True, False)
            if ctx_needed:
                hytc, agc = _inproj_odd(ctx, n1g, csc1, csh1, wht, wag)
                taps_c = _hyena_taps(lc, hy_w1[o], hy_b1[o], hy_w2[o], hy_b2[o], hy_w3[o], hy_freq[o])
                y1c = _hyena_context(hytc, hy_short_w[o], taps_c, hy_bias[o])
                y2c = _conformer(agc, *cf_args)

        x1, h2, cmb = _outproj(y1, y2, w_out_bf, x, g1, n2g, sc2, sh2, wr2, br, lat_cm)
        x = _moe_latent(h2, cmb, x1, g2, wg, wu, wd, l)
        if ctx_needed:
            c1, hc2, cmbc = _outproj(y1c, y2c, w_out_bf, ctx, cg1, n2g, csc2, csh2, wr2, br, lat_cm)
            ctx = _moe(hc2, cmbc, c1, cg2, wg, wu, wd, l)
    return x
```

```python
import functools
import math

import numpy as np
import jax
import jax.numpy as jnp
from jax import lax
from jax.experimental import pallas as pl
from jax.experimental.pallas import tpu as pltpu

F32 = jnp.float32
BF16 = jnp.bfloat16

D_MODEL = 1024
GRID_W = 64
NA_HEADS = 8
HEAD_DIM = 64
NA_WIDTH = 512
NA_WIN_H = 8
NA_WIN_W = 16
SC_WIDTH = 512
HY_WIDTH = 512
HY_BANDS = 16
HY_EMB = 1 + 2 * HY_BANDS
HY_FFN = 64
HY_MAX_DECAY = math.log(1e-2) / 0.3
HY_MIN_DECAY = math.log(1e-2) / 1.5
CF_WIDTH = 512
CF_TAPS = 31
N_GROUPS = 4
EXPERTS_PER_GROUP = 4
N_EXPERTS = 16
D_EXPERT = 256
RMS_EPS = 1e-6
LN_EPS = 1e-5
NEG_INF = -1e30

VMEM_LIMIT_BYTES = 56 * 1024 * 1024
LANES = 128

FFT_NA = 64
FFT_NB = 128
FFT_N = FFT_NA * FFT_NB


def _params(*sem):
    return pltpu.CompilerParams(dimension_semantics=tuple(sem), vmem_limit_bytes=VMEM_LIMIT_BYTES)


def _dot(a, b):
    return jnp.dot(a, b, preferred_element_type=F32)


def _dot_nt(a, b):
    return lax.dot_general(a, b, (((1,), (1,)), ((), ())), preferred_element_type=F32)


def _dot_tn(a, b):
    return lax.dot_general(a, b, (((0,), (0,)), ((), ())), preferred_element_type=F32)


def _dot_f32(a, b):
    return jnp.dot(a, b, preferred_element_type=F32, precision=lax.Precision.HIGHEST)


def _silu(x):
    return x * jax.nn.sigmoid(x)


def _ada_kernel(ct_ref, w_ref, b_ref, o_ref, *, n_cond):
    ct = ct_ref[...]
    s = _silu(ct)
    w = w_ref[0]
    rows = [jnp.sum(w * s[:, r:r + 1], axis=0, keepdims=True) for r in range(n_cond)]
    rows.append(jnp.zeros((8 - n_cond, w.shape[1]), F32))
    o_ref[0] = jnp.concatenate(rows, axis=0) + b_ref[0]


def _ada_modulation(cond, ada_w, ada_b):
    n_cond, d = cond.shape
    depth, _, n6 = ada_w.shape
    tn = 1536
    ct = jnp.zeros((d, 8), F32).at[:, :n_cond].set(cond.T)
    return pl.pallas_call(
        functools.partial(_ada_kernel, n_cond=n_cond),
        out_shape=jax.ShapeDtypeStruct((depth, 8, n6), F32),
        grid=(depth, n6 // tn),
        in_specs=[pl.BlockSpec((d, 8), lambda l, j: (0, 0)),
                  pl.BlockSpec((1, d, tn), lambda l, j: (l, 0, j)),
                  pl.BlockSpec((1, 1, tn), lambda l, j: (l, 0, j))],
        out_specs=pl.BlockSpec((1, 8, tn), lambda l, j: (l, 0, j)),
        compiler_params=_params("parallel", "parallel"),
        name="ada_modulation",
    )(ct, ada_w, ada_b.reshape(depth, 1, n6))


def _norm_modulate(x, g, sc, sh):
    ms = jnp.mean(x * x, axis=-1, keepdims=True)
    return x * lax.rsqrt(ms + RMS_EPS) * g * (1.0 + sc) + sh


def _head_rmsnorm(t, seg, gain):
    ss = _dot((t * t).astype(BF16), seg)
    return t * lax.rsqrt(ss * (1.0 / HEAD_DIM) + RMS_EPS) * gain


def _inproj_even_kernel(x_ref, g_ref, sc_ref, sh_ref, w_ref, qg_ref, kg_ref, seg_ref,
                        q_ref, k_ref, v_ref, gb_ref, p_ref):
    u = _norm_modulate(x_ref[0], g_ref[...], sc_ref[0], sh_ref[0]).astype(BF16)
    seg = seg_ref[...]
    w = NA_WIDTH
    q = _dot(u, w_ref[:, 0 * w:1 * w])
    q_ref[0] = (_head_rmsnorm(q, seg, qg_ref[...]) * (HEAD_DIM ** -0.5)).astype(BF16)
    k = _dot(u, w_ref[:, 1 * w:2 * w])
    k_ref[0] = _head_rmsnorm(k, seg, kg_ref[...]).astype(BF16)
    v_ref[0] = _dot(u, w_ref[:, 2 * w:3 * w]).astype(BF16)
    gb_ref[0] = _dot(u, w_ref[:, 3 * w:4 * w]).astype(BF16)
    gc = _dot(u, w_ref[:, 4 * w:5 * w])
    hv = _dot(u, w_ref[:, 5 * w:6 * w])
    p_ref[0] = (gc * hv).astype(BF16)


def _inproj_even(x, g, sc, sh, w_bf, qg, kg, seg):
    b, n, d = x.shape
    tm = min(512, n)
    tok = lambda i, j: (i, j, 0)
    mod = lambda i, j: (i, 0, 0)
    const = lambda i, j: (0, 0)
    out = jax.ShapeDtypeStruct((b, n, NA_WIDTH), BF16)
    return pl.pallas_call(
        _inproj_even_kernel,
        out_shape=(out,) * 5,
        grid=(b, n // tm),
        in_specs=[pl.BlockSpec((1, tm, d), tok),
                  pl.BlockSpec((1, d), const),
                  pl.BlockSpec((1, 1, d), mod),
                  pl.BlockSpec((1, 1, d), mod),
                  pl.BlockSpec(w_bf.shape, const),
                  pl.BlockSpec((1, NA_WIDTH), const),
                  pl.BlockSpec((1, NA_WIDTH), const),
                  pl.BlockSpec((NA_WIDTH, NA_WIDTH), const)],
        out_specs=(pl.BlockSpec((1, tm, NA_WIDTH), tok),) * 5,
        compiler_params=_params("parallel", "parallel"),
        name="inproj_even",
    )(x, g, sc, sh, w_bf, qg, kg, seg)


def _bias_kernel(rpb_ref, o_ref):
    h = pl.program_id(0)
    qi = lax.broadcasted_iota(jnp.int32, (GRID_W, GRID_W), 0)
    ki = lax.broadcasted_iota(jnp.int32, (GRID_W, GRID_W), 1)
    start = jnp.clip(qi - NA_WIN_W // 2, 0, GRID_W - NA_WIN_W)
    valid = jnp.logical_and(ki >= start, ki < start + NA_WIN_W)
    cidx = jnp.clip(ki - qi, -(NA_WIN_W - 1), NA_WIN_W - 1) + (NA_WIN_W - 1)
    n_dr = 2 * NA_WIN_H - 1
    n_dc = 2 * NA_WIN_W - 1

    def body(j, accs):
        m = cidx == j
        return tuple(jnp.where(m, rpb_ref[(h * n_dr + d) * n_dc + j], a) for d, a in enumerate(accs))

    accs = lax.fori_loop(0, n_dc, body, tuple(jnp.zeros((GRID_W, GRID_W), F32) for _ in range(n_dr)))
    tiles = [jnp.where(valid, a, NEG_INF) for a in accs]
    for d0 in range(NA_WIN_H):
        o_ref[0, d0] = jnp.concatenate(tiles[d0:d0 + NA_WIN_H], axis=1)


def _bias_table(rpb):
    return pl.pallas_call(
        _bias_kernel,
        out_shape=jax.ShapeDtypeStruct((NA_HEADS, NA_WIN_H, GRID_W, NA_WIN_H * GRID_W), F32),
        grid=(NA_HEADS,),
        in_specs=[pl.BlockSpec(memory_space=pltpu.SMEM)],
        out_specs=pl.BlockSpec((1, NA_WIN_H, GRID_W, NA_WIN_H * GRID_W), lambda h: (h, 0, 0, 0)),
        compiler_params=_params("arbitrary"),
        name="rpb_bias_table",
    )(rpb.reshape(-1))


def _pair_attention(q2, kw, vw, bias, kc, vc, first_half):
    s_c = _dot_nt(q2, kc)
    m = jnp.max(s_c, axis=-1, keepdims=True)
    if kw is not None:
        s_w = _dot_nt(q2, kw) + bias
        m = jnp.maximum(m, jnp.max(s_w, axis=-1, keepdims=True))
        p_w = jnp.exp(s_w - m)
    p_c = jnp.exp(s_c - m)
    den = jnp.sum(p_c, axis=-1, keepdims=True)
    o = _dot(p_c.astype(BF16), vc)
    if kw is not None:
        den = den + jnp.sum(p_w, axis=-1, keepdims=True)
        o = o + _dot(p_w.astype(BF16), vw)
    o = o / den
    half = o.shape[0] // 2
    return jnp.where(first_half, o[:half], o[half:])


def _stack_heads(qp, first_half):
    zero = jnp.zeros_like(qp)
    return jnp.concatenate([jnp.where(first_half, qp, zero), jnp.where(first_half, zero, qp)], axis=0)


def _natten_kernel(q_ref, k_ref, v_ref, kc_ref, vc_ref, bias_ref, o_ref, *, rows_per_step, n_rows):
    blk = pl.program_id(1)
    lane = lax.broadcasted_iota(jnp.int32, (GRID_W, LANES), 1)
    first_half = lane < HEAD_DIM
    band = NA_WIN_H * GRID_W

    n_pairs = NA_HEADS // 2

    def row_body(j, carry):
        r = blk * rows_per_step + j
        start = jnp.clip(r - NA_WIN_H // 2, 0, n_rows - NA_WIN_H)
        d0 = start - r + (NA_WIN_H - 1)
        koff = pl.multiple_of(start * GRID_W, GRID_W)
        qoff = pl.multiple_of(j * GRID_W, GRID_W)
        cols = [slice(hp * LANES, (hp + 1) * LANES) for hp in range(n_pairs)]
        q2 = [_stack_heads(q_ref[0, pl.ds(qoff, GRID_W), cs], first_half) for cs in cols]
        s_w = [_dot_nt(q2[hp], k_ref[0, pl.ds(koff, band), cols[hp]])
               + jnp.concatenate([bias_ref[2 * hp, d0], bias_ref[2 * hp + 1, d0]], axis=0) for hp in range(n_pairs)]
        s_c = [_dot_nt(q2[hp], kc_ref[0, :, cols[hp]]) for hp in range(n_pairs)]
        m = [jnp.maximum(jnp.max(s_w[hp], axis=-1, keepdims=True), jnp.max(s_c[hp], axis=-1, keepdims=True))
             for hp in range(n_pairs)]
        p_w = [jnp.exp(s_w[hp] - m[hp]) for hp in range(n_pairs)]
        p_c = [jnp.exp(s_c[hp] - m[hp]) for hp in range(n_pairs)]
        den = [jnp.sum(p_w[hp], axis=-1, keepdims=True) + jnp.sum(p_c[hp], axis=-1, keepdims=True)
               for hp in range(n_pairs)]
        outs = []
        for hp in range(n_pairs):
            o = (_dot(p_w[hp].astype(BF16), v_ref[0, pl.ds(koff, band), cols[hp]])
                 + _dot(p_c[hp].astype(BF16), vc_ref[0, :, cols[hp]])) / den[hp]
            outs.append(jnp.where(first_half, o[:GRID_W], o[GRID_W:]))
        o_ref[0, pl.ds(qoff, GRID_W), :] = jnp.concatenate(outs, axis=1).astype(o_ref.dtype)
        return carry

    lax.fori_loop(0, rows_per_step, row_body, 0, unroll=2)


def _natten(q, k, v, kc, vc, bias):
    b, n, w = q.shape
    n_rows = n // GRID_W
    rows_per_step = 8
    tq = rows_per_step * GRID_W
    lc = kc.shape[1]
    return pl.pallas_call(
        functools.partial(_natten_kernel, rows_per_step=rows_per_step, n_rows=n_rows),
        out_shape=jax.ShapeDtypeStruct((b, n, w), BF16),
        grid=(b, n_rows // rows_per_step),
        in_specs=[pl.BlockSpec((1, tq, w), lambda i, j: (i, j, 0)),
                  pl.BlockSpec((1, n, w), lambda i, j: (i, 0, 0)),
                  pl.BlockSpec((1, n, w), lambda i, j: (i, 0, 0)),
                  pl.BlockSpec((1, lc, w), lambda i, j: (i, 0, 0)),
                  pl.BlockSpec((1, lc, w), lambda i, j: (i, 0, 0)),
                  pl.BlockSpec(bias.shape, lambda i, j: (0, 0, 0, 0))],
        out_specs=pl.BlockSpec((1, tq, w), lambda i, j: (i, j, 0)),
        compiler_params=_params("parallel", "arbitrary"),
        name="neighbourhood_attention",
    )(q, k, v, kc, vc, bias)


def _ctx_attn_kernel(q_ref, k_ref, v_ref, o_ref):
    lc = q_ref.shape[1]
    lane = lax.broadcasted_iota(jnp.int32, (lc, LANES), 1)
    first_half = lane < HEAD_DIM
    outs = []
    for hp in range(NA_HEADS // 2):
        cs = slice(hp * LANES, (hp + 1) * LANES)
        q2 = _stack_heads(q_ref[0, :, cs], first_half)
        outs.append(_pair_attention(q2, None, None, None, k_ref[0, :, cs], v_ref[0, :, cs], first_half))
    o_ref[0] = jnp.concatenate(outs, axis=1).astype(o_ref.dtype)


def _ctx_attention(q, k, v):
    b, lc, w = q.shape
    spec = pl.BlockSpec((1, lc, w), lambda i: (i, 0, 0))
    return pl.pallas_call(
        _ctx_attn_kernel,
        out_shape=jax.ShapeDtypeStruct((b, lc, w), BF16),
        grid=(b,),
        in_specs=[spec, spec, spec],
        out_specs=spec,
        compiler_params=_params("parallel"),
        name="context_attention",
    )(q, k, v)


def _sgconv_kernel(gb_ref, p_ref, w_ref, o_ref):
    p = p_ref[0].astype(F32)
    n = p.shape[0]
    row = lax.broadcasted_iota(jnp.int32, p.shape, 0)
    prev = jnp.where(row == 0, 0.0, pltpu.roll(p, 1, axis=0))
    nxt = jnp.where(row == n - 1, 0.0, pltpu.roll(p, n - 1, axis=0))
    y = w_ref[0:1, :] * prev + w_ref[1:2, :] * p + w_ref[2:3, :] * nxt
    o_ref[0] = (gb_ref[0].astype(F32) * y).astype(o_ref.dtype)


def _short_gated_conv(gb, p, w):
    b, n, c = p.shape
    spec = pl.BlockSpec((1, n, LANES), lambda i, j: (i, 0, j))
    return pl.pallas_call(
        _sgconv_kernel,
        out_shape=jax.ShapeDtypeStruct((b, n, c), BF16),
        grid=(b, c // LANES),
        in_specs=[spec, spec, pl.BlockSpec((3, LANES), lambda i, j: (0, j))],
        out_specs=spec,
        compiler_params=_params("parallel", "parallel"),
        name="short_gated_conv",
    )(gb, p, w)


def _inproj_odd_kernel(x_ref, g_ref, sc_ref, sh_ref, wht_ref, wag_ref, hy_ref, ag_ref):
    u = _norm_modulate(x_ref[0], g_ref[...], sc_ref[0], sh_ref[0]).astype(BF16)
    hy_ref[0] = _dot_nt(wht_ref[...], u).astype(BF16)
    a = _dot(u, wag_ref[:, :CF_WIDTH])
    g = _dot(u, wag_ref[:, CF_WIDTH:])
    ag_ref[0] = (a * jax.nn.sigmoid(g)).astype(BF16)


def _inproj_odd(x, g, sc, sh, wht_bf, wag_bf):
    b, n, d = x.shape
    tm = min(512, n)
    hw = wht_bf.shape[0]
    tok = lambda i, j: (i, j, 0)
    mod = lambda i, j: (i, 0, 0)
    const = lambda i, j: (0, 0)
    return pl.pallas_call(
        _inproj_odd_kernel,
        out_shape=(jax.ShapeDtypeStruct((b, hw, n), BF16), jax.ShapeDtypeStruct((b, n, CF_WIDTH), BF16)),
        grid=(b, n // tm),
        in_specs=[pl.BlockSpec((1, tm, d), tok),
                  pl.BlockSpec((1, d), const),
                  pl.BlockSpec((1, 1, d), mod),
                  pl.BlockSpec((1, 1, d), mod),
                  pl.BlockSpec(wht_bf.shape, const),
                  pl.BlockSpec(wag_bf.shape, const)],
        out_specs=(pl.BlockSpec((1, hw, tm), lambda i, j: (i, 0, j)), pl.BlockSpec((1, tm, CF_WIDTH), tok)),
        compiler_params=_params("parallel", "parallel"),
        name="inproj_odd",
    )(x, g, sc, sh, wht_bf, wag_bf)


SUBLANES = 8
CF_PAD = 2 * SUBLANES
CF_ROWS = 128


def _conformer_kernel(ag_ref, w_ref, cb_ref, lg_ref, lb_ref, o_ref, pad_ref, *, seq):
    zeros = jnp.zeros((CF_PAD, CF_WIDTH), F32)
    pad_ref[0:CF_PAD, :] = zeros
    pad_ref[CF_PAD + seq:2 * CF_PAD + seq, :] = zeros
    pad_ref[CF_PAD:CF_PAD + seq, :] = ag_ref[0].astype(F32)
    shift0 = CF_PAD - CF_TAPS // 2
    n_groups = (shift0 + CF_TAPS - 1) // SUBLANES + 1

    def conv_rows(i, carry):
        n0 = pl.multiple_of(i * CF_ROWS, CF_ROWS)
        wins = [pad_ref[pl.ds(n0 + SUBLANES * a, CF_ROWS + SUBLANES), :] for a in range(n_groups)]
        acc = None
        for b in range(SUBLANES):
            part = None
            for a in range(n_groups):
                j = SUBLANES * a + b - shift0
                if 0 <= j < CF_TAPS:
                    term = w_ref[j:j + 1, :] * wins[a]
                    part = term if part is None else part + term
            part = part[b:b + CF_ROWS, :]
            acc = part if acc is None else acc + part
        y = acc + cb_ref[...]
        mu = jnp.mean(y, axis=-1, keepdims=True)
        yc = y - mu
        var = jnp.mean(yc * yc, axis=-1, keepdims=True)
        z = yc * lax.rsqrt(var + LN_EPS) * lg_ref[...] + lb_ref[...]
        o_ref[0, pl.ds(n0, CF_ROWS), :] = _silu(z).astype(o_ref.dtype)
        return carry

    lax.fori_loop(0, seq // CF_ROWS, conv_rows, 0)


def _conformer(ag, w, cb, lg, lb):
    b, n, c = ag.shape
    spec = pl.BlockSpec((1, n, c), lambda i: (i, 0, 0))
    vec = pl.BlockSpec((1, c), lambda i: (0, 0))
    return pl.pallas_call(
        functools.partial(_conformer_kernel, seq=n),
        out_shape=jax.ShapeDtypeStruct((b, n, c), BF16),
        grid=(b,),
        in_specs=[spec, pl.BlockSpec((CF_TAPS, c), lambda i: (0, 0)), vec, vec, vec],
        out_specs=spec,
        scratch_shapes=[pltpu.VMEM((n + 2 * CF_PAD, c), F32)],
        compiler_params=_params("parallel"),
        name="conformer_conv",
    )(ag, w, cb[None, :], lg[None, :], lb[None, :])


def _hyena_features(length):
    t = np.linspace(0.0, 1.0, length, dtype=np.float32)
    w = (2.0 * math.pi * np.arange(length, dtype=np.float32) / length).astype(np.float32)
    bands = np.linspace(1e-4, HY_BANDS - 1, HY_BANDS, dtype=np.float32)
    ang = (bands[:, None] * w[None, :]).astype(np.float32)
    zt = np.concatenate([t[None, :], np.cos(ang), -np.sin(ang)], axis=0).astype(np.float32)
    deltas = np.abs(np.linspace(HY_MIN_DECAY, HY_MAX_DECAY, HY_WIDTH, dtype=np.float32))
    rev = (length - np.arange(length)) % length
    zt2 = np.concatenate([zt, zt[:, rev]], axis=1)
    t2 = np.concatenate([t, t[rev]])[None, :]
    return zt2, t2, deltas[:, None]


def _taps_kernel(zt_ref, t_ref, dl_ref, w1t_ref, b1_ref, f0_ref, w2t_ref, b2_ref, f1_ref, w3t_ref,
                 o_ref, hid_ref):
    first = jnp.logical_and(pl.program_id(0) == 0, pl.program_id(1) == 0)
    length = t_ref.shape[1] // 2

    @pl.when(first)
    def _():
        h1 = jnp.sin(f0_ref[...] * (_dot_f32(w1t_ref[...], zt_ref[...]) + b1_ref[...]))
        hid_ref[...] = jnp.sin(f1_ref[...] * (_dot_f32(w2t_ref[...], h1) + b2_ref[...]))

    decay = jnp.exp(-(dl_ref[...] * t_ref[...]))
    fwd = _dot_f32(w3t_ref[0, 0], hid_ref[:, :length])
    bwd = _dot_f32(w3t_ref[0, 1], hid_ref[:, length:])
    taps = jnp.concatenate([fwd, bwd], axis=1) * decay
    nrm = jnp.sum(jnp.abs(taps), axis=-1, keepdims=True)
    lane = lax.broadcasted_iota(jnp.int32, taps.shape, 1)
    o_ref[0] = jnp.where(lane == length, 0.0, taps / nrm)


def _hyena_taps(length, w1, b1, w2, b2, w3, freq):
    zt2, t2, deltas = _hyena_features(length)
    cb = 128
    w3t = w3.T.reshape(2, 2, HY_WIDTH, HY_FFN)
    col = lambda v: v.reshape(HY_FFN, 1)
    const = lambda o, j: (0, 0)
    return pl.pallas_call(
        _taps_kernel,
        out_shape=jax.ShapeDtypeStruct((2, HY_WIDTH, 2 * length), F32),
        grid=(2, HY_WIDTH // cb),
        in_specs=[pl.BlockSpec((HY_EMB, 2 * length), const),
                  pl.BlockSpec((1, 2 * length), const),
                  pl.BlockSpec((cb, 1), lambda o, j: (j, 0)),
                  pl.BlockSpec((HY_FFN, HY_EMB), const),
                  pl.BlockSpec((HY_FFN, 1), const),
                  pl.BlockSpec((HY_FFN, 1), const),
                  pl.BlockSpec((HY_FFN, HY_FFN), const),
                  pl.BlockSpec((HY_FFN, 1), const),
                  pl.BlockSpec((HY_FFN, 1), const),
                  pl.BlockSpec((1, 2, cb, HY_FFN), lambda o, j: (o, 0, j, 0))],
        out_specs=pl.BlockSpec((1, cb, 2 * length), lambda o, j: (o, j, 0)),
        scratch_shapes=[pltpu.VMEM((HY_FFN, 2 * length), F32)],
        compiler_params=_params("arbitrary", "arbitrary"),
        name="hyena_filter_taps",
    )(jnp.asarray(zt2), jnp.asarray(t2), jnp.asarray(deltas), w1.T, col(b1), col(freq[0]),
      w2.T, col(b2), col(freq[1]), w3t)


def _dft_constants():
    na, nb, n = FFT_NA, FFT_NB, FFT_N
    half = na // 2
    ka = np.arange(na)
    ang_a = 2.0 * np.pi * np.outer(ka, ka) / na
    ca, sa = np.cos(ang_a), np.sin(ang_a)
    fa = np.block([[ca[:half], -sa[:half]], [sa[:half], ca[:half]]])
    fai = np.block([[ca[:, :half], sa[:, :half]], [-sa[:, :half], ca[:, :half]]])
    kb = np.arange(nb)
    ang_b = 2.0 * np.pi * np.outer(kb, kb) / nb
    cbm, sbm = np.cos(ang_b), np.sin(ang_b)
    fb = np.block([[cbm, -sbm], [sbm, cbm]])
    fbi = np.block([[cbm, sbm], [-sbm, cbm]])
    ang_t = 2.0 * np.pi * np.outer(kb, ka) / n
    ct, st = np.cos(ang_t), np.sin(ang_t)
    tw_fc = np.concatenate([ct, ct], axis=1)
    tw_fs = np.concatenate([st, -st], axis=1)
    tw_ic, tw_is = ct.T.copy(), st.T.copy()
    bf = lambda a: jnp.asarray(a, dtype=F32).astype(BF16)
    f32 = lambda a: jnp.asarray(a, dtype=F32)
    fa_real = np.concatenate([ca, -sa], axis=1)
    return dict(fa=bf(fa), fa_real=bf(fa_real), fai=bf(fai), fb=bf(fb), fbi=bf(fbi),
                tw_fc=f32(tw_fc), tw_fs=f32(tw_fs), tw_ic=f32(tw_ic), tw_is=f32(tw_is))


def _fft_forward(zr, zi, fa, tw_fc, tw_fs, fb):
    c, _, nb = zr.shape
    tr = jnp.swapaxes(zr, 1, 2)
    lhs = tr if zi is None else jnp.concatenate([tr, jnp.swapaxes(zi, 1, 2)], axis=2)
    a = _dot(lhs.reshape(c * nb, lhs.shape[2]).astype(BF16), fa).reshape(c, nb, 2 * FFT_NA)
    a = a * tw_fc + pltpu.roll(a, FFT_NA, axis=2) * tw_fs
    t = jnp.swapaxes(a, 1, 2)
    lhs2 = jnp.concatenate([t[:, :FFT_NA, :], t[:, FFT_NA:, :]], axis=2)
    x = _dot(lhs2.reshape(c * FFT_NA, 2 * nb).astype(BF16), fb)
    return x.reshape(c, FFT_NA, 2 * nb)


def _fft_inverse(y, fbi, tw_ic, tw_is, fai):
    c = y.shape[0]
    nb = FFT_NB
    b = _dot(y.reshape(c * FFT_NA, 2 * nb).astype(BF16), fbi).reshape(c, FFT_NA, 2 * nb)
    br, bi = b[:, :, :nb], b[:, :, nb:]
    rr = br * tw_ic - bi * tw_is
    ii = bi * tw_ic + br * tw_is
    t = jnp.swapaxes(jnp.concatenate([rr, ii], axis=1), 1, 2)
    o = _dot(t.reshape(c * nb, 2 * FFT_NA).astype(BF16), fai).reshape(c, nb, FFT_NA)
    o = jnp.swapaxes(o, 1, 2)
    return o[:, :FFT_NA // 2, :], o[:, FFT_NA // 2:, :]


def _filter_fft_kernel(taps_ref, fa_ref, twc_ref, tws_ref, fb_ref, o_ref):
    h = _fft_forward(taps_ref[0], None, fa_ref[...], twc_ref[...], tws_ref[...], fb_ref[...])
    o_ref[0] = h * (1.0 / FFT_N)


def _filter_fft(taps, consts):
    _, c, n = taps.shape
    cb = 32
    taps4 = taps.reshape(2, c, FFT_NA, FFT_NB)
    cm = lambda o, j: (0, 0)
    return pl.pallas_call(
        _filter_fft_kernel,
        out_shape=jax.ShapeDtypeStruct((2, c, FFT_NA, 2 * FFT_NB), F32),
        grid=(2, c // cb),
        in_specs=[pl.BlockSpec((1, cb, FFT_NA, FFT_NB), lambda o, j: (o, j, 0, 0)),
                  pl.BlockSpec(consts["fa_real"].shape, cm),
                  pl.BlockSpec(consts["tw_fc"].shape, cm),
                  pl.BlockSpec(consts["tw_fs"].shape, cm),
                  pl.BlockSpec(consts["fb"].shape, cm)],
        out_specs=pl.BlockSpec((1, cb, FFT_NA, 2 * FFT_NB), lambda o, j: (o, j, 0, 0)),
        compiler_params=_params("parallel", "parallel"),
        name="hyena_filter_fft",
    )(taps4, consts["fa_real"], consts["tw_fc"], consts["tw_fs"], consts["fb"])


def _shift_tokens(a, direction):
    rows = a.shape[-2]
    lane = lax.broadcasted_iota(jnp.int32, a.shape, a.ndim - 1)
    row = lax.broadcasted_iota(jnp.int32, a.shape, a.ndim - 2)
    if direction == 1:
        l = pltpu.roll(a, 1, axis=a.ndim - 1)
        ls = pltpu.roll(l, 1, axis=a.ndim - 2)
        out = jnp.where(lane == 0, ls, l)
        edge = jnp.logical_and(lane == 0, row == 0)
    else:
        l = pltpu.roll(a, LANES - 1, axis=a.ndim - 1)
        ls = pltpu.roll(l, rows - 1, axis=a.ndim - 2)
        out = jnp.where(lane == LANES - 1, ls, l)
        edge = jnp.logical_and(lane == LANES - 1, row == rows - 1)
    return jnp.where(edge, 0.0, out)


def _short_conv3(a, w_ref):
    return w_ref[0] * _shift_tokens(a, 1) + w_ref[1] * a + w_ref[2] * _shift_tokens(a, -1)


def _hyena_kernel(v_ref, x1_ref, x2_ref, wv_ref, w1_ref, w2_ref, h_ref, hb_ref,
                  fa_ref, twfc_ref, twfs_ref, fb_ref, fbi_ref, twic_ref, twis_ref, fai_ref, o_ref):
    fwd_c = (fa_ref[...], twfc_ref[...], twfs_ref[...], fb_ref[...])
    inv_c = (fbi_ref[...], twic_ref[...], twis_ref[...], fai_ref[...])
    z = _short_conv3(v_ref[...].astype(F32), wv_ref)
    zr, zi = z[0], z[1]
    nb = FFT_NB
    for o, (g_ref, gw_ref) in enumerate(((x1_ref, w1_ref), (x2_ref, w2_ref))):
        x = _fft_forward(zr, zi, *fwd_c)
        h = h_ref[o]
        xr, xi, hr, hi = x[:, :, :nb], x[:, :, nb:], h[:, :, :nb], h[:, :, nb:]
        y = jnp.concatenate([xr * hr - xi * hi, xr * hi + xi * hr], axis=2)
        yr, yi = _fft_inverse(y, *inv_c)
        gate = _short_conv3(g_ref[...].astype(F32), gw_ref)
        bias = hb_ref[o]
        zr = gate[0] * (yr + zr * bias)
        zi = gate[1] * (yi + zi * bias)
    o_ref[0] = zr.astype(o_ref.dtype)
    o_ref[1] = zi.astype(o_ref.dtype)


def _hyena_latent(hyt, short_w, h_spec, hy_bias, consts):
    b, c3, length = hyt.shape
    c = c3 // 3
    rows = length // FFT_NB
    cb = 32
    nblk = c // cb
    hy4 = hyt.reshape(b, c3, rows, FFT_NB)
    w4 = jnp.broadcast_to(short_w.reshape(3, c3, 1, 1), (3, c3, 1, FFT_NB))
    hb4 = jnp.broadcast_to(hy_bias.reshape(2, c, 1, 1), (2, c, 1, FFT_NB))
    sig = lambda g: pl.BlockSpec((2, cb, rows, FFT_NB), lambda j, p: (p, g * nblk + j, 0, 0))
    wsp = lambda g: pl.BlockSpec((3, cb, 1, FFT_NB), lambda j, p: (0, g * nblk + j, 0, 0))
    cm = lambda j, p: (0, 0)
    names = ("fa", "tw_fc", "tw_fs", "fb", "fbi", "tw_ic", "tw_is", "fai")
    out = pl.pallas_call(
        _hyena_kernel,
        out_shape=jax.ShapeDtypeStruct((b, c, rows, FFT_NB), BF16),
        grid=(nblk, b // 2),
        in_specs=[sig(0), sig(1), sig(2), wsp(0), wsp(1), wsp(2),
                  pl.BlockSpec((2, cb, FFT_NA, 2 * FFT_NB), lambda j, p: (0, j, 0, 0)),
                  pl.BlockSpec((2, cb, 1, FFT_NB), lambda j, p: (0, j, 0, 0))]
                 + [pl.BlockSpec(consts[k].shape, cm) for k in names],
        out_specs=pl.BlockSpec((2, cb, rows, FFT_NB), lambda j, p: (p, j, 0, 0)),
        compiler_params=_params("parallel", "arbitrary"),
        name="hyena_long_conv",
    )(hy4, hy4, hy4, w4, w4, w4, h_spec, hb4, *[consts[k] for k in names])
    return out.reshape(b, c, length)


def _dense_dft_constants(length):
    n = 2 * length
    k = np.arange(n)
    ang = 2.0 * np.pi * np.outer(k, k) / n
    fwd = np.concatenate([np.cos(ang), -np.sin(ang)], axis=1)
    inv = np.concatenate([np.cos(ang[:length]).T, -np.sin(ang[:length]).T], axis=0) / n
    bf = lambda a: jnp.asarray(a, dtype=F32).astype(BF16)
    return bf(fwd), bf(inv)


def _shift_lanes(a, direction):
    n = a.shape[-1]
    lane = lax.broadcasted_iota(jnp.int32, a.shape, a.ndim - 1)
    if direction == 1:
        return jnp.where(lane == 0, 0.0, pltpu.roll(a, 1, axis=a.ndim - 1))
    return jnp.where(lane == n - 1, 0.0, pltpu.roll(a, n - 1, axis=a.ndim - 1))


def _hyena_ctx_kernel(v_ref, x1_ref, x2_ref, wv_ref, w1_ref, w2_ref, taps_ref, hb_ref, fwd_ref, inv_ref, o_ref):
    bsz, cb, length = v_ref.shape
    n = 2 * length
    fwd_m, inv_m = fwd_ref[0:length, :], inv_ref[...]

    def conv3(ref, w_ref):
        a = ref[...].astype(F32)
        return w_ref[0] * _shift_lanes(a, 1) + w_ref[1] * a + w_ref[2] * _shift_lanes(a, -1)

    z = conv3(v_ref, wv_ref)
    for o, (g_ref, gw_ref) in enumerate(((x1_ref, w1_ref), (x2_ref, w2_ref))):
        h = _dot(taps_ref[o].astype(BF16), fwd_ref[...])
        hr, hi = h[:, :n], h[:, n:]
        x = _dot(z.reshape(bsz * cb, length).astype(BF16), fwd_m).reshape(bsz, cb, 2 * n)
        xr, xi = x[:, :, :n], x[:, :, n:]
        y = jnp.concatenate([xr * hr - xi * hi, xr * hi + xi * hr], axis=2)
        yt = _dot(y.reshape(bsz * cb, 2 * n).astype(BF16), inv_m).reshape(bsz, cb, length)
        z = conv3(g_ref, gw_ref) * (yt + z * hb_ref[o])
    o_ref[...] = z.astype(o_ref.dtype)


def _hyena_context(hyt, short_w, taps, hy_bias):
    b, c3, length = hyt.shape
    c = c3 // 3
    cb = 128
    nblk = c // cb
    fwd_m, inv_m = _dense_dft_constants(length)
    w4 = jnp.broadcast_to(short_w.reshape(3, c3, 1), (3, c3, length))
    hb = jnp.broadcast_to(hy_bias.reshape(2, c, 1), (2, c, length))
    sig = lambda g: pl.BlockSpec((b, cb, length), lambda j: (0, g * nblk + j, 0))
    wsp = lambda g: pl.BlockSpec((3, cb, length), lambda j: (0, g * nblk + j, 0))
    return pl.pallas_call(
        _hyena_ctx_kernel,
        out_shape=jax.ShapeDtypeStruct((b, c, length), BF16),
        grid=(nblk,),
        in_specs=[sig(0), sig(1), sig(2), wsp(0), wsp(1), wsp(2),
                  pl.BlockSpec((2, cb, 2 * length), lambda j: (0, j, 0)),
                  pl.BlockSpec((2, cb, length), lambda j: (0, j, 0)),
                  pl.BlockSpec(fwd_m.shape, lambda j: (0, 0)),
                  pl.BlockSpec(inv_m.shape, lambda j: (0, 0))],
        out_specs=pl.BlockSpec((b, cb, length), lambda j: (0, j, 0)),
        compiler_params=_params("parallel"),
        name="hyena_context_conv",
    )(hyt, hyt, hyt, w4, w4, w4, taps, hb, fwd_m, inv_m)


def _route(gl, el):
    row = lax.broadcasted_iota(jnp.int32, gl.shape, 0).astype(F32)
    grp = jnp.floor(row * (1.0 / EXPERTS_PER_GROUP))
    big = float(N_EXPERTS)
    gmax = jnp.max(gl, axis=0, keepdims=True)
    gidx = jnp.min(jnp.where(gl == gmax, grp, big), axis=0, keepdims=True)
    gsum = jnp.sum(jnp.exp(gl - gmax), axis=0, keepdims=True) * (1.0 / EXPERTS_PER_GROUP)
    g_w = 1.0 / gsum
    em = jnp.where(grp == gidx, el, NEG_INF)
    t1 = jnp.max(em, axis=0, keepdims=True)
    i1 = jnp.min(jnp.where(em == t1, row, big), axis=0, keepdims=True)
    em2 = jnp.where(row == i1, 2.0 * NEG_INF, em)
    t2 = jnp.max(em2, axis=0, keepdims=True)
    i2 = jnp.min(jnp.where(em2 == t2, row, big), axis=0, keepdims=True)
    e2 = jnp.exp(t2 - t1)
    den = 1.0 + e2
    w1 = g_w / den
    w2 = g_w * e2 / den
    return jnp.where(row == i1, w1, 0.0) + jnp.where(row == i2, w2, 0.0), gidx


SLOT_LANE = N_EXPERTS
CHUNKS_LANE = N_EXPERTS + 1
MOE_CHUNK = 64


def _tile_chunks(tm):
    return tm // MOE_CHUNK + N_GROUPS


def _dispatch_slots(gidx, tri):
    tm = gidx.shape[1]
    sub = lax.broadcasted_iota(jnp.int32, (SUBLANES, tm), 0)
    grp = sub.astype(F32)
    member = jnp.where(grp == gidx, 1.0, 0.0)
    rank = _dot(member.astype(BF16), tri)
    count = jnp.sum(member, axis=1, keepdims=True)
    chunks = jnp.floor((count + float(MOE_CHUNK - 1)) * (1.0 / MOE_CHUNK))
    first = jnp.zeros_like(chunks)
    for g in range(1, N_GROUPS):
        first = first + jnp.where(sub[:, 0:1] >= g, chunks[g - 1:g, :], 0.0)
    slot = jnp.sum(member * (first * float(MOE_CHUNK) + rank - 1.0), axis=0, keepdims=True)
    out = jnp.where(sub == 0, slot, 0.0)
    for g in range(N_GROUPS):
        out = out + jnp.where(sub == g + 1, chunks[g:g + 1, :], 0.0)
    return out


def _outproj_kernel(y1_ref, y2_ref, w_ref, x_ref, g1_ref, n2_ref, sc2_ref, sh2_ref, wr_ref, br_ref, tri_ref,
                    xo_ref, cmb_ref, hs_ref, cs_ref, *, channel_major):
    half = w_ref.shape[0] // 2
    dots = [_dot_tn if cm else _dot for cm in channel_major]
    acc = dots[0](y1_ref[0], w_ref[:half, :]) + dots[1](y2_ref[0], w_ref[half:, :])
    xn = x_ref[0] + g1_ref[0] * acc
    xo_ref[0] = xn
    h = _norm_modulate(xn, n2_ref[...], sc2_ref[0], sh2_ref[0])
    hi = h.astype(BF16)
    lo = (h - hi.astype(F32)).astype(BF16)
    p = _dot(hi, wr_ref[...])
    logits = p[:, :LANES] + p[:, LANES:] + _dot(lo, wr_ref[:, :LANES]) + br_ref[...]
    lt = logits.T
    cmb, gidx = _route(lt[0:N_EXPERTS], lt[N_EXPERTS:2 * N_EXPERTS])
    disp = _dispatch_slots(gidx, tri_ref[...])
    pad = jnp.zeros((LANES - N_EXPERTS - SUBLANES, cmb.shape[1]), F32)
    rows = jnp.concatenate([cmb, disp, pad], axis=0).T
    cmb_ref[0] = rows
    n_chunks = hs_ref.shape[0]
    srow = lax.broadcasted_iota(jnp.int32, (n_chunks * MOE_CHUNK, rows.shape[0]), 0).astype(F32)
    gather = jnp.where(srow == disp[0:1, :], 1.0, 0.0).astype(BF16)
    hs = _dot(gather, hi).astype(BF16)
    r_hi = rows.astype(BF16)
    r_lo = (rows - r_hi.astype(F32)).astype(BF16)
    cs = _dot(gather, r_hi) + _dot(gather, r_lo)
    for k in range(n_chunks):
        hs_ref[k] = hs[k * MOE_CHUNK:(k + 1) * MOE_CHUNK]
        cs_ref[k] = cs[k * MOE_CHUNK:(k + 1) * MOE_CHUNK]


def _outproj(y1, y2, w_bf, x, g1, n2g, sc2, sh2, wr2, br, channel_major):
    b, n, d = x.shape
    tm = min(512, n)
    nt = n // tm
    ntc = _tile_chunks(tm)
    half = d // 2
    tok = lambda i, j: (i, j, 0)
    mod = lambda i, j: (i, 0, 0)
    const = lambda i, j: (0, 0)
    srt = lambda i, j: (i * nt + j, 0, 0)
    y_specs = [pl.BlockSpec((1, half, tm), lambda i, j: (i, 0, j)) if cm else pl.BlockSpec((1, tm, half), tok)
               for cm in channel_major]
    return pl.pallas_call(
        functools.partial(_outproj_kernel, channel_major=tuple(channel_major)),
        out_shape=(jax.ShapeDtypeStruct((b, n, d), F32), jax.ShapeDtypeStruct((b, n, LANES), F32),
                   jax.ShapeDtypeStruct((b * nt * ntc, MOE_CHUNK, d), BF16),
                   jax.ShapeDtypeStruct((b * nt * ntc, MOE_CHUNK, LANES), F32)),
        grid=(b, nt),
        in_specs=[y_specs[0], y_specs[1],
                  pl.BlockSpec((d, d), const),
                  pl.BlockSpec((1, tm, d), tok),
                  pl.BlockSpec((1, 1, d), mod),
                  pl.BlockSpec((1, d), const),
                  pl.BlockSpec((1, 1, d), mod),
                  pl.BlockSpec((1, 1, d), mod),
                  pl.BlockSpec(wr2.shape, const),
                  pl.BlockSpec(br.shape, const),
                  pl.BlockSpec((tm, tm), const)],
        out_specs=(pl.BlockSpec((1, tm, d), tok), pl.BlockSpec((1, tm, LANES), tok),
                   pl.BlockSpec((ntc, MOE_CHUNK, d), srt), pl.BlockSpec((ntc, MOE_CHUNK, LANES), srt)),
        compiler_params=_params("parallel", "parallel"),
        name="outproj_router",
    )(y1, y2, w_bf, x, g1, n2g, sc2, sh2, wr2, br, jnp.asarray(np.triu(np.ones((tm, tm), np.float32)), dtype=BF16))


def _swiglu_group(h, w_rows, j, wg_ref, wu_ref, wd_ref):
    lane = lax.broadcasted_iota(jnp.int32, w_rows.shape, 1)
    experts = range(EXPERTS_PER_GROUP)
    w_e = [jnp.sum(jnp.where(lane == j * EXPERTS_PER_GROUP + e, w_rows, 0.0), axis=1, keepdims=True) for e in experts]
    a = [_dot(h, wg_ref[e]) for e in experts]
    u = [_dot(h, wu_ref[e]) for e in experts]
    act = [(_silu(a[e]) * u[e] * w_e[e]).astype(BF16) for e in experts]
    out = _dot(act[0], wd_ref[0])
    for e in experts[1:]:
        out = out + _dot(act[e], wd_ref[e])
    return out


MOE_STEP_CHUNKS = 8


def _moe_sorted_kernel(group_ref, used_ref, fresh_ref, src_ref, *refs):
    n = MOE_STEP_CHUNKS
    hs_refs, cs_refs = refs[:n], refs[n:2 * n]
    wg_ref, wu_ref, wd_ref, ys_ref, wg_bf, wu_bf, wd_bf = refs[2 * n:]
    s = pl.program_id(0)

    @pl.when(fresh_ref[s] > 0)
    def _():
        wg_bf[...] = wg_ref[...].astype(BF16)
        wu_bf[...] = wu_ref[...].astype(BF16)
        wd_bf[...] = wd_ref[...].astype(BF16)

    @pl.when(used_ref[s] > 0)
    def _():
        h = jnp.concatenate([r[0] for r in hs_refs], axis=0)
        w_rows = jnp.concatenate([r[0] for r in cs_refs], axis=0)
        y = _swiglu_group(h, w_rows, group_ref[s], wg_bf, wu_bf, wd_bf).astype(BF16)
        for k in range(n):
            ys_ref[k] = y[k * MOE_CHUNK:(k + 1) * MOE_CHUNK]

    @pl.when(used_ref[s] == 0)
    def _():
        ys_ref[...] = jnp.zeros(ys_ref.shape, ys_ref.dtype)


def _moe_sorted(hs, cs, step_group, step_used, step_fresh, chunk_src, wg, wu, wd, layer):
    _, _, d = hs.shape
    n = MOE_STEP_CHUNKS
    steps = step_group.shape[0]
    epg = EXPERTS_PER_GROUP
    chunk = lambda k, width: pl.BlockSpec((1, MOE_CHUNK, width),
                                          lambda s, grp, used, fresh, src: (src[s * n + k], 0, 0))
    wmap = lambda s, grp, used, fresh, src: (layer * N_GROUPS + grp[s], 0, 0)
    return pl.pallas_call(
        _moe_sorted_kernel,
        out_shape=jax.ShapeDtypeStruct((steps * n, MOE_CHUNK, d), BF16),
        grid_spec=pltpu.PrefetchScalarGridSpec(
            num_scalar_prefetch=4,
            grid=(steps,),
            in_specs=[chunk(k, d) for k in range(n)] + [chunk(k, LANES) for k in range(n)]
                     + [pl.BlockSpec((epg, d, D_EXPERT), wmap),
                        pl.BlockSpec((epg, d, D_EXPERT), wmap),
                        pl.BlockSpec((epg, D_EXPERT, d), wmap)],
            out_specs=pl.BlockSpec((n, MOE_CHUNK, d), lambda s, grp, used, fresh, src: (s, 0, 0)),
            scratch_shapes=[pltpu.VMEM((epg, d, D_EXPERT), BF16), pltpu.VMEM((epg, d, D_EXPERT), BF16),
                            pltpu.VMEM((epg, D_EXPERT, d), BF16)]),
        compiler_params=_params("arbitrary"),
        name="moe_sorted_experts",
    )(step_group, step_used, step_fresh, chunk_src, *([hs] * n), *([cs] * n), wg, wu, wd)


def _moe_combine_kernel(pos_ref, *refs):
    n_chunks = len(refs) - 4
    ys_refs = refs[:n_chunks]
    cmb_ref, x_ref, g2_ref, o_ref = refs[n_chunks:]
    cmb = cmb_ref[0]
    lane = lax.broadcasted_iota(jnp.int32, cmb.shape, 1)
    slot = jnp.sum(jnp.where(lane == SLOT_LANE, cmb, 0.0), axis=1, keepdims=True)
    rows = lax.broadcasted_iota(jnp.int32, (cmb.shape[0], n_chunks * MOE_CHUNK), 1).astype(F32)
    scatter = jnp.where(slot == rows, 1.0, 0.0).astype(BF16)
    ys = jnp.concatenate([r[0] for r in ys_refs], axis=0)
    o_ref[0] = x_ref[0] + g2_ref[0] * _dot(scatter, ys)


def _moe_combine(ys, chunk_pos, cmb, x, g2):
    b, n, d = x.shape
    tm = min(512, n)
    nt = n // tm
    ntc = _tile_chunks(tm)
    tok = lambda i, t, pos: (i, t, 0)
    chunk = lambda k: pl.BlockSpec((1, MOE_CHUNK, d), lambda i, t, pos: (pos[(i * nt + t) * ntc + k], 0, 0))
    return pl.pallas_call(
        _moe_combine_kernel,
        out_shape=jax.ShapeDtypeStruct((b, n, d), F32),
        grid_spec=pltpu.PrefetchScalarGridSpec(
            num_scalar_prefetch=1,
            grid=(b, nt),
            in_specs=[chunk(k) for k in range(ntc)]
                     + [pl.BlockSpec((1, tm, LANES), tok),
                        pl.BlockSpec((1, tm, d), tok),
                        pl.BlockSpec((1, 1, d), lambda i, t, pos: (i, 0, 0))],
            out_specs=pl.BlockSpec((1, tm, d), tok)),
        compiler_params=_params("parallel", "parallel"),
        name="moe_combine",
    )(chunk_pos, *([ys] * ntc), cmb, x, g2)


def _moe_chunk_schedule(cmb):
    b, n, _ = cmb.shape
    tm = min(512, n)
    nt = b * (n // tm)
    ntc, nsc = _tile_chunks(tm), MOE_STEP_CHUNKS
    steps = nt * ntc // nsc + N_GROUPS
    cnt = cmb[:, ::tm, CHUNKS_LANE:CHUNKS_LANE + N_GROUPS].reshape(nt, N_GROUPS).astype(jnp.int32)
    in_tile = jnp.cumsum(cnt, axis=1) - cnt
    before = jnp.cumsum(cnt, axis=0) - cnt
    total = jnp.sum(cnt, axis=0)
    padded = (total + nsc - 1) // nsc * nsc
    gstart = jnp.cumsum(padded) - padded
    k = jnp.arange(ntc, dtype=jnp.int32)
    c = jnp.arange(tm // MOE_CHUNK, dtype=jnp.int32)
    dst = gstart[None, :, None] + before[:, :, None] + c[None, None, :]
    src = (jnp.arange(nt, dtype=jnp.int32) * ntc)[:, None, None] + in_tile[:, :, None] + c[None, None, :]
    valid = c[None, None, :] < cnt[:, :, None]
    chunk_src = jnp.zeros((steps * nsc,), jnp.int32).at[jnp.where(valid, dst, steps * nsc).reshape(-1)].set(
        src.reshape(-1), mode="drop")
    first_chunk = jnp.arange(steps, dtype=jnp.int32) * nsc
    step_group = jnp.sum((first_chunk[:, None] >= (gstart + padded)[None, :-1]).astype(jnp.int32), axis=1)
    step_used = (first_chunk < gstart[-1] + padded[-1]).astype(jnp.int32)
    step_fresh = jnp.concatenate([jnp.ones((1,), jnp.int32), (step_group[1:] != step_group[:-1]).astype(jnp.int32)])
    ends = in_tile + cnt
    grp_k = jnp.minimum(jnp.sum((k[None, :, None] >= ends[:, None, :]).astype(jnp.int32), axis=2), N_GROUPS - 1)
    take = lambda a: jnp.take_along_axis(a, grp_k, axis=1)
    pos = gstart[grp_k] + take(before) + (k[None, :] - take(in_tile))
    chunk_pos = jnp.where(k[None, :] < ends[:, -1:], pos, 0).reshape(-1).astype(jnp.int32)
    return step_group, step_used, step_fresh, chunk_src, chunk_pos


def _moe(hs, cs, cmb, x, g2, wg, wu, wd, layer):
    step_group, step_used, step_fresh, chunk_src, chunk_pos = _moe_chunk_schedule(cmb)
    ys = _moe_sorted(hs, cs, step_group, step_used, step_fresh, chunk_src, wg, wu, wd, layer)
    return _moe_combine(ys, chunk_pos, cmb, x, g2)


def kernel(x, c, ctx, c_ctx, ada_w, ada_b, norm1_g, norm2_g, w_in_even, qn_g, kn_g, na_rpb, sc_conv_w, w_in_odd, hy_short_w, hy_w1, hy_b1, hy_w2, hy_b2, hy_w3, hy_freq, hy_bias, cf_conv_w, cf_conv_b, cf_ln_g, cf_ln_b, w_out, moe_w_group, moe_b_group, moe_w_router, moe_b_router, moe_w_gate, moe_w_up, moe_w_down):
    depth = ada_w.shape[0]
    bsz, seq, d = x.shape
    lc = ctx.shape[1]
    assert 2 * seq == FFT_N and d == D_MODEL and bsz % 2 == 0

    mods = _ada_modulation(jnp.concatenate([c, c_ctx[None, :]], axis=0), ada_w, ada_b)
    seg = jnp.asarray(np.kron(np.eye(NA_HEADS), np.ones((HEAD_DIM, HEAD_DIM))), dtype=BF16)
    consts = _dft_constants()
    wg = moe_w_gate.reshape((-1,) + moe_w_gate.shape[2:])
    wu = moe_w_up.reshape((-1,) + moe_w_up.shape[2:])
    wd = moe_w_down.reshape((-1,) + moe_w_down.shape[2:])

    for l in range(depth):
        ctx_needed = any(j % 2 == 0 for j in range(l + 1, depth))
        lat_mod = [m[:, None, :] for m in jnp.split(mods[l, :bsz], 6, axis=-1)]
        ctx_mod = [jnp.broadcast_to(m[None, :, :], (bsz, 1, d)) for m in jnp.split(mods[l, bsz:bsz + 1], 6, axis=-1)]
        sh1, sc1, g1, sh2, sc2, g2 = lat_mod
        csh1, csc1, cg1, csh2, csc2, cg2 = ctx_mod
        n1g = norm1_g[l][None, :]
        n2g = norm2_g[l][None, :]
        w_out_bf = w_out[l].astype(BF16)
        wr = jnp.concatenate([jnp.repeat(moe_w_group[l], EXPERTS_PER_GROUP, axis=1), moe_w_router[l],
                              jnp.zeros((d, LANES - 2 * N_EXPERTS), F32)], axis=1)
        br = jnp.concatenate([jnp.repeat(moe_b_group[l], EXPERTS_PER_GROUP), moe_b_router[l],
                              jnp.zeros((LANES - 2 * N_EXPERTS,), F32)])[None, :]
        wrh = wr.astype(BF16)
        wr2 = jnp.concatenate([wrh, (wr - wrh.astype(F32)).astype(BF16)], axis=1)

        if l % 2 == 0:
            e = l // 2
            w_in = w_in_even[e].astype(BF16)
            qg = jnp.tile(qn_g[e], NA_HEADS)[None, :]
            kg = jnp.tile(kn_g[e], NA_HEADS)[None, :]
            ql, kl, vl, gbl, pl_ = _inproj_even(x, n1g, sc1, sh1, w_in, qg, kg, seg)
            qc, kc, vc, gbc, pc = _inproj_even(ctx, n1g, csc1, csh1, w_in, qg, kg, seg)
            bias = _bias_table(na_rpb[e])
            y1 = _natten(ql, kl, vl, kc, vc, bias)
            y2 = _short_gated_conv(gbl, pl_, sc_conv_w[e])
            lat_cm = (False, False)
            if ctx_needed:
                y1c = _ctx_attention(qc, kc, vc)
                y2c = _short_gated_conv(gbc, pc, sc_conv_w[e])
        else:
            o = l // 2
            wht = w_in_odd[o][:, :3 * HY_WIDTH].T.astype(BF16)
            wag = w_in_odd[o][:, 3 * HY_WIDTH:].astype(BF16)
            hyt, ag = _inproj_odd(x, n1g, sc1, sh1, wht, wag)
            taps = _hyena_taps(seq, hy_w1[o], hy_b1[o], hy_w2[o], hy_b2[o], hy_w3[o], hy_freq[o])
            h_spec = _filter_fft(taps, consts)
            y1 = _hyena_latent(hyt, hy_short_w[o], h_spec, hy_bias[o], consts)
            cf_args = (cf_conv_w[o], cf_conv_b[o], cf_ln_g[o], cf_ln_b[o])
            y2 = _conformer(ag, *cf_args)
            lat_cm = (True, False)
            if ctx_needed:
                hytc, agc = _inproj_odd(ctx, n1g, csc1, csh1, wht, wag)
                taps_c = _hyena_taps(lc, hy_w1[o], hy_b1[o], hy_w2[o], hy_b2[o], hy_w3[o], hy_freq[o])
                y1c = _hyena_context(hytc, hy_short_w[o], taps_c, hy_bias[o])
                y2c = _conformer(agc, *cf_args)

        x1, cmb, hs, cs = _outproj(y1, y2, w_out_bf, x, g1, n2g, sc2, sh2, wr2, br, lat_cm)
        x = _moe(hs, cs, cmb, x1, g2, wg, wu, wd, l)
        if ctx_needed:
            c1, cmbc, hsc, csc = _outproj(y1c, y2c, w_out_bf, ctx, cg1, n2g, csc2, csh2, wr2, br, lat_cm)
            ctx = _moe(hsc, csc, cmbc, c1, cg2, wg, wu, wd, l)
    return x
```

```python
import functools
import math

import numpy as np
import jax
import jax.numpy as jnp
from jax import lax
from jax.experimental import pallas as pl
from jax.experimental.pallas import tpu as pltpu

F32 = jnp.float32
BF16 = jnp.bfloat16

D_MODEL = 1024
GRID_W = 64
NA_HEADS = 8
HEAD_DIM = 64
NA_WIDTH = 512
NA_WIN_H = 8
NA_WIN_W = 16
SC_WIDTH = 512
HY_WIDTH = 512
HY_BANDS = 16
HY_EMB = 1 + 2 * HY_BANDS
HY_FFN = 64
HY_MAX_DECAY = math.log(1e-2) / 0.3
HY_MIN_DECAY = math.log(1e-2) / 1.5
CF_WIDTH = 512
CF_TAPS = 31
N_GROUPS = 4
EXPERTS_PER_GROUP = 4
N_EXPERTS = 16
D_EXPERT = 256
RMS_EPS = 1e-6
LN_EPS = 1e-5
NEG_INF = -1e30

VMEM_LIMIT_BYTES = 56 * 1024 * 1024
LANES = 128

FFT_NA = 64
FFT_NB = 128
FFT_N = FFT_NA * FFT_NB


def _params(*sem):
    return pltpu.CompilerParams(dimension_semantics=tuple(sem), vmem_limit_bytes=VMEM_LIMIT_BYTES)


def _dot(a, b):
    return jnp.dot(a, b, preferred_element_type=F32)


def _dot_nt(a, b):
    return lax.dot_general(a, b, (((1,), (1,)), ((), ())), preferred_element_type=F32)


def _dot_tn(a, b):
    return lax.dot_general(a, b, (((0,), (0,)), ((), ())), preferred_element_type=F32)


def _dot_f32(a, b):
    return jnp.dot(a, b, preferred_element_type=F32, precision=lax.Precision.HIGHEST)


def _silu(x):
    return x * jax.nn.sigmoid(x)


def _ada_kernel(ct_ref, w_ref, b_ref, o_ref, *, n_cond):
    ct = ct_ref[...]
    s = _silu(ct)
    w = w_ref[0]
    rows = [jnp.sum(w * s[:, r:r + 1], axis=0, keepdims=True) for r in range(n_cond)]
    rows.append(jnp.zeros((8 - n_cond, w.shape[1]), F32))
    o_ref[0] = jnp.concatenate(rows, axis=0) + b_ref[0]


def _ada_modulation(cond, ada_w, ada_b):
    n_cond, d = cond.shape
    depth, _, n6 = ada_w.shape
    tn = 1536
    ct = jnp.zeros((d, 8), F32).at[:, :n_cond].set(cond.T)
    return pl.pallas_call(
        functools.partial(_ada_kernel, n_cond=n_cond),
        out_shape=jax.ShapeDtypeStruct((depth, 8, n6), F32),
        grid=(depth, n6 // tn),
        in_specs=[pl.BlockSpec((d, 8), lambda l, j: (0, 0)),
                  pl.BlockSpec((1, d, tn), lambda l, j: (l, 0, j)),
                  pl.BlockSpec((1, 1, tn), lambda l, j: (l, 0, j))],
        out_specs=pl.BlockSpec((1, 8, tn), lambda l, j: (l, 0, j)),
        compiler_params=_params("parallel", "parallel"),
        name="ada_modulation",
    )(ct, ada_w, ada_b.reshape(depth, 1, n6))


def _norm_modulate(x, g, sc, sh):
    ms = jnp.mean(x * x, axis=-1, keepdims=True)
    return x * lax.rsqrt(ms + RMS_EPS) * g * (1.0 + sc) + sh


def _head_rmsnorm(t, seg, gain):
    ss = _dot((t * t).astype(BF16), seg)
    return t * lax.rsqrt(ss * (1.0 / HEAD_DIM) + RMS_EPS) * gain


def _inproj_even_kernel(x_ref, g_ref, sc_ref, sh_ref, w_ref, qg_ref, kg_ref, seg_ref,
                        q_ref, k_ref, v_ref, gb_ref, p_ref):
    u = _norm_modulate(x_ref[0], g_ref[...], sc_ref[0], sh_ref[0]).astype(BF16)
    seg = seg_ref[...]
    w = NA_WIDTH
    q = _dot(u, w_ref[:, 0 * w:1 * w])
    q_ref[0] = (_head_rmsnorm(q, seg, qg_ref[...]) * (HEAD_DIM ** -0.5)).astype(BF16)
    k = _dot(u, w_ref[:, 1 * w:2 * w])
    k_ref[0] = _head_rmsnorm(k, seg, kg_ref[...]).astype(BF16)
    v_ref[0] = _dot(u, w_ref[:, 2 * w:3 * w]).astype(BF16)
    gb_ref[0] = _dot(u, w_ref[:, 3 * w:4 * w]).astype(BF16)
    gc = _dot(u, w_ref[:, 4 * w:5 * w])
    hv = _dot(u, w_ref[:, 5 * w:6 * w])
    p_ref[0] = (gc * hv).astype(BF16)


def _inproj_even(x, g, sc, sh, w_bf, qg, kg, seg):
    b, n, d = x.shape
    tm = min(512, n)
    tok = lambda i, j: (i, j, 0)
    mod = lambda i, j: (i, 0, 0)
    const = lambda i, j: (0, 0)
    out = jax.ShapeDtypeStruct((b, n, NA_WIDTH), BF16)
    return pl.pallas_call(
        _inproj_even_kernel,
        out_shape=(out,) * 5,
        grid=(b, n // tm),
        in_specs=[pl.BlockSpec((1, tm, d), tok),
                  pl.BlockSpec((1, d), const),
                  pl.BlockSpec((1, 1, d), mod),
                  pl.BlockSpec((1, 1, d), mod),
                  pl.BlockSpec(w_bf.shape, const),
                  pl.BlockSpec((1, NA_WIDTH), const),
                  pl.BlockSpec((1, NA_WIDTH), const),
                  pl.BlockSpec((NA_WIDTH, NA_WIDTH), const)],
        out_specs=(pl.BlockSpec((1, tm, NA_WIDTH), tok),) * 5,
        compiler_params=_params("parallel", "parallel"),
        name="inproj_even",
    )(x, g, sc, sh, w_bf, qg, kg, seg)


def _bias_kernel(rpb_ref, o_ref):
    h = pl.program_id(0)
    qi = lax.broadcasted_iota(jnp.int32, (GRID_W, GRID_W), 0)
    ki = lax.broadcasted_iota(jnp.int32, (GRID_W, GRID_W), 1)
    start = jnp.clip(qi - NA_WIN_W // 2, 0, GRID_W - NA_WIN_W)
    valid = jnp.logical_and(ki >= start, ki < start + NA_WIN_W)
    cidx = jnp.clip(ki - qi, -(NA_WIN_W - 1), NA_WIN_W - 1) + (NA_WIN_W - 1)
    n_dr = 2 * NA_WIN_H - 1
    n_dc = 2 * NA_WIN_W - 1

    def body(j, accs):
        m = cidx == j
        return tuple(jnp.where(m, rpb_ref[(h * n_dr + d) * n_dc + j], a) for d, a in enumerate(accs))

    accs = lax.fori_loop(0, n_dc, body, tuple(jnp.zeros((GRID_W, GRID_W), F32) for _ in range(n_dr)))
    tiles = [jnp.where(valid, a, NEG_INF) for a in accs]
    for d0 in range(NA_WIN_H):
        o_ref[0, d0] = jnp.concatenate(tiles[d0:d0 + NA_WIN_H], axis=1)


def _bias_table(rpb):
    return pl.pallas_call(
        _bias_kernel,
        out_shape=jax.ShapeDtypeStruct((NA_HEADS, NA_WIN_H, GRID_W, NA_WIN_H * GRID_W), F32),
        grid=(NA_HEADS,),
        in_specs=[pl.BlockSpec(memory_space=pltpu.SMEM)],
        out_specs=pl.BlockSpec((1, NA_WIN_H, GRID_W, NA_WIN_H * GRID_W), lambda h: (h, 0, 0, 0)),
        compiler_params=_params("arbitrary"),
        name="rpb_bias_table",
    )(rpb.reshape(-1))


def _pair_attention(q2, kw, vw, bias, kc, vc, first_half):
    s_c = _dot_nt(q2, kc)
    m = jnp.max(s_c, axis=-1, keepdims=True)
    if kw is not None:
        s_w = _dot_nt(q2, kw) + bias
        m = jnp.maximum(m, jnp.max(s_w, axis=-1, keepdims=True))
        p_w = jnp.exp(s_w - m)
    p_c = jnp.exp(s_c - m)
    den = jnp.sum(p_c, axis=-1, keepdims=True)
    o = _dot(p_c.astype(BF16), vc)
    if kw is not None:
        den = den + jnp.sum(p_w, axis=-1, keepdims=True)
        o = o + _dot(p_w.astype(BF16), vw)
    o = o / den
    half = o.shape[0] // 2
    return jnp.where(first_half, o[:half], o[half:])


def _stack_heads(qp, first_half):
    zero = jnp.zeros_like(qp)
    return jnp.concatenate([jnp.where(first_half, qp, zero), jnp.where(first_half, zero, qp)], axis=0)


def _natten_kernel(q_ref, k_ref, v_ref, kc_ref, vc_ref, bias_ref, o_ref, *, rows_per_step, n_rows):
    blk = pl.program_id(1)
    lane = lax.broadcasted_iota(jnp.int32, (GRID_W, LANES), 1)
    first_half = lane < HEAD_DIM
    band = NA_WIN_H * GRID_W

    n_pairs = NA_HEADS // 2

    def row_body(j, carry):
        r = blk * rows_per_step + j
        start = jnp.clip(r - NA_WIN_H // 2, 0, n_rows - NA_WIN_H)
        d0 = start - r + (NA_WIN_H - 1)
        koff = pl.multiple_of(start * GRID_W, GRID_W)
        qoff = pl.multiple_of(j * GRID_W, GRID_W)
        cols = [slice(hp * LANES, (hp + 1) * LANES) for hp in range(n_pairs)]
        q2 = [_stack_heads(q_ref[0, pl.ds(qoff, GRID_W), cs], first_half) for cs in cols]
        s_w = [_dot_nt(q2[hp], k_ref[0, pl.ds(koff, band), cols[hp]])
               + jnp.concatenate([bias_ref[2 * hp, d0], bias_ref[2 * hp + 1, d0]], axis=0) for hp in range(n_pairs)]
        s_c = [_dot_nt(q2[hp], kc_ref[0, :, cols[hp]]) for hp in range(n_pairs)]
        m = [jnp.maximum(jnp.max(s_w[hp], axis=-1, keepdims=True), jnp.max(s_c[hp], axis=-1, keepdims=True))
             for hp in range(n_pairs)]
        p_w = [jnp.exp(s_w[hp] - m[hp]) for hp in range(n_pairs)]
        p_c = [jnp.exp(s_c[hp] - m[hp]) for hp in range(n_pairs)]
        den = [jnp.sum(p_w[hp], axis=-1, keepdims=True) + jnp.sum(p_c[hp], axis=-1, keepdims=True)
               for hp in range(n_pairs)]
        outs = []
        for hp in range(n_pairs):
            o = (_dot(p_w[hp].astype(BF16), v_ref[0, pl.ds(koff, band), cols[hp]])
                 + _dot(p_c[hp].astype(BF16), vc_ref[0, :, cols[hp]])) / den[hp]
            outs.append(jnp.where(first_half, o[:GRID_W], o[GRID_W:]))
        o_ref[0, pl.ds(qoff, GRID_W), :] = jnp.concatenate(outs, axis=1).astype(o_ref.dtype)
        return carry

    lax.fori_loop(0, rows_per_step, row_body, 0, unroll=2)


def _natten(q, k, v, kc, vc, bias):
    b, n, w = q.shape
    n_rows = n // GRID_W
    rows_per_step = 8
    tq = rows_per_step * GRID_W
    lc = kc.shape[1]
    return pl.pallas_call(
        functools.partial(_natten_kernel, rows_per_step=rows_per_step, n_rows=n_rows),
        out_shape=jax.ShapeDtypeStruct((b, n, w), BF16),
        grid=(b, n_rows // rows_per_step),
        in_specs=[pl.BlockSpec((1, tq, w), lambda i, j: (i, j, 0)),
                  pl.BlockSpec((1, n, w), lambda i, j: (i, 0, 0)),
                  pl.BlockSpec((1, n, w), lambda i, j: (i, 0, 0)),
                  pl.BlockSpec((1, lc, w), lambda i, j: (i, 0, 0)),
                  pl.BlockSpec((1, lc, w), lambda i, j: (i, 0, 0)),
                  pl.BlockSpec(bias.shape, lambda i, j: (0, 0, 0, 0))],
        out_specs=pl.BlockSpec((1, tq, w), lambda i, j: (i, j, 0)),
        compiler_params=_params("parallel", "arbitrary"),
        name="neighbourhood_attention",
    )(q, k, v, kc, vc, bias)


def _ctx_attn_kernel(q_ref, k_ref, v_ref, o_ref):
    lc = q_ref.shape[1]
    lane = lax.broadcasted_iota(jnp.int32, (lc, LANES), 1)
    first_half = lane < HEAD_DIM
    outs = []
    for hp in range(NA_HEADS // 2):
        cs = slice(hp * LANES, (hp + 1) * LANES)
        q2 = _stack_heads(q_ref[0, :, cs], first_half)
        outs.append(_pair_attention(q2, None, None, None, k_ref[0, :, cs], v_ref[0, :, cs], first_half))
    o_ref[0] = jnp.concatenate(outs, axis=1).astype(o_ref.dtype)


def _ctx_attention(q, k, v):
    b, lc, w = q.shape
    spec = pl.BlockSpec((1, lc, w), lambda i: (i, 0, 0))
    return pl.pallas_call(
        _ctx_attn_kernel,
        out_shape=jax.ShapeDtypeStruct((b, lc, w), BF16),
        grid=(b,),
        in_specs=[spec, spec, spec],
        out_specs=spec,
        compiler_params=_params("parallel"),
        name="context_attention",
    )(q, k, v)


def _sgconv_kernel(gb_ref, p_ref, w_ref, o_ref):
    p = p_ref[0].astype(F32)
    n = p.shape[0]
    row = lax.broadcasted_iota(jnp.int32, p.shape, 0)
    prev = jnp.where(row == 0, 0.0, pltpu.roll(p, 1, axis=0))
    nxt = jnp.where(row == n - 1, 0.0, pltpu.roll(p, n - 1, axis=0))
    y = w_ref[0:1, :] * prev + w_ref[1:2, :] * p + w_ref[2:3, :] * nxt
    o_ref[0] = (gb_ref[0].astype(F32) * y).astype(o_ref.dtype)


def _short_gated_conv(gb, p, w):
    b, n, c = p.shape
    spec = pl.BlockSpec((1, n, LANES), lambda i, j: (i, 0, j))
    return pl.pallas_call(
        _sgconv_kernel,
        out_shape=jax.ShapeDtypeStruct((b, n, c), BF16),
        grid=(b, c // LANES),
        in_specs=[spec, spec, pl.BlockSpec((3, LANES), lambda i, j: (0, j))],
        out_specs=spec,
        compiler_params=_params("parallel", "parallel"),
        name="short_gated_conv",
    )(gb, p, w)


def _inproj_odd_kernel(x_ref, g_ref, sc_ref, sh_ref, wht_ref, wag_ref, hy_ref, ag_ref):
    u = _norm_modulate(x_ref[0], g_ref[...], sc_ref[0], sh_ref[0]).astype(BF16)
    hy_ref[0] = _dot_nt(wht_ref[...], u).astype(BF16)
    a = _dot(u, wag_ref[:, :CF_WIDTH])
    g = _dot(u, wag_ref[:, CF_WIDTH:])
    ag_ref[0] = (a * jax.nn.sigmoid(g)).astype(BF16)


def _inproj_odd(x, g, sc, sh, wht_bf, wag_bf):
    b, n, d = x.shape
    tm = min(512, n)
    hw = wht_bf.shape[0]
    tok = lambda i, j: (i, j, 0)
    mod = lambda i, j: (i, 0, 0)
    const = lambda i, j: (0, 0)
    return pl.pallas_call(
        _inproj_odd_kernel,
        out_shape=(jax.ShapeDtypeStruct((b, hw, n), BF16), jax.ShapeDtypeStruct((b, n, CF_WIDTH), BF16)),
        grid=(b, n // tm),
        in_specs=[pl.BlockSpec((1, tm, d), tok),
                  pl.BlockSpec((1, d), const),
                  pl.BlockSpec((1, 1, d), mod),
                  pl.BlockSpec((1, 1, d), mod),
                  pl.BlockSpec(wht_bf.shape, const),
                  pl.BlockSpec(wag_bf.shape, const)],
        out_specs=(pl.BlockSpec((1, hw, tm), lambda i, j: (i, 0, j)), pl.BlockSpec((1, tm, CF_WIDTH), tok)),
        compiler_params=_params("parallel", "parallel"),
        name="inproj_odd",
    )(x, g, sc, sh, wht_bf, wag_bf)


SUBLANES = 8
CF_PAD = 2 * SUBLANES
CF_ROWS = 128


def _conformer_kernel(ag_ref, w_ref, cb_ref, lg_ref, lb_ref, o_ref, pad_ref, *, seq):
    zeros = jnp.zeros((CF_PAD, CF_WIDTH), F32)
    pad_ref[0:CF_PAD, :] = zeros
    pad_ref[CF_PAD + seq:2 * CF_PAD + seq, :] = zeros
    pad_ref[CF_PAD:CF_PAD + seq, :] = ag_ref[0].astype(F32)
    shift0 = CF_PAD - CF_TAPS // 2
    n_groups = (shift0 + CF_TAPS - 1) // SUBLANES + 1

    def conv_rows(i, carry):
        n0 = pl.multiple_of(i * CF_ROWS, CF_ROWS)
        wins = [pad_ref[pl.ds(n0 + SUBLANES * a, CF_ROWS + SUBLANES), :] for a in range(n_groups)]
        acc = None
        for b in range(SUBLANES):
            part = None
            for a in range(n_groups):
                j = SUBLANES * a + b - shift0
                if 0 <= j < CF_TAPS:
                    term = w_ref[j:j + 1, :] * wins[a]
                    part = term if part is None else part + term
            part = part[b:b + CF_ROWS, :]
            acc = part if acc is None else acc + part
        y = acc + cb_ref[...]
        mu = jnp.mean(y, axis=-1, keepdims=True)
        yc = y - mu
        var = jnp.mean(yc * yc, axis=-1, keepdims=True)
        z = yc * lax.rsqrt(var + LN_EPS) * lg_ref[...] + lb_ref[...]
        o_ref[0, pl.ds(n0, CF_ROWS), :] = _silu(z).astype(o_ref.dtype)
        return carry

    lax.fori_loop(0, seq // CF_ROWS, conv_rows, 0)


def _conformer(ag, w, cb, lg, lb):
    b, n, c = ag.shape
    spec = pl.BlockSpec((1, n, c), lambda i: (i, 0, 0))
    vec = pl.BlockSpec((1, c), lambda i: (0, 0))
    return pl.pallas_call(
        functools.partial(_conformer_kernel, seq=n),
        out_shape=jax.ShapeDtypeStruct((b, n, c), BF16),
        grid=(b,),
        in_specs=[spec, pl.BlockSpec((CF_TAPS, c), lambda i: (0, 0)), vec, vec, vec],
        out_specs=spec,
        scratch_shapes=[pltpu.VMEM((n + 2 * CF_PAD, c), F32)],
        compiler_params=_params("parallel"),
        name="conformer_conv",
    )(ag, w, cb[None, :], lg[None, :], lb[None, :])


def _hyena_features(length):
    t = np.linspace(0.0, 1.0, length, dtype=np.float32)
    w = (2.0 * math.pi * np.arange(length, dtype=np.float32) / length).astype(np.float32)
    bands = np.linspace(1e-4, HY_BANDS - 1, HY_BANDS, dtype=np.float32)
    ang = (bands[:, None] * w[None, :]).astype(np.float32)
    zt = np.concatenate([t[None, :], np.cos(ang), -np.sin(ang)], axis=0).astype(np.float32)
    deltas = np.abs(np.linspace(HY_MIN_DECAY, HY_MAX_DECAY, HY_WIDTH, dtype=np.float32))
    rev = (length - np.arange(length)) % length
    zt2 = np.concatenate([zt, zt[:, rev]], axis=1)
    t2 = np.concatenate([t, t[rev]])[None, :]
    return zt2, t2, deltas[:, None]


def _taps_kernel(zt_ref, t_ref, dl_ref, w1t_ref, b1_ref, f0_ref, w2t_ref, b2_ref, f1_ref, w3t_ref,
                 o_ref, hid_ref):
    first = jnp.logical_and(pl.program_id(0) == 0, pl.program_id(1) == 0)
    length = t_ref.shape[1] // 2

    @pl.when(first)
    def _():
        h1 = jnp.sin(f0_ref[...] * (_dot_f32(w1t_ref[...], zt_ref[...]) + b1_ref[...]))
        hid_ref[...] = jnp.sin(f1_ref[...] * (_dot_f32(w2t_ref[...], h1) + b2_ref[...]))

    decay = jnp.exp(-(dl_ref[...] * t_ref[...]))
    fwd = _dot_f32(w3t_ref[0, 0], hid_ref[:, :length])
    bwd = _dot_f32(w3t_ref[0, 1], hid_ref[:, length:])
    taps = jnp.concatenate([fwd, bwd], axis=1) * decay
    nrm = jnp.sum(jnp.abs(taps), axis=-1, keepdims=True)
    lane = lax.broadcasted_iota(jnp.int32, taps.shape, 1)
    o_ref[0] = jnp.where(lane == length, 0.0, taps / nrm)


def _hyena_taps(length, w1, b1, w2, b2, w3, freq):
    zt2, t2, deltas = _hyena_features(length)
    cb = 128
    w3t = w3.T.reshape(2, 2, HY_WIDTH, HY_FFN)
    col = lambda v: v.reshape(HY_FFN, 1)
    const = lambda o, j: (0, 0)
    return pl.pallas_call(
        _taps_kernel,
        out_shape=jax.ShapeDtypeStruct((2, HY_WIDTH, 2 * length), F32),
        grid=(2, HY_WIDTH // cb),
        in_specs=[pl.BlockSpec((HY_EMB, 2 * length), const),
                  pl.BlockSpec((1, 2 * length), const),
                  pl.BlockSpec((cb, 1), lambda o, j: (j, 0)),
                  pl.BlockSpec((HY_FFN, HY_EMB), const),
                  pl.BlockSpec((HY_FFN, 1), const),
                  pl.BlockSpec((HY_FFN, 1), const),
                  pl.BlockSpec((HY_FFN, HY_FFN), const),
                  pl.BlockSpec((HY_FFN, 1), const),
                  pl.BlockSpec((HY_FFN, 1), const),
                  pl.BlockSpec((1, 2, cb, HY_FFN), lambda o, j: (o, 0, j, 0))],
        out_specs=pl.BlockSpec((1, cb, 2 * length), lambda o, j: (o, j, 0)),
        scratch_shapes=[pltpu.VMEM((HY_FFN, 2 * length), F32)],
        compiler_params=_params("arbitrary", "arbitrary"),
        name="hyena_filter_taps",
    )(jnp.asarray(zt2), jnp.asarray(t2), jnp.asarray(deltas), w1.T, col(b1), col(freq[0]),
      w2.T, col(b2), col(freq[1]), w3t)


def _dft_constants():
    na, nb, n = FFT_NA, FFT_NB, FFT_N
    half = na // 2
    ka = np.arange(na)
    ang_a = 2.0 * np.pi * np.outer(ka, ka) / na
    ca, sa = np.cos(ang_a), np.sin(ang_a)
    fa = np.block([[ca[:half], -sa[:half]], [sa[:half], ca[:half]]])
    fai = np.block([[ca[:, :half], sa[:, :half]], [-sa[:, :half], ca[:, :half]]])
    kb = np.arange(nb)
    ang_b = 2.0 * np.pi * np.outer(kb, kb) / nb
    cbm, sbm = np.cos(ang_b), np.sin(ang_b)
    fb = np.block([[cbm, -sbm], [sbm, cbm]])
    fbi = np.block([[cbm, sbm], [-sbm, cbm]])
    ang_t = 2.0 * np.pi * np.outer(kb, ka) / n
    ct, st = np.cos(ang_t), np.sin(ang_t)
    tw_fc = np.concatenate([ct, ct], axis=1)
    tw_fs = np.concatenate([st, -st], axis=1)
    tw_ic, tw_is = ct.T.copy(), st.T.copy()
    bf = lambda a: jnp.asarray(a, dtype=F32).astype(BF16)
    f32 = lambda a: jnp.asarray(a, dtype=F32)
    fa_real = np.concatenate([ca, -sa], axis=1)
    return dict(fa=bf(fa), fa_real=bf(fa_real), fai=bf(fai), fb=bf(fb), fbi=bf(fbi),
                tw_fc=f32(tw_fc), tw_fs=f32(tw_fs), tw_ic=f32(tw_ic), tw_is=f32(tw_is))


def _fft_forward(zr, zi, fa, tw_fc, tw_fs, fb):
    c, _, nb = zr.shape
    tr = jnp.swapaxes(zr, 1, 2)
    lhs = tr if zi is None else jnp.concatenate([tr, jnp.swapaxes(zi, 1, 2)], axis=2)
    a = _dot(lhs.reshape(c * nb, lhs.shape[2]).astype(BF16), fa).reshape(c, nb, 2 * FFT_NA)
    a = a * tw_fc + pltpu.roll(a, FFT_NA, axis=2) * tw_fs
    t = jnp.swapaxes(a, 1, 2)
    lhs2 = jnp.concatenate([t[:, :FFT_NA, :], t[:, FFT_NA:, :]], axis=2)
    x = _dot(lhs2.reshape(c * FFT_NA, 2 * nb).astype(BF16), fb)
    return x.reshape(c, FFT_NA, 2 * nb)


def _fft_inverse(y, fbi, tw_ic, tw_is, fai):
    c = y.shape[0]
    nb = FFT_NB
    b = _dot(y.reshape(c * FFT_NA, 2 * nb).astype(BF16), fbi).reshape(c, FFT_NA, 2 * nb)
    br, bi = b[:, :, :nb], b[:, :, nb:]
    rr = br * tw_ic - bi * tw_is
    ii = bi * tw_ic + br * tw_is
    t = jnp.swapaxes(jnp.concatenate([rr, ii], axis=1), 1, 2)
    o = _dot(t.reshape(c * nb, 2 * FFT_NA).astype(BF16), fai).reshape(c, nb, FFT_NA)
    o = jnp.swapaxes(o, 1, 2)
    return o[:, :FFT_NA // 2, :], o[:, FFT_NA // 2:, :]


def _filter_fft_kernel(taps_ref, fa_ref, twc_ref, tws_ref, fb_ref, o_ref):
    h = _fft_forward(taps_ref[0], None, fa_ref[...], twc_ref[...], tws_ref[...], fb_ref[...])
    o_ref[0] = h * (1.0 / FFT_N)


def _filter_fft(taps, consts):
    _, c, n = taps.shape
    cb = 32
    taps4 = taps.reshape(2, c, FFT_NA, FFT_NB)
    cm = lambda o, j: (0, 0)
    return pl.pallas_call(
        _filter_fft_kernel,
        out_shape=jax.ShapeDtypeStruct((2, c, FFT_NA, 2 * FFT_NB), F32),
        grid=(2, c // cb),
        in_specs=[pl.BlockSpec((1, cb, FFT_NA, FFT_NB), lambda o, j: (o, j, 0, 0)),
                  pl.BlockSpec(consts["fa_real"].shape, cm),
                  pl.BlockSpec(consts["tw_fc"].shape, cm),
                  pl.BlockSpec(consts["tw_fs"].shape, cm),
                  pl.BlockSpec(consts["fb"].shape, cm)],
        out_specs=pl.BlockSpec((1, cb, FFT_NA, 2 * FFT_NB), lambda o, j: (o, j, 0, 0)),
        compiler_params=_params("parallel", "parallel"),
        name="hyena_filter_fft",
    )(taps4, consts["fa_real"], consts["tw_fc"], consts["tw_fs"], consts["fb"])


def _shift_tokens(a, direction):
    rows = a.shape[-2]
    lane = lax.broadcasted_iota(jnp.int32, a.shape, a.ndim - 1)
    row = lax.broadcasted_iota(jnp.int32, a.shape, a.ndim - 2)
    if direction == 1:
        l = pltpu.roll(a, 1, axis=a.ndim - 1)
        ls = pltpu.roll(l, 1, axis=a.ndim - 2)
        out = jnp.where(lane == 0, ls, l)
        edge = jnp.logical_and(lane == 0, row == 0)
    else:
        l = pltpu.roll(a, LANES - 1, axis=a.ndim - 1)
        ls = pltpu.roll(l, rows - 1, axis=a.ndim - 2)
        out = jnp.where(lane == LANES - 1, ls, l)
        edge = jnp.logical_and(lane == LANES - 1, row == rows - 1)
    return jnp.where(edge, 0.0, out)


def _short_conv3(a, w_ref):
    return w_ref[0] * _shift_tokens(a, 1) + w_ref[1] * a + w_ref[2] * _shift_tokens(a, -1)


def _hyena_kernel(v_ref, x1_ref, x2_ref, wv_ref, w1_ref, w2_ref, h_ref, hb_ref,
                  fa_ref, twfc_ref, twfs_ref, fb_ref, fbi_ref, twic_ref, twis_ref, fai_ref, o_ref):
    fwd_c = (fa_ref[...], twfc_ref[...], twfs_ref[...], fb_ref[...])
    inv_c = (fbi_ref[...], twic_ref[...], twis_ref[...], fai_ref[...])
    z = _short_conv3(v_ref[...].astype(F32), wv_ref)
    zr, zi = z[0], z[1]
    nb = FFT_NB
    for o, (g_ref, gw_ref) in enumerate(((x1_ref, w1_ref), (x2_ref, w2_ref))):
        x = _fft_forward(zr, zi, *fwd_c)
        h = h_ref[o]
        xr, xi, hr, hi = x[:, :, :nb], x[:, :, nb:], h[:, :, :nb], h[:, :, nb:]
        y = jnp.concatenate([xr * hr - xi * hi, xr * hi + xi * hr], axis=2)
        yr, yi = _fft_inverse(y, *inv_c)
        gate = _short_conv3(g_ref[...].astype(F32), gw_ref)
        bias = hb_ref[o]
        zr = gate[0] * (yr + zr * bias)
        zi = gate[1] * (yi + zi * bias)
    o_ref[0] = zr.astype(o_ref.dtype)
    o_ref[1] = zi.astype(o_ref.dtype)


def _hyena_latent(hyt, short_w, h_spec, hy_bias, consts):
    b, c3, length = hyt.shape
    c = c3 // 3
    rows = length // FFT_NB
    cb = 32
    nblk = c // cb
    hy4 = hyt.reshape(b, c3, rows, FFT_NB)
    w4 = jnp.broadcast_to(short_w.reshape(3, c3, 1, 1), (3, c3, 1, FFT_NB))
    hb4 = jnp.broadcast_to(hy_bias.reshape(2, c, 1, 1), (2, c, 1, FFT_NB))
    sig = lambda g: pl.BlockSpec((2, cb, rows, FFT_NB), lambda j, p: (p, g * nblk + j, 0, 0))
    wsp = lambda g: pl.BlockSpec((3, cb, 1, FFT_NB), lambda j, p: (0, g * nblk + j, 0, 0))
    cm = lambda j, p: (0, 0)
    names = ("fa", "tw_fc", "tw_fs", "fb", "fbi", "tw_ic", "tw_is", "fai")
    out = pl.pallas_call(
        _hyena_kernel,
        out_shape=jax.ShapeDtypeStruct((b, c, rows, FFT_NB), BF16),
        grid=(nblk, b // 2),
        in_specs=[sig(0), sig(1), sig(2), wsp(0), wsp(1), wsp(2),
                  pl.BlockSpec((2, cb, FFT_NA, 2 * FFT_NB), lambda j, p: (0, j, 0, 0)),
                  pl.BlockSpec((2, cb, 1, FFT_NB), lambda j, p: (0, j, 0, 0))]
                 + [pl.BlockSpec(consts[k].shape, cm) for k in names],
        out_specs=pl.BlockSpec((2, cb, rows, FFT_NB), lambda j, p: (p, j, 0, 0)),
        compiler_params=_params("parallel", "arbitrary"),
        name="hyena_long_conv",
    )(hy4, hy4, hy4, w4, w4, w4, h_spec, hb4, *[consts[k] for k in names])
    return out.reshape(b, c, length)


def _dense_dft_constants(length):
    n = 2 * length
    k = np.arange(n)
    ang = 2.0 * np.pi * np.outer(k, k) / n
    fwd = np.concatenate([np.cos(ang), -np.sin(ang)], axis=1)
    inv = np.concatenate([np.cos(ang[:length]).T, -np.sin(ang[:length]).T], axis=0) / n
    bf = lambda a: jnp.asarray(a, dtype=F32).astype(BF16)
    return bf(fwd), bf(inv)


def _shift_lanes(a, direction):
    n = a.shape[-1]
    lane = lax.broadcasted_iota(jnp.int32, a.shape, a.ndim - 1)
    if direction == 1:
        return jnp.where(lane == 0, 0.0, pltpu.roll(a, 1, axis=a.ndim - 1))
    return jnp.where(lane == n - 1, 0.0, pltpu.roll(a, n - 1, axis=a.ndim - 1))


def _hyena_ctx_kernel(v_ref, x1_ref, x2_ref, wv_ref, w1_ref, w2_ref, taps_ref, hb_ref, fwd_ref, inv_ref, o_ref):
    bsz, cb, length = v_ref.shape
    n = 2 * length
    fwd_m, inv_m = fwd_ref[0:length, :], inv_ref[...]

    def conv3(ref, w_ref):
        a = ref[...].astype(F32)
        return w_ref[0] * _shift_lanes(a, 1) + w_ref[1] * a + w_ref[2] * _shift_lanes(a, -1)

    z = conv3(v_ref, wv_ref)
    for o, (g_ref, gw_ref) in enumerate(((x1_ref, w1_ref), (x2_ref, w2_ref))):
        h = _dot(taps_ref[o].astype(BF16), fwd_ref[...])
        hr, hi = h[:, :n], h[:, n:]
        x = _dot(z.reshape(bsz * cb, length).astype(BF16), fwd_m).reshape(bsz, cb, 2 * n)
        xr, xi = x[:, :, :n], x[:, :, n:]
        y = jnp.concatenate([xr * hr - xi * hi, xr * hi + xi * hr], axis=2)
        yt = _dot(y.reshape(bsz * cb, 2 * n).astype(BF16), inv_m).reshape(bsz, cb, length)
        z = conv3(g_ref, gw_ref) * (yt + z * hb_ref[o])
    o_ref[...] = z.astype(o_ref.dtype)


def _hyena_context(hyt, short_w, taps, hy_bias):
    b, c3, length = hyt.shape
    c = c3 // 3
    cb = 128
    nblk = c // cb
    fwd_m, inv_m = _dense_dft_constants(length)
    w4 = jnp.broadcast_to(short_w.reshape(3, c3, 1), (3, c3, length))
    hb = jnp.broadcast_to(hy_bias.reshape(2, c, 1), (2, c, length))
    sig = lambda g: pl.BlockSpec((b, cb, length), lambda j: (0, g * nblk + j, 0))
    wsp = lambda g: pl.BlockSpec((3, cb, length), lambda j: (0, g * nblk + j, 0))
    return pl.pallas_call(
        _hyena_ctx_kernel,
        out_shape=jax.ShapeDtypeStruct((b, c, length), BF16),
        grid=(nblk,),
        in_specs=[sig(0), sig(1), sig(2), wsp(0), wsp(1), wsp(2),
                  pl.BlockSpec((2, cb, 2 * length), lambda j: (0, j, 0)),
                  pl.BlockSpec((2, cb, length), lambda j: (0, j, 0)),
                  pl.BlockSpec(fwd_m.shape, lambda j: (0, 0)),
                  pl.BlockSpec(inv_m.shape, lambda j: (0, 0))],
        out_specs=pl.BlockSpec((b, cb, length), lambda j: (0, j, 0)),
        compiler_params=_params("parallel"),
        name="hyena_context_conv",
    )(hyt, hyt, hyt, w4, w4, w4, taps, hb, fwd_m, inv_m)


def _route(gl, el):
    row = lax.broadcasted_iota(jnp.int32, gl.shape, 0).astype(F32)
    grp = jnp.floor(row * (1.0 / EXPERTS_PER_GROUP))
    big = float(N_EXPERTS)
    gmax = jnp.max(gl, axis=0, keepdims=True)
    gidx = jnp.min(jnp.where(gl == gmax, grp, big), axis=0, keepdims=True)
    gsum = jnp.sum(jnp.exp(gl - gmax), axis=0, keepdims=True) * (1.0 / EXPERTS_PER_GROUP)
    g_w = 1.0 / gsum
    em = jnp.where(grp == gidx, el, NEG_INF)
    t1 = jnp.max(em, axis=0, keepdims=True)
    i1 = jnp.min(jnp.where(em == t1, row, big), axis=0, keepdims=True)
    em2 = jnp.where(row == i1, 2.0 * NEG_INF, em)
    t2 = jnp.max(em2, axis=0, keepdims=True)
    i2 = jnp.min(jnp.where(em2 == t2, row, big), axis=0, keepdims=True)
    e2 = jnp.exp(t2 - t1)
    den = 1.0 + e2
    w1 = g_w / den
    w2 = g_w * e2 / den
    return jnp.where(row == i1, w1, 0.0) + jnp.where(row == i2, w2, 0.0), gidx


SLOT_LANE = N_EXPERTS
CHUNKS_LANE = N_EXPERTS + 1
MOE_CHUNK = 64


def _tile_chunks(tm):
    return tm // MOE_CHUNK + N_GROUPS


def _dispatch_slots(gidx, tri):
    tm = gidx.shape[1]
    sub = lax.broadcasted_iota(jnp.int32, (SUBLANES, tm), 0)
    grp = sub.astype(F32)
    member = jnp.where(grp == gidx, 1.0, 0.0)
    rank = _dot(member.astype(BF16), tri)
    count = jnp.sum(member, axis=1, keepdims=True)
    chunks = jnp.floor((count + float(MOE_CHUNK - 1)) * (1.0 / MOE_CHUNK))
    first = jnp.zeros_like(chunks)
    for g in range(1, N_GROUPS):
        first = first + jnp.where(sub[:, 0:1] >= g, chunks[g - 1:g, :], 0.0)
    slot = jnp.sum(member * (first * float(MOE_CHUNK) + rank - 1.0), axis=0, keepdims=True)
    out = jnp.where(sub == 0, slot, 0.0)
    for g in range(N_GROUPS):
        out = out + jnp.where(sub == g + 1, chunks[g:g + 1, :], 0.0)
    return out


def _outproj_kernel(y1_ref, y2_ref, w_ref, x_ref, g1_ref, n2_ref, sc2_ref, sh2_ref, wr_ref, br_ref, tri_ref,
                    xo_ref, cmb_ref, hs_ref, cs_ref, *, channel_major):
    half = w_ref.shape[0] // 2
    dots = [_dot_tn if cm else _dot for cm in channel_major]
    acc = dots[0](y1_ref[0], w_ref[:half, :]) + dots[1](y2_ref[0], w_ref[half:, :])
    xn = x_ref[0] + g1_ref[0] * acc
    xo_ref[0] = xn
    h = _norm_modulate(xn, n2_ref[...], sc2_ref[0], sh2_ref[0])
    hi = h.astype(BF16)
    lo = (h - hi.astype(F32)).astype(BF16)
    p = _dot(hi, wr_ref[...])
    logits = p[:, :LANES] + p[:, LANES:] + _dot(lo, wr_ref[:, :LANES]) + br_ref[...]
    lt = logits.T
    cmb, gidx = _route(lt[0:N_EXPERTS], lt[N_EXPERTS:2 * N_EXPERTS])
    disp = _dispatch_slots(gidx, tri_ref[...])
    pad = jnp.zeros((LANES - N_EXPERTS - SUBLANES, cmb.shape[1]), F32)
    rows = jnp.concatenate([cmb, disp, pad], axis=0).T
    cmb_ref[0] = rows
    n_chunks = hs_ref.shape[0]
    srow = lax.broadcasted_iota(jnp.int32, (n_chunks * MOE_CHUNK, rows.shape[0]), 0).astype(F32)
    gather = jnp.where(srow == disp[0:1, :], 1.0, 0.0).astype(BF16)
    hs = _dot(gather, hi).astype(BF16)
    r_hi = rows.astype(BF16)
    r_lo = (rows - r_hi.astype(F32)).astype(BF16)
    cs2 = _dot(gather, jnp.concatenate([r_hi, r_lo], axis=1))
    cs = cs2[:, :LANES] + cs2[:, LANES:]
    for k in range(n_chunks):
        hs_ref[k] = hs[k * MOE_CHUNK:(k + 1) * MOE_CHUNK]
        cs_ref[k] = cs[k * MOE_CHUNK:(k + 1) * MOE_CHUNK]


def _outproj(y1, y2, w_bf, x, g1, n2g, sc2, sh2, wr2, br, channel_major):
    b, n, d = x.shape
    tm = min(512, n)
    nt = n // tm
    ntc = _tile_chunks(tm)
    half = d // 2
    tok = lambda i, j: (i, j, 0)
    mod = lambda i, j: (i, 0, 0)
    const = lambda i, j: (0, 0)
    srt = lambda i, j: (i * nt + j, 0, 0)
    y_specs = [pl.BlockSpec((1, half, tm), lambda i, j: (i, 0, j)) if cm else pl.BlockSpec((1, tm, half), tok)
               for cm in channel_major]
    return pl.pallas_call(
        functools.partial(_outproj_kernel, channel_major=tuple(channel_major)),
        out_shape=(jax.ShapeDtypeStruct((b, n, d), F32), jax.ShapeDtypeStruct((b, n, LANES), F32),
                   jax.ShapeDtypeStruct((b * nt * ntc, MOE_CHUNK, d), BF16),
                   jax.ShapeDtypeStruct((b * nt * ntc, MOE_CHUNK, LANES), F32)),
        grid=(b, nt),
        in_specs=[y_specs[0], y_specs[1],
                  pl.BlockSpec((d, d), const),
                  pl.BlockSpec((1, tm, d), tok),
                  pl.BlockSpec((1, 1, d), mod),
                  pl.BlockSpec((1, d), const),
                  pl.BlockSpec((1, 1, d), mod),
                  pl.BlockSpec((1, 1, d), mod),
                  pl.BlockSpec(wr2.shape, const),
                  pl.BlockSpec(br.shape, const),
                  pl.BlockSpec((tm, tm), const)],
        out_specs=(pl.BlockSpec((1, tm, d), tok), pl.BlockSpec((1, tm, LANES), tok),
                   pl.BlockSpec((ntc, MOE_CHUNK, d), srt), pl.BlockSpec((ntc, MOE_CHUNK, LANES), srt)),
        compiler_params=_params("parallel", "parallel"),
        name="outproj_router",
    )(y1, y2, w_bf, x, g1, n2g, sc2, sh2, wr2, br, jnp.asarray(np.triu(np.ones((tm, tm), np.float32)), dtype=BF16))


def _swiglu_group(h, w_rows, j, wg_ref, wu_ref, wd_ref):
    lane = lax.broadcasted_iota(jnp.int32, w_rows.shape, 1)
    experts = range(EXPERTS_PER_GROUP)
    w_e = [jnp.sum(jnp.where(lane == j * EXPERTS_PER_GROUP + e, w_rows, 0.0), axis=1, keepdims=True) for e in experts]
    a = [_dot(h, wg_ref[e]) for e in experts]
    u = [_dot(h, wu_ref[e]) for e in experts]
    act = [(_silu(a[e]) * u[e] * w_e[e]).astype(BF16) for e in experts]
    out = _dot(act[0], wd_ref[0])
    for e in experts[1:]:
        out = out + _dot(act[e], wd_ref[e])
    return out


MOE_STEP_CHUNKS = 8


def _moe_sorted_kernel(group_ref, used_ref, fresh_ref, src_ref, *refs):
    n = MOE_STEP_CHUNKS
    hs_refs, cs_refs = refs[:n], refs[n:2 * n]
    wg_ref, wu_ref, wd_ref, ys_ref, wg_bf, wu_bf, wd_bf = refs[2 * n:]
    s = pl.program_id(0)

    @pl.when(fresh_ref[s] > 0)
    def _():
        wg_bf[...] = wg_ref[...].astype(BF16)
        wu_bf[...] = wu_ref[...].astype(BF16)
        wd_bf[...] = wd_ref[...].astype(BF16)

    @pl.when(used_ref[s] > 0)
    def _():
        h = jnp.concatenate([r[0] for r in hs_refs], axis=0)
        w_rows = jnp.concatenate([r[0] for r in cs_refs], axis=0)
        y = _swiglu_group(h, w_rows, group_ref[s], wg_bf, wu_bf, wd_bf).astype(BF16)
        for k in range(n):
            ys_ref[k] = y[k * MOE_CHUNK:(k + 1) * MOE_CHUNK]

    @pl.when(used_ref[s] == 0)
    def _():
        ys_ref[...] = jnp.zeros(ys_ref.shape, ys_ref.dtype)


def _moe_sorted(hs, cs, step_group, step_used, step_fresh, chunk_src, wg, wu, wd, layer):
    _, _, d = hs.shape
    n = MOE_STEP_CHUNKS
    steps = step_group.shape[0]
    epg = EXPERTS_PER_GROUP
    chunk = lambda k, width: pl.BlockSpec((1, MOE_CHUNK, width),
                                          lambda s, grp, used, fresh, src: (src[s * n + k], 0, 0))
    wmap = lambda s, grp, used, fresh, src: (layer * N_GROUPS + grp[s], 0, 0)
    return pl.pallas_call(
        _moe_sorted_kernel,
        out_shape=jax.ShapeDtypeStruct((steps * n, MOE_CHUNK, d), BF16),
        grid_spec=pltpu.PrefetchScalarGridSpec(
            num_scalar_prefetch=4,
            grid=(steps,),
            in_specs=[chunk(k, d) for k in range(n)] + [chunk(k, LANES) for k in range(n)]
                     + [pl.BlockSpec((epg, d, D_EXPERT), wmap),
                        pl.BlockSpec((epg, d, D_EXPERT), wmap),
                        pl.BlockSpec((epg, D_EXPERT, d), wmap)],
            out_specs=pl.BlockSpec((n, MOE_CHUNK, d), lambda s, grp, used, fresh, src: (s, 0, 0)),
            scratch_shapes=[pltpu.VMEM((epg, d, D_EXPERT), BF16), pltpu.VMEM((epg, d, D_EXPERT), BF16),
                            pltpu.VMEM((epg, D_EXPERT, d), BF16)]),
        compiler_params=_params("arbitrary"),
        name="moe_sorted_experts",
    )(step_group, step_used, step_fresh, chunk_src, *([hs] * n), *([cs] * n), wg, wu, wd)


def _moe_combine_kernel(pos_ref, *refs):
    n_chunks = len(refs) - 4
    ys_refs = refs[:n_chunks]
    cmb_ref, x_ref, g2_ref, o_ref = refs[n_chunks:]
    cmb = cmb_ref[0]
    lane = lax.broadcasted_iota(jnp.int32, cmb.shape, 1)
    slot = jnp.sum(jnp.where(lane == SLOT_LANE, cmb, 0.0), axis=1, keepdims=True)
    rows = lax.broadcasted_iota(jnp.int32, (cmb.shape[0], n_chunks * MOE_CHUNK), 1).astype(F32)
    scatter = jnp.where(slot == rows, 1.0, 0.0).astype(BF16)
    ys = jnp.concatenate([r[0] for r in ys_refs], axis=0)
    o_ref[0] = x_ref[0] + g2_ref[0] * _dot(scatter, ys)


def _moe_combine(ys, chunk_pos, cmb, x, g2):
    b, n, d = x.shape
    tm = min(512, n)
    nt = n // tm
    ntc = _tile_chunks(tm)
    tok = lambda i, t, pos: (i, t, 0)
    chunk = lambda k: pl.BlockSpec((1, MOE_CHUNK, d), lambda i, t, pos: (pos[(i * nt + t) * ntc + k], 0, 0))
    return pl.pallas_call(
        _moe_combine_kernel,
        out_shape=jax.ShapeDtypeStruct((b, n, d), F32),
        grid_spec=pltpu.PrefetchScalarGridSpec(
            num_scalar_prefetch=1,
            grid=(b, nt),
            in_specs=[chunk(k) for k in range(ntc)]
                     + [pl.BlockSpec((1, tm, LANES), tok),
                        pl.BlockSpec((1, tm, d), tok),
                        pl.BlockSpec((1, 1, d), lambda i, t, pos: (i, 0, 0))],
            out_specs=pl.BlockSpec((1, tm, d), tok)),
        compiler_params=_params("parallel", "parallel"),
        name="moe_combine",
    )(chunk_pos, *([ys] * ntc), cmb, x, g2)


def _moe_chunk_schedule(cmb):
    b, n, _ = cmb.shape
    tm = min(512, n)
    nt = b * (n // tm)
    ntc, nsc = _tile_chunks(tm), MOE_STEP_CHUNKS
    steps = nt * ntc // nsc + N_GROUPS
    i32 = jnp.int32
    cnt = cmb[:, ::tm, CHUNKS_LANE:CHUNKS_LANE + N_GROUPS].reshape(nt, N_GROUPS).astype(i32)
    gi = jnp.arange(N_GROUPS, dtype=i32)
    ti = jnp.arange(nt, dtype=i32)
    earlier_g = (gi[:, None] < gi[None, :]).astype(i32)
    in_tile = jnp.sum(cnt[:, :, None] * earlier_g[None], axis=1)
    before = jnp.sum(cnt[:, None, :] * (ti[:, None] < ti[None, :]).astype(i32)[:, :, None], axis=0)
    total = jnp.sum(cnt, axis=0)
    padded = (total + nsc - 1) // nsc * nsc
    gstart = jnp.sum(padded[:, None] * earlier_g, axis=0)
    gend = gstart + padded
    c = jnp.arange(tm // MOE_CHUNK, dtype=i32)
    dst = gstart[None, :, None] + before[:, :, None] + c[None, None, :]
    src = (ti * ntc)[:, None, None] + in_tile[:, :, None] + c[None, None, :]
    dst = jnp.where(c[None, None, :] < cnt[:, :, None], dst, -1).reshape(-1)
    p = jnp.arange(steps * nsc, dtype=i32)
    chunk_src = jnp.sum(jnp.where(dst[None, :] == p[:, None], src.reshape(-1)[None, :], 0), axis=1)
    first_chunk = jnp.arange(steps, dtype=i32) * nsc
    step_group = jnp.sum((first_chunk[:, None] >= gend[None, :-1]).astype(i32), axis=1)
    prev_group = jnp.sum((first_chunk[:, None] - nsc >= gend[None, :-1]).astype(i32), axis=1)
    step_used = (first_chunk < gend[-1]).astype(i32)
    step_fresh = jnp.logical_or(first_chunk == 0, step_group != prev_group).astype(i32)
    k = jnp.arange(ntc, dtype=i32)
    ends = in_tile + cnt
    grp_k = jnp.minimum(jnp.sum((k[None, :, None] >= ends[:, None, :]).astype(i32), axis=2), N_GROUPS - 1)
    base = gstart[None, :] + before - in_tile
    pos = k[None, :] + jnp.sum(jnp.where(grp_k[:, :, None] == gi[None, None, :], base[:, None, :], 0), axis=2)
    chunk_pos = jnp.where(k[None, :] < ends[:, -1:], pos, 0).reshape(-1).astype(i32)
    return step_group, step_used, step_fresh, chunk_src, chunk_pos


def _moe(hs, cs, cmb, x, g2, wg, wu, wd, layer):
    step_group, step_used, step_fresh, chunk_src, chunk_pos = _moe_chunk_schedule(cmb)
    ys = _moe_sorted(hs, cs, step_group, step_used, step_fresh, chunk_src, wg, wu, wd, layer)
    return _moe_combine(ys, chunk_pos, cmb, x, g2)


def kernel(x, c, ctx, c_ctx, ada_w, ada_b, norm1_g, norm2_g, w_in_even, qn_g, kn_g, na_rpb, sc_conv_w, w_in_odd, hy_short_w, hy_w1, hy_b1, hy_w2, hy_b2, hy_w3, hy_freq, hy_bias, cf_conv_w, cf_conv_b, cf_ln_g, cf_ln_b, w_out, moe_w_group, moe_b_group, moe_w_router, moe_b_router, moe_w_gate, moe_w_up, moe_w_down):
    depth = ada_w.shape[0]
    bsz, seq, d = x.shape
    lc = ctx.shape[1]
    assert 2 * seq == FFT_N and d == D_MODEL and bsz % 2 == 0

    mods = _ada_modulation(jnp.concatenate([c, c_ctx[None, :]], axis=0), ada_w, ada_b)
    seg = jnp.asarray(np.kron(np.eye(NA_HEADS), np.ones((HEAD_DIM, HEAD_DIM))), dtype=BF16)
    consts = _dft_constants()
    wg = moe_w_gate.reshape((-1,) + moe_w_gate.shape[2:])
    wu = moe_w_up.reshape((-1,) + moe_w_up.shape[2:])
    wd = moe_w_down.reshape((-1,) + moe_w_down.shape[2:])

    for l in range(depth):
        ctx_needed = any(j % 2 == 0 for j in range(l + 1, depth))
        lat_mod = [m[:, None, :] for m in jnp.split(mods[l, :bsz], 6, axis=-1)]
        ctx_mod = [jnp.broadcast_to(m[None, :, :], (bsz, 1, d)) for m in jnp.split(mods[l, bsz:bsz + 1], 6, axis=-1)]
        sh1, sc1, g1, sh2, sc2, g2 = lat_mod
        csh1, csc1, cg1, csh2, csc2, cg2 = ctx_mod
        n1g = norm1_g[l][None, :]
        n2g = norm2_g[l][None, :]
        w_out_bf = w_out[l].astype(BF16)
        wr = jnp.concatenate([jnp.repeat(moe_w_group[l], EXPERTS_PER_GROUP, axis=1), moe_w_router[l],
                              jnp.zeros((d, LANES - 2 * N_EXPERTS), F32)], axis=1)
        br = jnp.concatenate([jnp.repeat(moe_b_group[l], EXPERTS_PER_GROUP), moe_b_router[l],
                              jnp.zeros((LANES - 2 * N_EXPERTS,), F32)])[None, :]
        wrh = wr.astype(BF16)
        wr2 = jnp.concatenate([wrh, (wr - wrh.astype(F32)).astype(BF16)], axis=1)

        if l % 2 == 0:
            e = l // 2
            w_in = w_in_even[e].astype(BF16)
            qg = jnp.tile(qn_g[e], NA_HEADS)[None, :]
            kg = jnp.tile(kn_g[e], NA_HEADS)[None, :]
            ql, kl, vl, gbl, pl_ = _inproj_even(x, n1g, sc1, sh1, w_in, qg, kg, seg)
            qc, kc, vc, gbc, pc = _inproj_even(ctx, n1g, csc1, csh1, w_in, qg, kg, seg)
            bias = _bias_table(na_rpb[e])
            y1 = _natten(ql, kl, vl, kc, vc, bias)
            y2 = _short_gated_conv(gbl, pl_, sc_conv_w[e])
            lat_cm = (False, False)
            if ctx_needed:
                y1c = _ctx_attention(qc, kc, vc)
                y2c = _short_gated_conv(gbc, pc, sc_conv_w[e])
        else:
            o = l // 2
            wht = w_in_odd[o][:, :3 * HY_WIDTH].T.astype(BF16)
            wag = w_in_odd[o][:, 3 * HY_WIDTH:].astype(BF16)
            hyt, ag = _inproj_odd(x, n1g, sc1, sh1, wht, wag)
            taps = _hyena_taps(seq, hy_w1[o], hy_b1[o], hy_w2[o], hy_b2[o], hy_w3[o], hy_freq[o])
            h_spec = _filter_fft(taps, consts)
            y1 = _hyena_latent(hyt, hy_short_w[o], h_spec, hy_bias[o], consts)
            cf_args = (cf_conv_w[o], cf_conv_b[o], cf_ln_g[o], cf_ln_b[o])
            y2 = _conformer(ag, *cf_args)
            lat_cm = (True, False)
            if ctx_needed:
                hytc, agc = _inproj_odd(ctx, n1g, csc1, csh1, wht, wag)
                taps_c = _hyena_taps(lc, hy_w1[o], hy_b1[o], hy_w2[o], hy_b2[o], hy_w3[o], hy_freq[o])
                y1c = _hyena_context(hytc, hy_short_w[o], taps_c, hy_bias[o])
                y2c = _conformer(agc, *cf_args)

        x1, cmb, hs, cs = _outproj(y1, y2, w_out_bf, x, g1, n2g, sc2, sh2, wr2, br, lat_cm)
        x = _moe(hs, cs, cmb, x1, g2, wg, wu, wd, l)
        if ctx_needed:
            c1, cmbc, hsc, csc = _outproj(y1c, y2c, w_out_bf, ctx, cg1, n2g, csc2, csh2, wr2, br, lat_cm)
            ctx = _moe(hsc, csc, cmbc, c1, cg2, wg, wu, wd, l)
    return x
```

```python
import functools
import math
from typing import NamedTuple

import numpy as np
import jax
import jax.numpy as jnp
from jax import lax
from jax.experimental import pallas as pl
from jax.experimental.pallas import tpu as pltpu

F32 = jnp.float32
BF16 = jnp.bfloat16

D_MODEL = 1024
GRID_W = 64
NA_HEADS = 8
HEAD_DIM = 64
NA_WIDTH = 512
NA_WIN_H = 8
NA_WIN_W = 16
SC_WIDTH = 512
HY_WIDTH = 512
HY_BANDS = 16
HY_EMB = 1 + 2 * HY_BANDS
HY_FFN = 64
HY_MAX_DECAY = math.log(1e-2) / 0.3
HY_MIN_DECAY = math.log(1e-2) / 1.5
CF_WIDTH = 512
CF_TAPS = 31
N_GROUPS = 4
EXPERTS_PER_GROUP = 4
N_EXPERTS = 16
D_EXPERT = 256
RMS_EPS = 1e-6
LN_EPS = 1e-5
NEG_INF = -1e30

VMEM_LIMIT_BYTES = 56 * 1024 * 1024
LANES = 128

FFT_NA = 64
FFT_NB = 128
FFT_N = FFT_NA * FFT_NB


def _params(*sem):
    return pltpu.CompilerParams(dimension_semantics=tuple(sem), vmem_limit_bytes=VMEM_LIMIT_BYTES)


def _dot(a, b):
    return jnp.dot(a, b, preferred_element_type=F32)


def _dot_nt(a, b):
    return lax.dot_general(a, b, (((1,), (1,)), ((), ())), preferred_element_type=F32)


def _dot_tn(a, b):
    return lax.dot_general(a, b, (((0,), (0,)), ((), ())), preferred_element_type=F32)


def _dot_f32(a, b):
    return jnp.dot(a, b, preferred_element_type=F32, precision=lax.Precision.HIGHEST)


def _silu(x):
    return x * jax.nn.sigmoid(x)


def _ada_kernel(ct_ref, w_ref, b_ref, o_ref, *, n_cond):
    ct = ct_ref[...]
    s = _silu(ct)
    w = w_ref[0]
    rows = [jnp.sum(w * s[:, r:r + 1], axis=0, keepdims=True) for r in range(n_cond)]
    rows.append(jnp.zeros((8 - n_cond, w.shape[1]), F32))
    o_ref[0] = jnp.concatenate(rows, axis=0) + b_ref[0]


def _ada_modulation(cond, ada_w, ada_b):
    n_cond, d = cond.shape
    depth, _, n6 = ada_w.shape
    tn = 1536
    ct = jnp.zeros((d, 8), F32).at[:, :n_cond].set(cond.T)
    return pl.pallas_call(
        functools.partial(_ada_kernel, n_cond=n_cond),
        out_shape=jax.ShapeDtypeStruct((depth, 8, n6), F32),
        grid=(depth, n6 // tn),
        in_specs=[pl.BlockSpec((d, 8), lambda l, j: (0, 0)),
                  pl.BlockSpec((1, d, tn), lambda l, j: (l, 0, j)),
                  pl.BlockSpec((1, 1, tn), lambda l, j: (l, 0, j))],
        out_specs=pl.BlockSpec((1, 8, tn), lambda l, j: (l, 0, j)),
        compiler_params=_params("parallel", "parallel"),
        name="ada_modulation",
    )(ct, ada_w, ada_b.reshape(depth, 1, n6))


def _norm_modulate(x, g, sc, sh):
    ms = jnp.mean(x * x, axis=-1, keepdims=True)
    return x * lax.rsqrt(ms + RMS_EPS) * g * (1.0 + sc) + sh


def _head_rmsnorm(t, seg, gain):
    ss = _dot((t * t).astype(BF16), seg)
    return t * lax.rsqrt(ss * (1.0 / HEAD_DIM) + RMS_EPS) * gain


class _PendingMoe(NamedTuple):
    ys: jax.Array
    chunk_pos: jax.Array
    cmb: jax.Array
    x1: jax.Array
    g2: jax.Array


def _token_source(xsrc, tm, nt, d):
    tok = lambda i, j, pos: (i, j, 0)
    if not isinstance(xsrc, _PendingMoe):
        return 0, jnp.zeros((1,), jnp.int32), [xsrc], [pl.BlockSpec((1, tm, d), tok)]
    ntc = _tile_chunks(tm)
    chunk = lambda k: pl.BlockSpec((1, MOE_CHUNK, d), lambda i, j, pos: (pos[(i * nt + j) * ntc + k], 0, 0))
    specs = [chunk(k) for k in range(ntc)] + [pl.BlockSpec((1, tm, LANES), tok), pl.BlockSpec((1, tm, d), tok),
                                              pl.BlockSpec((1, 1, d), lambda i, j, pos: (i, 0, 0))]
    return ntc, xsrc.chunk_pos, [xsrc.ys] * ntc + [xsrc.cmb, xsrc.x1, xsrc.g2], specs


def _load_tokens(refs, n_chunks):
    if n_chunks == 0:
        return refs[0][0], refs[1:]
    cmb_ref, x1_ref, g2_ref = refs[n_chunks:n_chunks + 3]
    return _moe_unpermute(refs[:n_chunks], cmb_ref, x1_ref, g2_ref), refs[n_chunks + 3:]


def _inproj_even_kernel(pos_ref, *refs, n_chunks):
    x, refs = _load_tokens(refs, n_chunks)
    g_ref, sc_ref, sh_ref, w_ref, qg_ref, kg_ref, seg_ref = refs[:7]
    outs = refs[7:]
    if n_chunks:
        outs[0][0] = x
        outs = outs[1:]
    q_ref, k_ref, v_ref, gb_ref, p_ref = outs
    u = _norm_modulate(x, g_ref[...], sc_ref[0], sh_ref[0]).astype(BF16)
    seg = seg_ref[...]
    w = NA_WIDTH
    q = _dot(u, w_ref[:, 0 * w:1 * w])
    q_ref[0] = (_head_rmsnorm(q, seg, qg_ref[...]) * (HEAD_DIM ** -0.5)).astype(BF16)
    k = _dot(u, w_ref[:, 1 * w:2 * w])
    k_ref[0] = _head_rmsnorm(k, seg, kg_ref[...]).astype(BF16)
    v_ref[0] = _dot(u, w_ref[:, 2 * w:3 * w]).astype(BF16)
    gb_ref[0] = _dot(u, w_ref[:, 3 * w:4 * w]).astype(BF16)
    gc = _dot(u, w_ref[:, 4 * w:5 * w])
    hv = _dot(u, w_ref[:, 5 * w:6 * w])
    p_ref[0] = (gc * hv).astype(BF16)


def _inproj_even(xsrc, g, sc, sh, w_bf, qg, kg, seg):
    b, n, d = (xsrc.x1 if isinstance(xsrc, _PendingMoe) else xsrc).shape
    tm = min(512, n)
    nt = n // tm
    n_chunks, pos, x_ops, x_specs = _token_source(xsrc, tm, nt, d)
    tok = lambda i, j, pos: (i, j, 0)
    mod = lambda i, j, pos: (i, 0, 0)
    const = lambda i, j, pos: (0, 0)
    out = jax.ShapeDtypeStruct((b, n, NA_WIDTH), BF16)
    x_out = [jax.ShapeDtypeStruct((b, n, d), F32)] if n_chunks else []
    x_out_spec = [pl.BlockSpec((1, tm, d), tok)] if n_chunks else []
    res = pl.pallas_call(
        functools.partial(_inproj_even_kernel, n_chunks=n_chunks),
        out_shape=tuple(x_out) + (out,) * 5,
        grid_spec=pltpu.PrefetchScalarGridSpec(
            num_scalar_prefetch=1,
            grid=(b, nt),
            in_specs=x_specs + [pl.BlockSpec((1, d), const),
                                pl.BlockSpec((1, 1, d), mod),
                                pl.BlockSpec((1, 1, d), mod),
                                pl.BlockSpec(w_bf.shape, const),
                                pl.BlockSpec((1, NA_WIDTH), const),
                                pl.BlockSpec((1, NA_WIDTH), const),
                                pl.BlockSpec((NA_WIDTH, NA_WIDTH), const)],
            out_specs=tuple(x_out_spec) + (pl.BlockSpec((1, tm, NA_WIDTH), tok),) * 5),
        compiler_params=_params("parallel", "parallel"),
        name="inproj_even",
    )(pos, *x_ops, g, sc, sh, w_bf, qg, kg, seg)
    return tuple(res) if n_chunks else (xsrc,) + tuple(res)


def _bias_kernel(rpb_ref, o_ref):
    h = pl.program_id(0)
    qi = lax.broadcasted_iota(jnp.int32, (GRID_W, GRID_W), 0)
    ki = lax.broadcasted_iota(jnp.int32, (GRID_W, GRID_W), 1)
    start = jnp.clip(qi - NA_WIN_W // 2, 0, GRID_W - NA_WIN_W)
    valid = jnp.logical_and(ki >= start, ki < start + NA_WIN_W)
    cidx = jnp.clip(ki - qi, -(NA_WIN_W - 1), NA_WIN_W - 1) + (NA_WIN_W - 1)
    n_dr = 2 * NA_WIN_H - 1
    n_dc = 2 * NA_WIN_W - 1

    def body(j, accs):
        m = cidx == j
        return tuple(jnp.where(m, rpb_ref[(h * n_dr + d) * n_dc + j], a) for d, a in enumerate(accs))

    accs = lax.fori_loop(0, n_dc, body, tuple(jnp.zeros((GRID_W, GRID_W), F32) for _ in range(n_dr)))
    tiles = [jnp.where(valid, a, NEG_INF) for a in accs]
    for d0 in range(NA_WIN_H):
        o_ref[0, d0] = jnp.concatenate(tiles[d0:d0 + NA_WIN_H], axis=1)


def _bias_table(rpb):
    return pl.pallas_call(
        _bias_kernel,
        out_shape=jax.ShapeDtypeStruct((NA_HEADS, NA_WIN_H, GRID_W, NA_WIN_H * GRID_W), F32),
        grid=(NA_HEADS,),
        in_specs=[pl.BlockSpec(memory_space=pltpu.SMEM)],
        out_specs=pl.BlockSpec((1, NA_WIN_H, GRID_W, NA_WIN_H * GRID_W), lambda h: (h, 0, 0, 0)),
        compiler_params=_params("arbitrary"),
        name="rpb_bias_table",
    )(rpb.reshape(-1))


def _pair_attention(q2, kw, vw, bias, kc, vc, first_half):
    s_c = _dot_nt(q2, kc)
    m = jnp.max(s_c, axis=-1, keepdims=True)
    if kw is not None:
        s_w = _dot_nt(q2, kw) + bias
        m = jnp.maximum(m, jnp.max(s_w, axis=-1, keepdims=True))
        p_w = jnp.exp(s_w - m)
    p_c = jnp.exp(s_c - m)
    den = jnp.sum(p_c, axis=-1, keepdims=True)
    o = _dot(p_c.astype(BF16), vc)
    if kw is not None:
        den = den + jnp.sum(p_w, axis=-1, keepdims=True)
        o = o + _dot(p_w.astype(BF16), vw)
    o = o / den
    half = o.shape[0] // 2
    return jnp.where(first_half, o[:half], o[half:])


def _stack_heads(qp, first_half):
    zero = jnp.zeros_like(qp)
    return jnp.concatenate([jnp.where(first_half, qp, zero), jnp.where(first_half, zero, qp)], axis=0)


def _natten_kernel(q_ref, k_ref, v_ref, kc_ref, vc_ref, bias_ref, o_ref, *, rows_per_step, n_rows):
    blk = pl.program_id(1)
    lane = lax.broadcasted_iota(jnp.int32, (GRID_W, LANES), 1)
    first_half = lane < HEAD_DIM
    band = NA_WIN_H * GRID_W

    n_pairs = NA_HEADS // 2

    def row_body(j, carry):
        r = blk * rows_per_step + j
        start = jnp.clip(r - NA_WIN_H // 2, 0, n_rows - NA_WIN_H)
        d0 = start - r + (NA_WIN_H - 1)
        koff = pl.multiple_of(start * GRID_W, GRID_W)
        qoff = pl.multiple_of(j * GRID_W, GRID_W)
        cols = [slice(hp * LANES, (hp + 1) * LANES) for hp in range(n_pairs)]
        q2 = [_stack_heads(q_ref[0, pl.ds(qoff, GRID_W), cs], first_half) for cs in cols]
        s_w = [_dot_nt(q2[hp], k_ref[0, pl.ds(koff, band), cols[hp]])
               + jnp.concatenate([bias_ref[2 * hp, d0], bias_ref[2 * hp + 1, d0]], axis=0) for hp in range(n_pairs)]
        s_c = [_dot_nt(q2[hp], kc_ref[0, :, cols[hp]]) for hp in range(n_pairs)]
        m = [jnp.maximum(jnp.max(s_w[hp], axis=-1, keepdims=True), jnp.max(s_c[hp], axis=-1, keepdims=True))
             for hp in range(n_pairs)]
        p_w = [jnp.exp(s_w[hp] - m[hp]) for hp in range(n_pairs)]
        p_c = [jnp.exp(s_c[hp] - m[hp]) for hp in range(n_pairs)]
        den = [jnp.sum(p_w[hp], axis=-1, keepdims=True) + jnp.sum(p_c[hp], axis=-1, keepdims=True)
               for hp in range(n_pairs)]
        outs = []
        for hp in range(n_pairs):
            o = (_dot(p_w[hp].astype(BF16), v_ref[0, pl.ds(koff, band), cols[hp]])
                 + _dot(p_c[hp].astype(BF16), vc_ref[0, :, cols[hp]])) / den[hp]
            outs.append(jnp.where(first_half, o[:GRID_W], o[GRID_W:]))
        o_ref[0, pl.ds(qoff, GRID_W), :] = jnp.concatenate(outs, axis=1).astype(o_ref.dtype)
        return carry

    lax.fori_loop(0, rows_per_step, row_body, 0, unroll=2)


def _natten(q, k, v, kc, vc, bias):
    b, n, w = q.shape
    n_rows = n // GRID_W
    rows_per_step = 8
    tq = rows_per_step * GRID_W
    lc = kc.shape[1]
    return pl.pallas_call(
        functools.partial(_natten_kernel, rows_per_step=rows_per_step, n_rows=n_rows),
        out_shape=jax.ShapeDtypeStruct((b, n, w), BF16),
        grid=(b, n_rows // rows_per_step),
        in_specs=[pl.BlockSpec((1, tq, w), lambda i, j: (i, j, 0)),
                  pl.BlockSpec((1, n, w), lambda i, j: (i, 0, 0)),
                  pl.BlockSpec((1, n, w), lambda i, j: (i, 0, 0)),
                  pl.BlockSpec((1, lc, w), lambda i, j: (i, 0, 0)),
                  pl.BlockSpec((1, lc, w), lambda i, j: (i, 0, 0)),
                  pl.BlockSpec(bias.shape, lambda i, j: (0, 0, 0, 0))],
        out_specs=pl.BlockSpec((1, tq, w), lambda i, j: (i, j, 0)),
        compiler_params=_params("parallel", "arbitrary"),
        name="neighbourhood_attention",
    )(q, k, v, kc, vc, bias)


def _ctx_attn_kernel(q_ref, k_ref, v_ref, o_ref):
    lc = q_ref.shape[1]
    lane = lax.broadcasted_iota(jnp.int32, (lc, LANES), 1)
    first_half = lane < HEAD_DIM
    outs = []
    for hp in range(NA_HEADS // 2):
        cs = slice(hp * LANES, (hp + 1) * LANES)
        q2 = _stack_heads(q_ref[0, :, cs], first_half)
        outs.append(_pair_attention(q2, None, None, None, k_ref[0, :, cs], v_ref[0, :, cs], first_half))
    o_ref[0] = jnp.concatenate(outs, axis=1).astype(o_ref.dtype)


def _ctx_attention(q, k, v):
    b, lc, w = q.shape
    spec = pl.BlockSpec((1, lc, w), lambda i: (i, 0, 0))
    return pl.pallas_call(
        _ctx_attn_kernel,
        out_shape=jax.ShapeDtypeStruct((b, lc, w), BF16),
        grid=(b,),
        in_specs=[spec, spec, spec],
        out_specs=spec,
        compiler_params=_params("parallel"),
        name="context_attention",
    )(q, k, v)


def _sgconv_kernel(gb_ref, p_ref, w_ref, o_ref):
    p = p_ref[0].astype(F32)
    n = p.shape[0]
    row = lax.broadcasted_iota(jnp.int32, p.shape, 0)
    prev = jnp.where(row == 0, 0.0, pltpu.roll(p, 1, axis=0))
    nxt = jnp.where(row == n - 1, 0.0, pltpu.roll(p, n - 1, axis=0))
    y = w_ref[0:1, :] * prev + w_ref[1:2, :] * p + w_ref[2:3, :] * nxt
    o_ref[0] = (gb_ref[0].astype(F32) * y).astype(o_ref.dtype)


def _short_gated_conv(gb, p, w):
    b, n, c = p.shape
    spec = pl.BlockSpec((1, n, LANES), lambda i, j: (i, 0, j))
    return pl.pallas_call(
        _sgconv_kernel,
        out_shape=jax.ShapeDtypeStruct((b, n, c), BF16),
        grid=(b, c // LANES),
        in_specs=[spec, spec, pl.BlockSpec((3, LANES), lambda i, j: (0, j))],
        out_specs=spec,
        compiler_params=_params("parallel", "parallel"),
        name="short_gated_conv",
    )(gb, p, w)


def _inproj_odd_kernel(pos_ref, *refs, n_chunks):
    x, refs = _load_tokens(refs, n_chunks)
    g_ref, sc_ref, sh_ref, wht_ref, wag_ref = refs[:5]
    outs = refs[5:]
    if n_chunks:
        outs[0][0] = x
        outs = outs[1:]
    hy_ref, ag_ref = outs
    u = _norm_modulate(x, g_ref[...], sc_ref[0], sh_ref[0]).astype(BF16)
    hy_ref[0] = _dot_nt(wht_ref[...], u).astype(BF16)
    a = _dot(u, wag_ref[:, :CF_WIDTH])
    g = _dot(u, wag_ref[:, CF_WIDTH:])
    ag_ref[0] = (a * jax.nn.sigmoid(g)).astype(BF16)


def _inproj_odd(xsrc, g, sc, sh, wht_bf, wag_bf):
    b, n, d = (xsrc.x1 if isinstance(xsrc, _PendingMoe) else xsrc).shape
    tm = min(512, n)
    nt = n // tm
    hw = wht_bf.shape[0]
    n_chunks, pos, x_ops, x_specs = _token_source(xsrc, tm, nt, d)
    tok = lambda i, j, pos: (i, j, 0)
    mod = lambda i, j, pos: (i, 0, 0)
    const = lambda i, j, pos: (0, 0)
    x_out = [jax.ShapeDtypeStruct((b, n, d), F32)] if n_chunks else []
    x_out_spec = [pl.BlockSpec((1, tm, d), tok)] if n_chunks else []
    res = pl.pallas_call(
        functools.partial(_inproj_odd_kernel, n_chunks=n_chunks),
        out_shape=tuple(x_out) + (jax.ShapeDtypeStruct((b, hw, n), BF16),
                                  jax.ShapeDtypeStruct((b, n, CF_WIDTH), BF16)),
        grid_spec=pltpu.PrefetchScalarGridSpec(
            num_scalar_prefetch=1,
            grid=(b, nt),
            in_specs=x_specs + [pl.BlockSpec((1, d), const),
                                pl.BlockSpec((1, 1, d), mod),
                                pl.BlockSpec((1, 1, d), mod),
                                pl.BlockSpec(wht_bf.shape, const),
                                pl.BlockSpec(wag_bf.shape, const)],
            out_specs=tuple(x_out_spec) + (pl.BlockSpec((1, hw, tm), lambda i, j, pos: (i, 0, j)),
                                           pl.BlockSpec((1, tm, CF_WIDTH), tok))),
        compiler_params=_params("parallel", "parallel"),
        name="inproj_odd",
    )(pos, *x_ops, g, sc, sh, wht_bf, wag_bf)
    return tuple(res) if n_chunks else (xsrc,) + tuple(res)


SUBLANES = 8
CF_PAD = 2 * SUBLANES
CF_ROWS = 128


def _conformer_kernel(ag_ref, w_ref, cb_ref, lg_ref, lb_ref, o_ref, pad_ref, *, seq):
    zeros = jnp.zeros((CF_PAD, CF_WIDTH), F32)
    pad_ref[0:CF_PAD, :] = zeros
    pad_ref[CF_PAD + seq:2 * CF_PAD + seq, :] = zeros
    pad_ref[CF_PAD:CF_PAD + seq, :] = ag_ref[0].astype(F32)
    shift0 = CF_PAD - CF_TAPS // 2
    n_groups = (shift0 + CF_TAPS - 1) // SUBLANES + 1

    def conv_rows(i, carry):
        n0 = pl.multiple_of(i * CF_ROWS, CF_ROWS)
        wins = [pad_ref[pl.ds(n0 + SUBLANES * a, CF_ROWS + SUBLANES), :] for a in range(n_groups)]
        acc = None
        for b in range(SUBLANES):
            part = None
            for a in range(n_groups):
                j = SUBLANES * a + b - shift0
                if 0 <= j < CF_TAPS:
                    term = w_ref[j:j + 1, :] * wins[a]
                    part = term if part is None else part + term
            part = part[b:b + CF_ROWS, :]
            acc = part if acc is None else acc + part
        y = acc + cb_ref[...]
        mu = jnp.mean(y, axis=-1, keepdims=True)
        yc = y - mu
        var = jnp.mean(yc * yc, axis=-1, keepdims=True)
        z = yc * lax.rsqrt(var + LN_EPS) * lg_ref[...] + lb_ref[...]
        o_ref[0, pl.ds(n0, CF_ROWS), :] = _silu(z).astype(o_ref.dtype)
        return carry

    lax.fori_loop(0, seq // CF_ROWS, conv_rows, 0)


def _conformer(ag, w, cb, lg, lb):
    b, n, c = ag.shape
    spec = pl.BlockSpec((1, n, c), lambda i: (i, 0, 0))
    vec = pl.BlockSpec((1, c), lambda i: (0, 0))
    return pl.pallas_call(
        functools.partial(_conformer_kernel, seq=n),
        out_shape=jax.ShapeDtypeStruct((b, n, c), BF16),
        grid=(b,),
        in_specs=[spec, pl.BlockSpec((CF_TAPS, c), lambda i: (0, 0)), vec, vec, vec],
        out_specs=spec,
        scratch_shapes=[pltpu.VMEM((n + 2 * CF_PAD, c), F32)],
        compiler_params=_params("parallel"),
        name="conformer_conv",
    )(ag, w, cb[None, :], lg[None, :], lb[None, :])


def _hyena_features(length):
    t = np.linspace(0.0, 1.0, length, dtype=np.float32)
    w = (2.0 * math.pi * np.arange(length, dtype=np.float32) / length).astype(np.float32)
    bands = np.linspace(1e-4, HY_BANDS - 1, HY_BANDS, dtype=np.float32)
    ang = (bands[:, None] * w[None, :]).astype(np.float32)
    zt = np.concatenate([t[None, :], np.cos(ang), -np.sin(ang)], axis=0).astype(np.float32)
    deltas = np.abs(np.linspace(HY_MIN_DECAY, HY_MAX_DECAY, HY_WIDTH, dtype=np.float32))
    rev = (length - np.arange(length)) % length
    zt2 = np.concatenate([zt, zt[:, rev]], axis=1)
    t2 = np.concatenate([t, t[rev]])[None, :]
    return zt2, t2, deltas[:, None]


def _taps_kernel(zt_ref, t_ref, dl_ref, w1t_ref, b1_ref, f0_ref, w2t_ref, b2_ref, f1_ref, w3t_ref,
                 o_ref, hid_ref):
    first = jnp.logical_and(pl.program_id(0) == 0, pl.program_id(1) == 0)
    length = t_ref.shape[1] // 2

    @pl.when(first)
    def _():
        h1 = jnp.sin(f0_ref[...] * (_dot_f32(w1t_ref[...], zt_ref[...]) + b1_ref[...]))
        hid_ref[...] = jnp.sin(f1_ref[...] * (_dot_f32(w2t_ref[...], h1) + b2_ref[...]))

    decay = jnp.exp(-(dl_ref[...] * t_ref[...]))
    fwd = _dot_f32(w3t_ref[0, 0], hid_ref[:, :length])
    bwd = _dot_f32(w3t_ref[0, 1], hid_ref[:, length:])
    taps = jnp.concatenate([fwd, bwd], axis=1) * decay
    nrm = jnp.sum(jnp.abs(taps), axis=-1, keepdims=True)
    lane = lax.broadcasted_iota(jnp.int32, taps.shape, 1)
    o_ref[0] = jnp.where(lane == length, 0.0, taps / nrm)


def _hyena_taps(length, w1, b1, w2, b2, w3, freq):
    zt2, t2, deltas = _hyena_features(length)
    cb = 128
    w3t = w3.T.reshape(2, 2, HY_WIDTH, HY_FFN)
    col = lambda v: v.reshape(HY_FFN, 1)
    const = lambda o, j: (0, 0)
    return pl.pallas_call(
        _taps_kernel,
        out_shape=jax.ShapeDtypeStruct((2, HY_WIDTH, 2 * length), F32),
        grid=(2, HY_WIDTH // cb),
        in_specs=[pl.BlockSpec((HY_EMB, 2 * length), const),
                  pl.BlockSpec((1, 2 * length), const),
                  pl.BlockSpec((cb, 1), lambda o, j: (j, 0)),
                  pl.BlockSpec((HY_FFN, HY_EMB), const),
                  pl.BlockSpec((HY_FFN, 1), const),
                  pl.BlockSpec((HY_FFN, 1), const),
                  pl.BlockSpec((HY_FFN, HY_FFN), const),
                  pl.BlockSpec((HY_FFN, 1), const),
                  pl.BlockSpec((HY_FFN, 1), const),
                  pl.BlockSpec((1, 2, cb, HY_FFN), lambda o, j: (o, 0, j, 0))],
        out_specs=pl.BlockSpec((1, cb, 2 * length), lambda o, j: (o, j, 0)),
        scratch_shapes=[pltpu.VMEM((HY_FFN, 2 * length), F32)],
        compiler_params=_params("arbitrary", "arbitrary"),
        name="hyena_filter_taps",
    )(jnp.asarray(zt2), jnp.asarray(t2), jnp.asarray(deltas), w1.T, col(b1), col(freq[0]),
      w2.T, col(b2), col(freq[1]), w3t)


def _dft_constants():
    na, nb, n = FFT_NA, FFT_NB, FFT_N
    half = na // 2
    ka = np.arange(na)
    ang_a = 2.0 * np.pi * np.outer(ka, ka) / na
    ca, sa = np.cos(ang_a), np.sin(ang_a)
    fa = np.block([[ca[:half], -sa[:half]], [sa[:half], ca[:half]]])
    fai = np.block([[ca[:, :half], sa[:, :half]], [-sa[:, :half], ca[:, :half]]])
    kb = np.arange(nb)
    ang_b = 2.0 * np.pi * np.outer(kb, kb) / nb
    cbm, sbm = np.cos(ang_b), np.sin(ang_b)
    fb = np.block([[cbm, -sbm], [sbm, cbm]])
    fbi = np.block([[cbm, sbm], [-sbm, cbm]])
    ang_t = 2.0 * np.pi * np.outer(kb, ka) / n
    ct, st = np.cos(ang_t), np.sin(ang_t)
    tw_fc = np.concatenate([ct, ct], axis=1)
    tw_fs = np.concatenate([st, -st], axis=1)
    tw_ic, tw_is = ct.T.copy(), st.T.copy()
    bf = lambda a: jnp.asarray(a, dtype=F32).astype(BF16)
    f32 = lambda a: jnp.asarray(a, dtype=F32)
    fa_real = np.concatenate([ca, -sa], axis=1)
    return dict(fa=bf(fa), fa_real=bf(fa_real), fai=bf(fai), fb=bf(fb), fbi=bf(fbi),
                tw_fc=f32(tw_fc), tw_fs=f32(tw_fs), tw_ic=f32(tw_ic), tw_is=f32(tw_is))


def _fft_forward(zr, zi, fa, tw_fc, tw_fs, fb):
    c, _, nb = zr.shape
    tr = jnp.swapaxes(zr, 1, 2)
    lhs = tr if zi is None else jnp.concatenate([tr, jnp.swapaxes(zi, 1, 2)], axis=2)
    a = _dot(lhs.reshape(c * nb, lhs.shape[2]).astype(BF16), fa).reshape(c, nb, 2 * FFT_NA)
    a = a * tw_fc + pltpu.roll(a, FFT_NA, axis=2) * tw_fs
    t = jnp.swapaxes(a, 1, 2)
    lhs2 = jnp.concatenate([t[:, :FFT_NA, :], t[:, FFT_NA:, :]], axis=2)
    x = _dot(lhs2.reshape(c * FFT_NA, 2 * nb).astype(BF16), fb)
    return x.reshape(c, FFT_NA, 2 * nb)


def _fft_inverse(y, fbi, tw_ic, tw_is, fai):
    c = y.shape[0]
    nb = FFT_NB
    b = _dot(y.reshape(c * FFT_NA, 2 * nb).astype(BF16), fbi).reshape(c, FFT_NA, 2 * nb)
    br, bi = b[:, :, :nb], b[:, :, nb:]
    rr = br * tw_ic - bi * tw_is
    ii = bi * tw_ic + br * tw_is
    t = jnp.swapaxes(jnp.concatenate([rr, ii], axis=1), 1, 2)
    o = _dot(t.reshape(c * nb, 2 * FFT_NA).astype(BF16), fai).reshape(c, nb, FFT_NA)
    o = jnp.swapaxes(o, 1, 2)
    return o[:, :FFT_NA // 2, :], o[:, FFT_NA // 2:, :]


def _filter_fft_kernel(taps_ref, fa_ref, twc_ref, tws_ref, fb_ref, o_ref):
    h = _fft_forward(taps_ref[0], None, fa_ref[...], twc_ref[...], tws_ref[...], fb_ref[...])
    o_ref[0] = h * (1.0 / FFT_N)


def _filter_fft(taps, consts):
    _, c, n = taps.shape
    cb = 32
    taps4 = taps.reshape(2, c, FFT_NA, FFT_NB)
    cm = lambda o, j: (0, 0)
    return pl.pallas_call(
        _filter_fft_kernel,
        out_shape=jax.ShapeDtypeStruct((2, c, FFT_NA, 2 * FFT_NB), F32),
        grid=(2, c // cb),
        in_specs=[pl.BlockSpec((1, cb, FFT_NA, FFT_NB), lambda o, j: (o, j, 0, 0)),
                  pl.BlockSpec(consts["fa_real"].shape, cm),
                  pl.BlockSpec(consts["tw_fc"].shape, cm),
                  pl.BlockSpec(consts["tw_fs"].shape, cm),
                  pl.BlockSpec(consts["fb"].shape, cm)],
        out_specs=pl.BlockSpec((1, cb, FFT_NA, 2 * FFT_NB), lambda o, j: (o, j, 0, 0)),
        compiler_params=_params("parallel", "parallel"),
        name="hyena_filter_fft",
    )(taps4, consts["fa_real"], consts["tw_fc"], consts["tw_fs"], consts["fb"])


def _shift_tokens(a, direction):
    rows = a.shape[-2]
    lane = lax.broadcasted_iota(jnp.int32, a.shape, a.ndim - 1)
    row = lax.broadcasted_iota(jnp.int32, a.shape, a.ndim - 2)
    if direction == 1:
        l = pltpu.roll(a, 1, axis=a.ndim - 1)
        ls = pltpu.roll(l, 1, axis=a.ndim - 2)
        out = jnp.where(lane == 0, ls, l)
        edge = jnp.logical_and(lane == 0, row == 0)
    else:
        l = pltpu.roll(a, LANES - 1, axis=a.ndim - 1)
        ls = pltpu.roll(l, rows - 1, axis=a.ndim - 2)
        out = jnp.where(lane == LANES - 1, ls, l)
        edge = jnp.logical_and(lane == LANES - 1, row == rows - 1)
    return jnp.where(edge, 0.0, out)


def _short_conv3(a, w_ref):
    return w_ref[0] * _shift_tokens(a, 1) + w_ref[1] * a + w_ref[2] * _shift_tokens(a, -1)


def _hyena_kernel(v_ref, x1_ref, x2_ref, wv_ref, w1_ref, w2_ref, h_ref, hb_ref,
                  fa_ref, twfc_ref, twfs_ref, fb_ref, fbi_ref, twic_ref, twis_ref, fai_ref, o_ref):
    fwd_c = (fa_ref[...], twfc_ref[...], twfs_ref[...], fb_ref[...])
    inv_c = (fbi_ref[...], twic_ref[...], twis_ref[...], fai_ref[...])
    z = _short_conv3(v_ref[...].astype(F32), wv_ref)
    zr, zi = z[0], z[1]
    nb = FFT_NB
    for o, (g_ref, gw_ref) in enumerate(((x1_ref, w1_ref), (x2_ref, w2_ref))):
        x = _fft_forward(zr, zi, *fwd_c)
        h = h_ref[o]
        xr, xi, hr, hi = x[:, :, :nb], x[:, :, nb:], h[:, :, :nb], h[:, :, nb:]
        y = jnp.concatenate([xr * hr - xi * hi, xr * hi + xi * hr], axis=2)
        yr, yi = _fft_inverse(y, *inv_c)
        gate = _short_conv3(g_ref[...].astype(F32), gw_ref)
        bias = hb_ref[o]
        zr = gate[0] * (yr + zr * bias)
        zi = gate[1] * (yi + zi * bias)
    o_ref[0] = zr.astype(o_ref.dtype)
    o_ref[1] = zi.astype(o_ref.dtype)


def _hyena_latent(hyt, short_w, h_spec, hy_bias, consts):
    b, c3, length = hyt.shape
    c = c3 // 3
    rows = length // FFT_NB
    cb = 32
    nblk = c // cb
    hy4 = hyt.reshape(b, c3, rows, FFT_NB)
    w4 = jnp.broadcast_to(short_w.reshape(3, c3, 1, 1), (3, c3, 1, FFT_NB))
    hb4 = jnp.broadcast_to(hy_bias.reshape(2, c, 1, 1), (2, c, 1, FFT_NB))
    sig = lambda g: pl.BlockSpec((2, cb, rows, FFT_NB), lambda j, p: (p, g * nblk + j, 0, 0))
    wsp = lambda g: pl.BlockSpec((3, cb, 1, FFT_NB), lambda j, p: (0, g * nblk + j, 0, 0))
    cm = lambda j, p: (0, 0)
    names = ("fa", "tw_fc", "tw_fs", "fb", "fbi", "tw_ic", "tw_is", "fai")
    out = pl.pallas_call(
        _hyena_kernel,
        out_shape=jax.ShapeDtypeStruct((b, c, rows, FFT_NB), BF16),
        grid=(nblk, b // 2),
        in_specs=[sig(0), sig(1), sig(2), wsp(0), wsp(1), wsp(2),
                  pl.BlockSpec((2, cb, FFT_NA, 2 * FFT_NB), lambda j, p: (0, j, 0, 0)),
                  pl.BlockSpec((2, cb, 1, FFT_NB), lambda j, p: (0, j, 0, 0))]
                 + [pl.BlockSpec(consts[k].shape, cm) for k in names],
        out_specs=pl.BlockSpec((2, cb, rows, FFT_NB), lambda j, p: (p, j, 0, 0)),
        compiler_params=_params("parallel", "arbitrary"),
        name="hyena_long_conv",
    )(hy4, hy4, hy4, w4, w4, w4, h_spec, hb4, *[consts[k] for k in names])
    return out.reshape(b, c, length)


def _dense_dft_constants(length):
    n = 2 * length
    k = np.arange(n)
    ang = 2.0 * np.pi * np.outer(k, k) / n
    fwd = np.concatenate([np.cos(ang), -np.sin(ang)], axis=1)
    inv = np.concatenate([np.cos(ang[:length]).T, -np.sin(ang[:length]).T], axis=0) / n
    bf = lambda a: jnp.asarray(a, dtype=F32).astype(BF16)
    return bf(fwd), bf(inv)


def _shift_lanes(a, direction):
    n = a.shape[-1]
    lane = lax.broadcasted_iota(jnp.int32, a.shape, a.ndim - 1)
    if direction == 1:
        return jnp.where(lane == 0, 0.0, pltpu.roll(a, 1, axis=a.ndim - 1))
    return jnp.where(lane == n - 1, 0.0, pltpu.roll(a, n - 1, axis=a.ndim - 1))


def _hyena_ctx_kernel(v_ref, x1_ref, x2_ref, wv_ref, w1_ref, w2_ref, taps_ref, hb_ref, fwd_ref, inv_ref, o_ref):
    bsz, cb, length = v_ref.shape
    n = 2 * length
    fwd_m, inv_m = fwd_ref[0:length, :], inv_ref[...]

    def conv3(ref, w_ref):
        a = ref[...].astype(F32)
        return w_ref[0] * _shift_lanes(a, 1) + w_ref[1] * a + w_ref[2] * _shift_lanes(a, -1)

    z = conv3(v_ref, wv_ref)
    for o, (g_ref, gw_ref) in enumerate(((x1_ref, w1_ref), (x2_ref, w2_ref))):
        h = _dot(taps_ref[o].astype(BF16), fwd_ref[...])
        hr, hi = h[:, :n], h[:, n:]
        x = _dot(z.reshape(bsz * cb, length).astype(BF16), fwd_m).reshape(bsz, cb, 2 * n)
        xr, xi = x[:, :, :n], x[:, :, n:]
        y = jnp.concatenate([xr * hr - xi * hi, xr * hi + xi * hr], axis=2)
        yt = _dot(y.reshape(bsz * cb, 2 * n).astype(BF16), inv_m).reshape(bsz, cb, length)
        z = conv3(g_ref, gw_ref) * (yt + z * hb_ref[o])
    o_ref[...] = z.astype(o_ref.dtype)


def _hyena_context(hyt, short_w, taps, hy_bias):
    b, c3, length = hyt.shape
    c = c3 // 3
    cb = 128
    nblk = c // cb
    fwd_m, inv_m = _dense_dft_constants(length)
    w4 = jnp.broadcast_to(short_w.reshape(3, c3, 1), (3, c3, length))
    hb = jnp.broadcast_to(hy_bias.reshape(2, c, 1), (2, c, length))
    sig = lambda g: pl.BlockSpec((b, cb, length), lambda j: (0, g * nblk + j, 0))
    wsp = lambda g: pl.BlockSpec((3, cb, length), lambda j: (0, g * nblk + j, 0))
    return pl.pallas_call(
        _hyena_ctx_kernel,
        out_shape=jax.ShapeDtypeStruct((b, c, length), BF16),
        grid=(nblk,),
        in_specs=[sig(0), sig(1), sig(2), wsp(0), wsp(1), wsp(2),
                  pl.BlockSpec((2, cb, 2 * length), lambda j: (0, j, 0)),
                  pl.BlockSpec((2, cb, length), lambda j: (0, j, 0)),
                  pl.BlockSpec(fwd_m.shape, lambda j: (0, 0)),
                  pl.BlockSpec(inv_m.shape, lambda j: (0, 0))],
        out_specs=pl.BlockSpec((b, cb, length), lambda j: (0, j, 0)),
        compiler_params=_params("parallel"),
        name="hyena_context_conv",
    )(hyt, hyt, hyt, w4, w4, w4, taps, hb, fwd_m, inv_m)


def _route(gl, el):
    row = lax.broadcasted_iota(jnp.int32, gl.shape, 0).astype(F32)
    grp = jnp.floor(row * (1.0 / EXPERTS_PER_GROUP))
    big = float(N_EXPERTS)
    gmax = jnp.max(gl, axis=0, keepdims=True)
    gidx = jnp.min(jnp.where(gl == gmax, grp, big), axis=0, keepdims=True)
    gsum = jnp.sum(jnp.exp(gl - gmax), axis=0, keepdims=True) * (1.0 / EXPERTS_PER_GROUP)
    g_w = 1.0 / gsum
    em = jnp.where(grp == gidx, el, NEG_INF)
    t1 = jnp.max(em, axis=0, keepdims=True)
    i1 = jnp.min(jnp.where(em == t1, row, big), axis=0, keepdims=True)
    em2 = jnp.where(row == i1, 2.0 * NEG_INF, em)
    t2 = jnp.max(em2, axis=0, keepdims=True)
    i2 = jnp.min(jnp.where(em2 == t2, row, big), axis=0, keepdims=True)
    e2 = jnp.exp(t2 - t1)
    den = 1.0 + e2
    w1 = g_w / den
    w2 = g_w * e2 / den
    return jnp.where(row == i1, w1, 0.0) + jnp.where(row == i2, w2, 0.0), gidx


SLOT_LANE = N_EXPERTS
CHUNKS_LANE = N_EXPERTS + 1
MOE_CHUNK = 64


def _tile_chunks(tm):
    return tm // MOE_CHUNK + N_GROUPS


def _dispatch_slots(gidx, tri):
    tm = gidx.shape[1]
    sub = lax.broadcasted_iota(jnp.int32, (SUBLANES, tm), 0)
    grp = sub.astype(F32)
    member = jnp.where(grp == gidx, 1.0, 0.0)
    rank = _dot(member.astype(BF16), tri)
    count = jnp.sum(member, axis=1, keepdims=True)
    chunks = jnp.floor((count + float(MOE_CHUNK - 1)) * (1.0 / MOE_CHUNK))
    first = jnp.zeros_like(chunks)
    for g in range(1, N_GROUPS):
        first = first + jnp.where(sub[:, 0:1] >= g, chunks[g - 1:g, :], 0.0)
    slot = jnp.sum(member * (first * float(MOE_CHUNK) + rank - 1.0), axis=0, keepdims=True)
    out = jnp.where(sub == 0, slot, 0.0)
    for g in range(N_GROUPS):
        out = out + jnp.where(sub == g + 1, chunks[g:g + 1, :], 0.0)
    return out


def _outproj_kernel(y1_ref, y2_ref, w_ref, x_ref, g1_ref, n2_ref, sc2_ref, sh2_ref, wr_ref, br_ref, tri_ref,
                    xo_ref, cmb_ref, hs_ref, cs_ref, *, channel_major):
    half = w_ref.shape[0] // 2
    dots = [_dot_tn if cm else _dot for cm in channel_major]
    acc = dots[0](y1_ref[0], w_ref[:half, :]) + dots[1](y2_ref[0], w_ref[half:, :])
    xn = x_ref[0] + g1_ref[0] * acc
    xo_ref[0] = xn
    h = _norm_modulate(xn, n2_ref[...], sc2_ref[0], sh2_ref[0])
    hi = h.astype(BF16)
    lo = (h - hi.astype(F32)).astype(BF16)
    p = _dot(hi, wr_ref[...])
    logits = p[:, :LANES] + p[:, LANES:] + _dot(lo, wr_ref[:, :LANES]) + br_ref[...]
    lt = logits.T
    cmb, gidx = _route(lt[0:N_EXPERTS], lt[N_EXPERTS:2 * N_EXPERTS])
    disp = _dispatch_slots(gidx, tri_ref[...])
    pad = jnp.zeros((LANES - N_EXPERTS - SUBLANES, cmb.shape[1]), F32)
    rows = jnp.concatenate([cmb, disp, pad], axis=0).T
    cmb_ref[0] = rows
    n_chunks = hs_ref.shape[0]
    srow = lax.broadcasted_iota(jnp.int32, (n_chunks * MOE_CHUNK, rows.shape[0]), 0).astype(F32)
    gather = jnp.where(srow == disp[0:1, :], 1.0, 0.0).astype(BF16)
    hs = _dot(gather, hi).astype(BF16)
    r_hi = rows.astype(BF16)
    r_lo = (rows - r_hi.astype(F32)).astype(BF16)
    cs2 = _dot(gather, jnp.concatenate([r_hi, r_lo], axis=1))
    cs = cs2[:, :LANES] + cs2[:, LANES:]
    for k in range(n_chunks):
        hs_ref[k] = hs[k * MOE_CHUNK:(k + 1) * MOE_CHUNK]
        cs_ref[k] = cs[k * MOE_CHUNK:(k + 1) * MOE_CHUNK]


def _outproj(y1, y2, w_bf, x, g1, n2g, sc2, sh2, wr2, br, channel_major):
    b, n, d = x.shape
    tm = min(512, n)
    nt = n // tm
    ntc = _tile_chunks(tm)
    half = d // 2
    tok = lambda i, j: (i, j, 0)
    mod = lambda i, j: (i, 0, 0)
    const = lambda i, j: (0, 0)
    srt = lambda i, j: (i * nt + j, 0, 0)
    y_specs = [pl.BlockSpec((1, half, tm), lambda i, j: (i, 0, j)) if cm else pl.BlockSpec((1, tm, half), tok)
               for cm in channel_major]
    return pl.pallas_call(
        functools.partial(_outproj_kernel, channel_major=tuple(channel_major)),
        out_shape=(jax.ShapeDtypeStruct((b, n, d), F32), jax.ShapeDtypeStruct((b, n, LANES), F32),
                   jax.ShapeDtypeStruct((b * nt * ntc, MOE_CHUNK, d), BF16),
                   jax.ShapeDtypeStruct((b * nt * ntc, MOE_CHUNK, LANES), F32)),
        grid=(b, nt),
        in_specs=[y_specs[0], y_specs[1],
                  pl.BlockSpec((d, d), const),
                  pl.BlockSpec((1, tm, d), tok),
                  pl.BlockSpec((1, 1, d), mod),
                  pl.BlockSpec((1, d), const),
                  pl.BlockSpec((1, 1, d), mod),
                  pl.BlockSpec((1, 1, d), mod),
                  pl.BlockSpec(wr2.shape, const),
                  pl.BlockSpec(br.shape, const),
                  pl.BlockSpec((tm, tm), const)],
        out_specs=(pl.BlockSpec((1, tm, d), tok), pl.BlockSpec((1, tm, LANES), tok),
                   pl.BlockSpec((ntc, MOE_CHUNK, d), srt), pl.BlockSpec((ntc, MOE_CHUNK, LANES), srt)),
        compiler_params=_params("parallel", "parallel"),
        name="outproj_router",
    )(y1, y2, w_bf, x, g1, n2g, sc2, sh2, wr2, br, jnp.asarray(np.triu(np.ones((tm, tm), np.float32)), dtype=BF16))


def _swiglu_group(h, w_rows, j, wg_ref, wu_ref, wd_ref):
    lane = lax.broadcasted_iota(jnp.int32, w_rows.shape, 1)
    experts = range(EXPERTS_PER_GROUP)
    w_e = [jnp.sum(jnp.where(lane == j * EXPERTS_PER_GROUP + e, w_rows, 0.0), axis=1, keepdims=True) for e in experts]
    a = [_dot(h, wg_ref[e]) for e in experts]
    u = [_dot(h, wu_ref[e]) for e in experts]
    act = [(_silu(a[e]) * u[e] * w_e[e]).astype(BF16) for e in experts]
    out = _dot(act[0], wd_ref[0])
    for e in experts[1:]:
        out = out + _dot(act[e], wd_ref[e])
    return out


MOE_STEP_CHUNKS = 8


def _moe_sorted_kernel(group_ref, used_ref, fresh_ref, src_ref, *refs):
    n = MOE_STEP_CHUNKS
    hs_refs, cs_refs = refs[:n], refs[n:2 * n]
    wg_ref, wu_ref, wd_ref, ys_ref, wg_bf, wu_bf, wd_bf = refs[2 * n:]
    s = pl.program_id(0)

    @pl.when(fresh_ref[s] > 0)
    def _():
        wg_bf[...] = wg_ref[...].astype(BF16)
        wu_bf[...] = wu_ref[...].astype(BF16)
        wd_bf[...] = wd_ref[...].astype(BF16)

    @pl.when(used_ref[s] > 0)
    def _():
        h = jnp.concatenate([r[0] for r in hs_refs], axis=0)
        w_rows = jnp.concatenate([r[0] for r in cs_refs], axis=0)
        y = _swiglu_group(h, w_rows, group_ref[s], wg_bf, wu_bf, wd_bf).astype(BF16)
        for k in range(n):
            ys_ref[k] = y[k * MOE_CHUNK:(k + 1) * MOE_CHUNK]

    @pl.when(used_ref[s] == 0)
    def _():
        ys_ref[...] = jnp.zeros(ys_ref.shape, ys_ref.dtype)


def _moe_sorted(hs, cs, step_group, step_used, step_fresh, chunk_src, wg, wu, wd, layer):
    _, _, d = hs.shape
    n = MOE_STEP_CHUNKS
    steps = step_group.shape[0]
    epg = EXPERTS_PER_GROUP
    chunk = lambda k, width: pl.BlockSpec((1, MOE_CHUNK, width),
                                          lambda s, grp, used, fresh, src: (src[s * n + k], 0, 0))
    wmap = lambda s, grp, used, fresh, src: (layer * N_GROUPS + grp[s], 0, 0)
    return pl.pallas_call(
        _moe_sorted_kernel,
        out_shape=jax.ShapeDtypeStruct((steps * n, MOE_CHUNK, d), BF16),
        grid_spec=pltpu.PrefetchScalarGridSpec(
            num_scalar_prefetch=4,
            grid=(steps,),
            in_specs=[chunk(k, d) for k in range(n)] + [chunk(k, LANES) for k in range(n)]
                     + [pl.BlockSpec((epg, d, D_EXPERT), wmap),
                        pl.BlockSpec((epg, d, D_EXPERT), wmap),
                        pl.BlockSpec((epg, D_EXPERT, d), wmap)],
            out_specs=pl.BlockSpec((n, MOE_CHUNK, d), lambda s, grp, used, fresh, src: (s, 0, 0)),
            scratch_shapes=[pltpu.VMEM((epg, d, D_EXPERT), BF16), pltpu.VMEM((epg, d, D_EXPERT), BF16),
                            pltpu.VMEM((epg, D_EXPERT, d), BF16)]),
        compiler_params=_params("arbitrary"),
        name="moe_sorted_experts",
    )(step_group, step_used, step_fresh, chunk_src, *([hs] * n), *([cs] * n), wg, wu, wd)


def _moe_unpermute(ys_refs, cmb_ref, x1_ref, g2_ref):
    cmb = cmb_ref[0]
    lane = lax.broadcasted_iota(jnp.int32, cmb.shape, 1)
    slot = jnp.sum(jnp.where(lane == SLOT_LANE, cmb, 0.0), axis=1, keepdims=True)
    rows = lax.broadcasted_iota(jnp.int32, (cmb.shape[0], len(ys_refs) * MOE_CHUNK), 1).astype(F32)
    scatter = jnp.where(slot == rows, 1.0, 0.0).astype(BF16)
    ys = jnp.concatenate([r[0] for r in ys_refs], axis=0)
    return x1_ref[0] + g2_ref[0] * _dot(scatter, ys)


def _moe_combine_kernel(pos_ref, *refs):
    x, (o_ref,) = _load_tokens(refs, len(refs) - 4)
    o_ref[0] = x


def _moe_combine(pending):
    b, n, d = pending.x1.shape
    tm = min(512, n)
    nt = n // tm
    n_chunks, pos, x_ops, x_specs = _token_source(pending, tm, nt, d)
    return pl.pallas_call(
        _moe_combine_kernel,
        out_shape=jax.ShapeDtypeStruct((b, n, d), F32),
        grid_spec=pltpu.PrefetchScalarGridSpec(
            num_scalar_prefetch=1,
            grid=(b, nt),
            in_specs=x_specs,
            out_specs=pl.BlockSpec((1, tm, d), lambda i, t, pos: (i, t, 0))),
        compiler_params=_params("parallel", "parallel"),
        name="moe_combine",
    )(pos, *x_ops)


def _moe_chunk_schedule(cmb):
    b, n, _ = cmb.shape
    tm = min(512, n)
    nt = b * (n // tm)
    ntc, nsc = _tile_chunks(tm), MOE_STEP_CHUNKS
    steps = nt * ntc // nsc + N_GROUPS
    i32 = jnp.int32
    cnt = cmb[:, ::tm, CHUNKS_LANE:CHUNKS_LANE + N_GROUPS].reshape(nt, N_GROUPS).astype(i32)
    gi = jnp.arange(N_GROUPS, dtype=i32)
    ti = jnp.arange(nt, dtype=i32)
    earlier_g = (gi[:, None] < gi[None, :]).astype(i32)
    in_tile = jnp.sum(cnt[:, :, None] * earlier_g[None], axis=1)
    before = jnp.sum(cnt[:, None, :] * (ti[:, None] < ti[None, :]).astype(i32)[:, :, None], axis=0)
    total = jnp.sum(cnt, axis=0)
    padded = (total + nsc - 1) // nsc * nsc
    gstart = jnp.sum(padded[:, None] * earlier_g, axis=0)
    gend = gstart + padded
    c = jnp.arange(tm // MOE_CHUNK, dtype=i32)
    dst = gstart[None, :, None] + before[:, :, None] + c[None, None, :]
    src = (ti * ntc)[:, None, None] + in_tile[:, :, None] + c[None, None, :]
    dst = jnp.where(c[None, None, :] < cnt[:, :, None], dst, -1).reshape(-1)
    p = jnp.arange(steps * nsc, dtype=i32)
    chunk_src = jnp.sum(jnp.where(dst[None, :] == p[:, None], src.reshape(-1)[None, :], 0), axis=1)
    first_chunk = jnp.arange(steps, dtype=i32) * nsc
    step_group = jnp.sum((first_chunk[:, None] >= gend[None, :-1]).astype(i32), axis=1)
    prev_group = jnp.sum((first_chunk[:, None] - nsc >= gend[None, :-1]).astype(i32), axis=1)
    step_used = (first_chunk < gend[-1]).astype(i32)
    step_fresh = jnp.logical_or(first_chunk == 0, step_group != prev_group).astype(i32)
    k = jnp.arange(ntc, dtype=i32)
    ends = in_tile + cnt
    grp_k = jnp.minimum(jnp.sum((k[None, :, None] >= ends[:, None, :]).astype(i32), axis=2), N_GROUPS - 1)
    base = gstart[None, :] + before - in_tile
    pos = k[None, :] + jnp.sum(jnp.where(grp_k[:, :, None] == gi[None, None, :], base[:, None, :], 0), axis=2)
    chunk_pos = jnp.where(k[None, :] < ends[:, -1:], pos, 0).reshape(-1).astype(i32)
    return step_group, step_used, step_fresh, chunk_src, chunk_pos


def _moe(hs, cs, cmb, x, g2, wg, wu, wd, layer):
    step_group, step_used, step_fresh, chunk_src, chunk_pos = _moe_chunk_schedule(cmb)
    ys = _moe_sorted(hs, cs, step_group, step_used, step_fresh, chunk_src, wg, wu, wd, layer)
    return _PendingMoe(ys, chunk_pos, cmb, x, g2)


def kernel(x, c, ctx, c_ctx, ada_w, ada_b, norm1_g, norm2_g, w_in_even, qn_g, kn_g, na_rpb, sc_conv_w, w_in_odd, hy_short_w, hy_w1, hy_b1, hy_w2, hy_b2, hy_w3, hy_freq, hy_bias, cf_conv_w, cf_conv_b, cf_ln_g, cf_ln_b, w_out, moe_w_group, moe_b_group, moe_w_router, moe_b_router, moe_w_gate, moe_w_up, moe_w_down):
    depth = ada_w.shape[0]
    bsz, seq, d = x.shape
    lc = ctx.shape[1]
    assert 2 * seq == FFT_N and d == D_MODEL and bsz % 2 == 0

    mods = _ada_modulation(jnp.concatenate([c, c_ctx[None, :]], axis=0), ada_w, ada_b)
    seg = jnp.asarray(np.kron(np.eye(NA_HEADS), np.ones((HEAD_DIM, HEAD_DIM))), dtype=BF16)
    consts = _dft_constants()
    wg = moe_w_gate.reshape((-1,) + moe_w_gate.shape[2:])
    wu = moe_w_up.reshape((-1,) + moe_w_up.shape[2:])
    wd = moe_w_down.reshape((-1,) + moe_w_down.shape[2:])

    for l in range(depth):
        ctx_needed = any(j % 2 == 0 for j in range(l + 1, depth))
        lat_mod = [m[:, None, :] for m in jnp.split(mods[l, :bsz], 6, axis=-1)]
        ctx_mod = [jnp.broadcast_to(m[None, :, :], (bsz, 1, d)) for m in jnp.split(mods[l, bsz:bsz + 1], 6, axis=-1)]
        sh1, sc1, g1, sh2, sc2, g2 = lat_mod
        csh1, csc1, cg1, csh2, csc2, cg2 = ctx_mod
        n1g = norm1_g[l][None, :]
        n2g = norm2_g[l][None, :]
        w_out_bf = w_out[l].astype(BF16)
        wr = jnp.concatenate([jnp.repeat(moe_w_group[l], EXPERTS_PER_GROUP, axis=1), moe_w_router[l],
                              jnp.zeros((d, LANES - 2 * N_EXPERTS), F32)], axis=1)
        br = jnp.concatenate([jnp.repeat(moe_b_group[l], EXPERTS_PER_GROUP), moe_b_router[l],
                              jnp.zeros((LANES - 2 * N_EXPERTS,), F32)])[None, :]
        wrh = wr.astype(BF16)
        wr2 = jnp.concatenate([wrh, (wr - wrh.astype(F32)).astype(BF16)], axis=1)

        if l % 2 == 0:
            e = l // 2
            w_in = w_in_even[e].astype(BF16)
            qg = jnp.tile(qn_g[e], NA_HEADS)[None, :]
            kg = jnp.tile(kn_g[e], NA_HEADS)[None, :]
            x, ql, kl, vl, gbl, pl_ = _inproj_even(x, n1g, sc1, sh1, w_in, qg, kg, seg)
            ctx, qc, kc, vc, gbc, pc = _inproj_even(ctx, n1g, csc1, csh1, w_in, qg, kg, seg)
            bias = _bias_table(na_rpb[e])
            y1 = _natten(ql, kl, vl, kc, vc, bias)
            y2 = _short_gated_conv(gbl, pl_, sc_conv_w[e])
            lat_cm = (False, False)
            if ctx_needed:
                y1c = _ctx_attention(qc, kc, vc)
                y2c = _short_gated_conv(gbc, pc, sc_conv_w[e])
        else:
            o = l // 2
            wht = w_in_odd[o][:, :3 * HY_WIDTH].T.astype(BF16)
            wag = w_in_odd[o][:, 3 * HY_WIDTH:].astype(BF16)
            x, hyt, ag = _inproj_odd(x, n1g, sc1, sh1, wht, wag)
            taps = _hyena_taps(seq, hy_w1[o], hy_b1[o], hy_w2[o], hy_b2[o], hy_w3[o], hy_freq[o])
            h_spec = _filter_fft(taps, consts)
            y1 = _hyena_latent(hyt, hy_short_w[o], h_spec, hy_bias[o], consts)
            cf_args = (cf_conv_w[o], cf_conv_b[o], cf_ln_g[o], cf_ln_b[o])
            y2 = _conformer(ag, *cf_args)
            lat_cm = (True, False)
            if ctx_needed:
                ctx, hytc, agc = _inproj_odd(ctx, n1g, csc1, csh1, wht, wag)
                taps_c = _hyena_taps(lc, hy_w1[o], hy_b1[o], hy_w2[o], hy_b2[o], hy_w3[o], hy_freq[o])
                y1c = _hyena_context(hytc, hy_short_w[o], taps_c, hy_bias[o])
                y2c = _conformer(agc, *cf_args)

        x1, cmb, hs, cs = _outproj(y1, y2, w_out_bf, x, g1, n2g, sc2, sh2, wr2, br, lat_cm)
        x = _moe(hs, cs, cmb, x1, g2, wg, wu, wd, l)
        if ctx_needed:
            c1, cmbc, hsc, csc = _outproj(y1c, y2c, w_out_bf, ctx, cg1, n2g, csc2, csh2, wr2, br, lat_cm)
            ctx = _moe(hsc, csc, cmbc, c1, cg2, wg, wu, wd, l)
    return _moe_combine(x)
```

```python
import functools
import math
from typing import NamedTuple

import numpy as np
import jax
import jax.numpy as jnp
from jax import lax
from jax.experimental import pallas as pl
from jax.experimental.pallas import tpu as pltpu

F32 = jnp.float32
BF16 = jnp.bfloat16

D_MODEL = 1024
GRID_W = 64
NA_HEADS = 8
HEAD_DIM = 64
NA_WIDTH = 512
NA_WIN_H = 8
NA_WIN_W = 16
SC_WIDTH = 512
HY_WIDTH = 512
HY_BANDS = 16
HY_EMB = 1 + 2 * HY_BANDS
HY_FFN = 64
HY_MAX_DECAY = math.log(1e-2) / 0.3
HY_MIN_DECAY = math.log(1e-2) / 1.5
CF_WIDTH = 512
CF_TAPS = 31
N_GROUPS = 4
EXPERTS_PER_GROUP = 4
N_EXPERTS = 16
D_EXPERT = 256
RMS_EPS = 1e-6
LN_EPS = 1e-5
NEG_INF = -1e30

VMEM_LIMIT_BYTES = 56 * 1024 * 1024
LANES = 128

FFT_NA = 64
FFT_NB = 128
FFT_N = FFT_NA * FFT_NB


def _params(*sem):
    return pltpu.CompilerParams(dimension_semantics=tuple(sem), vmem_limit_bytes=VMEM_LIMIT_BYTES)


def _dot(a, b):
    return jnp.dot(a, b, preferred_element_type=F32)


def _dot_nt(a, b):
    return lax.dot_general(a, b, (((1,), (1,)), ((), ())), preferred_element_type=F32)


def _dot_tn(a, b):
    return lax.dot_general(a, b, (((0,), (0,)), ((), ())), preferred_element_type=F32)


def _dot_f32(a, b):
    return jnp.dot(a, b, preferred_element_type=F32, precision=lax.Precision.HIGHEST)


def _silu(x):
    return x * jax.nn.sigmoid(x)


def _ada_kernel(ct_ref, w_ref, b_ref, o_ref, *, n_cond):
    ct = ct_ref[...]
    s = _silu(ct)
    w = w_ref[0]
    rows = [jnp.sum(w * s[:, r:r + 1], axis=0, keepdims=True) for r in range(n_cond)]
    rows.append(jnp.zeros((8 - n_cond, w.shape[1]), F32))
    o_ref[0] = jnp.concatenate(rows, axis=0) + b_ref[0]


def _ada_modulation(cond, ada_w, ada_b):
    n_cond, d = cond.shape
    depth, _, n6 = ada_w.shape
    tn = 1536
    ct = jnp.zeros((d, 8), F32).at[:, :n_cond].set(cond.T)
    return pl.pallas_call(
        functools.partial(_ada_kernel, n_cond=n_cond),
        out_shape=jax.ShapeDtypeStruct((depth, 8, n6), F32),
        grid=(depth, n6 // tn),
        in_specs=[pl.BlockSpec((d, 8), lambda l, j: (0, 0)),
                  pl.BlockSpec((1, d, tn), lambda l, j: (l, 0, j)),
                  pl.BlockSpec((1, 1, tn), lambda l, j: (l, 0, j))],
        out_specs=pl.BlockSpec((1, 8, tn), lambda l, j: (l, 0, j)),
        compiler_params=_params("parallel", "parallel"),
        name="ada_modulation",
    )(ct, ada_w, ada_b.reshape(depth, 1, n6))


def _norm_modulate(x, g, sc, sh):
    ms = jnp.mean(x * x, axis=-1, keepdims=True)
    return x * lax.rsqrt(ms + RMS_EPS) * g * (1.0 + sc) + sh


def _head_rmsnorm(t, seg, gain):
    ss = _dot((t * t).astype(BF16), seg)
    return t * lax.rsqrt(ss * (1.0 / HEAD_DIM) + RMS_EPS) * gain


class _PendingMoe(NamedTuple):
    ys: jax.Array
    chunk_pos: jax.Array
    cmb: jax.Array
    x1: jax.Array
    g2: jax.Array


def _token_source(xsrc, tm, nt, d):
    tok = lambda i, j, pos: (i, j, 0)
    if not isinstance(xsrc, _PendingMoe):
        return 0, jnp.zeros((1,), jnp.int32), [xsrc], [pl.BlockSpec((1, tm, d), tok)]
    ntc = _tile_chunks(tm)
    chunk = lambda k: pl.BlockSpec((1, MOE_CHUNK, d), lambda i, j, pos: (pos[(i * nt + j) * ntc + k], 0, 0))
    specs = [chunk(k) for k in range(ntc)] + [pl.BlockSpec((1, tm, LANES), tok), pl.BlockSpec((1, tm, d), tok),
                                              pl.BlockSpec((1, 1, d), lambda i, j, pos: (i, 0, 0))]
    return ntc, xsrc.chunk_pos, [xsrc.ys] * ntc + [xsrc.cmb, xsrc.x1, xsrc.g2], specs


def _load_tokens(refs, n_chunks):
    if n_chunks == 0:
        return refs[0][0], refs[1:]
    cmb_ref, x1_ref, g2_ref = refs[n_chunks:n_chunks + 3]
    return _moe_unpermute(refs[:n_chunks], cmb_ref, x1_ref, g2_ref), refs[n_chunks + 3:]


def _inproj_even_kernel(pos_ref, *refs, n_chunks):
    x, refs = _load_tokens(refs, n_chunks)
    g_ref, sc_ref, sh_ref, w_ref, qg_ref, kg_ref, seg_ref = refs[:7]
    outs = refs[7:]
    if n_chunks:
        outs[0][0] = x
        outs = outs[1:]
    q_ref, k_ref, v_ref, gb_ref, p_ref = outs
    u = _norm_modulate(x, g_ref[...], sc_ref[0], sh_ref[0]).astype(BF16)
    seg = seg_ref[...]
    w = NA_WIDTH
    q = _dot(u, w_ref[:, 0 * w:1 * w])
    q_ref[0] = (_head_rmsnorm(q, seg, qg_ref[...]) * (HEAD_DIM ** -0.5)).astype(BF16)
    k = _dot(u, w_ref[:, 1 * w:2 * w])
    k_ref[0] = _head_rmsnorm(k, seg, kg_ref[...]).astype(BF16)
    v_ref[0] = _dot(u, w_ref[:, 2 * w:3 * w]).astype(BF16)
    gb_ref[0] = _dot(u, w_ref[:, 3 * w:4 * w]).astype(BF16)
    gc = _dot(u, w_ref[:, 4 * w:5 * w])
    hv = _dot(u, w_ref[:, 5 * w:6 * w])
    p_ref[0] = (gc * hv).astype(BF16)


def _inproj_even(xsrc, g, sc, sh, w_bf, qg, kg, seg):
    b, n, d = (xsrc.x1 if isinstance(xsrc, _PendingMoe) else xsrc).shape
    tm = min(512, n)
    nt = n // tm
    n_chunks, pos, x_ops, x_specs = _token_source(xsrc, tm, nt, d)
    tok = lambda i, j, pos: (i, j, 0)
    mod = lambda i, j, pos: (i, 0, 0)
    const = lambda i, j, pos: (0, 0)
    out = jax.ShapeDtypeStruct((b, n, NA_WIDTH), BF16)
    x_out = [jax.ShapeDtypeStruct((b, n, d), F32)] if n_chunks else []
    x_out_spec = [pl.BlockSpec((1, tm, d), tok)] if n_chunks else []
    res = pl.pallas_call(
        functools.partial(_inproj_even_kernel, n_chunks=n_chunks),
        out_shape=tuple(x_out) + (out,) * 5,
        grid_spec=pltpu.PrefetchScalarGridSpec(
            num_scalar_prefetch=1,
            grid=(b, nt),
            in_specs=x_specs + [pl.BlockSpec((1, d), const),
                                pl.BlockSpec((1, 1, d), mod),
                                pl.BlockSpec((1, 1, d), mod),
                                pl.BlockSpec(w_bf.shape, const),
                                pl.BlockSpec((1, NA_WIDTH), const),
                                pl.BlockSpec((1, NA_WIDTH), const),
                                pl.BlockSpec((NA_WIDTH, NA_WIDTH), const)],
            out_specs=tuple(x_out_spec) + (pl.BlockSpec((1, tm, NA_WIDTH), tok),) * 5),
        compiler_params=_params("parallel", "parallel"),
        name="inproj_even",
    )(pos, *x_ops, g, sc, sh, w_bf, qg, kg, seg)
    return tuple(res) if n_chunks else (xsrc,) + tuple(res)


def _bias_kernel(rpb_ref, o_ref):
    h = pl.program_id(0)
    qi = lax.broadcasted_iota(jnp.int32, (GRID_W, GRID_W), 0)
    ki = lax.broadcasted_iota(jnp.int32, (GRID_W, GRID_W), 1)
    start = jnp.clip(qi - NA_WIN_W // 2, 0, GRID_W - NA_WIN_W)
    valid = jnp.logical_and(ki >= start, ki < start + NA_WIN_W)
    cidx = jnp.clip(ki - qi, -(NA_WIN_W - 1), NA_WIN_W - 1) + (NA_WIN_W - 1)
    n_dr = 2 * NA_WIN_H - 1
    n_dc = 2 * NA_WIN_W - 1

    def body(j, accs):
        m = cidx == j
        return tuple(jnp.where(m, rpb_ref[(h * n_dr + d) * n_dc + j], a) for d, a in enumerate(accs))

    accs = lax.fori_loop(0, n_dc, body, tuple(jnp.zeros((GRID_W, GRID_W), F32) for _ in range(n_dr)))
    tiles = [jnp.where(valid, a, NEG_INF) for a in accs]
    for d0 in range(NA_WIN_H):
        o_ref[0, d0] = jnp.concatenate(tiles[d0:d0 + NA_WIN_H], axis=1)


def _bias_table(rpb):
    return pl.pallas_call(
        _bias_kernel,
        out_shape=jax.ShapeDtypeStruct((NA_HEADS, NA_WIN_H, GRID_W, NA_WIN_H * GRID_W), F32),
        grid=(NA_HEADS,),
        in_specs=[pl.BlockSpec(memory_space=pltpu.SMEM)],
        out_specs=pl.BlockSpec((1, NA_WIN_H, GRID_W, NA_WIN_H * GRID_W), lambda h: (h, 0, 0, 0)),
        compiler_params=_params("arbitrary"),
        name="rpb_bias_table",
    )(rpb.reshape(-1))


def _pair_attention(q2, kw, vw, bias, kc, vc, first_half):
    s_c = _dot_nt(q2, kc)
    m = jnp.max(s_c, axis=-1, keepdims=True)
    if kw is not None:
        s_w = _dot_nt(q2, kw) + bias
        m = jnp.maximum(m, jnp.max(s_w, axis=-1, keepdims=True))
        p_w = jnp.exp(s_w - m)
    p_c = jnp.exp(s_c - m)
    den = jnp.sum(p_c, axis=-1, keepdims=True)
    o = _dot(p_c.astype(BF16), vc)
    if kw is not None:
        den = den + jnp.sum(p_w, axis=-1, keepdims=True)
        o = o + _dot(p_w.astype(BF16), vw)
    o = o / den
    half = o.shape[0] // 2
    return jnp.where(first_half, o[:half], o[half:])


def _stack_heads(qp, first_half):
    zero = jnp.zeros_like(qp)
    return jnp.concatenate([jnp.where(first_half, qp, zero), jnp.where(first_half, zero, qp)], axis=0)


def _natten_kernel(q_ref, k_ref, v_ref, kc_ref, vc_ref, bias_ref, o_ref, sc_ref, ow_ref, pc_ref,
                   *, rows_per_step, n_rows):
    blk = pl.program_id(1)
    lane = lax.broadcasted_iota(jnp.int32, (GRID_W, LANES), 1)
    first_half = lane < HEAD_DIM
    band = NA_WIN_H * GRID_W
    n_pairs = NA_HEADS // 2
    stacked = 2 * GRID_W
    cols = [slice(hp * LANES, (hp + 1) * LANES) for hp in range(n_pairs)]
    lane3 = lax.broadcasted_iota(jnp.int32, (rows_per_step, GRID_W, LANES), 2)

    for hp in range(n_pairs):
        q3 = q_ref[0, :, cols[hp]].reshape(rows_per_step, GRID_W, LANES)
        zero = jnp.zeros_like(q3)
        q_all = jnp.concatenate([jnp.where(lane3 < HEAD_DIM, q3, zero), jnp.where(lane3 < HEAD_DIM, zero, q3)], axis=1)
        sc_ref[hp] = _dot_nt(q_all.reshape(rows_per_step * stacked, LANES), kc_ref[0, :, cols[hp]])

    def row_body(j, carry):
        r = blk * rows_per_step + j
        start = jnp.clip(r - NA_WIN_H // 2, 0, n_rows - NA_WIN_H)
        d0 = start - r + (NA_WIN_H - 1)
        koff = pl.multiple_of(start * GRID_W, GRID_W)
        qoff = pl.multiple_of(j * GRID_W, GRID_W)
        soff = pl.multiple_of(j * stacked, stacked)
        q2 = [_stack_heads(q_ref[0, pl.ds(qoff, GRID_W), cs], first_half) for cs in cols]
        s_w = [_dot_nt(q2[hp], k_ref[0, pl.ds(koff, band), cols[hp]])
               + jnp.concatenate([bias_ref[2 * hp, d0], bias_ref[2 * hp + 1, d0]], axis=0) for hp in range(n_pairs)]
        s_c = [sc_ref[hp, pl.ds(soff, stacked), :] for hp in range(n_pairs)]
        m = [jnp.maximum(jnp.max(s_w[hp], axis=-1, keepdims=True), jnp.max(s_c[hp], axis=-1, keepdims=True))
             for hp in range(n_pairs)]
        p_w = [jnp.exp(s_w[hp] - m[hp]) for hp in range(n_pairs)]
        p_c = [jnp.exp(s_c[hp] - m[hp]) for hp in range(n_pairs)]
        rden = [1.0 / (jnp.sum(p_w[hp], axis=-1, keepdims=True) + jnp.sum(p_c[hp], axis=-1, keepdims=True))
                for hp in range(n_pairs)]
        for hp in range(n_pairs):
            ow_ref[hp, pl.ds(soff, stacked), :] = _dot(p_w[hp].astype(BF16),
                                                       v_ref[0, pl.ds(koff, band), cols[hp]]) * rden[hp]
            pc_ref[hp, pl.ds(soff, stacked), :] = (p_c[hp] * rden[hp]).astype(BF16)
        return carry

    lax.fori_loop(0, rows_per_step, row_body, 0, unroll=2)

    for hp in range(n_pairs):
        o = ow_ref[hp] + _dot(pc_ref[hp], vc_ref[0, :, cols[hp]])
        o = o.reshape(rows_per_step, stacked, LANES)
        o = jnp.where(lane3 < HEAD_DIM, o[:, :GRID_W, :], o[:, GRID_W:, :])
        o_ref[0, :, cols[hp]] = o.reshape(rows_per_step * GRID_W, LANES).astype(o_ref.dtype)


def _natten(q, k, v, kc, vc, bias):
    b, n, w = q.shape
    n_rows = n // GRID_W
    rows_per_step = 8
    tq = rows_per_step * GRID_W
    lc = kc.shape[1]
    return pl.pallas_call(
        functools.partial(_natten_kernel, rows_per_step=rows_per_step, n_rows=n_rows),
        out_shape=jax.ShapeDtypeStruct((b, n, w), BF16),
        grid=(b, n_rows // rows_per_step),
        in_specs=[pl.BlockSpec((1, tq, w), lambda i, j: (i, j, 0)),
                  pl.BlockSpec((1, n, w), lambda i, j: (i, 0, 0)),
                  pl.BlockSpec((1, n, w), lambda i, j: (i, 0, 0)),
                  pl.BlockSpec((1, lc, w), lambda i, j: (i, 0, 0)),
                  pl.BlockSpec((1, lc, w), lambda i, j: (i, 0, 0)),
                  pl.BlockSpec(bias.shape, lambda i, j: (0, 0, 0, 0))],
        out_specs=pl.BlockSpec((1, tq, w), lambda i, j: (i, j, 0)),
        scratch_shapes=[pltpu.VMEM((NA_HEADS // 2, 2 * tq, lc), F32),
                        pltpu.VMEM((NA_HEADS // 2, 2 * tq, LANES), F32),
                        pltpu.VMEM((NA_HEADS // 2, 2 * tq, lc), BF16)],
        compiler_params=_params("parallel", "arbitrary"),
        name="neighbourhood_attention",
    )(q, k, v, kc, vc, bias)


def _ctx_attn_kernel(q_ref, k_ref, v_ref, o_ref):
    lc = q_ref.shape[1]
    lane = lax.broadcasted_iota(jnp.int32, (lc, LANES), 1)
    first_half = lane < HEAD_DIM
    outs = []
    for hp in range(NA_HEADS // 2):
        cs = slice(hp * LANES, (hp + 1) * LANES)
        q2 = _stack_heads(q_ref[0, :, cs], first_half)
        outs.append(_pair_attention(q2, None, None, None, k_ref[0, :, cs], v_ref[0, :, cs], first_half))
    o_ref[0] = jnp.concatenate(outs, axis=1).astype(o_ref.dtype)


def _ctx_attention(q, k, v):
    b, lc, w = q.shape
    spec = pl.BlockSpec((1, lc, w), lambda i: (i, 0, 0))
    return pl.pallas_call(
        _ctx_attn_kernel,
        out_shape=jax.ShapeDtypeStruct((b, lc, w), BF16),
        grid=(b,),
        in_specs=[spec, spec, spec],
        out_specs=spec,
        compiler_params=_params("parallel"),
        name="context_attention",
    )(q, k, v)


def _sgconv_kernel(gb_ref, p_ref, w_ref, o_ref):
    p = p_ref[0].astype(F32)
    n = p.shape[0]
    row = lax.broadcasted_iota(jnp.int32, p.shape, 0)
    prev = jnp.where(row == 0, 0.0, pltpu.roll(p, 1, axis=0))
    nxt = jnp.where(row == n - 1, 0.0, pltpu.roll(p, n - 1, axis=0))
    y = w_ref[0:1, :] * prev + w_ref[1:2, :] * p + w_ref[2:3, :] * nxt
    o_ref[0] = (gb_ref[0].astype(F32) * y).astype(o_ref.dtype)


def _short_gated_conv(gb, p, w):
    b, n, c = p.shape
    spec = pl.BlockSpec((1, n, LANES), lambda i, j: (i, 0, j))
    return pl.pallas_call(
        _sgconv_kernel,
        out_shape=jax.ShapeDtypeStruct((b, n, c), BF16),
        grid=(b, c // LANES),
        in_specs=[spec, spec, pl.BlockSpec((3, LANES), lambda i, j: (0, j))],
        out_specs=spec,
        compiler_params=_params("parallel", "parallel"),
        name="short_gated_conv",
    )(gb, p, w)


def _inproj_odd_kernel(pos_ref, *refs, n_chunks):
    x, refs = _load_tokens(refs, n_chunks)
    g_ref, sc_ref, sh_ref, wht_ref, wag_ref = refs[:5]
    outs = refs[5:]
    if n_chunks:
        outs[0][0] = x
        outs = outs[1:]
    hy_ref, ag_ref = outs
    u = _norm_modulate(x, g_ref[...], sc_ref[0], sh_ref[0]).astype(BF16)
    hy_ref[0] = _dot_nt(wht_ref[...], u).astype(BF16)
    a = _dot(u, wag_ref[:, :CF_WIDTH])
    g = _dot(u, wag_ref[:, CF_WIDTH:])
    ag_ref[0] = (a * jax.nn.sigmoid(g)).astype(BF16)


def _inproj_odd(xsrc, g, sc, sh, wht_bf, wag_bf):
    b, n, d = (xsrc.x1 if isinstance(xsrc, _PendingMoe) else xsrc).shape
    tm = min(512, n)
    nt = n // tm
    hw = wht_bf.shape[0]
    n_chunks, pos, x_ops, x_specs = _token_source(xsrc, tm, nt, d)
    tok = lambda i, j, pos: (i, j, 0)
    mod = lambda i, j, pos: (i, 0, 0)
    const = lambda i, j, pos: (0, 0)
    x_out = [jax.ShapeDtypeStruct((b, n, d), F32)] if n_chunks else []
    x_out_spec = [pl.BlockSpec((1, tm, d), tok)] if n_chunks else []
    res = pl.pallas_call(
        functools.partial(_inproj_odd_kernel, n_chunks=n_chunks),
        out_shape=tuple(x_out) + (jax.ShapeDtypeStruct((b, hw, n), BF16),
                                  jax.ShapeDtypeStruct((b, n, CF_WIDTH), BF16)),
        grid_spec=pltpu.PrefetchScalarGridSpec(
            num_scalar_prefetch=1,
            grid=(b, nt),
            in_specs=x_specs + [pl.BlockSpec((1, d), const),
                                pl.BlockSpec((1, 1, d), mod),
                                pl.BlockSpec((1, 1, d), mod),
                                pl.BlockSpec(wht_bf.shape, const),
                                pl.BlockSpec(wag_bf.shape, const)],
            out_specs=tuple(x_out_spec) + (pl.BlockSpec((1, hw, tm), lambda i, j, pos: (i, 0, j)),
                                           pl.BlockSpec((1, tm, CF_WIDTH), tok))),
        compiler_params=_params("parallel", "parallel"),
        name="inproj_odd",
    )(pos, *x_ops, g, sc, sh, wht_bf, wag_bf)
    return tuple(res) if n_chunks else (xsrc,) + tuple(res)


SUBLANES = 8
CF_PAD = 2 * SUBLANES
CF_ROWS = 128


def _conformer_kernel(ag_ref, w_ref, cb_ref, lg_ref, lb_ref, o_ref, pad_ref, *, seq):
    zeros = jnp.zeros((CF_PAD, CF_WIDTH), F32)
    pad_ref[0:CF_PAD, :] = zeros
    pad_ref[CF_PAD + seq:2 * CF_PAD + seq, :] = zeros
    pad_ref[CF_PAD:CF_PAD + seq, :] = ag_ref[0].astype(F32)
    shift0 = CF_PAD - CF_TAPS // 2
    n_groups = (shift0 + CF_TAPS - 1) // SUBLANES + 1

    def conv_rows(i, carry):
        n0 = pl.multiple_of(i * CF_ROWS, CF_ROWS)
        wins = [pad_ref[pl.ds(n0 + SUBLANES * a, CF_ROWS + SUBLANES), :] for a in range(n_groups)]
        acc = None
        for b in range(SUBLANES):
            part = None
            for a in range(n_groups):
                j = SUBLANES * a + b - shift0
                if 0 <= j < CF_TAPS:
                    term = w_ref[j:j + 1, :] * wins[a]
                    part = term if part is None else part + term
            part = part[b:b + CF_ROWS, :]
            acc = part if acc is None else acc + part
        y = acc + cb_ref[...]
        mu = jnp.mean(y, axis=-1, keepdims=True)
        yc = y - mu
        var = jnp.mean(yc * yc, axis=-1, keepdims=True)
        z = yc * lax.rsqrt(var + LN_EPS) * lg_ref[...] + lb_ref[...]
        o_ref[0, pl.ds(n0, CF_ROWS), :] = _silu(z).astype(o_ref.dtype)
        return carry

    lax.fori_loop(0, seq // CF_ROWS, conv_rows, 0)


def _conformer(ag, w, cb, lg, lb):
    b, n, c = ag.shape
    spec = pl.BlockSpec((1, n, c), lambda i: (i, 0, 0))
    vec = pl.BlockSpec((1, c), lambda i: (0, 0))
    return pl.pallas_call(
        functools.partial(_conformer_kernel, seq=n),
        out_shape=jax.ShapeDtypeStruct((b, n, c), BF16),
        grid=(b,),
        in_specs=[spec, pl.BlockSpec((CF_TAPS, c), lambda i: (0, 0)), vec, vec, vec],
        out_specs=spec,
        scratch_shapes=[pltpu.VMEM((n + 2 * CF_PAD, c), F32)],
        compiler_params=_params("parallel"),
        name="conformer_conv",
    )(ag, w, cb[None, :], lg[None, :], lb[None, :])


def _hyena_features(length):
    t = np.linspace(0.0, 1.0, length, dtype=np.float32)
    w = (2.0 * math.pi * np.arange(length, dtype=np.float32) / length).astype(np.float32)
    bands = np.linspace(1e-4, HY_BANDS - 1, HY_BANDS, dtype=np.float32)
    ang = (bands[:, None] * w[None, :]).astype(np.float32)
    zt = np.concatenate([t[None, :], np.cos(ang), -np.sin(ang)], axis=0).astype(np.float32)
    deltas = np.abs(np.linspace(HY_MIN_DECAY, HY_MAX_DECAY, HY_WIDTH, dtype=np.float32))
    rev = (length - np.arange(length)) % length
    zt2 = np.concatenate([zt, zt[:, rev]], axis=1)
    t2 = np.concatenate([t, t[rev]])[None, :]
    return zt2, t2, deltas[:, None]


def _taps_kernel(zt_ref, t_ref, dl_ref, w1t_ref, b1_ref, f0_ref, w2t_ref, b2_ref, f1_ref, w3t_ref,
                 o_ref, hid_ref):
    first = jnp.logical_and(pl.program_id(0) == 0, pl.program_id(1) == 0)
    length = t_ref.shape[1] // 2

    @pl.when(first)
    def _():
        h1 = jnp.sin(f0_ref[...] * (_dot_f32(w1t_ref[...], zt_ref[...]) + b1_ref[...]))
        hid_ref[...] = jnp.sin(f1_ref[...] * (_dot_f32(w2t_ref[...], h1) + b2_ref[...]))

    decay = jnp.exp(-(dl_ref[...] * t_ref[...]))
    fwd = _dot_f32(w3t_ref[0, 0], hid_ref[:, :length])
    bwd = _dot_f32(w3t_ref[0, 1], hid_ref[:, length:])
    taps = jnp.concatenate([fwd, bwd], axis=1) * decay
    nrm = jnp.sum(jnp.abs(taps), axis=-1, keepdims=True)
    lane = lax.broadcasted_iota(jnp.int32, taps.shape, 1)
    o_ref[0] = jnp.where(lane == length, 0.0, taps / nrm)


def _hyena_taps(length, w1, b1, w2, b2, w3, freq):
    zt2, t2, deltas = _hyena_features(length)
    cb = 128
    w3t = w3.T.reshape(2, 2, HY_WIDTH, HY_FFN)
    col = lambda v: v.reshape(HY_FFN, 1)
    const = lambda o, j: (0, 0)
    return pl.pallas_call(
        _taps_kernel,
        out_shape=jax.ShapeDtypeStruct((2, HY_WIDTH, 2 * length), F32),
        grid=(2, HY_WIDTH // cb),
        in_specs=[pl.BlockSpec((HY_EMB, 2 * length), const),
                  pl.BlockSpec((1, 2 * length), const),
                  pl.BlockSpec((cb, 1), lambda o, j: (j, 0)),
                  pl.BlockSpec((HY_FFN, HY_EMB), const),
                  pl.BlockSpec((HY_FFN, 1), const),
                  pl.BlockSpec((HY_FFN, 1), const),
                  pl.BlockSpec((HY_FFN, HY_FFN), const),
                  pl.BlockSpec((HY_FFN, 1), const),
                  pl.BlockSpec((HY_FFN, 1), const),
                  pl.BlockSpec((1, 2, cb, HY_FFN), lambda o, j: (o, 0, j, 0))],
        out_specs=pl.BlockSpec((1, cb, 2 * length), lambda o, j: (o, j, 0)),
        scratch_shapes=[pltpu.VMEM((HY_FFN, 2 * length), F32)],
        compiler_params=_params("arbitrary", "arbitrary"),
        name="hyena_filter_taps",
    )(jnp.asarray(zt2), jnp.asarray(t2), jnp.asarray(deltas), w1.T, col(b1), col(freq[0]),
      w2.T, col(b2), col(freq[1]), w3t)


def _dft_constants():
    na, nb, n = FFT_NA, FFT_NB, FFT_N
    half = na // 2
    ka = np.arange(na)
    ang_a = 2.0 * np.pi * np.outer(ka, ka) / na
    ca, sa = np.cos(ang_a), np.sin(ang_a)
    fa = np.block([[ca[:half], -sa[:half]], [sa[:half], ca[:half]]])
    fai = np.block([[ca[:, :half], sa[:, :half]], [-sa[:, :half], ca[:, :half]]])
    kb = np.arange(nb)
    ang_b = 2.0 * np.pi * np.outer(kb, kb) / nb
    cbm, sbm = np.cos(ang_b), np.sin(ang_b)
    fb = np.block([[cbm, -sbm], [sbm, cbm]])
    fbi = np.block([[cbm, sbm], [-sbm, cbm]])
    ang_t = 2.0 * np.pi * np.outer(kb, ka) / n
    ct, st = np.cos(ang_t), np.sin(ang_t)
    tw_fc = np.concatenate([ct, ct], axis=1)
    tw_fs = np.concatenate([st, -st], axis=1)
    tw_ic, tw_is = ct.T.copy(), st.T.copy()
    bf = lambda a: jnp.asarray(a, dtype=F32).astype(BF16)
    f32 = lambda a: jnp.asarray(a, dtype=F32)
    fa_real = np.concatenate([ca, -sa], axis=1)
    return dict(fa=bf(fa), fa_real=bf(fa_real), fai=bf(fai), fb=bf(fb), fbi=bf(fbi),
                tw_fc=f32(tw_fc), tw_fs=f32(tw_fs), tw_ic=f32(tw_ic), tw_is=f32(tw_is))


def _fft_forward(zr, zi, fa, tw_fc, tw_fs, fb):
    c, _, nb = zr.shape
    tr = jnp.swapaxes(zr, 1, 2)
    lhs = tr if zi is None else jnp.concatenate([tr, jnp.swapaxes(zi, 1, 2)], axis=2)
    a = _dot(lhs.reshape(c * nb, lhs.shape[2]).astype(BF16), fa).reshape(c, nb, 2 * FFT_NA)
    a = a * tw_fc + pltpu.roll(a, FFT_NA, axis=2) * tw_fs
    t = jnp.swapaxes(a, 1, 2)
    lhs2 = jnp.concatenate([t[:, :FFT_NA, :], t[:, FFT_NA:, :]], axis=2)
    x = _dot(lhs2.reshape(c * FFT_NA, 2 * nb).astype(BF16), fb)
    return x.reshape(c, FFT_NA, 2 * nb)


def _fft_inverse(y, fbi, tw_ic, tw_is, fai):
    c = y.shape[0]
    nb = FFT_NB
    b = _dot(y.reshape(c * FFT_NA, 2 * nb).astype(BF16), fbi).reshape(c, FFT_NA, 2 * nb)
    br, bi = b[:, :, :nb], b[:, :, nb:]
    rr = br * tw_ic - bi * tw_is
    ii = bi * tw_ic + br * tw_is
    t = jnp.swapaxes(jnp.concatenate([rr, ii], axis=1), 1, 2)
    o = _dot(t.reshape(c * nb, 2 * FFT_NA).astype(BF16), fai).reshape(c, nb, FFT_NA)
    o = jnp.swapaxes(o, 1, 2)
    return o[:, :FFT_NA // 2, :], o[:, FFT_NA // 2:, :]


def _filter_fft_kernel(taps_ref, fa_ref, twc_ref, tws_ref, fb_ref, o_ref):
    h = _fft_forward(taps_ref[0], None, fa_ref[...], twc_ref[...], tws_ref[...], fb_ref[...])
    o_ref[0] = h * (1.0 / FFT_N)


def _filter_fft(taps, consts):
    _, c, n = taps.shape
    cb = 32
    taps4 = taps.reshape(2, c, FFT_NA, FFT_NB)
    cm = lambda o, j: (0, 0)
    return pl.pallas_call(
        _filter_fft_kernel,
        out_shape=jax.ShapeDtypeStruct((2, c, FFT_NA, 2 * FFT_NB), F32),
        grid=(2, c // cb),
        in_specs=[pl.BlockSpec((1, cb, FFT_NA, FFT_NB), lambda o, j: (o, j, 0, 0)),
                  pl.BlockSpec(consts["fa_real"].shape, cm),
                  pl.BlockSpec(consts["tw_fc"].shape, cm),
                  pl.BlockSpec(consts["tw_fs"].shape, cm),
                  pl.BlockSpec(consts["fb"].shape, cm)],
        out_specs=pl.BlockSpec((1, cb, FFT_NA, 2 * FFT_NB), lambda o, j: (o, j, 0, 0)),
        compiler_params=_params("parallel", "parallel"),
        name="hyena_filter_fft",
    )(taps4, consts["fa_real"], consts["tw_fc"], consts["tw_fs"], consts["fb"])


def _shift_tokens(a, direction):
    rows = a.shape[-2]
    lane = lax.broadcasted_iota(jnp.int32, a.shape, a.ndim - 1)
    row = lax.broadcasted_iota(jnp.int32, a.shape, a.ndim - 2)
    if direction == 1:
        l = pltpu.roll(a, 1, axis=a.ndim - 1)
        ls = pltpu.roll(l, 1, axis=a.ndim - 2)
        out = jnp.where(lane == 0, ls, l)
        edge = jnp.logical_and(lane == 0, row == 0)
    else:
        l = pltpu.roll(a, LANES - 1, axis=a.ndim - 1)
        ls = pltpu.roll(l, rows - 1, axis=a.ndim - 2)
        out = jnp.where(lane == LANES - 1, ls, l)
        edge = jnp.logical_and(lane == LANES - 1, row == rows - 1)
    return jnp.where(edge, 0.0, out)


def _short_conv3(a, w_ref):
    return w_ref[0] * _shift_tokens(a, 1) + w_ref[1] * a + w_ref[2] * _shift_tokens(a, -1)


def _hyena_kernel(v_ref, x1_ref, x2_ref, wv_ref, w1_ref, w2_ref, h_ref, hb_ref,
                  fa_ref, twfc_ref, twfs_ref, fb_ref, fbi_ref, twic_ref, twis_ref, fai_ref, o_ref):
    fwd_c = (fa_ref[...], twfc_ref[...], twfs_ref[...], fb_ref[...])
    inv_c = (fbi_ref[...], twic_ref[...], twis_ref[...], fai_ref[...])
    z = _short_conv3(v_ref[...].astype(F32), wv_ref)
    zr, zi = z[0], z[1]
    nb = FFT_NB
    for o, (g_ref, gw_ref) in enumerate(((x1_ref, w1_ref), (x2_ref, w2_ref))):
        x = _fft_forward(zr, zi, *fwd_c)
        h = h_ref[o]
        xr, xi, hr, hi = x[:, :, :nb], x[:, :, nb:], h[:, :, :nb], h[:, :, nb:]
        y = jnp.concatenate([xr * hr - xi * hi, xr * hi + xi * hr], axis=2)
        yr, yi = _fft_inverse(y, *inv_c)
        gate = _short_conv3(g_ref[...].astype(F32), gw_ref)
        bias = hb_ref[o]
        zr = gate[0] * (yr + zr * bias)
        zi = gate[1] * (yi + zi * bias)
    o_ref[0] = zr.astype(o_ref.dtype)
    o_ref[1] = zi.astype(o_ref.dtype)


def _hyena_latent(hyt, short_w, h_spec, hy_bias, consts):
    b, c3, length = hyt.shape
    c = c3 // 3
    rows = length // FFT_NB
    cb = 32
    nblk = c // cb
    hy4 = hyt.reshape(b, c3, rows, FFT_NB)
    w4 = jnp.broadcast_to(short_w.reshape(3, c3, 1, 1), (3, c3, 1, FFT_NB))
    hb4 = jnp.broadcast_to(hy_bias.reshape(2, c, 1, 1), (2, c, 1, FFT_NB))
    sig = lambda g: pl.BlockSpec((2, cb, rows, FFT_NB), lambda j, p: (p, g * nblk + j, 0, 0))
    wsp = lambda g: pl.BlockSpec((3, cb, 1, FFT_NB), lambda j, p: (0, g * nblk + j, 0, 0))
    cm = lambda j, p: (0, 0)
    names = ("fa", "tw_fc", "tw_fs", "fb", "fbi", "tw_ic", "tw_is", "fai")
    out = pl.pallas_call(
        _hyena_kernel,
        out_shape=jax.ShapeDtypeStruct((b, c, rows, FFT_NB), BF16),
        grid=(nblk, b // 2),
        in_specs=[sig(0), sig(1), sig(2), wsp(0), wsp(1), wsp(2),
                  pl.BlockSpec((2, cb, FFT_NA, 2 * FFT_NB), lambda j, p: (0, j, 0, 0)),
                  pl.BlockSpec((2, cb, 1, FFT_NB), lambda j, p: (0, j, 0, 0))]
                 + [pl.BlockSpec(consts[k].shape, cm) for k in names],
        out_specs=pl.BlockSpec((2, cb, rows, FFT_NB), lambda j, p: (p, j, 0, 0)),
        compiler_params=_params("parallel", "arbitrary"),
        name="hyena_long_conv",
    )(hy4, hy4, hy4, w4, w4, w4, h_spec, hb4, *[consts[k] for k in names])
    return out.reshape(b, c, length)


def _dense_dft_constants(length):
    n = 2 * length
    k = np.arange(n)
    ang = 2.0 * np.pi * np.outer(k, k) / n
    fwd = np.concatenate([np.cos(ang), -np.sin(ang)], axis=1)
    inv = np.concatenate([np.cos(ang[:length]).T, -np.sin(ang[:length]).T], axis=0) / n
    bf = lambda a: jnp.asarray(a, dtype=F32).astype(BF16)
    return bf(fwd), bf(inv)


def _shift_lanes(a, direction):
    n = a.shape[-1]
    lane = lax.broadcasted_iota(jnp.int32, a.shape, a.ndim - 1)
    if direction == 1:
        return jnp.where(lane == 0, 0.0, pltpu.roll(a, 1, axis=a.ndim - 1))
    return jnp.where(lane == n - 1, 0.0, pltpu.roll(a, n - 1, axis=a.ndim - 1))


def _hyena_ctx_kernel(v_ref, x1_ref, x2_ref, wv_ref, w1_ref, w2_ref, taps_ref, hb_ref, fwd_ref, inv_ref, o_ref):
    bsz, cb, length = v_ref.shape
    n = 2 * length
    fwd_m, inv_m = fwd_ref[0:length, :], inv_ref[...]

    def conv3(ref, w_ref):
        a = ref[...].astype(F32)
        return w_ref[0] * _shift_lanes(a, 1) + w_ref[1] * a + w_ref[2] * _shift_lanes(a, -1)

    z = conv3(v_ref, wv_ref)
    for o, (g_ref, gw_ref) in enumerate(((x1_ref, w1_ref), (x2_ref, w2_ref))):
        h = _dot(taps_ref[o].astype(BF16), fwd_ref[...])
        hr, hi = h[:, :n], h[:, n:]
        x = _dot(z.reshape(bsz * cb, length).astype(BF16), fwd_m).reshape(bsz, cb, 2 * n)
        xr, xi = x[:, :, :n], x[:, :, n:]
        y = jnp.concatenate([xr * hr - xi * hi, xr * hi + xi * hr], axis=2)
        yt = _dot(y.reshape(bsz * cb, 2 * n).astype(BF16), inv_m).reshape(bsz, cb, length)
        z = conv3(g_ref, gw_ref) * (yt + z * hb_ref[o])
    o_ref[...] = z.astype(o_ref.dtype)


def _hyena_context(hyt, short_w, taps, hy_bias):
    b, c3, length = hyt.shape
    c = c3 // 3
    cb = 128
    nblk = c // cb
    fwd_m, inv_m = _dense_dft_constants(length)
    w4 = jnp.broadcast_to(short_w.reshape(3, c3, 1), (3, c3, length))
    hb = jnp.broadcast_to(hy_bias.reshape(2, c, 1), (2, c, length))
    sig = lambda g: pl.BlockSpec((b, cb, length), lambda j: (0, g * nblk + j, 0))
    wsp = lambda g: pl.BlockSpec((3, cb, length), lambda j: (0, g * nblk + j, 0))
    return pl.pallas_call(
        _hyena_ctx_kernel,
        out_shape=jax.ShapeDtypeStruct((b, c, length), BF16),
        grid=(nblk,),
        in_specs=[sig(0), sig(1), sig(2), wsp(0), wsp(1), wsp(2),
                  pl.BlockSpec((2, cb, 2 * length), lambda j: (0, j, 0)),
                  pl.BlockSpec((2, cb, length), lambda j: (0, j, 0)),
                  pl.BlockSpec(fwd_m.shape, lambda j: (0, 0)),
                  pl.BlockSpec(inv_m.shape, lambda j: (0, 0))],
        out_specs=pl.BlockSpec((b, cb, length), lambda j: (0, j, 0)),
        compiler_params=_params("parallel"),
        name="hyena_context_conv",
    )(hyt, hyt, hyt, w4, w4, w4, taps, hb, fwd_m, inv_m)


def _route(gl, el):
    row = lax.broadcasted_iota(jnp.int32, gl.shape, 0).astype(F32)
    grp = jnp.floor(row * (1.0 / EXPERTS_PER_GROUP))
    big = float(N_EXPERTS)
    gmax = jnp.max(gl, axis=0, keepdims=True)
    gidx = jnp.min(jnp.where(gl == gmax, grp, big), axis=0, keepdims=True)
    gsum = jnp.sum(jnp.exp(gl - gmax), axis=0, keepdims=True) * (1.0 / EXPERTS_PER_GROUP)
    g_w = 1.0 / gsum
    em = jnp.where(grp == gidx, el, NEG_INF)
    t1 = jnp.max(em, axis=0, keepdims=True)
    i1 = jnp.min(jnp.where(em == t1, row, big), axis=0, keepdims=True)
    em2 = jnp.where(row == i1, 2.0 * NEG_INF, em)
    t2 = jnp.max(em2, axis=0, keepdims=True)
    i2 = jnp.min(jnp.where(em2 == t2, row, big), axis=0, keepdims=True)
    e2 = jnp.exp(t2 - t1)
    den = 1.0 + e2
    w1 = g_w / den
    w2 = g_w * e2 / den
    return jnp.where(row == i1, w1, 0.0) + jnp.where(row == i2, w2, 0.0), gidx


SLOT_LANE = N_EXPERTS
CHUNKS_LANE = N_EXPERTS + 1
MOE_CHUNK = 64


def _tile_chunks(tm):
    return tm // MOE_CHUNK + N_GROUPS


def _dispatch_slots(gidx, tri):
    tm = gidx.shape[1]
    sub = lax.broadcasted_iota(jnp.int32, (SUBLANES, tm), 0)
    grp = sub.astype(F32)
    member = jnp.where(grp == gidx, 1.0, 0.0)
    rank = _dot(member.astype(BF16), tri)
    count = jnp.sum(member, axis=1, keepdims=True)
    chunks = jnp.floor((count + float(MOE_CHUNK - 1)) * (1.0 / MOE_CHUNK))
    first = jnp.zeros_like(chunks)
    for g in range(1, N_GROUPS):
        first = first + jnp.where(sub[:, 0:1] >= g, chunks[g - 1:g, :], 0.0)
    slot = jnp.sum(member * (first * float(MOE_CHUNK) + rank - 1.0), axis=0, keepdims=True)
    out = jnp.where(sub == 0, slot, 0.0)
    for g in range(N_GROUPS):
        out = out + jnp.where(sub == g + 1, chunks[g:g + 1, :], 0.0)
    return out


def _outproj_kernel(y1_ref, y2_ref, w_ref, x_ref, g1_ref, n2_ref, sc2_ref, sh2_ref, wr_ref, br_ref, tri_ref,
                    xo_ref, cmb_ref, hs_ref, cs_ref, *, channel_major):
    half = w_ref.shape[0] // 2
    dots = [_dot_tn if cm else _dot for cm in channel_major]
    acc = dots[0](y1_ref[0], w_ref[:half, :]) + dots[1](y2_ref[0], w_ref[half:, :])
    xn = x_ref[0] + g1_ref[0] * acc
    xo_ref[0] = xn
    h = _norm_modulate(xn, n2_ref[...], sc2_ref[0], sh2_ref[0])
    hi = h.astype(BF16)
    lo = (h - hi.astype(F32)).astype(BF16)
    p = _dot(hi, wr_ref[...])
    logits = p[:, :LANES] + p[:, LANES:] + _dot(lo, wr_ref[:, :LANES]) + br_ref[...]
    lt = logits.T
    cmb, gidx = _route(lt[0:N_EXPERTS], lt[N_EXPERTS:2 * N_EXPERTS])
    disp = _dispatch_slots(gidx, tri_ref[...])
    pad = jnp.zeros((LANES - N_EXPERTS - SUBLANES, cmb.shape[1]), F32)
    rows = jnp.concatenate([cmb, disp, pad], axis=0).T
    cmb_ref[0] = rows
    n_chunks = hs_ref.shape[0]
    srow = lax.broadcasted_iota(jnp.int32, (n_chunks * MOE_CHUNK, rows.shape[0]), 0).astype(F32)
    gather = jnp.where(srow == disp[0:1, :], 1.0, 0.0).astype(BF16)
    hs = _dot(gather, hi).astype(BF16)
    r_hi = rows.astype(BF16)
    r_lo = (rows - r_hi.astype(F32)).astype(BF16)
    cs2 = _dot(gather, jnp.concatenate([r_hi, r_lo], axis=1))
    cs = cs2[:, :LANES] + cs2[:, LANES:]
    for k in range(n_chunks):
        hs_ref[k] = hs[k * MOE_CHUNK:(k + 1) * MOE_CHUNK]
        cs_ref[k] = cs[k * MOE_CHUNK:(k + 1) * MOE_CHUNK]


def _outproj(y1, y2, w_bf, x, g1, n2g, sc2, sh2, wr2, br, channel_major):
    b, n, d = x.shape
    tm = min(512, n)
    nt = n // tm
    ntc = _tile_chunks(tm)
    half = d // 2
    tok = lambda i, j: (i, j, 0)
    mod = lambda i, j: (i, 0, 0)
    const = lambda i, j: (0, 0)
    srt = lambda i, j: (i * nt + j, 0, 0)
    y_specs = [pl.BlockSpec((1, half, tm), lambda i, j: (i, 0, j)) if cm else pl.BlockSpec((1, tm, half), tok)
               for cm in channel_major]
    return pl.pallas_call(
        functools.partial(_outproj_kernel, channel_major=tuple(channel_major)),
        out_shape=(jax.ShapeDtypeStruct((b, n, d), F32), jax.ShapeDtypeStruct((b, n, LANES), F32),
                   jax.ShapeDtypeStruct((b * nt * ntc, MOE_CHUNK, d), BF16),
                   jax.ShapeDtypeStruct((b * nt * ntc, MOE_CHUNK, LANES), F32)),
        grid=(b, nt),
        in_specs=[y_specs[0], y_specs[1],
                  pl.BlockSpec((d, d), const),
                  pl.BlockSpec((1, tm, d), tok),
                  pl.BlockSpec((1, 1, d), mod),
                  pl.BlockSpec((1, d), const),
                  pl.BlockSpec((1, 1, d), mod),
                  pl.BlockSpec((1, 1, d), mod),
                  pl.BlockSpec(wr2.shape, const),
                  pl.BlockSpec(br.shape, const),
                  pl.BlockSpec((tm, tm), const)],
        out_specs=(pl.BlockSpec((1, tm, d), tok), pl.BlockSpec((1, tm, LANES), tok),
                   pl.BlockSpec((ntc, MOE_CHUNK, d), srt), pl.BlockSpec((ntc, MOE_CHUNK, LANES), srt)),
        compiler_params=_params("parallel", "parallel"),
        name="outproj_router",
    )(y1, y2, w_bf, x, g1, n2g, sc2, sh2, wr2, br, jnp.asarray(np.triu(np.ones((tm, tm), np.float32)), dtype=BF16))


def _swiglu_group(h, w_rows, j, wg_ref, wu_ref, wd_ref):
    lane = lax.broadcasted_iota(jnp.int32, w_rows.shape, 1)
    experts = range(EXPERTS_PER_GROUP)
    w_e = [jnp.sum(jnp.where(lane == j * EXPERTS_PER_GROUP + e, w_rows, 0.0), axis=1, keepdims=True) for e in experts]
    a = [_dot(h, wg_ref[e]) for e in experts]
    u = [_dot(h, wu_ref[e]) for e in experts]
    act = [(_silu(a[e]) * u[e] * w_e[e]).astype(BF16) for e in experts]
    out = _dot(act[0], wd_ref[0])
    for e in experts[1:]:
        out = out + _dot(act[e], wd_ref[e])
    return out


MOE_STEP_CHUNKS = 8


def _moe_sorted_kernel(group_ref, used_ref, fresh_ref, src_ref, *refs):
    n = MOE_STEP_CHUNKS
    hs_refs, cs_refs = refs[:n], refs[n:2 * n]
    wg_ref, wu_ref, wd_ref, ys_ref, wg_bf, wu_bf, wd_bf = refs[2 * n:]
    s = pl.program_id(0)

    @pl.when(fresh_ref[s] > 0)
    def _():
        wg_bf[...] = wg_ref[...].astype(BF16)
        wu_bf[...] = wu_ref[...].astype(BF16)
        wd_bf[...] = wd_ref[...].astype(BF16)

    @pl.when(used_ref[s] > 0)
    def _():
        h = jnp.concatenate([r[0] for r in hs_refs], axis=0)
        w_rows = jnp.concatenate([r[0] for r in cs_refs], axis=0)
        y = _swiglu_group(h, w_rows, group_ref[s], wg_bf, wu_bf, wd_bf).astype(BF16)
        for k in range(n):
            ys_ref[k] = y[k * MOE_CHUNK:(k + 1) * MOE_CHUNK]

    @pl.when(used_ref[s] == 0)
    def _():
        ys_ref[...] = jnp.zeros(ys_ref.shape, ys_ref.dtype)


def _moe_sorted(hs, cs, step_group, step_used, step_fresh, chunk_src, wg, wu, wd, layer):
    _, _, d = hs.shape
    n = MOE_STEP_CHUNKS
    steps = step_group.shape[0]
    epg = EXPERTS_PER_GROUP
    chunk = lambda k, width: pl.BlockSpec((1, MOE_CHUNK, width),
                                          lambda s, grp, used, fresh, src: (src[s * n + k], 0, 0))
    wmap = lambda s, grp, used, fresh, src: (layer * N_GROUPS + grp[s], 0, 0)
    return pl.pallas_call(
        _moe_sorted_kernel,
        out_shape=jax.ShapeDtypeStruct((steps * n, MOE_CHUNK, d), BF16),
        grid_spec=pltpu.PrefetchScalarGridSpec(
            num_scalar_prefetch=4,
            grid=(steps,),
            in_specs=[chunk(k, d) for k in range(n)] + [chunk(k, LANES) for k in range(n)]
                     + [pl.BlockSpec((epg, d, D_EXPERT), wmap),
                        pl.BlockSpec((epg, d, D_EXPERT), wmap),
                        pl.BlockSpec((epg, D_EXPERT, d), wmap)],
            out_specs=pl.BlockSpec((n, MOE_CHUNK, d), lambda s, grp, used, fresh, src: (s, 0, 0)),
            scratch_shapes=[pltpu.VMEM((epg, d, D_EXPERT), BF16), pltpu.VMEM((epg, d, D_EXPERT), BF16),
                            pltpu.VMEM((epg, D_EXPERT, d), BF16)]),
        compiler_params=_params("arbitrary"),
        name="moe_sorted_experts",
    )(step_group, step_used, step_fresh, chunk_src, *([hs] * n), *([cs] * n), wg, wu, wd)


def _moe_unpermute(ys_refs, cmb_ref, x1_ref, g2_ref):
    cmb = cmb_ref[0]
    lane = lax.broadcasted_iota(jnp.int32, cmb.shape, 1)
    slot = jnp.sum(jnp.where(lane == SLOT_LANE, cmb, 0.0), axis=1, keepdims=True)
    rows = lax.broadcasted_iota(jnp.int32, (cmb.shape[0], len(ys_refs) * MOE_CHUNK), 1).astype(F32)
    scatter = jnp.where(slot == rows, 1.0, 0.0).astype(BF16)
    ys = jnp.concatenate([r[0] for r in ys_refs], axis=0)
    return x1_ref[0] + g2_ref[0] * _dot(scatter, ys)


def _moe_combine_kernel(pos_ref, *refs):
    x, (o_ref,) = _load_tokens(refs, len(refs) - 4)
    o_ref[0] = x


def _moe_combine(pending):
    b, n, d = pending.x1.shape
    tm = min(512, n)
    nt = n // tm
    n_chunks, pos, x_ops, x_specs = _token_source(pending, tm, nt, d)
    return pl.pallas_call(
        _moe_combine_kernel,
        out_shape=jax.ShapeDtypeStruct((b, n, d), F32),
        grid_spec=pltpu.PrefetchScalarGridSpec(
            num_scalar_prefetch=1,
            grid=(b, nt),
            in_specs=x_specs,
            out_specs=pl.BlockSpec((1, tm, d), lambda i, t, pos: (i, t, 0))),
        compiler_params=_params("parallel", "parallel"),
        name="moe_combine",
    )(pos, *x_ops)


def _moe_chunk_schedule(cmb):
    b, n, _ = cmb.shape
    tm = min(512, n)
    nt = b * (n // tm)
    ntc, nsc = _tile_chunks(tm), MOE_STEP_CHUNKS
    steps = nt * ntc // nsc + N_GROUPS
    i32 = jnp.int32
    cnt = cmb[:, ::tm, CHUNKS_LANE:CHUNKS_LANE + N_GROUPS].reshape(nt, N_GROUPS).astype(i32)
    gi = jnp.arange(N_GROUPS, dtype=i32)
    ti = jnp.arange(nt, dtype=i32)
    earlier_g = (gi[:, None] < gi[None, :]).astype(i32)
    in_tile = jnp.sum(cnt[:, :, None] * earlier_g[None], axis=1)
    before = jnp.sum(cnt[:, None, :] * (ti[:, None] < ti[None, :]).astype(i32)[:, :, None], axis=0)
    total = jnp.sum(cnt, axis=0)
    padded = (total + nsc - 1) // nsc * nsc
    gstart = jnp.sum(padded[:, None] * earlier_g, axis=0)
    gend = gstart + padded
    c = jnp.arange(tm // MOE_CHUNK, dtype=i32)
    dst = gstart[None, :, None] + before[:, :, None] + c[None, None, :]
    src = (ti * ntc)[:, None, None] + in_tile[:, :, None] + c[None, None, :]
    dst = jnp.where(c[None, None, :] < cnt[:, :, None], dst, -1).reshape(-1)
    p = jnp.arange(steps * nsc, dtype=i32)
    chunk_src = jnp.sum(jnp.where(dst[None, :] == p[:, None], src.reshape(-1)[None, :], 0), axis=1)
    first_chunk = jnp.arange(steps, dtype=i32) * nsc
    step_group = jnp.sum((first_chunk[:, None] >= gend[None, :-1]).astype(i32), axis=1)
    prev_group = jnp.sum((first_chunk[:, None] - nsc >= gend[None, :-1]).astype(i32), axis=1)
    step_used = (first_chunk < gend[-1]).astype(i32)
    step_fresh = jnp.logical_or(first_chunk == 0, step_group != prev_group).astype(i32)
    k = jnp.arange(ntc, dtype=i32)
    ends = in_tile + cnt
    grp_k = jnp.minimum(jnp.sum((k[None, :, None] >= ends[:, None, :]).astype(i32), axis=2), N_GROUPS - 1)
    base = gstart[None, :] + before - in_tile
    pos = k[None, :] + jnp.sum(jnp.where(grp_k[:, :, None] == gi[None, None, :], base[:, None, :], 0), axis=2)
    chunk_pos = jnp.where(k[None, :] < ends[:, -1:], pos, 0).reshape(-1).astype(i32)
    return step_group, step_used, step_fresh, chunk_src, chunk_pos


def _moe(hs, cs, cmb, x, g2, wg, wu, wd, layer):
    step_group, step_used, step_fresh, chunk_src, chunk_pos = _moe_chunk_schedule(cmb)
    ys = _moe_sorted(hs, cs, step_group, step_used, step_fresh, chunk_src, wg, wu, wd, layer)
    return _PendingMoe(ys, chunk_pos, cmb, x, g2)


def kernel(x, c, ctx, c_ctx, ada_w, ada_b, norm1_g, norm2_g, w_in_even, qn_g, kn_g, na_rpb, sc_conv_w, w_in_odd, hy_short_w, hy_w1, hy_b1, hy_w2, hy_b2, hy_w3, hy_freq, hy_bias, cf_conv_w, cf_conv_b, cf_ln_g, cf_ln_b, w_out, moe_w_group, moe_b_group, moe_w_router, moe_b_router, moe_w_gate, moe_w_up, moe_w_down):
    depth = ada_w.shape[0]
    bsz, seq, d = x.shape
    lc = ctx.shape[1]
    assert 2 * seq == FFT_N and d == D_MODEL and bsz % 2 == 0

    mods = _ada_modulation(jnp.concatenate([c, c_ctx[None, :]], axis=0), ada_w, ada_b)
    seg = jnp.asarray(np.kron(np.eye(NA_HEADS), np.ones((HEAD_DIM, HEAD_DIM))), dtype=BF16)
    consts = _dft_constants()
    wg = moe_w_gate.reshape((-1,) + moe_w_gate.shape[2:])
    wu = moe_w_up.reshape((-1,) + moe_w_up.shape[2:])
    wd = moe_w_down.reshape((-1,) + moe_w_down.shape[2:])

    for l in range(depth):
        ctx_needed = any(j % 2 == 0 for j in range(l + 1, depth))
        lat_mod = [m[:, None, :] for m in jnp.split(mods[l, :bsz], 6, axis=-1)]
        ctx_mod = [jnp.broadcast_to(m[None, :, :], (bsz, 1, d)) for m in jnp.split(mods[l, bsz:bsz + 1], 6, axis=-1)]
        sh1, sc1, g1, sh2, sc2, g2 = lat_mod
        csh1, csc1, cg1, csh2, csc2, cg2 = ctx_mod
        n1g = norm1_g[l][None, :]
        n2g = norm2_g[l][None, :]
        w_out_bf = w_out[l].astype(BF16)
        wr = jnp.concatenate([jnp.repeat(moe_w_group[l], EXPERTS_PER_GROUP, axis=1), moe_w_router[l],
                              jnp.zeros((d, LANES - 2 * N_EXPERTS), F32)], axis=1)
        br = jnp.concatenate([jnp.repeat(moe_b_group[l], EXPERTS_PER_GROUP), moe_b_router[l],
                              jnp.zeros((LANES - 2 * N_EXPERTS,), F32)])[None, :]
        wrh = wr.astype(BF16)
        wr2 = jnp.concatenate([wrh, (wr - wrh.astype(F32)).astype(BF16)], axis=1)

        if l % 2 == 0:
            e = l // 2
            w_in = w_in_even[e].astype(BF16)
            qg = jnp.tile(qn_g[e], NA_HEADS)[None, :]
            kg = jnp.tile(kn_g[e], NA_HEADS)[None, :]
            x, ql, kl, vl, gbl, pl_ = _inproj_even(x, n1g, sc1, sh1, w_in, qg, kg, seg)
            ctx, qc, kc, vc, gbc, pc = _inproj_even(ctx, n1g, csc1, csh1, w_in, qg, kg, seg)
            bias = _bias_table(na_rpb[e])
            y1 = _natten(ql, kl, vl, kc, vc, bias)
            y2 = _short_gated_conv(gbl, pl_, sc_conv_w[e])
            lat_cm = (False, False)
            if ctx_needed:
                y1c = _ctx_attention(qc, kc, vc)
                y2c = _short_gated_conv(gbc, pc, sc_conv_w[e])
        else:
            o = l // 2
            wht = w_in_odd[o][:, :3 * HY_WIDTH].T.astype(BF16)
            wag = w_in_odd[o][:, 3 * HY_WIDTH:].astype(BF16)
            x, hyt, ag = _inproj_odd(x, n1g, sc1, sh1, wht, wag)
            taps = _hyena_taps(seq, hy_w1[o], hy_b1[o], hy_w2[o], hy_b2[o], hy_w3[o], hy_freq[o])
            h_spec = _filter_fft(taps, consts)
            y1 = _hyena_latent(hyt, hy_short_w[o], h_spec, hy_bias[o], consts)
            cf_args = (cf_conv_w[o], cf_conv_b[o], cf_ln_g[o], cf_ln_b[o])
            y2 = _conformer(ag, *cf_args)
            lat_cm = (True, False)
            if ctx_needed:
                ctx, hytc, agc = _inproj_odd(ctx, n1g, csc1, csh1, wht, wag)
                taps_c = _hyena_taps(lc, hy_w1[o], hy_b1[o], hy_w2[o], hy_b2[o], hy_w3[o], hy_freq[o])
                y1c = _hyena_context(hytc, hy_short_w[o], taps_c, hy_bias[o])
                y2c = _conformer(agc, *cf_args)

        x1, cmb, hs, cs = _outproj(y1, y2, w_out_bf, x, g1, n2g, sc2, sh2, wr2, br, lat_cm)
        x = _moe(hs, cs, cmb, x1, g2, wg, wu, wd, l)
        if ctx_needed:
            c1, cmbc, hsc, csc = _outproj(y1c, y2c, w_out_bf, ctx, cg1, n2g, csc2, csh2, wr2, br, lat_cm)
            ctx = _moe(hsc, csc, cmbc, c1, cg2, wg, wu, wd, l)
    return _moe_combine(x)
```

```python
import functools
import math
from typing import NamedTuple

import numpy as np
import jax
import jax.numpy as jnp
from jax import lax
from jax.experimental import pallas as pl
from jax.experimental.pallas import tpu as pltpu

F32 = jnp.float32
BF16 = jnp.bfloat16

D_MODEL = 1024
GRID_W = 64
NA_HEADS = 8
HEAD_DIM = 64
NA_WIDTH = 512
NA_WIN_H = 8
NA_WIN_W = 16
SC_WIDTH = 512
HY_WIDTH = 512
HY_BANDS = 16
HY_EMB = 1 + 2 * HY_BANDS
HY_FFN = 64
HY_MAX_DECAY = math.log(1e-2) / 0.3
HY_MIN_DECAY = math.log(1e-2) / 1.5
CF_WIDTH = 512
CF_TAPS = 31
N_GROUPS = 4
EXPERTS_PER_GROUP = 4
N_EXPERTS = 16
D_EXPERT = 256
RMS_EPS = 1e-6
LN_EPS = 1e-5
NEG_INF = -1e30

VMEM_LIMIT_BYTES = 56 * 1024 * 1024
LANES = 128

FFT_NA = 64
FFT_NB = 128
FFT_N = FFT_NA * FFT_NB


def _params(*sem):
    return pltpu.CompilerParams(dimension_semantics=tuple(sem), vmem_limit_bytes=VMEM_LIMIT_BYTES)


def _dot(a, b):
    return jnp.dot(a, b, preferred_element_type=F32)


def _dot_nt(a, b):
    return lax.dot_general(a, b, (((1,), (1,)), ((), ())), preferred_element_type=F32)


def _dot_tn(a, b):
    return lax.dot_general(a, b, (((0,), (0,)), ((), ())), preferred_element_type=F32)


def _dot_f32(a, b):
    return jnp.dot(a, b, preferred_element_type=F32, precision=lax.Precision.HIGHEST)


def _silu(x):
    return x * jax.nn.sigmoid(x)


def _ada_kernel(ct_ref, w_ref, b_ref, o_ref, *, n_cond):
    ct = ct_ref[...]
    s = _silu(ct)
    w = w_ref[0]
    rows = [jnp.sum(w * s[:, r:r + 1], axis=0, keepdims=True) for r in range(n_cond)]
    rows.append(jnp.zeros((8 - n_cond, w.shape[1]), F32))
    o_ref[0] = jnp.concatenate(rows, axis=0) + b_ref[0]


def _ada_modulation(cond, ada_w, ada_b):
    n_cond, d = cond.shape
    depth, _, n6 = ada_w.shape
    tn = 3072
    ct = jnp.zeros((d, 8), F32).at[:, :n_cond].set(cond.T)
    return pl.pallas_call(
        functools.partial(_ada_kernel, n_cond=n_cond),
        out_shape=jax.ShapeDtypeStruct((depth, 8, n6), F32),
        grid=(depth, n6 // tn),
        in_specs=[pl.BlockSpec((d, 8), lambda l, j: (0, 0)),
                  pl.BlockSpec((1, d, tn), lambda l, j: (l, 0, j)),
                  pl.BlockSpec((1, 1, tn), lambda l, j: (l, 0, j))],
        out_specs=pl.BlockSpec((1, 8, tn), lambda l, j: (l, 0, j)),
        compiler_params=_params("parallel", "parallel"),
        name="ada_modulation",
    )(ct, ada_w, ada_b.reshape(depth, 1, n6))


def _norm_modulate(x, g, sc, sh):
    ms = jnp.mean(x * x, axis=-1, keepdims=True)
    return x * lax.rsqrt(ms + RMS_EPS) * g * (1.0 + sc) + sh


def _head_rmsnorm(t, seg, gain):
    ss = _dot((t * t).astype(BF16), seg)
    return t * lax.rsqrt(ss * (1.0 / HEAD_DIM) + RMS_EPS) * gain


class _PendingMoe(NamedTuple):
    ys: jax.Array
    chunk_pos: jax.Array
    cmb: jax.Array
    x1: jax.Array
    g2: jax.Array


def _token_source(xsrc, tm, nt, d):
    tok = lambda i, j, pos: (i, j, 0)
    if not isinstance(xsrc, _PendingMoe):
        return 0, jnp.zeros((1,), jnp.int32), [xsrc], [pl.BlockSpec((1, tm, d), tok)]
    ntc = _tile_chunks(tm)
    chunk = lambda k: pl.BlockSpec((1, MOE_CHUNK, d), lambda i, j, pos: (pos[(i * nt + j) * ntc + k], 0, 0))
    specs = [chunk(k) for k in range(ntc)] + [pl.BlockSpec((1, tm, LANES), tok), pl.BlockSpec((1, tm, d), tok),
                                              pl.BlockSpec((1, 1, d), lambda i, j, pos: (i, 0, 0))]
    return ntc, xsrc.chunk_pos, [xsrc.ys] * ntc + [xsrc.cmb, xsrc.x1, xsrc.g2], specs


def _load_tokens(refs, n_chunks):
    if n_chunks == 0:
        return refs[0][0], refs[1:]
    cmb_ref, x1_ref, g2_ref = refs[n_chunks:n_chunks + 3]
    return _moe_unpermute(refs[:n_chunks], cmb_ref, x1_ref, g2_ref), refs[n_chunks + 3:]


def _inproj_even_kernel(pos_ref, *refs, n_chunks):
    x, refs = _load_tokens(refs, n_chunks)
    g_ref, sc_ref, sh_ref, w_ref, qg_ref, kg_ref, seg_ref = refs[:7]
    outs = refs[7:]
    if n_chunks:
        outs[0][0] = x
        outs = outs[1:]
    q_ref, k_ref, v_ref, gb_ref, p_ref = outs
    u = _norm_modulate(x, g_ref[...], sc_ref[0], sh_ref[0]).astype(BF16)
    seg = seg_ref[...]
    w = NA_WIDTH
    q = _dot(u, w_ref[:, 0 * w:1 * w])
    q_ref[0] = (_head_rmsnorm(q, seg, qg_ref[...]) * (HEAD_DIM ** -0.5)).astype(BF16)
    k = _dot(u, w_ref[:, 1 * w:2 * w])
    k_ref[0] = _head_rmsnorm(k, seg, kg_ref[...]).astype(BF16)
    v_ref[0] = _dot(u, w_ref[:, 2 * w:3 * w]).astype(BF16)
    gb_ref[0] = _dot(u, w_ref[:, 3 * w:4 * w]).astype(BF16)
    gc = _dot(u, w_ref[:, 4 * w:5 * w])
    hv = _dot(u, w_ref[:, 5 * w:6 * w])
    p_ref[0] = (gc * hv).astype(BF16)


def _inproj_even(xsrc, g, sc, sh, w_bf, qg, kg, seg):
    b, n, d = (xsrc.x1 if isinstance(xsrc, _PendingMoe) else xsrc).shape
    tm = min(512, n)
    nt = n // tm
    n_chunks, pos, x_ops, x_specs = _token_source(xsrc, tm, nt, d)
    tok = lambda i, j, pos: (i, j, 0)
    mod = lambda i, j, pos: (i, 0, 0)
    const = lambda i, j, pos: (0, 0)
    out = jax.ShapeDtypeStruct((b, n, NA_WIDTH), BF16)
    x_out = [jax.ShapeDtypeStruct((b, n, d), F32)] if n_chunks else []
    x_out_spec = [pl.BlockSpec((1, tm, d), tok)] if n_chunks else []
    res = pl.pallas_call(
        functools.partial(_inproj_even_kernel, n_chunks=n_chunks),
        out_shape=tuple(x_out) + (out,) * 5,
        grid_spec=pltpu.PrefetchScalarGridSpec(
            num_scalar_prefetch=1,
            grid=(b, nt),
            in_specs=x_specs + [pl.BlockSpec((1, d), const),
                                pl.BlockSpec((1, 1, d), mod),
                                pl.BlockSpec((1, 1, d), mod),
                                pl.BlockSpec(w_bf.shape, const),
                                pl.BlockSpec((1, NA_WIDTH), const),
                                pl.BlockSpec((1, NA_WIDTH), const),
                                pl.BlockSpec((NA_WIDTH, NA_WIDTH), const)],
            out_specs=tuple(x_out_spec) + (pl.BlockSpec((1, tm, NA_WIDTH), tok),) * 5),
        compiler_params=_params("parallel", "parallel"),
        name="inproj_even",
    )(pos, *x_ops, g, sc, sh, w_bf, qg, kg, seg)
    return tuple(res) if n_chunks else (xsrc,) + tuple(res)


def _bias_kernel(rpb_ref, o_ref):
    h = pl.program_id(0)
    qi = lax.broadcasted_iota(jnp.int32, (GRID_W, GRID_W), 0)
    ki = lax.broadcasted_iota(jnp.int32, (GRID_W, GRID_W), 1)
    start = jnp.clip(qi - NA_WIN_W // 2, 0, GRID_W - NA_WIN_W)
    valid = jnp.logical_and(ki >= start, ki < start + NA_WIN_W)
    cidx = jnp.clip(ki - qi, -(NA_WIN_W - 1), NA_WIN_W - 1) + (NA_WIN_W - 1)
    n_dr = 2 * NA_WIN_H - 1
    n_dc = 2 * NA_WIN_W - 1

    def body(j, accs):
        m = cidx == j
        return tuple(jnp.where(m, rpb_ref[(h * n_dr + d) * n_dc + j], a) for d, a in enumerate(accs))

    accs = lax.fori_loop(0, n_dc, body, tuple(jnp.zeros((GRID_W, GRID_W), F32) for _ in range(n_dr)))
    tiles = [jnp.where(valid, a, NEG_INF) for a in accs]
    for d0 in range(NA_WIN_H):
        o_ref[0, d0] = jnp.concatenate(tiles[d0:d0 + NA_WIN_H], axis=1)


def _bias_table(rpb):
    return pl.pallas_call(
        _bias_kernel,
        out_shape=jax.ShapeDtypeStruct((NA_HEADS, NA_WIN_H, GRID_W, NA_WIN_H * GRID_W), F32),
        grid=(NA_HEADS,),
        in_specs=[pl.BlockSpec(memory_space=pltpu.SMEM)],
        out_specs=pl.BlockSpec((1, NA_WIN_H, GRID_W, NA_WIN_H * GRID_W), lambda h: (h, 0, 0, 0)),
        compiler_params=_params("arbitrary"),
        name="rpb_bias_table",
    )(rpb.reshape(-1))


def _pair_attention(q2, kw, vw, bias, kc, vc, first_half):
    s_c = _dot_nt(q2, kc)
    m = jnp.max(s_c, axis=-1, keepdims=True)
    if kw is not None:
        s_w = _dot_nt(q2, kw) + bias
        m = jnp.maximum(m, jnp.max(s_w, axis=-1, keepdims=True))
        p_w = jnp.exp(s_w - m)
    p_c = jnp.exp(s_c - m)
    den = jnp.sum(p_c, axis=-1, keepdims=True)
    o = _dot(p_c.astype(BF16), vc)
    if kw is not None:
        den = den + jnp.sum(p_w, axis=-1, keepdims=True)
        o = o + _dot(p_w.astype(BF16), vw)
    o = o / den
    half = o.shape[0] // 2
    return jnp.where(first_half, o[:half], o[half:])


def _stack_heads(qp, first_half):
    zero = jnp.zeros_like(qp)
    return jnp.concatenate([jnp.where(first_half, qp, zero), jnp.where(first_half, zero, qp)], axis=0)


def _natten_kernel(q_ref, k_ref, v_ref, kc_ref, vc_ref, bias_ref, o_ref, sc_ref, ow_ref, pc_ref,
                   *, rows_per_step, n_rows):
    blk = pl.program_id(1)
    lane = lax.broadcasted_iota(jnp.int32, (GRID_W, LANES), 1)
    first_half = lane < HEAD_DIM
    band = NA_WIN_H * GRID_W
    n_pairs = NA_HEADS // 2
    stacked = 2 * GRID_W
    cols = [slice(hp * LANES, (hp + 1) * LANES) for hp in range(n_pairs)]
    lane3 = lax.broadcasted_iota(jnp.int32, (rows_per_step, GRID_W, LANES), 2)

    for hp in range(n_pairs):
        q3 = q_ref[0, :, cols[hp]].reshape(rows_per_step, GRID_W, LANES)
        zero = jnp.zeros_like(q3)
        q_all = jnp.concatenate([jnp.where(lane3 < HEAD_DIM, q3, zero), jnp.where(lane3 < HEAD_DIM, zero, q3)], axis=1)
        sc_ref[hp] = _dot_nt(q_all.reshape(rows_per_step * stacked, LANES), kc_ref[0, :, cols[hp]])

    def row_body(j, carry):
        r = blk * rows_per_step + j
        start = jnp.clip(r - NA_WIN_H // 2, 0, n_rows - NA_WIN_H)
        d0 = start - r + (NA_WIN_H - 1)
        koff = pl.multiple_of(start * GRID_W, GRID_W)
        qoff = pl.multiple_of(j * GRID_W, GRID_W)
        soff = pl.multiple_of(j * stacked, stacked)
        q2 = [_stack_heads(q_ref[0, pl.ds(qoff, GRID_W), cs], first_half) for cs in cols]
        s_w = [_dot_nt(q2[hp], k_ref[0, pl.ds(koff, band), cols[hp]])
               + jnp.concatenate([bias_ref[2 * hp, d0], bias_ref[2 * hp + 1, d0]], axis=0) for hp in range(n_pairs)]
        s_c = [sc_ref[hp, pl.ds(soff, stacked), :] for hp in range(n_pairs)]
        m = [jnp.maximum(jnp.max(s_w[hp], axis=-1, keepdims=True), jnp.max(s_c[hp], axis=-1, keepdims=True))
             for hp in range(n_pairs)]
        p_w = [jnp.exp(s_w[hp] - m[hp]) for hp in range(n_pairs)]
        p_c = [jnp.exp(s_c[hp] - m[hp]) for hp in range(n_pairs)]
        rden = [1.0 / (jnp.sum(p_w[hp], axis=-1, keepdims=True) + jnp.sum(p_c[hp], axis=-1, keepdims=True))
                for hp in range(n_pairs)]
        for hp in range(n_pairs):
            ow_ref[hp, pl.ds(soff, stacked), :] = _dot(p_w[hp].astype(BF16),
                                                       v_ref[0, pl.ds(koff, band), cols[hp]]) * rden[hp]
            pc_ref[hp, pl.ds(soff, stacked), :] = (p_c[hp] * rden[hp]).astype(BF16)
        return carry

    lax.fori_loop(0, rows_per_step, row_body, 0, unroll=2)

    for hp in range(n_pairs):
        o = ow_ref[hp] + _dot(pc_ref[hp], vc_ref[0, :, cols[hp]])
        o = o.reshape(rows_per_step, stacked, LANES)
        o = jnp.where(lane3 < HEAD_DIM, o[:, :GRID_W, :], o[:, GRID_W:, :])
        o_ref[0, :, cols[hp]] = o.reshape(rows_per_step * GRID_W, LANES).astype(o_ref.dtype)


def _natten(q, k, v, kc, vc, bias):
    b, n, w = q.shape
    n_rows = n // GRID_W
    rows_per_step = 8
    tq = rows_per_step * GRID_W
    lc = kc.shape[1]
    return pl.pallas_call(
        functools.partial(_natten_kernel, rows_per_step=rows_per_step, n_rows=n_rows),
        out_shape=jax.ShapeDtypeStruct((b, n, w), BF16),
        grid=(b, n_rows // rows_per_step),
        in_specs=[pl.BlockSpec((1, tq, w), lambda i, j: (i, j, 0)),
                  pl.BlockSpec((1, n, w), lambda i, j: (i, 0, 0)),
                  pl.BlockSpec((1, n, w), lambda i, j: (i, 0, 0)),
                  pl.BlockSpec((1, lc, w), lambda i, j: (i, 0, 0)),
                  pl.BlockSpec((1, lc, w), lambda i, j: (i, 0, 0)),
                  pl.BlockSpec(bias.shape, lambda i, j: (0, 0, 0, 0))],
        out_specs=pl.BlockSpec((1, tq, w), lambda i, j: (i, j, 0)),
        scratch_shapes=[pltpu.VMEM((NA_HEADS // 2, 2 * tq, lc), F32),
                        pltpu.VMEM((NA_HEADS // 2, 2 * tq, LANES), F32),
                        pltpu.VMEM((NA_HEADS // 2, 2 * tq, lc), BF16)],
        compiler_params=_params("parallel", "arbitrary"),
        name="neighbourhood_attention",
    )(q, k, v, kc, vc, bias)


def _ctx_attn_kernel(q_ref, k_ref, v_ref, o_ref):
    lc = q_ref.shape[1]
    lane = lax.broadcasted_iota(jnp.int32, (lc, LANES), 1)
    first_half = lane < HEAD_DIM
    outs = []
    for hp in range(NA_HEADS // 2):
        cs = slice(hp * LANES, (hp + 1) * LANES)
        q2 = _stack_heads(q_ref[0, :, cs], first_half)
        outs.append(_pair_attention(q2, None, None, None, k_ref[0, :, cs], v_ref[0, :, cs], first_half))
    o_ref[0] = jnp.concatenate(outs, axis=1).astype(o_ref.dtype)


def _ctx_attention(q, k, v):
    b, lc, w = q.shape
    spec = pl.BlockSpec((1, lc, w), lambda i: (i, 0, 0))
    return pl.pallas_call(
        _ctx_attn_kernel,
        out_shape=jax.ShapeDtypeStruct((b, lc, w), BF16),
        grid=(b,),
        in_specs=[spec, spec, spec],
        out_specs=spec,
        compiler_params=_params("parallel"),
        name="context_attention",
    )(q, k, v)


def _sgconv_kernel(gb_ref, p_ref, w_ref, o_ref):
    p = p_ref[0].astype(F32)
    n = p.shape[0]
    row = lax.broadcasted_iota(jnp.int32, p.shape, 0)
    prev = jnp.where(row == 0, 0.0, pltpu.roll(p, 1, axis=0))
    nxt = jnp.where(row == n - 1, 0.0, pltpu.roll(p, n - 1, axis=0))
    y = w_ref[0:1, :] * prev + w_ref[1:2, :] * p + w_ref[2:3, :] * nxt
    o_ref[0] = (gb_ref[0].astype(F32) * y).astype(o_ref.dtype)


def _short_gated_conv(gb, p, w):
    b, n, c = p.shape
    spec = pl.BlockSpec((1, n, LANES), lambda i, j: (i, 0, j))
    return pl.pallas_call(
        _sgconv_kernel,
        out_shape=jax.ShapeDtypeStruct((b, n, c), BF16),
        grid=(b, c // LANES),
        in_specs=[spec, spec, pl.BlockSpec((3, LANES), lambda i, j: (0, j))],
        out_specs=spec,
        compiler_params=_params("parallel", "parallel"),
        name="short_gated_conv",
    )(gb, p, w)


def _inproj_odd_kernel(pos_ref, *refs, n_chunks):
    x, refs = _load_tokens(refs, n_chunks)
    g_ref, sc_ref, sh_ref, wht_ref, wag_ref = refs[:5]
    outs = refs[5:]
    if n_chunks:
        outs[0][0] = x
        outs = outs[1:]
    hy_ref, ag_ref = outs
    u = _norm_modulate(x, g_ref[...], sc_ref[0], sh_ref[0]).astype(BF16)
    hy_ref[0] = _dot_nt(wht_ref[...], u).astype(BF16)
    a = _dot(u, wag_ref[:, :CF_WIDTH])
    g = _dot(u, wag_ref[:, CF_WIDTH:])
    ag_ref[0] = (a * jax.nn.sigmoid(g)).astype(BF16)


def _inproj_odd(xsrc, g, sc, sh, wht_bf, wag_bf):
    b, n, d = (xsrc.x1 if isinstance(xsrc, _PendingMoe) else xsrc).shape
    tm = min(512, n)
    nt = n // tm
    hw = wht_bf.shape[0]
    n_chunks, pos, x_ops, x_specs = _token_source(xsrc, tm, nt, d)
    tok = lambda i, j, pos: (i, j, 0)
    mod = lambda i, j, pos: (i, 0, 0)
    const = lambda i, j, pos: (0, 0)
    x_out = [jax.ShapeDtypeStruct((b, n, d), F32)] if n_chunks else []
    x_out_spec = [pl.BlockSpec((1, tm, d), tok)] if n_chunks else []
    res = pl.pallas_call(
        functools.partial(_inproj_odd_kernel, n_chunks=n_chunks),
        out_shape=tuple(x_out) + (jax.ShapeDtypeStruct((b, hw, n), BF16),
                                  jax.ShapeDtypeStruct((b, n, CF_WIDTH), BF16)),
        grid_spec=pltpu.PrefetchScalarGridSpec(
            num_scalar_prefetch=1,
            grid=(b, nt),
            in_specs=x_specs + [pl.BlockSpec((1, d), const),
                                pl.BlockSpec((1, 1, d), mod),
                                pl.BlockSpec((1, 1, d), mod),
                                pl.BlockSpec(wht_bf.shape, const),
                                pl.BlockSpec(wag_bf.shape, const)],
            out_specs=tuple(x_out_spec) + (pl.BlockSpec((1, hw, tm), lambda i, j, pos: (i, 0, j)),
                                           pl.BlockSpec((1, tm, CF_WIDTH), tok))),
        compiler_params=_params("parallel", "parallel"),
        name="inproj_odd",
    )(pos, *x_ops, g, sc, sh, wht_bf, wag_bf)
    return tuple(res) if n_chunks else (xsrc,) + tuple(res)


SUBLANES = 8
CF_PAD = 2 * SUBLANES
CF_ROWS = 128


def _conformer_kernel(ag_ref, w_ref, cb_ref, lg_ref, lb_ref, o_ref, pad_ref, *, seq):
    zeros = jnp.zeros((CF_PAD, CF_WIDTH), F32)
    pad_ref[0:CF_PAD, :] = zeros
    pad_ref[CF_PAD + seq:2 * CF_PAD + seq, :] = zeros
    pad_ref[CF_PAD:CF_PAD + seq, :] = ag_ref[0].astype(F32)
    shift0 = CF_PAD - CF_TAPS // 2
    n_groups = (shift0 + CF_TAPS - 1) // SUBLANES + 1

    def conv_rows(i, carry):
        n0 = pl.multiple_of(i * CF_ROWS, CF_ROWS)
        wins = [pad_ref[pl.ds(n0 + SUBLANES * a, CF_ROWS + SUBLANES), :] for a in range(n_groups)]
        acc = None
        for b in range(SUBLANES):
            part = None
            for a in range(n_groups):
                j = SUBLANES * a + b - shift0
                if 0 <= j < CF_TAPS:
                    term = w_ref[j:j + 1, :] * wins[a]
                    part = term if part is None else part + term
            part = part[b:b + CF_ROWS, :]
            acc = part if acc is None else acc + part
        y = acc + cb_ref[...]
        mu = jnp.mean(y, axis=-1, keepdims=True)
        yc = y - mu
        var = jnp.mean(yc * yc, axis=-1, keepdims=True)
        z = yc * lax.rsqrt(var + LN_EPS) * lg_ref[...] + lb_ref[...]
        o_ref[0, pl.ds(n0, CF_ROWS), :] = _silu(z).astype(o_ref.dtype)
        return carry

    lax.fori_loop(0, seq // CF_ROWS, conv_rows, 0)


def _conformer(ag, w, cb, lg, lb):
    b, n, c = ag.shape
    spec = pl.BlockSpec((1, n, c), lambda i: (i, 0, 0))
    vec = pl.BlockSpec((1, c), lambda i: (0, 0))
    return pl.pallas_call(
        functools.partial(_conformer_kernel, seq=n),
        out_shape=jax.ShapeDtypeStruct((b, n, c), BF16),
        grid=(b,),
        in_specs=[spec, pl.BlockSpec((CF_TAPS, c), lambda i: (0, 0)), vec, vec, vec],
        out_specs=spec,
        scratch_shapes=[pltpu.VMEM((n + 2 * CF_PAD, c), F32)],
        compiler_params=_params("parallel"),
        name="conformer_conv",
    )(ag, w, cb[None, :], lg[None, :], lb[None, :])


def _hyena_features(length):
    t = np.linspace(0.0, 1.0, length, dtype=np.float32)
    w = (2.0 * math.pi * np.arange(length, dtype=np.float32) / length).astype(np.float32)
    bands = np.linspace(1e-4, HY_BANDS - 1, HY_BANDS, dtype=np.float32)
    ang = (bands[:, None] * w[None, :]).astype(np.float32)
    zt = np.concatenate([t[None, :], np.cos(ang), -np.sin(ang)], axis=0).astype(np.float32)
    deltas = np.abs(np.linspace(HY_MIN_DECAY, HY_MAX_DECAY, HY_WIDTH, dtype=np.float32))
    rev = (length - np.arange(length)) % length
    zt2 = np.concatenate([zt, zt[:, rev]], axis=1)
    t2 = np.concatenate([t, t[rev]])[None, :]
    return zt2, t2, deltas[:, None]


def _taps_kernel(zt_ref, t_ref, dl_ref, w1t_ref, b1_ref, f0_ref, w2t_ref, b2_ref, f1_ref, w3t_ref,
                 o_ref, hid_ref):
    first = jnp.logical_and(pl.program_id(0) == 0, pl.program_id(1) == 0)
    length = t_ref.shape[1] // 2

    @pl.when(first)
    def _():
        h1 = jnp.sin(f0_ref[...] * (_dot_f32(w1t_ref[...], zt_ref[...]) + b1_ref[...]))
        hid_ref[...] = jnp.sin(f1_ref[...] * (_dot_f32(w2t_ref[...], h1) + b2_ref[...]))

    decay = jnp.exp(-(dl_ref[...] * t_ref[...]))
    fwd = _dot_f32(w3t_ref[0, 0], hid_ref[:, :length])
    bwd = _dot_f32(w3t_ref[0, 1], hid_ref[:, length:])
    taps = jnp.concatenate([fwd, bwd], axis=1) * decay
    nrm = jnp.sum(jnp.abs(taps), axis=-1, keepdims=True)
    lane = lax.broadcasted_iota(jnp.int32, taps.shape, 1)
    o_ref[0] = jnp.where(lane == length, 0.0, taps / nrm)


def _hyena_taps(length, w1, b1, w2, b2, w3, freq):
    zt2, t2, deltas = _hyena_features(length)
    cb = 128
    w3t = w3.T.reshape(2, 2, HY_WIDTH, HY_FFN)
    col = lambda v: v.reshape(HY_FFN, 1)
    const = lambda o, j: (0, 0)
    return pl.pallas_call(
        _taps_kernel,
        out_shape=jax.ShapeDtypeStruct((2, HY_WIDTH, 2 * length), F32),
        grid=(2, HY_WIDTH // cb),
        in_specs=[pl.BlockSpec((HY_EMB, 2 * length), const),
                  pl.BlockSpec((1, 2 * length), const),
                  pl.BlockSpec((cb, 1), lambda o, j: (j, 0)),
                  pl.BlockSpec((HY_FFN, HY_EMB), const),
                  pl.BlockSpec((HY_FFN, 1), const),
                  pl.BlockSpec((HY_FFN, 1), const),
                  pl.BlockSpec((HY_FFN, HY_FFN), const),
                  pl.BlockSpec((HY_FFN, 1), const),
                  pl.BlockSpec((HY_FFN, 1), const),
                  pl.BlockSpec((1, 2, cb, HY_FFN), lambda o, j: (o, 0, j, 0))],
        out_specs=pl.BlockSpec((1, cb, 2 * length), lambda o, j: (o, j, 0)),
        scratch_shapes=[pltpu.VMEM((HY_FFN, 2 * length), F32)],
        compiler_params=_params("arbitrary", "arbitrary"),
        name="hyena_filter_taps",
    )(jnp.asarray(zt2), jnp.asarray(t2), jnp.asarray(deltas), w1.T, col(b1), col(freq[0]),
      w2.T, col(b2), col(freq[1]), w3t)


def _dft_constants():
    na, nb, n = FFT_NA, FFT_NB, FFT_N
    half = na // 2
    ka = np.arange(na)
    ang_a = 2.0 * np.pi * np.outer(ka, ka) / na
    ca, sa = np.cos(ang_a), np.sin(ang_a)
    fa = np.block([[ca[:half], -sa[:half]], [sa[:half], ca[:half]]])
    fai = np.block([[ca[:, :half], sa[:, :half]], [-sa[:, :half], ca[:, :half]]])
    kb = np.arange(nb)
    ang_b = 2.0 * np.pi * np.outer(kb, kb) / nb
    cbm, sbm = np.cos(ang_b), np.sin(ang_b)
    fb = np.block([[cbm, -sbm], [sbm, cbm]])
    fbi = np.block([[cbm, sbm], [-sbm, cbm]])
    ang_t = 2.0 * np.pi * np.outer(kb, ka) / n
    ct, st = np.cos(ang_t), np.sin(ang_t)
    tw_fc = np.concatenate([ct, ct], axis=1)
    tw_fs = np.concatenate([st, -st], axis=1)
    tw_ic, tw_is = ct.T.copy(), st.T.copy()
    bf = lambda a: jnp.asarray(a, dtype=F32).astype(BF16)
    f32 = lambda a: jnp.asarray(a, dtype=F32)
    fa_real = np.concatenate([ca, -sa], axis=1)
    return dict(fa=bf(fa), fa_real=bf(fa_real), fai=bf(fai), fb=bf(fb), fbi=bf(fbi),
                tw_fc=f32(tw_fc), tw_fs=f32(tw_fs), tw_ic=f32(tw_ic), tw_is=f32(tw_is))


def _fft_forward(zr, zi, fa, tw_fc, tw_fs, fb):
    c, _, nb = zr.shape
    tr = jnp.swapaxes(zr, 1, 2)
    lhs = tr if zi is None else jnp.concatenate([tr, jnp.swapaxes(zi, 1, 2)], axis=2)
    a = _dot(lhs.reshape(c * nb, lhs.shape[2]).astype(BF16), fa).reshape(c, nb, 2 * FFT_NA)
    a = a * tw_fc + pltpu.roll(a, FFT_NA, axis=2) * tw_fs
    t = jnp.swapaxes(a, 1, 2)
    lhs2 = jnp.concatenate([t[:, :FFT_NA, :], t[:, FFT_NA:, :]], axis=2)
    x = _dot(lhs2.reshape(c * FFT_NA, 2 * nb).astype(BF16), fb)
    return x.reshape(c, FFT_NA, 2 * nb)


def _fft_inverse(y, fbi, tw_ic, tw_is, fai):
    c = y.shape[0]
    nb = FFT_NB
    b = _dot(y.reshape(c * FFT_NA, 2 * nb).astype(BF16), fbi).reshape(c, FFT_NA, 2 * nb)
    br, bi = b[:, :, :nb], b[:, :, nb:]
    rr = br * tw_ic - bi * tw_is
    ii = bi * tw_ic + br * tw_is
    t = jnp.swapaxes(jnp.concatenate([rr, ii], axis=1), 1, 2)
    o = _dot(t.reshape(c * nb, 2 * FFT_NA).astype(BF16), fai).reshape(c, nb, FFT_NA)
    o = jnp.swapaxes(o, 1, 2)
    return o[:, :FFT_NA // 2, :], o[:, FFT_NA // 2:, :]


def _filter_fft_kernel(taps_ref, fa_ref, twc_ref, tws_ref, fb_ref, o_ref):
    h = _fft_forward(taps_ref[0], None, fa_ref[...], twc_ref[...], tws_ref[...], fb_ref[...])
    o_ref[0] = h * (1.0 / FFT_N)


def _filter_fft(taps, consts):
    _, c, n = taps.shape
    cb = 32
    taps4 = taps.reshape(2, c, FFT_NA, FFT_NB)
    cm = lambda o, j: (0, 0)
    return pl.pallas_call(
        _filter_fft_kernel,
        out_shape=jax.ShapeDtypeStruct((2, c, FFT_NA, 2 * FFT_NB), F32),
        grid=(2, c // cb),
        in_specs=[pl.BlockSpec((1, cb, FFT_NA, FFT_NB), lambda o, j: (o, j, 0, 0)),
                  pl.BlockSpec(consts["fa_real"].shape, cm),
                  pl.BlockSpec(consts["tw_fc"].shape, cm),
                  pl.BlockSpec(consts["tw_fs"].shape, cm),
                  pl.BlockSpec(consts["fb"].shape, cm)],
        out_specs=pl.BlockSpec((1, cb, FFT_NA, 2 * FFT_NB), lambda o, j: (o, j, 0, 0)),
        compiler_params=_params("parallel", "parallel"),
        name="hyena_filter_fft",
    )(taps4, consts["fa_real"], consts["tw_fc"], consts["tw_fs"], consts["fb"])


def _shift_tokens(a, direction):
    rows = a.shape[-2]
    lane = lax.broadcasted_iota(jnp.int32, a.shape, a.ndim - 1)
    row = lax.broadcasted_iota(jnp.int32, a.shape, a.ndim - 2)
    if direction == 1:
        l = pltpu.roll(a, 1, axis=a.ndim - 1)
        ls = pltpu.roll(l, 1, axis=a.ndim - 2)
        out = jnp.where(lane == 0, ls, l)
        edge = jnp.logical_and(lane == 0, row == 0)
    else:
        l = pltpu.roll(a, LANES - 1, axis=a.ndim - 1)
        ls = pltpu.roll(l, rows - 1, axis=a.ndim - 2)
        out = jnp.where(lane == LANES - 1, ls, l)
        edge = jnp.logical_and(lane == LANES - 1, row == rows - 1)
    return jnp.where(edge, 0.0, out)


def _short_conv3(a, w_ref):
    return w_ref[0] * _shift_tokens(a, 1) + w_ref[1] * a + w_ref[2] * _shift_tokens(a, -1)


def _hyena_kernel(v_ref, x1_ref, x2_ref, wv_ref, w1_ref, w2_ref, h_ref, hb_ref,
                  fa_ref, twfc_ref, twfs_ref, fb_ref, fbi_ref, twic_ref, twis_ref, fai_ref, o_ref):
    fwd_c = (fa_ref[...], twfc_ref[...], twfs_ref[...], fb_ref[...])
    inv_c = (fbi_ref[...], twic_ref[...], twis_ref[...], fai_ref[...])
    z = _short_conv3(v_ref[...].astype(F32), wv_ref)
    zr, zi = z[0], z[1]
    nb = FFT_NB
    for o, (g_ref, gw_ref) in enumerate(((x1_ref, w1_ref), (x2_ref, w2_ref))):
        x = _fft_forward(zr, zi, *fwd_c)
        h = h_ref[o]
        xr, xi, hr, hi = x[:, :, :nb], x[:, :, nb:], h[:, :, :nb], h[:, :, nb:]
        y = jnp.concatenate([xr * hr - xi * hi, xr * hi + xi * hr], axis=2)
        yr, yi = _fft_inverse(y, *inv_c)
        gate = _short_conv3(g_ref[...].astype(F32), gw_ref)
        bias = hb_ref[o]
        zr = gate[0] * (yr + zr * bias)
        zi = gate[1] * (yi + zi * bias)
    o_ref[0] = zr.astype(o_ref.dtype)
    o_ref[1] = zi.astype(o_ref.dtype)


def _hyena_latent(hyt, short_w, h_spec, hy_bias, consts):
    b, c3, length = hyt.shape
    c = c3 // 3
    rows = length // FFT_NB
    cb = 32
    nblk = c // cb
    hy4 = hyt.reshape(b, c3, rows, FFT_NB)
    w4 = jnp.broadcast_to(short_w.reshape(3, c3, 1, 1), (3, c3, 1, FFT_NB))
    hb4 = jnp.broadcast_to(hy_bias.reshape(2, c, 1, 1), (2, c, 1, FFT_NB))
    sig = lambda g: pl.BlockSpec((2, cb, rows, FFT_NB), lambda j, p: (p, g * nblk + j, 0, 0))
    wsp = lambda g: pl.BlockSpec((3, cb, 1, FFT_NB), lambda j, p: (0, g * nblk + j, 0, 0))
    cm = lambda j, p: (0, 0)
    names = ("fa", "tw_fc", "tw_fs", "fb", "fbi", "tw_ic", "tw_is", "fai")
    out = pl.pallas_call(
        _hyena_kernel,
        out_shape=jax.ShapeDtypeStruct((b, c, rows, FFT_NB), BF16),
        grid=(nblk, b // 2),
        in_specs=[sig(0), sig(1), sig(2), wsp(0), wsp(1), wsp(2),
                  pl.BlockSpec((2, cb, FFT_NA, 2 * FFT_NB), lambda j, p: (0, j, 0, 0)),
                  pl.BlockSpec((2, cb, 1, FFT_NB), lambda j, p: (0, j, 0, 0))]
                 + [pl.BlockSpec(consts[k].shape, cm) for k in names],
        out_specs=pl.BlockSpec((2, cb, rows, FFT_NB), lambda j, p: (p, j, 0, 0)),
        compiler_params=_params("parallel", "arbitrary"),
        name="hyena_long_conv",
    )(hy4, hy4, hy4, w4, w4, w4, h_spec, hb4, *[consts[k] for k in names])
    return out.reshape(b, c, length)


def _dense_dft_constants(length):
    n = 2 * length
    k = np.arange(n)
    ang = 2.0 * np.pi * np.outer(k, k) / n
    fwd = np.concatenate([np.cos(ang), -np.sin(ang)], axis=1)
    inv = np.concatenate([np.cos(ang[:length]).T, -np.sin(ang[:length]).T], axis=0) / n
    bf = lambda a: jnp.asarray(a, dtype=F32).astype(BF16)
    return bf(fwd), bf(inv)


def _shift_lanes(a, direction):
    n = a.shape[-1]
    lane = lax.broadcasted_iota(jnp.int32, a.shape, a.ndim - 1)
    if direction == 1:
        return jnp.where(lane == 0, 0.0, pltpu.roll(a, 1, axis=a.ndim - 1))
    return jnp.where(lane == n - 1, 0.0, pltpu.roll(a, n - 1, axis=a.ndim - 1))


def _hyena_ctx_kernel(v_ref, x1_ref, x2_ref, wv_ref, w1_ref, w2_ref, taps_ref, hb_ref, fwd_ref, inv_ref, o_ref):
    bsz, cb, length = v_ref.shape
    n = 2 * length
    fwd_m, inv_m = fwd_ref[0:length, :], inv_ref[...]

    def conv3(ref, w_ref):
        a = ref[...].astype(F32)
        return w_ref[0] * _shift_lanes(a, 1) + w_ref[1] * a + w_ref[2] * _shift_lanes(a, -1)

    z = conv3(v_ref, wv_ref)
    for o, (g_ref, gw_ref) in enumerate(((x1_ref, w1_ref), (x2_ref, w2_ref))):
        h = _dot(taps_ref[o].astype(BF16), fwd_ref[...])
        hr, hi = h[:, :n], h[:, n:]
        x = _dot(z.reshape(bsz * cb, length).astype(BF16), fwd_m).reshape(bsz, cb, 2 * n)
        xr, xi = x[:, :, :n], x[:, :, n:]
        y = jnp.concatenate([xr * hr - xi * hi, xr * hi + xi * hr], axis=2)
        yt = _dot(y.reshape(bsz * cb, 2 * n).astype(BF16), inv_m).reshape(bsz, cb, length)
        z = conv3(g_ref, gw_ref) * (yt + z * hb_ref[o])
    o_ref[...] = z.astype(o_ref.dtype)


def _hyena_context(hyt, short_w, taps, hy_bias):
    b, c3, length = hyt.shape
    c = c3 // 3
    cb = 128
    nblk = c // cb
    fwd_m, inv_m = _dense_dft_constants(length)
    w4 = jnp.broadcast_to(short_w.reshape(3, c3, 1), (3, c3, length))
    hb = jnp.broadcast_to(hy_bias.reshape(2, c, 1), (2, c, length))
    sig = lambda g: pl.BlockSpec((b, cb, length), lambda j: (0, g * nblk + j, 0))
    wsp = lambda g: pl.BlockSpec((3, cb, length), lambda j: (0, g * nblk + j, 0))
    return pl.pallas_call(
        _hyena_ctx_kernel,
        out_shape=jax.ShapeDtypeStruct((b, c, length), BF16),
        grid=(nblk,),
        in_specs=[sig(0), sig(1), sig(2), wsp(0), wsp(1), wsp(2),
                  pl.BlockSpec((2, cb, 2 * length), lambda j: (0, j, 0)),
                  pl.BlockSpec((2, cb, length), lambda j: (0, j, 0)),
                  pl.BlockSpec(fwd_m.shape, lambda j: (0, 0)),
                  pl.BlockSpec(inv_m.shape, lambda j: (0, 0))],
        out_specs=pl.BlockSpec((b, cb, length), lambda j: (0, j, 0)),
        compiler_params=_params("parallel"),
        name="hyena_context_conv",
    )(hyt, hyt, hyt, w4, w4, w4, taps, hb, fwd_m, inv_m)


def _route(gl, el):
    row = lax.broadcasted_iota(jnp.int32, gl.shape, 0).astype(F32)
    grp = jnp.floor(row * (1.0 / EXPERTS_PER_GROUP))
    big = float(N_EXPERTS)
    gmax = jnp.max(gl, axis=0, keepdims=True)
    gidx = jnp.min(jnp.where(gl == gmax, grp, big), axis=0, keepdims=True)
    gsum = jnp.sum(jnp.exp(gl - gmax), axis=0, keepdims=True) * (1.0 / EXPERTS_PER_GROUP)
    g_w = 1.0 / gsum
    em = jnp.where(grp == gidx, el, NEG_INF)
    t1 = jnp.max(em, axis=0, keepdims=True)
    i1 = jnp.min(jnp.where(em == t1, row, big), axis=0, keepdims=True)
    em2 = jnp.where(row == i1, 2.0 * NEG_INF, em)
    t2 = jnp.max(em2, axis=0, keepdims=True)
    i2 = jnp.min(jnp.where(em2 == t2, row, big), axis=0, keepdims=True)
    e2 = jnp.exp(t2 - t1)
    den = 1.0 + e2
    w1 = g_w / den
    w2 = g_w * e2 / den
    return jnp.where(row == i1, w1, 0.0) + jnp.where(row == i2, w2, 0.0), gidx


SLOT_LANE = N_EXPERTS
CHUNKS_LANE = N_EXPERTS + 1
MOE_CHUNK = 64


def _tile_chunks(tm):
    return tm // MOE_CHUNK + N_GROUPS


def _dispatch_slots(gidx, tri):
    tm = gidx.shape[1]
    sub = lax.broadcasted_iota(jnp.int32, (SUBLANES, tm), 0)
    grp = sub.astype(F32)
    member = jnp.where(grp == gidx, 1.0, 0.0)
    rank = _dot(member.astype(BF16), tri)
    count = jnp.sum(member, axis=1, keepdims=True)
    chunks = jnp.floor((count + float(MOE_CHUNK - 1)) * (1.0 / MOE_CHUNK))
    first = jnp.zeros_like(chunks)
    for g in range(1, N_GROUPS):
        first = first + jnp.where(sub[:, 0:1] >= g, chunks[g - 1:g, :], 0.0)
    slot = jnp.sum(member * (first * float(MOE_CHUNK) + rank - 1.0), axis=0, keepdims=True)
    out = jnp.where(sub == 0, slot, 0.0)
    for g in range(N_GROUPS):
        out = out + jnp.where(sub == g + 1, chunks[g:g + 1, :], 0.0)
    return out


def _outproj_kernel(y1_ref, y2_ref, w_ref, x_ref, g1_ref, n2_ref, sc2_ref, sh2_ref, wr_ref, br_ref, tri_ref,
                    xo_ref, cmb_ref, hs_ref, cs_ref, cnt_ref, *, channel_major):
    half = w_ref.shape[0] // 2
    dots = [_dot_tn if cm else _dot for cm in channel_major]
    acc = dots[0](y1_ref[0], w_ref[:half, :]) + dots[1](y2_ref[0], w_ref[half:, :])
    xn = x_ref[0] + g1_ref[0] * acc
    xo_ref[0] = xn
    h = _norm_modulate(xn, n2_ref[...], sc2_ref[0], sh2_ref[0])
    hi = h.astype(BF16)
    lo = (h - hi.astype(F32)).astype(BF16)
    p = _dot(hi, wr_ref[...])
    logits = p[:, :LANES] + p[:, LANES:] + _dot(lo, wr_ref[:, :LANES]) + br_ref[...]
    lt = logits.T
    cmb, gidx = _route(lt[0:N_EXPERTS], lt[N_EXPERTS:2 * N_EXPERTS])
    disp = _dispatch_slots(gidx, tri_ref[...])
    pad = jnp.zeros((LANES - N_EXPERTS - SUBLANES, cmb.shape[1]), F32)
    rows = jnp.concatenate([cmb, disp, pad], axis=0).T
    cmb_ref[0] = rows
    cnt_ref[0] = disp[:, :LANES]
    n_chunks = hs_ref.shape[0]
    srow = lax.broadcasted_iota(jnp.int32, (n_chunks * MOE_CHUNK, rows.shape[0]), 0).astype(F32)
    gather = jnp.where(srow == disp[0:1, :], 1.0, 0.0).astype(BF16)
    hs = _dot(gather, hi).astype(BF16)
    r_hi = rows.astype(BF16)
    r_lo = (rows - r_hi.astype(F32)).astype(BF16)
    cs2 = _dot(gather, jnp.concatenate([r_hi, r_lo], axis=1))
    cs = cs2[:, :LANES] + cs2[:, LANES:]
    for k in range(n_chunks):
        hs_ref[k] = hs[k * MOE_CHUNK:(k + 1) * MOE_CHUNK]
        cs_ref[k] = cs[k * MOE_CHUNK:(k + 1) * MOE_CHUNK]


def _outproj(y1, y2, w_bf, x, g1, n2g, sc2, sh2, wr2, br, channel_major):
    b, n, d = x.shape
    tm = min(512, n)
    nt = n // tm
    ntc = _tile_chunks(tm)
    half = d // 2
    tok = lambda i, j: (i, j, 0)
    mod = lambda i, j: (i, 0, 0)
    const = lambda i, j: (0, 0)
    srt = lambda i, j: (i * nt + j, 0, 0)
    y_specs = [pl.BlockSpec((1, half, tm), lambda i, j: (i, 0, j)) if cm else pl.BlockSpec((1, tm, half), tok)
               for cm in channel_major]
    return pl.pallas_call(
        functools.partial(_outproj_kernel, channel_major=tuple(channel_major)),
        out_shape=(jax.ShapeDtypeStruct((b, n, d), F32), jax.ShapeDtypeStruct((b, n, LANES), F32),
                   jax.ShapeDtypeStruct((b * nt * ntc, MOE_CHUNK, d), BF16),
                   jax.ShapeDtypeStruct((b * nt * ntc, MOE_CHUNK, LANES), F32),
                   jax.ShapeDtypeStruct((b * nt, SUBLANES, LANES), F32)),
        grid=(b, nt),
        in_specs=[y_specs[0], y_specs[1],
                  pl.BlockSpec((d, d), const),
                  pl.BlockSpec((1, tm, d), tok),
                  pl.BlockSpec((1, 1, d), mod),
                  pl.BlockSpec((1, d), const),
                  pl.BlockSpec((1, 1, d), mod),
                  pl.BlockSpec((1, 1, d), mod),
                  pl.BlockSpec(wr2.shape, const),
                  pl.BlockSpec(br.shape, const),
                  pl.BlockSpec((tm, tm), const)],
        out_specs=(pl.BlockSpec((1, tm, d), tok), pl.BlockSpec((1, tm, LANES), tok),
                   pl.BlockSpec((ntc, MOE_CHUNK, d), srt), pl.BlockSpec((ntc, MOE_CHUNK, LANES), srt),
                   pl.BlockSpec((1, SUBLANES, LANES), srt)),
        compiler_params=_params("parallel", "parallel"),
        name="outproj_router",
    )(y1, y2, w_bf, x, g1, n2g, sc2, sh2, wr2, br, jnp.asarray(np.triu(np.ones((tm, tm), np.float32)), dtype=BF16))


def _swiglu_group(h, w_rows, j, wg_ref, wu_ref, wd_ref):
    lane = lax.broadcasted_iota(jnp.int32, w_rows.shape, 1)
    experts = range(EXPERTS_PER_GROUP)
    w_e = [jnp.sum(jnp.where(lane == j * EXPERTS_PER_GROUP + e, w_rows, 0.0), axis=1, keepdims=True) for e in experts]
    a = [_dot(h, wg_ref[e]) for e in experts]
    u = [_dot(h, wu_ref[e]) for e in experts]
    act = [(_silu(a[e]) * u[e] * w_e[e]).astype(BF16) for e in experts]
    out = _dot(act[0], wd_ref[0])
    for e in experts[1:]:
        out = out + _dot(act[e], wd_ref[e])
    return out


MOE_STEP_CHUNKS = 8


def _moe_sorted_kernel(group_ref, used_ref, fresh_ref, src_ref, *refs):
    n = MOE_STEP_CHUNKS
    hs_refs, cs_refs = refs[:n], refs[n:2 * n]
    wg_ref, wu_ref, wd_ref, ys_ref, wg_bf, wu_bf, wd_bf = refs[2 * n:]
    s = pl.program_id(0)

    @pl.when(fresh_ref[s] > 0)
    def _():
        wg_bf[...] = wg_ref[...].astype(BF16)
        wu_bf[...] = wu_ref[...].astype(BF16)
        wd_bf[...] = wd_ref[...].astype(BF16)

    @pl.when(used_ref[s] > 0)
    def _():
        h = jnp.concatenate([r[0] for r in hs_refs], axis=0)
        w_rows = jnp.concatenate([r[0] for r in cs_refs], axis=0)
        y = _swiglu_group(h, w_rows, group_ref[s], wg_bf, wu_bf, wd_bf).astype(BF16)
        for k in range(n):
            ys_ref[k] = y[k * MOE_CHUNK:(k + 1) * MOE_CHUNK]

    @pl.when(used_ref[s] == 0)
    def _():
        ys_ref[...] = jnp.zeros(ys_ref.shape, ys_ref.dtype)


def _moe_sorted(hs, cs, step_group, step_used, step_fresh, chunk_src, wg, wu, wd, layer):
    _, _, d = hs.shape
    n = MOE_STEP_CHUNKS
    steps = step_group.shape[0]
    epg = EXPERTS_PER_GROUP
    chunk = lambda k, width: pl.BlockSpec((1, MOE_CHUNK, width),
                                          lambda s, grp, used, fresh, src: (src[s * n + k], 0, 0))
    wmap = lambda s, grp, used, fresh, src: (layer * N_GROUPS + grp[s], 0, 0)
    return pl.pallas_call(
        _moe_sorted_kernel,
        out_shape=jax.ShapeDtypeStruct((steps * n, MOE_CHUNK, d), BF16),
        grid_spec=pltpu.PrefetchScalarGridSpec(
            num_scalar_prefetch=4,
            grid=(steps,),
            in_specs=[chunk(k, d) for k in range(n)] + [chunk(k, LANES) for k in range(n)]
                     + [pl.BlockSpec((epg, d, D_EXPERT), wmap),
                        pl.BlockSpec((epg, d, D_EXPERT), wmap),
                        pl.BlockSpec((epg, D_EXPERT, d), wmap)],
            out_specs=pl.BlockSpec((n, MOE_CHUNK, d), lambda s, grp, used, fresh, src: (s, 0, 0)),
            scratch_shapes=[pltpu.VMEM((epg, d, D_EXPERT), BF16), pltpu.VMEM((epg, d, D_EXPERT), BF16),
                            pltpu.VMEM((epg, D_EXPERT, d), BF16)]),
        compiler_params=_params("arbitrary"),
        name="moe_sorted_experts",
    )(step_group, step_used, step_fresh, chunk_src, *([hs] * n), *([cs] * n), wg, wu, wd)


def _moe_unpermute(ys_refs, cmb_ref, x1_ref, g2_ref):
    cmb = cmb_ref[0]
    lane = lax.broadcasted_iota(jnp.int32, cmb.shape, 1)
    slot = jnp.sum(jnp.where(lane == SLOT_LANE, cmb, 0.0), axis=1, keepdims=True)
    rows = lax.broadcasted_iota(jnp.int32, (cmb.shape[0], len(ys_refs) * MOE_CHUNK), 1).astype(F32)
    scatter = jnp.where(slot == rows, 1.0, 0.0).astype(BF16)
    ys = jnp.concatenate([r[0] for r in ys_refs], axis=0)
    return x1_ref[0] + g2_ref[0] * _dot(scatter, ys)


def _moe_combine_kernel(pos_ref, *refs):
    x, (o_ref,) = _load_tokens(refs, len(refs) - 4)
    o_ref[0] = x


def _moe_combine(pending):
    b, n, d = pending.x1.shape
    tm = min(512, n)
    nt = n // tm
    n_chunks, pos, x_ops, x_specs = _token_source(pending, tm, nt, d)
    return pl.pallas_call(
        _moe_combine_kernel,
        out_shape=jax.ShapeDtypeStruct((b, n, d), F32),
        grid_spec=pltpu.PrefetchScalarGridSpec(
            num_scalar_prefetch=1,
            grid=(b, nt),
            in_specs=x_specs,
            out_specs=pl.BlockSpec((1, tm, d), lambda i, t, pos: (i, t, 0))),
        compiler_params=_params("parallel", "parallel"),
        name="moe_combine",
    )(pos, *x_ops)


def _moe_chunk_schedule(counts, n):
    tm = min(512, n)
    nt = counts.shape[0]
    ntc, nsc = _tile_chunks(tm), MOE_STEP_CHUNKS
    steps = nt * ntc // nsc + N_GROUPS
    i32 = jnp.int32
    cnt = counts[:, 1:1 + N_GROUPS, 0].astype(i32)
    gi = jnp.arange(N_GROUPS, dtype=i32)
    ti = jnp.arange(nt, dtype=i32)
    earlier_g = (gi[:, None] < gi[None, :]).astype(i32)
    in_tile = jnp.sum(cnt[:, :, None] * earlier_g[None], axis=1)
    before = jnp.sum(cnt[:, None, :] * (ti[:, None] < ti[None, :]).astype(i32)[:, :, None], axis=0)
    total = jnp.sum(cnt, axis=0)
    padded = (total + nsc - 1) // nsc * nsc
    gstart = jnp.sum(padded[:, None] * earlier_g, axis=0)
    gend = gstart + padded
    c = jnp.arange(tm // MOE_CHUNK, dtype=i32)
    dst = gstart[None, :, None] + before[:, :, None] + c[None, None, :]
    src = (ti * ntc)[:, None, None] + in_tile[:, :, None] + c[None, None, :]
    dst = jnp.where(c[None, None, :] < cnt[:, :, None], dst, -1).reshape(-1)
    p = jnp.arange(steps * nsc, dtype=i32)
    chunk_src = jnp.sum(jnp.where(dst[None, :] == p[:, None], src.reshape(-1)[None, :], 0), axis=1)
    first_chunk = jnp.arange(steps, dtype=i32) * nsc
    step_group = jnp.sum((first_chunk[:, None] >= gend[None, :-1]).astype(i32), axis=1)
    prev_group = jnp.sum((first_chunk[:, None] - nsc >= gend[None, :-1]).astype(i32), axis=1)
    step_used = (first_chunk < gend[-1]).astype(i32)
    step_fresh = jnp.logical_or(first_chunk == 0, step_group != prev_group).astype(i32)
    k = jnp.arange(ntc, dtype=i32)
    ends = in_tile + cnt
    grp_k = jnp.minimum(jnp.sum((k[None, :, None] >= ends[:, None, :]).astype(i32), axis=2), N_GROUPS - 1)
    base = gstart[None, :] + before - in_tile
    pos = k[None, :] + jnp.sum(jnp.where(grp_k[:, :, None] == gi[None, None, :], base[:, None, :], 0), axis=2)
    chunk_pos = jnp.where(k[None, :] < ends[:, -1:], pos, 0).reshape(-1).astype(i32)
    return step_group, step_used, step_fresh, chunk_src, chunk_pos


def _moe(hs, cs, cmb, counts, x, g2, wg, wu, wd, layer):
    step_group, step_used, step_fresh, chunk_src, chunk_pos = _moe_chunk_schedule(counts, x.shape[1])
    ys = _moe_sorted(hs, cs, step_group, step_used, step_fresh, chunk_src, wg, wu, wd, layer)
    return _PendingMoe(ys, chunk_pos, cmb, x, g2)


def kernel(x, c, ctx, c_ctx, ada_w, ada_b, norm1_g, norm2_g, w_in_even, qn_g, kn_g, na_rpb, sc_conv_w, w_in_odd, hy_short_w, hy_w1, hy_b1, hy_w2, hy_b2, hy_w3, hy_freq, hy_bias, cf_conv_w, cf_conv_b, cf_ln_g, cf_ln_b, w_out, moe_w_group, moe_b_group, moe_w_router, moe_b_router, moe_w_gate, moe_w_up, moe_w_down):
    depth = ada_w.shape[0]
    bsz, seq, d = x.shape
    lc = ctx.shape[1]
    assert 2 * seq == FFT_N and d == D_MODEL and bsz % 2 == 0

    mods = _ada_modulation(jnp.concatenate([c, c_ctx[None, :]], axis=0), ada_w, ada_b)
    seg = jnp.asarray(np.kron(np.eye(NA_HEADS), np.ones((HEAD_DIM, HEAD_DIM))), dtype=BF16)
    consts = _dft_constants()
    wg = moe_w_gate.reshape((-1,) + moe_w_gate.shape[2:])
    wu = moe_w_up.reshape((-1,) + moe_w_up.shape[2:])
    wd = moe_w_down.reshape((-1,) + moe_w_down.shape[2:])

    for l in range(depth):
        ctx_needed = any(j % 2 == 0 for j in range(l + 1, depth))
        lat_mod = [m[:, None, :] for m in jnp.split(mods[l, :bsz], 6, axis=-1)]
        ctx_mod = [jnp.broadcast_to(m[None, :, :], (bsz, 1, d)) for m in jnp.split(mods[l, bsz:bsz + 1], 6, axis=-1)]
        sh1, sc1, g1, sh2, sc2, g2 = lat_mod
        csh1, csc1, cg1, csh2, csc2, cg2 = ctx_mod
        n1g = norm1_g[l][None, :]
        n2g = norm2_g[l][None, :]
        w_out_bf = w_out[l].astype(BF16)
        wr = jnp.concatenate([jnp.repeat(moe_w_group[l], EXPERTS_PER_GROUP, axis=1), moe_w_router[l],
                              jnp.zeros((d, LANES - 2 * N_EXPERTS), F32)], axis=1)
        br = jnp.concatenate([jnp.repeat(moe_b_group[l], EXPERTS_PER_GROUP), moe_b_router[l],
                              jnp.zeros((LANES - 2 * N_EXPERTS,), F32)])[None, :]
        wrh = wr.astype(BF16)
        wr2 = jnp.concatenate([wrh, (wr - wrh.astype(F32)).astype(BF16)], axis=1)

        if l % 2 == 0:
            e = l // 2
            w_in = w_in_even[e].astype(BF16)
            qg = jnp.tile(qn_g[e], NA_HEADS)[None, :]
            kg = jnp.tile(kn_g[e], NA_HEADS)[None, :]
            x, ql, kl, vl, gbl, pl_ = _inproj_even(x, n1g, sc1, sh1, w_in, qg, kg, seg)
            ctx, qc, kc, vc, gbc, pc = _inproj_even(ctx, n1g, csc1, csh1, w_in, qg, kg, seg)
            bias = _bias_table(na_rpb[e])
            y1 = _natten(ql, kl, vl, kc, vc, bias)
            y2 = _short_gated_conv(gbl, pl_, sc_conv_w[e])
            lat_cm = (False, False)
            if ctx_needed:
                y1c = _ctx_attention(qc, kc, vc)
                y2c = _short_gated_conv(gbc, pc, sc_conv_w[e])
        else:
            o = l // 2
            wht = w_in_odd[o][:, :3 * HY_WIDTH].T.astype(BF16)
            wag = w_in_odd[o][:, 3 * HY_WIDTH:].astype(BF16)
            x, hyt, ag = _inproj_odd(x, n1g, sc1, sh1, wht, wag)
            taps = _hyena_taps(seq, hy_w1[o], hy_b1[o], hy_w2[o], hy_b2[o], hy_w3[o], hy_freq[o])
            h_spec = _filter_fft(taps, consts)
            y1 = _hyena_latent(hyt, hy_short_w[o], h_spec, hy_bias[o], consts)
            cf_args = (cf_conv_w[o], cf_conv_b[o], cf_ln_g[o], cf_ln_b[o])
            y2 = _conformer(ag, *cf_args)
            lat_cm = (True, False)
            if ctx_needed:
                ctx, hytc, agc = _inproj_odd(ctx, n1g, csc1, csh1, wht, wag)
                taps_c = _hyena_taps(lc, hy_w1[o], hy_b1[o], hy_w2[o], hy_b2[o], hy_w3[o], hy_freq[o])
                y1c = _hyena_context(hytc, hy_short_w[o], taps_c, hy_bias[o])
                y2c = _conformer(agc, *cf_args)

        x1, cmb, hs, cs, cnt = _outproj(y1, y2, w_out_bf, x, g1, n2g, sc2, sh2, wr2, br, lat_cm)
        x = _moe(hs, cs, cmb, cnt, x1, g2, wg, wu, wd, l)
        if ctx_needed:
            c1, cmbc, hsc, csc, cntc = _outproj(y1c, y2c, w_out_bf, ctx, cg1, n2g, csc2, csh2, wr2, br, lat_cm)
            ctx = _moe(hsc, csc, cmbc, cntc, c1, cg2, wg, wu, wd, l)
    return _moe_combine(x)
```

```python
import functools
import math
from typing import NamedTuple

import numpy as np
import jax
import jax.numpy as jnp
from jax import lax
from jax.experimental import pallas as pl
from jax.experimental.pallas import tpu as pltpu

F32 = jnp.float32
BF16 = jnp.bfloat16

D_MODEL = 1024
GRID_W = 64
NA_HEADS = 8
HEAD_DIM = 64
NA_WIDTH = 512
NA_WIN_H = 8
NA_WIN_W = 16
SC_WIDTH = 512
HY_WIDTH = 512
HY_BANDS = 16
HY_EMB = 1 + 2 * HY_BANDS
HY_FFN = 64
HY_MAX_DECAY = math.log(1e-2) / 0.3
HY_MIN_DECAY = math.log(1e-2) / 1.5
CF_WIDTH = 512
CF_TAPS = 31
N_GROUPS = 4
EXPERTS_PER_GROUP = 4
N_EXPERTS = 16
D_EXPERT = 256
RMS_EPS = 1e-6
LN_EPS = 1e-5
NEG_INF = -1e30

VMEM_LIMIT_BYTES = 56 * 1024 * 1024
LANES = 128

FFT_NA = 64
FFT_NB = 128
FFT_N = FFT_NA * FFT_NB


def _params(*sem):
    return pltpu.CompilerParams(dimension_semantics=tuple(sem), vmem_limit_bytes=VMEM_LIMIT_BYTES)


def _dot(a, b):
    return jnp.dot(a, b, preferred_element_type=F32)


def _dot_nt(a, b):
    return lax.dot_general(a, b, (((1,), (1,)), ((), ())), preferred_element_type=F32)


def _dot_tn(a, b):
    return lax.dot_general(a, b, (((0,), (0,)), ((), ())), preferred_element_type=F32)


def _dot_f32(a, b):
    return jnp.dot(a, b, preferred_element_type=F32, precision=lax.Precision.HIGHEST)


def _silu(x):
    return x * jax.nn.sigmoid(x)


def _ada_kernel(ct_ref, w_ref, b_ref, o_ref, *, n_cond):
    ct = ct_ref[...]
    s = _silu(ct)
    w = w_ref[0]
    rows = [jnp.sum(w * s[:, r:r + 1], axis=0, keepdims=True) for r in range(n_cond)]
    rows.append(jnp.zeros((8 - n_cond, w.shape[1]), F32))
    o_ref[0] = jnp.concatenate(rows, axis=0) + b_ref[0]


def _ada_modulation(cond, ada_w, ada_b):
    n_cond, d = cond.shape
    depth, _, n6 = ada_w.shape
    tn = 1536
    ct = jnp.zeros((d, 8), F32).at[:, :n_cond].set(cond.T)
    return pl.pallas_call(
        functools.partial(_ada_kernel, n_cond=n_cond),
        out_shape=jax.ShapeDtypeStruct((depth, 8, n6), F32),
        grid=(depth, n6 // tn),
        in_specs=[pl.BlockSpec((d, 8), lambda l, j: (0, 0)),
                  pl.BlockSpec((1, d, tn), lambda l, j: (l, 0, j)),
                  pl.BlockSpec((1, 1, tn), lambda l, j: (l, 0, j))],
        out_specs=pl.BlockSpec((1, 8, tn), lambda l, j: (l, 0, j)),
        compiler_params=_params("parallel", "parallel"),
        name="ada_modulation",
    )(ct, ada_w, ada_b.reshape(depth, 1, n6))


def _norm_modulate(x, g, sc, sh):
    ms = jnp.mean(x * x, axis=-1, keepdims=True)
    return x * lax.rsqrt(ms + RMS_EPS) * g * (1.0 + sc) + sh


def _head_rmsnorm(t, seg, gain):
    ss = _dot((t * t).astype(BF16), seg)
    return t * lax.rsqrt(ss * (1.0 / HEAD_DIM) + RMS_EPS) * gain


class _PendingMoe(NamedTuple):
    ys: jax.Array
    chunk_pos: jax.Array
    cmb: jax.Array
    x1: jax.Array
    g2: jax.Array


def _token_source(xsrc, tm, nt, d):
    tok = lambda i, j, pos: (i, j, 0)
    if not isinstance(xsrc, _PendingMoe):
        return 0, jnp.zeros((1,), jnp.int32), [xsrc], [pl.BlockSpec((1, tm, d), tok)]
    ntc = _tile_chunks(tm)
    chunk = lambda k: pl.BlockSpec((1, MOE_CHUNK, d), lambda i, j, pos: (pos[(i * nt + j) * ntc + k], 0, 0))
    specs = [chunk(k) for k in range(ntc)] + [pl.BlockSpec((1, tm, LANES), tok), pl.BlockSpec((1, tm, d), tok),
                                              pl.BlockSpec((1, 1, d), lambda i, j, pos: (i, 0, 0))]
    return ntc, xsrc.chunk_pos, [xsrc.ys] * ntc + [xsrc.cmb, xsrc.x1, xsrc.g2], specs


def _load_tokens(refs, n_chunks):
    if n_chunks == 0:
        return refs[0][0], refs[1:]
    cmb_ref, x1_ref, g2_ref = refs[n_chunks:n_chunks + 3]
    return _moe_unpermute(refs[:n_chunks], cmb_ref, x1_ref, g2_ref), refs[n_chunks + 3:]


def _inproj_even_kernel(pos_ref, *refs, n_chunks):
    x, refs = _load_tokens(refs, n_chunks)
    g_ref, sc_ref, sh_ref, w_ref, qg_ref, kg_ref, seg_ref = refs[:7]
    outs = refs[7:]
    if n_chunks:
        outs[0][0] = x
        outs = outs[1:]
    q_ref, k_ref, v_ref, gb_ref, p_ref = outs
    u = _norm_modulate(x, g_ref[...], sc_ref[0], sh_ref[0]).astype(BF16)
    seg = seg_ref[...]
    w = NA_WIDTH
    q = _dot(u, w_ref[:, 0 * w:1 * w])
    q_ref[0] = (_head_rmsnorm(q, seg, qg_ref[...]) * (HEAD_DIM ** -0.5)).astype(BF16)
    k = _dot(u, w_ref[:, 1 * w:2 * w])
    k_ref[0] = _head_rmsnorm(k, seg, kg_ref[...]).astype(BF16)
    v_ref[0] = _dot(u, w_ref[:, 2 * w:3 * w]).astype(BF16)
    gb_ref[0] = _dot(u, w_ref[:, 3 * w:4 * w]).astype(BF16)
    gc = _dot(u, w_ref[:, 4 * w:5 * w])
    hv = _dot(u, w_ref[:, 5 * w:6 * w])
    p_ref[0] = (gc * hv).astype(BF16)


def _inproj_even(xsrc, g, sc, sh, w_bf, qg, kg, seg):
    b, n, d = (xsrc.x1 if isinstance(xsrc, _PendingMoe) else xsrc).shape
    tm = min(512, n)
    nt = n // tm
    n_chunks, pos, x_ops, x_specs = _token_source(xsrc, tm, nt, d)
    tok = lambda i, j, pos: (i, j, 0)
    mod = lambda i, j, pos: (i, 0, 0)
    const = lambda i, j, pos: (0, 0)
    out = jax.ShapeDtypeStruct((b, n, NA_WIDTH), BF16)
    x_out = [jax.ShapeDtypeStruct((b, n, d), F32)] if n_chunks else []
    x_out_spec = [pl.BlockSpec((1, tm, d), tok)] if n_chunks else []
    res = pl.pallas_call(
        functools.partial(_inproj_even_kernel, n_chunks=n_chunks),
        out_shape=tuple(x_out) + (out,) * 5,
        grid_spec=pltpu.PrefetchScalarGridSpec(
            num_scalar_prefetch=1,
            grid=(b, nt),
            in_specs=x_specs + [pl.BlockSpec((1, d), const),
                                pl.BlockSpec((1, 1, d), mod),
                                pl.BlockSpec((1, 1, d), mod),
                                pl.BlockSpec(w_bf.shape, const),
                                pl.BlockSpec((1, NA_WIDTH), const),
                                pl.BlockSpec((1, NA_WIDTH), const),
                                pl.BlockSpec((NA_WIDTH, NA_WIDTH), const)],
            out_specs=tuple(x_out_spec) + (pl.BlockSpec((1, tm, NA_WIDTH), tok),) * 5),
        compiler_params=_params("parallel", "parallel"),
        name="inproj_even",
    )(pos, *x_ops, g, sc, sh, w_bf, qg, kg, seg)
    return tuple(res) if n_chunks else (xsrc,) + tuple(res)


def _bias_kernel(rpb_ref, o_ref):
    h = pl.program_id(0)
    qi = lax.broadcasted_iota(jnp.int32, (GRID_W, GRID_W), 0)
    ki = lax.broadcasted_iota(jnp.int32, (GRID_W, GRID_W), 1)
    start = jnp.clip(qi - NA_WIN_W // 2, 0, GRID_W - NA_WIN_W)
    valid = jnp.logical_and(ki >= start, ki < start + NA_WIN_W)
    cidx = jnp.clip(ki - qi, -(NA_WIN_W - 1), NA_WIN_W - 1) + (NA_WIN_W - 1)
    n_dr = 2 * NA_WIN_H - 1
    n_dc = 2 * NA_WIN_W - 1

    def body(j, accs):
        m = cidx == j
        return tuple(jnp.where(m, rpb_ref[(h * n_dr + d) * n_dc + j], a) for d, a in enumerate(accs))

    accs = lax.fori_loop(0, n_dc, body, tuple(jnp.zeros((GRID_W, GRID_W), F32) for _ in range(n_dr)))
    tiles = [jnp.where(valid, a, NEG_INF) for a in accs]
    for d0 in range(NA_WIN_H):
        o_ref[0, d0] = jnp.concatenate(tiles[d0:d0 + NA_WIN_H], axis=1)


def _bias_table(rpb):
    return pl.pallas_call(
        _bias_kernel,
        out_shape=jax.ShapeDtypeStruct((NA_HEADS, NA_WIN_H, GRID_W, NA_WIN_H * GRID_W), F32),
        grid=(NA_HEADS,),
        in_specs=[pl.BlockSpec(memory_space=pltpu.SMEM)],
        out_specs=pl.BlockSpec((1, NA_WIN_H, GRID_W, NA_WIN_H * GRID_W), lambda h: (h, 0, 0, 0)),
        compiler_params=_params("arbitrary"),
        name="rpb_bias_table",
    )(rpb.reshape(-1))


def _pair_attention(q2, kw, vw, bias, kc, vc, first_half):
    s_c = _dot_nt(q2, kc)
    m = jnp.max(s_c, axis=-1, keepdims=True)
    if kw is not None:
        s_w = _dot_nt(q2, kw) + bias
        m = jnp.maximum(m, jnp.max(s_w, axis=-1, keepdims=True))
        p_w = jnp.exp(s_w - m)
    p_c = jnp.exp(s_c - m)
    den = jnp.sum(p_c, axis=-1, keepdims=True)
    o = _dot(p_c.astype(BF16), vc)
    if kw is not None:
        den = den + jnp.sum(p_w, axis=-1, keepdims=True)
        o = o + _dot(p_w.astype(BF16), vw)
    o = o / den
    half = o.shape[0] // 2
    return jnp.where(first_half, o[:half], o[half:])


def _stack_heads(qp, first_half):
    zero = jnp.zeros_like(qp)
    return jnp.concatenate([jnp.where(first_half, qp, zero), jnp.where(first_half, zero, qp)], axis=0)


def _natten_kernel(q_ref, k_ref, v_ref, kc_ref, vc_ref, bias_ref, o_ref, sc_ref, ow_ref, pc_ref,
                   *, rows_per_step, n_rows):
    blk = pl.program_id(1)
    lane = lax.broadcasted_iota(jnp.int32, (GRID_W, LANES), 1)
    first_half = lane < HEAD_DIM
    band = NA_WIN_H * GRID_W
    n_pairs = NA_HEADS // 2
    stacked = 2 * GRID_W
    cols = [slice(hp * LANES, (hp + 1) * LANES) for hp in range(n_pairs)]
    lane3 = lax.broadcasted_iota(jnp.int32, (rows_per_step, GRID_W, LANES), 2)

    for hp in range(n_pairs):
        q3 = q_ref[0, :, cols[hp]].reshape(rows_per_step, GRID_W, LANES)
        zero = jnp.zeros_like(q3)
        q_all = jnp.concatenate([jnp.where(lane3 < HEAD_DIM, q3, zero), jnp.where(lane3 < HEAD_DIM, zero, q3)], axis=1)
        sc_ref[hp] = _dot_nt(q_all.reshape(rows_per_step * stacked, LANES), kc_ref[0, :, cols[hp]])

    def row_body(j, carry):
        r = blk * rows_per_step + j
        start = jnp.clip(r - NA_WIN_H // 2, 0, n_rows - NA_WIN_H)
        d0 = start - r + (NA_WIN_H - 1)
        koff = pl.multiple_of(start * GRID_W, GRID_W)
        qoff = pl.multiple_of(j * GRID_W, GRID_W)
        soff = pl.multiple_of(j * stacked, stacked)
        q2 = [_stack_heads(q_ref[0, pl.ds(qoff, GRID_W), cs], first_half) for cs in cols]
        s_w = [_dot_nt(q2[hp], k_ref[0, pl.ds(koff, band), cols[hp]])
               + jnp.concatenate([bias_ref[2 * hp, d0], bias_ref[2 * hp + 1, d0]], axis=0) for hp in range(n_pairs)]
        s_c = [sc_ref[hp, pl.ds(soff, stacked), :] for hp in range(n_pairs)]
        m = [jnp.maximum(jnp.max(s_w[hp], axis=-1, keepdims=True), jnp.max(s_c[hp], axis=-1, keepdims=True))
             for hp in range(n_pairs)]
        p_w = [jnp.exp(s_w[hp] - m[hp]) for hp in range(n_pairs)]
        p_c = [jnp.exp(s_c[hp] - m[hp]) for hp in range(n_pairs)]
        rden = [1.0 / (jnp.sum(p_w[hp], axis=-1, keepdims=True) + jnp.sum(p_c[hp], axis=-1, keepdims=True))
                for hp in range(n_pairs)]
        for hp in range(n_pairs):
            ow_ref[hp, pl.ds(soff, stacked), :] = _dot(p_w[hp].astype(BF16),
                                                       v_ref[0, pl.ds(koff, band), cols[hp]]) * rden[hp]
            pc_ref[hp, pl.ds(soff, stacked), :] = (p_c[hp] * rden[hp]).astype(BF16)
        return carry

    lax.fori_loop(0, rows_per_step, row_body, 0, unroll=2)

    for hp in range(n_pairs):
        o = ow_ref[hp] + _dot(pc_ref[hp], vc_ref[0, :, cols[hp]])
        o = o.reshape(rows_per_step, stacked, LANES)
        o = jnp.where(lane3 < HEAD_DIM, o[:, :GRID_W, :], o[:, GRID_W:, :])
        o_ref[0, :, cols[hp]] = o.reshape(rows_per_step * GRID_W, LANES).astype(o_ref.dtype)


def _natten(q, k, v, kc, vc, bias):
    b, n, w = q.shape
    n_rows = n // GRID_W
    rows_per_step = 8
    tq = rows_per_step * GRID_W
    lc = kc.shape[1]
    return pl.pallas_call(
        functools.partial(_natten_kernel, rows_per_step=rows_per_step, n_rows=n_rows),
        out_shape=jax.ShapeDtypeStruct((b, n, w), BF16),
        grid=(b, n_rows // rows_per_step),
        in_specs=[pl.BlockSpec((1, tq, w), lambda i, j: (i, j, 0)),
                  pl.BlockSpec((1, n, w), lambda i, j: (i, 0, 0)),
                  pl.BlockSpec((1, n, w), lambda i, j: (i, 0, 0)),
                  pl.BlockSpec((1, lc, w), lambda i, j: (i, 0, 0)),
                  pl.BlockSpec((1, lc, w), lambda i, j: (i, 0, 0)),
                  pl.BlockSpec(bias.shape, lambda i, j: (0, 0, 0, 0))],
        out_specs=pl.BlockSpec((1, tq, w), lambda i, j: (i, j, 0)),
        scratch_shapes=[pltpu.VMEM((NA_HEADS // 2, 2 * tq, lc), F32),
                        pltpu.VMEM((NA_HEADS // 2, 2 * tq, LANES), F32),
                        pltpu.VMEM((NA_HEADS // 2, 2 * tq, lc), BF16)],
        compiler_params=_params("parallel", "arbitrary"),
        name="neighbourhood_attention",
    )(q, k, v, kc, vc, bias)


def _ctx_attn_kernel(q_ref, k_ref, v_ref, o_ref):
    lc = q_ref.shape[1]
    lane = lax.broadcasted_iota(jnp.int32, (lc, LANES), 1)
    first_half = lane < HEAD_DIM
    outs = []
    for hp in range(NA_HEADS // 2):
        cs = slice(hp * LANES, (hp + 1) * LANES)
        q2 = _stack_heads(q_ref[0, :, cs], first_half)
        outs.append(_pair_attention(q2, None, None, None, k_ref[0, :, cs], v_ref[0, :, cs], first_half))
    o_ref[0] = jnp.concatenate(outs, axis=1).astype(o_ref.dtype)


def _ctx_attention(q, k, v):
    b, lc, w = q.shape
    spec = pl.BlockSpec((1, lc, w), lambda i: (i, 0, 0))
    return pl.pallas_call(
        _ctx_attn_kernel,
        out_shape=jax.ShapeDtypeStruct((b, lc, w), BF16),
        grid=(b,),
        in_specs=[spec, spec, spec],
        out_specs=spec,
        compiler_params=_params("parallel"),
        name="context_attention",
    )(q, k, v)


def _inproj_odd_kernel(pos_ref, *refs, n_chunks):
    x, refs = _load_tokens(refs, n_chunks)
    g_ref, sc_ref, sh_ref, wht_ref, wag_ref = refs[:5]
    outs = refs[5:]
    if n_chunks:
        outs[0][0] = x
        outs = outs[1:]
    hy_ref, ag_ref = outs
    u = _norm_modulate(x, g_ref[...], sc_ref[0], sh_ref[0]).astype(BF16)
    hy_ref[0] = _dot_nt(wht_ref[...], u).astype(BF16)
    a = _dot(u, wag_ref[:, :CF_WIDTH])
    g = _dot(u, wag_ref[:, CF_WIDTH:])
    ag_ref[0] = (a * jax.nn.sigmoid(g)).astype(BF16)


def _inproj_odd(xsrc, g, sc, sh, wht_bf, wag_bf):
    b, n, d = (xsrc.x1 if isinstance(xsrc, _PendingMoe) else xsrc).shape
    tm = min(512, n)
    nt = n // tm
    hw = wht_bf.shape[0]
    n_chunks, pos, x_ops, x_specs = _token_source(xsrc, tm, nt, d)
    tok = lambda i, j, pos: (i, j, 0)
    mod = lambda i, j, pos: (i, 0, 0)
    const = lambda i, j, pos: (0, 0)
    x_out = [jax.ShapeDtypeStruct((b, n, d), F32)] if n_chunks else []
    x_out_spec = [pl.BlockSpec((1, tm, d), tok)] if n_chunks else []
    res = pl.pallas_call(
        functools.partial(_inproj_odd_kernel, n_chunks=n_chunks),
        out_shape=tuple(x_out) + (jax.ShapeDtypeStruct((b, hw, n), BF16),
                                  jax.ShapeDtypeStruct((b, n, CF_WIDTH), BF16)),
        grid_spec=pltpu.PrefetchScalarGridSpec(
            num_scalar_prefetch=1,
            grid=(b, nt),
            in_specs=x_specs + [pl.BlockSpec((1, d), const),
                                pl.BlockSpec((1, 1, d), mod),
                                pl.BlockSpec((1, 1, d), mod),
                                pl.BlockSpec(wht_bf.shape, const),
                                pl.BlockSpec(wag_bf.shape, const)],
            out_specs=tuple(x_out_spec) + (pl.BlockSpec((1, hw, tm), lambda i, j, pos: (i, 0, j)),
                                           pl.BlockSpec((1, tm, CF_WIDTH), tok))),
        compiler_params=_params("parallel", "parallel"),
        name="inproj_odd",
    )(pos, *x_ops, g, sc, sh, wht_bf, wag_bf)
    return tuple(res) if n_chunks else (xsrc,) + tuple(res)


SUBLANES = 8
CF_PAD = 2 * SUBLANES
CF_ROWS = 128


def _conformer_kernel(ag_ref, w_ref, cb_ref, lg_ref, lb_ref, o_ref, pad_ref, *, seq):
    zeros = jnp.zeros((CF_PAD, CF_WIDTH), F32)
    pad_ref[0:CF_PAD, :] = zeros
    pad_ref[CF_PAD + seq:2 * CF_PAD + seq, :] = zeros
    pad_ref[CF_PAD:CF_PAD + seq, :] = ag_ref[0].astype(F32)
    shift0 = CF_PAD - CF_TAPS // 2
    n_groups = (shift0 + CF_TAPS - 1) // SUBLANES + 1

    def conv_rows(i, carry):
        n0 = pl.multiple_of(i * CF_ROWS, CF_ROWS)
        wins = [pad_ref[pl.ds(n0 + SUBLANES * a, CF_ROWS + SUBLANES), :] for a in range(n_groups)]
        acc = None
        for b in range(SUBLANES):
            part = None
            for a in range(n_groups):
                j = SUBLANES * a + b - shift0
                if 0 <= j < CF_TAPS:
                    term = w_ref[j:j + 1, :] * wins[a]
                    part = term if part is None else part + term
            part = part[b:b + CF_ROWS, :]
            acc = part if acc is None else acc + part
        y = acc + cb_ref[...]
        mu = jnp.mean(y, axis=-1, keepdims=True)
        yc = y - mu
        var = jnp.mean(yc * yc, axis=-1, keepdims=True)
        z = yc * lax.rsqrt(var + LN_EPS) * lg_ref[...] + lb_ref[...]
        o_ref[0, pl.ds(n0, CF_ROWS), :] = _silu(z).astype(o_ref.dtype)
        return carry

    lax.fori_loop(0, seq // CF_ROWS, conv_rows, 0)


def _conformer(ag, w, cb, lg, lb):
    b, n, c = ag.shape
    spec = pl.BlockSpec((1, n, c), lambda i: (i, 0, 0))
    vec = pl.BlockSpec((1, c), lambda i: (0, 0))
    return pl.pallas_call(
        functools.partial(_conformer_kernel, seq=n),
        out_shape=jax.ShapeDtypeStruct((b, n, c), BF16),
        grid=(b,),
        in_specs=[spec, pl.BlockSpec((CF_TAPS, c), lambda i: (0, 0)), vec, vec, vec],
        out_specs=spec,
        scratch_shapes=[pltpu.VMEM((n + 2 * CF_PAD, c), F32)],
        compiler_params=_params("parallel"),
        name="conformer_conv",
    )(ag, w, cb[None, :], lg[None, :], lb[None, :])


def _hyena_features(length):
    t = np.linspace(0.0, 1.0, length, dtype=np.float32)
    w = (2.0 * math.pi * np.arange(length, dtype=np.float32) / length).astype(np.float32)
    bands = np.linspace(1e-4, HY_BANDS - 1, HY_BANDS, dtype=np.float32)
    ang = (bands[:, None] * w[None, :]).astype(np.float32)
    zt = np.concatenate([t[None, :], np.cos(ang), -np.sin(ang)], axis=0).astype(np.float32)
    deltas = np.abs(np.linspace(HY_MIN_DECAY, HY_MAX_DECAY, HY_WIDTH, dtype=np.float32))
    rev = (length - np.arange(length)) % length
    zt2 = np.concatenate([zt, zt[:, rev]], axis=1)
    t2 = np.concatenate([t, t[rev]])[None, :]
    return zt2, t2, deltas[:, None]


def _taps_kernel(zt_ref, t_ref, dl_ref, w1t_ref, b1_ref, f0_ref, w2t_ref, b2_ref, f1_ref, w3t_ref,
                 o_ref, hid_ref):
    first = jnp.logical_and(pl.program_id(0) == 0, pl.program_id(1) == 0)
    length = t_ref.shape[1] // 2

    @pl.when(first)
    def _():
        h1 = jnp.sin(f0_ref[...] * (_dot_f32(w1t_ref[...], zt_ref[...]) + b1_ref[...]))
        hid_ref[...] = jnp.sin(f1_ref[...] * (_dot_f32(w2t_ref[...], h1) + b2_ref[...]))

    decay = jnp.exp(-(dl_ref[...] * t_ref[...]))
    fwd = _dot_f32(w3t_ref[0, 0], hid_ref[:, :length])
    bwd = _dot_f32(w3t_ref[0, 1], hid_ref[:, length:])
    taps = jnp.concatenate([fwd, bwd], axis=1) * decay
    nrm = jnp.sum(jnp.abs(taps), axis=-1, keepdims=True)
    lane = lax.broadcasted_iota(jnp.int32, taps.shape, 1)
    o_ref[0] = jnp.where(lane == length, 0.0, taps / nrm)


def _hyena_taps(length, w1, b1, w2, b2, w3, freq):
    zt2, t2, deltas = _hyena_features(length)
    cb = 128
    w3t = w3.T.reshape(2, 2, HY_WIDTH, HY_FFN)
    col = lambda v: v.reshape(HY_FFN, 1)
    const = lambda o, j: (0, 0)
    return pl.pallas_call(
        _taps_kernel,
        out_shape=jax.ShapeDtypeStruct((2, HY_WIDTH, 2 * length), F32),
        grid=(2, HY_WIDTH // cb),
        in_specs=[pl.BlockSpec((HY_EMB, 2 * length), const),
                  pl.BlockSpec((1, 2 * length), const),
                  pl.BlockSpec((cb, 1), lambda o, j: (j, 0)),
                  pl.BlockSpec((HY_FFN, HY_EMB), const),
                  pl.BlockSpec((HY_FFN, 1), const),
                  pl.BlockSpec((HY_FFN, 1), const),
                  pl.BlockSpec((HY_FFN, HY_FFN), const),
                  pl.BlockSpec((HY_FFN, 1), const),
                  pl.BlockSpec((HY_FFN, 1), const),
                  pl.BlockSpec((1, 2, cb, HY_FFN), lambda o, j: (o, 0, j, 0))],
        out_specs=pl.BlockSpec((1, cb, 2 * length), lambda o, j: (o, j, 0)),
        scratch_shapes=[pltpu.VMEM((HY_FFN, 2 * length), F32)],
        compiler_params=_params("arbitrary", "arbitrary"),
        name="hyena_filter_taps",
    )(jnp.asarray(zt2), jnp.asarray(t2), jnp.asarray(deltas), w1.T, col(b1), col(freq[0]),
      w2.T, col(b2), col(freq[1]), w3t)


def _dft_constants():
    na, nb, n = FFT_NA, FFT_NB, FFT_N
    half = na // 2
    ka = np.arange(na)
    ang_a = 2.0 * np.pi * np.outer(ka, ka) / na
    ca, sa = np.cos(ang_a), np.sin(ang_a)
    fa = np.block([[ca[:half], -sa[:half]], [sa[:half], ca[:half]]])
    fai = np.block([[ca[:, :half], sa[:, :half]], [-sa[:, :half], ca[:, :half]]])
    kb = np.arange(nb)
    ang_b = 2.0 * np.pi * np.outer(kb, kb) / nb
    cbm, sbm = np.cos(ang_b), np.sin(ang_b)
    fb = np.block([[cbm, -sbm], [sbm, cbm]])
    fbi = np.block([[cbm, sbm], [-sbm, cbm]])
    ang_t = 2.0 * np.pi * np.outer(kb, ka) / n
    ct, st = np.cos(ang_t), np.sin(ang_t)
    tw_fc = np.concatenate([ct, ct], axis=1)
    tw_fs = np.concatenate([st, -st], axis=1)
    tw_ic, tw_is = ct.T.copy(), st.T.copy()
    bf = lambda a: jnp.asarray(a, dtype=F32).astype(BF16)
    f32 = lambda a: jnp.asarray(a, dtype=F32)
    fa_real = np.concatenate([ca, -sa], axis=1)
    return dict(fa=bf(fa), fa_real=bf(fa_real), fai=bf(fai), fb=bf(fb), fbi=bf(fbi),
                tw_fc=f32(tw_fc), tw_fs=f32(tw_fs), tw_ic=f32(tw_ic), tw_is=f32(tw_is))


def _fft_forward(zr, zi, fa, tw_fc, tw_fs, fb):
    c, _, nb = zr.shape
    tr = jnp.swapaxes(zr, 1, 2)
    lhs = tr if zi is None else jnp.concatenate([tr, jnp.swapaxes(zi, 1, 2)], axis=2)
    a = _dot(lhs.reshape(c * nb, lhs.shape[2]).astype(BF16), fa).reshape(c, nb, 2 * FFT_NA)
    a = a * tw_fc + pltpu.roll(a, FFT_NA, axis=2) * tw_fs
    t = jnp.swapaxes(a, 1, 2)
    lhs2 = jnp.concatenate([t[:, :FFT_NA, :], t[:, FFT_NA:, :]], axis=2)
    x = _dot(lhs2.reshape(c * FFT_NA, 2 * nb).astype(BF16), fb)
    return x.reshape(c, FFT_NA, 2 * nb)


def _fft_inverse(y, fbi, tw_ic, tw_is, fai):
    c = y.shape[0]
    nb = FFT_NB
    b = _dot(y.reshape(c * FFT_NA, 2 * nb).astype(BF16), fbi).reshape(c, FFT_NA, 2 * nb)
    br, bi = b[:, :, :nb], b[:, :, nb:]
    rr = br * tw_ic - bi * tw_is
    ii = bi * tw_ic + br * tw_is
    t = jnp.swapaxes(jnp.concatenate([rr, ii], axis=1), 1, 2)
    o = _dot(t.reshape(c * nb, 2 * FFT_NA).astype(BF16), fai).reshape(c, nb, FFT_NA)
    o = jnp.swapaxes(o, 1, 2)
    return o[:, :FFT_NA // 2, :], o[:, FFT_NA // 2:, :]


def _filter_fft_kernel(taps_ref, fa_ref, twc_ref, tws_ref, fb_ref, o_ref):
    h = _fft_forward(taps_ref[0], None, fa_ref[...], twc_ref[...], tws_ref[...], fb_ref[...])
    o_ref[0] = h * (1.0 / FFT_N)


def _filter_fft(taps, consts):
    _, c, n = taps.shape
    cb = 32
    taps4 = taps.reshape(2, c, FFT_NA, FFT_NB)
    cm = lambda o, j: (0, 0)
    return pl.pallas_call(
        _filter_fft_kernel,
        out_shape=jax.ShapeDtypeStruct((2, c, FFT_NA, 2 * FFT_NB), F32),
        grid=(2, c // cb),
        in_specs=[pl.BlockSpec((1, cb, FFT_NA, FFT_NB), lambda o, j: (o, j, 0, 0)),
                  pl.BlockSpec(consts["fa_real"].shape, cm),
                  pl.BlockSpec(consts["tw_fc"].shape, cm),
                  pl.BlockSpec(consts["tw_fs"].shape, cm),
                  pl.BlockSpec(consts["fb"].shape, cm)],
        out_specs=pl.BlockSpec((1, cb, FFT_NA, 2 * FFT_NB), lambda o, j: (o, j, 0, 0)),
        compiler_params=_params("parallel", "parallel"),
        name="hyena_filter_fft",
    )(taps4, consts["fa_real"], consts["tw_fc"], consts["tw_fs"], consts["fb"])


def _shift_tokens(a, direction):
    rows = a.shape[-2]
    lane = lax.broadcasted_iota(jnp.int32, a.shape, a.ndim - 1)
    row = lax.broadcasted_iota(jnp.int32, a.shape, a.ndim - 2)
    if direction == 1:
        l = pltpu.roll(a, 1, axis=a.ndim - 1)
        ls = pltpu.roll(l, 1, axis=a.ndim - 2)
        out = jnp.where(lane == 0, ls, l)
        edge = jnp.logical_and(lane == 0, row == 0)
    else:
        l = pltpu.roll(a, LANES - 1, axis=a.ndim - 1)
        ls = pltpu.roll(l, rows - 1, axis=a.ndim - 2)
        out = jnp.where(lane == LANES - 1, ls, l)
        edge = jnp.logical_and(lane == LANES - 1, row == rows - 1)
    return jnp.where(edge, 0.0, out)


def _short_conv3(a, w_ref):
    return w_ref[0] * _shift_tokens(a, 1) + w_ref[1] * a + w_ref[2] * _shift_tokens(a, -1)


def _hyena_kernel(v_ref, x1_ref, x2_ref, wv_ref, w1_ref, w2_ref, h_ref, hb_ref,
                  fa_ref, twfc_ref, twfs_ref, fb_ref, fbi_ref, twic_ref, twis_ref, fai_ref, o_ref):
    fwd_c = (fa_ref[...], twfc_ref[...], twfs_ref[...], fb_ref[...])
    inv_c = (fbi_ref[...], twic_ref[...], twis_ref[...], fai_ref[...])
    z = _short_conv3(v_ref[...].astype(F32), wv_ref)
    zr, zi = z[0], z[1]
    nb = FFT_NB
    for o, (g_ref, gw_ref) in enumerate(((x1_ref, w1_ref), (x2_ref, w2_ref))):
        x = _fft_forward(zr, zi, *fwd_c)
        h = h_ref[o]
        xr, xi, hr, hi = x[:, :, :nb], x[:, :, nb:], h[:, :, :nb], h[:, :, nb:]
        y = jnp.concatenate([xr * hr - xi * hi, xr * hi + xi * hr], axis=2)
        yr, yi = _fft_inverse(y, *inv_c)
        gate = _short_conv3(g_ref[...].astype(F32), gw_ref)
        bias = hb_ref[o]
        zr = gate[0] * (yr + zr * bias)
        zi = gate[1] * (yi + zi * bias)
    o_ref[0] = zr.astype(o_ref.dtype)
    o_ref[1] = zi.astype(o_ref.dtype)


def _hyena_latent(hyt, short_w, h_spec, hy_bias, consts):
    b, c3, length = hyt.shape
    c = c3 // 3
    rows = length // FFT_NB
    cb = 32
    nblk = c // cb
    hy4 = hyt.reshape(b, c3, rows, FFT_NB)
    w4 = jnp.broadcast_to(short_w.reshape(3, c3, 1, 1), (3, c3, 1, FFT_NB))
    hb4 = jnp.broadcast_to(hy_bias.reshape(2, c, 1, 1), (2, c, 1, FFT_NB))
    sig = lambda g: pl.BlockSpec((2, cb, rows, FFT_NB), lambda j, p: (p, g * nblk + j, 0, 0))
    wsp = lambda g: pl.BlockSpec((3, cb, 1, FFT_NB), lambda j, p: (0, g * nblk + j, 0, 0))
    cm = lambda j, p: (0, 0)
    names = ("fa", "tw_fc", "tw_fs", "fb", "fbi", "tw_ic", "tw_is", "fai")
    out = pl.pallas_call(
        _hyena_kernel,
        out_shape=jax.ShapeDtypeStruct((b, c, rows, FFT_NB), BF16),
        grid=(nblk, b // 2),
        in_specs=[sig(0), sig(1), sig(2), wsp(0), wsp(1), wsp(2),
                  pl.BlockSpec((2, cb, FFT_NA, 2 * FFT_NB), lambda j, p: (0, j, 0, 0)),
                  pl.BlockSpec((2, cb, 1, FFT_NB), lambda j, p: (0, j, 0, 0))]
                 + [pl.BlockSpec(consts[k].shape, cm) for k in names],
        out_specs=pl.BlockSpec((2, cb, rows, FFT_NB), lambda j, p: (p, j, 0, 0)),
        compiler_params=_params("parallel", "arbitrary"),
        name="hyena_long_conv",
    )(hy4, hy4, hy4, w4, w4, w4, h_spec, hb4, *[consts[k] for k in names])
    return out.reshape(b, c, length)


def _dense_dft_constants(length):
    n = 2 * length
    k = np.arange(n)
    ang = 2.0 * np.pi * np.outer(k, k) / n
    fwd = np.concatenate([np.cos(ang), -np.sin(ang)], axis=1)
    inv = np.concatenate([np.cos(ang[:length]).T, -np.sin(ang[:length]).T], axis=0) / n
    bf = lambda a: jnp.asarray(a, dtype=F32).astype(BF16)
    return bf(fwd), bf(inv)


def _shift_lanes(a, direction):
    n = a.shape[-1]
    lane = lax.broadcasted_iota(jnp.int32, a.shape, a.ndim - 1)
    if direction == 1:
        return jnp.where(lane == 0, 0.0, pltpu.roll(a, 1, axis=a.ndim - 1))
    return jnp.where(lane == n - 1, 0.0, pltpu.roll(a, n - 1, axis=a.ndim - 1))


def _hyena_ctx_kernel(v_ref, x1_ref, x2_ref, wv_ref, w1_ref, w2_ref, taps_ref, hb_ref, fwd_ref, inv_ref, o_ref):
    bsz, cb, length = v_ref.shape
    n = 2 * length
    fwd_m, inv_m = fwd_ref[0:length, :], inv_ref[...]

    def conv3(ref, w_ref):
        a = ref[...].astype(F32)
        return w_ref[0] * _shift_lanes(a, 1) + w_ref[1] * a + w_ref[2] * _shift_lanes(a, -1)

    z = conv3(v_ref, wv_ref)
    for o, (g_ref, gw_ref) in enumerate(((x1_ref, w1_ref), (x2_ref, w2_ref))):
        h = _dot(taps_ref[o].astype(BF16), fwd_ref[...])
        hr, hi = h[:, :n], h[:, n:]
        x = _dot(z.reshape(bsz * cb, length).astype(BF16), fwd_m).reshape(bsz, cb, 2 * n)
        xr, xi = x[:, :, :n], x[:, :, n:]
        y = jnp.concatenate([xr * hr - xi * hi, xr * hi + xi * hr], axis=2)
        yt = _dot(y.reshape(bsz * cb, 2 * n).astype(BF16), inv_m).reshape(bsz, cb, length)
        z = conv3(g_ref, gw_ref) * (yt + z * hb_ref[o])
    o_ref[...] = z.astype(o_ref.dtype)


def _hyena_context(hyt, short_w, taps, hy_bias):
    b, c3, length = hyt.shape
    c = c3 // 3
    cb = 128
    nblk = c // cb
    fwd_m, inv_m = _dense_dft_constants(length)
    w4 = jnp.broadcast_to(short_w.reshape(3, c3, 1), (3, c3, length))
    hb = jnp.broadcast_to(hy_bias.reshape(2, c, 1), (2, c, length))
    sig = lambda g: pl.BlockSpec((b, cb, length), lambda j: (0, g * nblk + j, 0))
    wsp = lambda g: pl.BlockSpec((3, cb, length), lambda j: (0, g * nblk + j, 0))
    return pl.pallas_call(
        _hyena_ctx_kernel,
        out_shape=jax.ShapeDtypeStruct((b, c, length), BF16),
        grid=(nblk,),
        in_specs=[sig(0), sig(1), sig(2), wsp(0), wsp(1), wsp(2),
                  pl.BlockSpec((2, cb, 2 * length), lambda j: (0, j, 0)),
                  pl.BlockSpec((2, cb, length), lambda j: (0, j, 0)),
                  pl.BlockSpec(fwd_m.shape, lambda j: (0, 0)),
                  pl.BlockSpec(inv_m.shape, lambda j: (0, 0))],
        out_specs=pl.BlockSpec((b, cb, length), lambda j: (0, j, 0)),
        compiler_params=_params("parallel"),
        name="hyena_context_conv",
    )(hyt, hyt, hyt, w4, w4, w4, taps, hb, fwd_m, inv_m)


def _route(gl, el):
    row = lax.broadcasted_iota(jnp.int32, gl.shape, 0).astype(F32)
    grp = jnp.floor(row * (1.0 / EXPERTS_PER_GROUP))
    big = float(N_EXPERTS)
    gmax = jnp.max(gl, axis=0, keepdims=True)
    gidx = jnp.min(jnp.where(gl == gmax, grp, big), axis=0, keepdims=True)
    gsum = jnp.sum(jnp.exp(gl - gmax), axis=0, keepdims=True) * (1.0 / EXPERTS_PER_GROUP)
    g_w = 1.0 / gsum
    em = jnp.where(grp == gidx, el, NEG_INF)
    t1 = jnp.max(em, axis=0, keepdims=True)
    i1 = jnp.min(jnp.where(em == t1, row, big), axis=0, keepdims=True)
    em2 = jnp.where(row == i1, 2.0 * NEG_INF, em)
    t2 = jnp.max(em2, axis=0, keepdims=True)
    i2 = jnp.min(jnp.where(em2 == t2, row, big), axis=0, keepdims=True)
    e2 = jnp.exp(t2 - t1)
    den = 1.0 + e2
    w1 = g_w / den
    w2 = g_w * e2 / den
    return jnp.where(row == i1, w1, 0.0) + jnp.where(row == i2, w2, 0.0), gidx


SLOT_LANE = N_EXPERTS
CHUNKS_LANE = N_EXPERTS + 1
MOE_CHUNK = 64


def _tile_chunks(tm):
    return tm // MOE_CHUNK + N_GROUPS


def _dispatch_slots(gidx, tri):
    tm = gidx.shape[1]
    sub = lax.broadcasted_iota(jnp.int32, (SUBLANES, tm), 0)
    grp = sub.astype(F32)
    member = jnp.where(grp == gidx, 1.0, 0.0)
    rank = _dot(member.astype(BF16), tri)
    count = jnp.sum(member, axis=1, keepdims=True)
    chunks = jnp.floor((count + float(MOE_CHUNK - 1)) * (1.0 / MOE_CHUNK))
    first = jnp.zeros_like(chunks)
    for g in range(1, N_GROUPS):
        first = first + jnp.where(sub[:, 0:1] >= g, chunks[g - 1:g, :], 0.0)
    slot = jnp.sum(member * (first * float(MOE_CHUNK) + rank - 1.0), axis=0, keepdims=True)
    out = jnp.where(sub == 0, slot, 0.0)
    for g in range(N_GROUPS):
        out = out + jnp.where(sub == g + 1, chunks[g:g + 1, :], 0.0)
    return out


def _gated_conv_tile(gb_ref, p_ref, cw_ref, tm):
    j = pl.program_id(1)
    n = p_ref.shape[1]
    halo = 2 * SUBLANES
    t0 = pl.multiple_of(j * tm, tm)
    p = p_ref[0, pl.ds(t0, tm), :].astype(F32)
    lo = pl.multiple_of(jnp.maximum(t0 - halo, 0), halo)
    hi = pl.multiple_of(jnp.minimum(t0 + tm, n - halo), halo)
    before = p_ref[0, pl.ds(lo, halo), :].astype(F32)[halo - 1:halo, :]
    after = p_ref[0, pl.ds(hi, halo), :].astype(F32)[0:1, :]
    before = jnp.where(t0 == 0, 0.0, before)
    after = jnp.where(t0 + tm >= n, 0.0, after)
    row = lax.broadcasted_iota(jnp.int32, p.shape, 0)
    prev = jnp.where(row == 0, before, pltpu.roll(p, 1, axis=0))
    nxt = jnp.where(row == tm - 1, after, pltpu.roll(p, tm - 1, axis=0))
    y = cw_ref[0:1, :] * prev + cw_ref[1:2, :] * p + cw_ref[2:3, :] * nxt
    return (gb_ref[0].astype(F32) * y).astype(BF16)


def _outproj_kernel(*refs, channel_major, gated_conv):
    y1_ref = refs[0]
    n_y2 = 3 if gated_conv else 1
    (w_ref, x_ref, g1_ref, n2_ref, sc2_ref, sh2_ref, wr_ref, br_ref, tri_ref,
     xo_ref, cmb_ref, hs_ref, cs_ref) = refs[1 + n_y2:]
    y2 = _gated_conv_tile(*refs[1:4], x_ref.shape[1]) if gated_conv else refs[1][0]
    half = w_ref.shape[0] // 2
    dots = [_dot_tn if cm else _dot for cm in channel_major]
    acc = dots[0](y1_ref[0], w_ref[:half, :]) + dots[1](y2, w_ref[half:, :])
    xn = x_ref[0] + g1_ref[0] * acc
    xo_ref[0] = xn
    h = _norm_modulate(xn, n2_ref[...], sc2_ref[0], sh2_ref[0])
    hi = h.astype(BF16)
    lo = (h - hi.astype(F32)).astype(BF16)
    p = _dot(hi, wr_ref[...])
    logits = p[:, :LANES] + p[:, LANES:] + _dot(lo, wr_ref[:, :LANES]) + br_ref[...]
    lt = logits.T
    cmb, gidx = _route(lt[0:N_EXPERTS], lt[N_EXPERTS:2 * N_EXPERTS])
    disp = _dispatch_slots(gidx, tri_ref[...])
    pad = jnp.zeros((LANES - N_EXPERTS - SUBLANES, cmb.shape[1]), F32)
    rows = jnp.concatenate([cmb, disp, pad], axis=0).T
    cmb_ref[0] = rows
    n_chunks = hs_ref.shape[0]
    srow = lax.broadcasted_iota(jnp.int32, (n_chunks * MOE_CHUNK, rows.shape[0]), 0).astype(F32)
    gather = jnp.where(srow == disp[0:1, :], 1.0, 0.0).astype(BF16)
    hs = _dot(gather, hi).astype(BF16)
    r_hi = rows.astype(BF16)
    r_lo = (rows - r_hi.astype(F32)).astype(BF16)
    cs2 = _dot(gather, jnp.concatenate([r_hi, r_lo], axis=1))
    cs = cs2[:, :LANES] + cs2[:, LANES:]
    for k in range(n_chunks):
        hs_ref[k] = hs[k * MOE_CHUNK:(k + 1) * MOE_CHUNK]
        cs_ref[k] = cs[k * MOE_CHUNK:(k + 1) * MOE_CHUNK]


def _outproj(y1, y2, w_bf, x, g1, n2g, sc2, sh2, wr2, br, channel_major):
    b, n, d = x.shape
    tm = min(512, n)
    nt = n // tm
    ntc = _tile_chunks(tm)
    half = d // 2
    tok = lambda i, j: (i, j, 0)
    mod = lambda i, j: (i, 0, 0)
    const = lambda i, j: (0, 0)
    srt = lambda i, j: (i * nt + j, 0, 0)
    y_spec = lambda cm: (pl.BlockSpec((1, half, tm), lambda i, j: (i, 0, j)) if cm
                         else pl.BlockSpec((1, tm, half), tok))
    gated_conv = isinstance(y2, tuple)
    if gated_conv:
        y2_ops = list(y2)
        y2_specs = [pl.BlockSpec((1, tm, half), tok), pl.BlockSpec((1, n, half), mod),
                    pl.BlockSpec(y2[2].shape, const)]
    else:
        y2_ops, y2_specs = [y2], [y_spec(channel_major[1])]
    return pl.pallas_call(
        functools.partial(_outproj_kernel, channel_major=tuple(channel_major), gated_conv=gated_conv),
        out_shape=(jax.ShapeDtypeStruct((b, n, d), F32), jax.ShapeDtypeStruct((b, n, LANES), F32),
                   jax.ShapeDtypeStruct((b * nt * ntc, MOE_CHUNK, d), BF16),
                   jax.ShapeDtypeStruct((b * nt * ntc, MOE_CHUNK, LANES), F32)),
        grid=(b, nt),
        in_specs=[y_spec(channel_major[0])] + y2_specs + [
                  pl.BlockSpec((d, d), const),
                  pl.BlockSpec((1, tm, d), tok),
                  pl.BlockSpec((1, 1, d), mod),
                  pl.BlockSpec((1, d), const),
                  pl.BlockSpec((1, 1, d), mod),
                  pl.BlockSpec((1, 1, d), mod),
                  pl.BlockSpec(wr2.shape, const),
                  pl.BlockSpec(br.shape, const),
                  pl.BlockSpec((tm, tm), const)],
        out_specs=(pl.BlockSpec((1, tm, d), tok), pl.BlockSpec((1, tm, LANES), tok),
                   pl.BlockSpec((ntc, MOE_CHUNK, d), srt), pl.BlockSpec((ntc, MOE_CHUNK, LANES), srt)),
        compiler_params=_params("parallel", "parallel"),
        name="outproj_router",
    )(y1, *y2_ops, w_bf, x, g1, n2g, sc2, sh2, wr2, br,
      jnp.asarray(np.triu(np.ones((tm, tm), np.float32)), dtype=BF16))


def _swiglu_group(h, w_rows, j, wg_ref, wu_ref, wd_ref):
    lane = lax.broadcasted_iota(jnp.int32, w_rows.shape, 1)
    experts = range(EXPERTS_PER_GROUP)
    w_e = [jnp.sum(jnp.where(lane == j * EXPERTS_PER_GROUP + e, w_rows, 0.0), axis=1, keepdims=True) for e in experts]
    a = [_dot(h, wg_ref[e]) for e in experts]
    u = [_dot(h, wu_ref[e]) for e in experts]
    act = [(_silu(a[e]) * u[e] * w_e[e]).astype(BF16) for e in experts]
    out = _dot(act[0], wd_ref[0])
    for e in experts[1:]:
        out = out + _dot(act[e], wd_ref[e])
    return out


MOE_STEP_CHUNKS = 8


def _moe_sorted_kernel(group_ref, used_ref, fresh_ref, src_ref, *refs):
    n = MOE_STEP_CHUNKS
    hs_refs, cs_refs = refs[:n], refs[n:2 * n]
    wg_ref, wu_ref, wd_ref, ys_ref, wg_bf, wu_bf, wd_bf = refs[2 * n:]
    s = pl.program_id(0)

    @pl.when(fresh_ref[s] > 0)
    def _():
        wg_bf[...] = wg_ref[...].astype(BF16)
        wu_bf[...] = wu_ref[...].astype(BF16)
        wd_bf[...] = wd_ref[...].astype(BF16)

    @pl.when(used_ref[s] > 0)
    def _():
        h = jnp.concatenate([r[0] for r in hs_refs], axis=0)
        w_rows = jnp.concatenate([r[0] for r in cs_refs], axis=0)
        y = _swiglu_group(h, w_rows, group_ref[s], wg_bf, wu_bf, wd_bf).astype(BF16)
        for k in range(n):
            ys_ref[k] = y[k * MOE_CHUNK:(k + 1) * MOE_CHUNK]

    @pl.when(used_ref[s] == 0)
    def _():
        ys_ref[...] = jnp.zeros(ys_ref.shape, ys_ref.dtype)


def _moe_sorted(hs, cs, step_group, step_used, step_fresh, chunk_src, wg, wu, wd, layer):
    _, _, d = hs.shape
    n = MOE_STEP_CHUNKS
    steps = step_group.shape[0]
    epg = EXPERTS_PER_GROUP
    chunk = lambda k, width: pl.BlockSpec((1, MOE_CHUNK, width),
                                          lambda s, grp, used, fresh, src: (src[s * n + k], 0, 0))
    wmap = lambda s, grp, used, fresh, src: (layer * N_GROUPS + grp[s], 0, 0)
    return pl.pallas_call(
        _moe_sorted_kernel,
        out_shape=jax.ShapeDtypeStruct((steps * n, MOE_CHUNK, d), BF16),
        grid_spec=pltpu.PrefetchScalarGridSpec(
            num_scalar_prefetch=4,
            grid=(steps,),
            in_specs=[chunk(k, d) for k in range(n)] + [chunk(k, LANES) for k in range(n)]
                     + [pl.BlockSpec((epg, d, D_EXPERT), wmap),
                        pl.BlockSpec((epg, d, D_EXPERT), wmap),
                        pl.BlockSpec((epg, D_EXPERT, d), wmap)],
            out_specs=pl.BlockSpec((n, MOE_CHUNK, d), lambda s, grp, used, fresh, src: (s, 0, 0)),
            scratch_shapes=[pltpu.VMEM((epg, d, D_EXPERT), BF16), pltpu.VMEM((epg, d, D_EXPERT), BF16),
                            pltpu.VMEM((epg, D_EXPERT, d), BF16)]),
        compiler_params=_params("arbitrary"),
        name="moe_sorted_experts",
    )(step_group, step_used, step_fresh, chunk_src, *([hs] * n), *([cs] * n), wg, wu, wd)


def _moe_unpermute(ys_refs, cmb_ref, x1_ref, g2_ref):
    cmb = cmb_ref[0]
    lane = lax.broadcasted_iota(jnp.int32, cmb.shape, 1)
    slot = jnp.sum(jnp.where(lane == SLOT_LANE, cmb, 0.0), axis=1, keepdims=True)
    rows = lax.broadcasted_iota(jnp.int32, (cmb.shape[0], len(ys_refs) * MOE_CHUNK), 1).astype(F32)
    scatter = jnp.where(slot == rows, 1.0, 0.0).astype(BF16)
    ys = jnp.concatenate([r[0] for r in ys_refs], axis=0)
    return x1_ref[0] + g2_ref[0] * _dot(scatter, ys)


def _moe_combine_kernel(pos_ref, *refs):
    x, (o_ref,) = _load_tokens(refs, len(refs) - 4)
    o_ref[0] = x


def _moe_combine(pending):
    b, n, d = pending.x1.shape
    tm = min(512, n)
    nt = n // tm
    n_chunks, pos, x_ops, x_specs = _token_source(pending, tm, nt, d)
    return pl.pallas_call(
        _moe_combine_kernel,
        out_shape=jax.ShapeDtypeStruct((b, n, d), F32),
        grid_spec=pltpu.PrefetchScalarGridSpec(
            num_scalar_prefetch=1,
            grid=(b, nt),
            in_specs=x_specs,
            out_specs=pl.BlockSpec((1, tm, d), lambda i, t, pos: (i, t, 0))),
        compiler_params=_params("parallel", "parallel"),
        name="moe_combine",
    )(pos, *x_ops)


def _moe_chunk_schedule(cmb):
    b, n, _ = cmb.shape
    tm = min(512, n)
    nt = b * (n // tm)
    ntc, nsc = _tile_chunks(tm), MOE_STEP_CHUNKS
    steps = nt * ntc // nsc + N_GROUPS
    i32 = jnp.int32
    cnt = cmb[:, ::tm, CHUNKS_LANE:CHUNKS_LANE + N_GROUPS].reshape(nt, N_GROUPS).astype(i32)
    gi = jnp.arange(N_GROUPS, dtype=i32)
    ti = jnp.arange(nt, dtype=i32)
    earlier_g = (gi[:, None] < gi[None, :]).astype(i32)
    in_tile = jnp.sum(cnt[:, :, None] * earlier_g[None], axis=1)
    before = jnp.sum(cnt[:, None, :] * (ti[:, None] < ti[None, :]).astype(i32)[:, :, None], axis=0)
    total = jnp.sum(cnt, axis=0)
    padded = (total + nsc - 1) // nsc * nsc
    gstart = jnp.sum(padded[:, None] * earlier_g, axis=0)
    gend = gstart + padded
    c = jnp.arange(tm // MOE_CHUNK, dtype=i32)
    dst = gstart[None, :, None] + before[:, :, None] + c[None, None, :]
    src = (ti * ntc)[:, None, None] + in_tile[:, :, None] + c[None, None, :]
    dst = jnp.where(c[None, None, :] < cnt[:, :, None], dst, -1).reshape(-1)
    p = jnp.arange(steps * nsc, dtype=i32)
    chunk_src = jnp.sum(jnp.where(dst[None, :] == p[:, None], src.reshape(-1)[None, :], 0), axis=1)
    first_chunk = jnp.arange(steps, dtype=i32) * nsc
    step_group = jnp.sum((first_chunk[:, None] >= gend[None, :-1]).astype(i32), axis=1)
    prev_group = jnp.sum((first_chunk[:, None] - nsc >= gend[None, :-1]).astype(i32), axis=1)
    step_used = (first_chunk < gend[-1]).astype(i32)
    step_fresh = jnp.logical_or(first_chunk == 0, step_group != prev_group).astype(i32)
    k = jnp.arange(ntc, dtype=i32)
    ends = in_tile + cnt
    grp_k = jnp.minimum(jnp.sum((k[None, :, None] >= ends[:, None, :]).astype(i32), axis=2), N_GROUPS - 1)
    base = gstart[None, :] + before - in_tile
    pos = k[None, :] + jnp.sum(jnp.where(grp_k[:, :, None] == gi[None, None, :], base[:, None, :], 0), axis=2)
    chunk_pos = jnp.where(k[None, :] < ends[:, -1:], pos, 0).reshape(-1).astype(i32)
    return step_group, step_used, step_fresh, chunk_src, chunk_pos


def _moe(hs, cs, cmb, x, g2, wg, wu, wd, layer):
    step_group, step_used, step_fresh, chunk_src, chunk_pos = _moe_chunk_schedule(cmb)
    ys = _moe_sorted(hs, cs, step_group, step_used, step_fresh, chunk_src, wg, wu, wd, layer)
    return _PendingMoe(ys, chunk_pos, cmb, x, g2)


def kernel(x, c, ctx, c_ctx, ada_w, ada_b, norm1_g, norm2_g, w_in_even, qn_g, kn_g, na_rpb, sc_conv_w, w_in_odd, hy_short_w, hy_w1, hy_b1, hy_w2, hy_b2, hy_w3, hy_freq, hy_bias, cf_conv_w, cf_conv_b, cf_ln_g, cf_ln_b, w_out, moe_w_group, moe_b_group, moe_w_router, moe_b_router, moe_w_gate, moe_w_up, moe_w_down):
    depth = ada_w.shape[0]
    bsz, seq, d = x.shape
    lc = ctx.shape[1]
    assert 2 * seq == FFT_N and d == D_MODEL and bsz % 2 == 0

    mods = _ada_modulation(jnp.concatenate([c, c_ctx[None, :]], axis=0), ada_w, ada_b)
    seg = jnp.asarray(np.kron(np.eye(NA_HEADS), np.ones((HEAD_DIM, HEAD_DIM))), dtype=BF16)
    consts = _dft_constants()
    wg = moe_w_gate.reshape((-1,) + moe_w_gate.shape[2:])
    wu = moe_w_up.reshape((-1,) + moe_w_up.shape[2:])
    wd = moe_w_down.reshape((-1,) + moe_w_down.shape[2:])

    for l in range(depth):
        ctx_needed = any(j % 2 == 0 for j in range(l + 1, depth))
        lat_mod = [m[:, None, :] for m in jnp.split(mods[l, :bsz], 6, axis=-1)]
        ctx_mod = [jnp.broadcast_to(m[None, :, :], (bsz, 1, d)) for m in jnp.split(mods[l, bsz:bsz + 1], 6, axis=-1)]
        sh1, sc1, g1, sh2, sc2, g2 = lat_mod
        csh1, csc1, cg1, csh2, csc2, cg2 = ctx_mod
        n1g = norm1_g[l][None, :]
        n2g = norm2_g[l][None, :]
        w_out_bf = w_out[l].astype(BF16)
        wr = jnp.concatenate([jnp.repeat(moe_w_group[l], EXPERTS_PER_GROUP, axis=1), moe_w_router[l],
                              jnp.zeros((d, LANES - 2 * N_EXPERTS), F32)], axis=1)
        br = jnp.concatenate([jnp.repeat(moe_b_group[l], EXPERTS_PER_GROUP), moe_b_router[l],
                              jnp.zeros((LANES - 2 * N_EXPERTS,), F32)])[None, :]
        wrh = wr.astype(BF16)
        wr2 = jnp.concatenate([wrh, (wr - wrh.astype(F32)).astype(BF16)], axis=1)

        if l % 2 == 0:
            e = l // 2
            w_in = w_in_even[e].astype(BF16)
            qg = jnp.tile(qn_g[e], NA_HEADS)[None, :]
            kg = jnp.tile(kn_g[e], NA_HEADS)[None, :]
            x, ql, kl, vl, gbl, pl_ = _inproj_even(x, n1g, sc1, sh1, w_in, qg, kg, seg)
            ctx, qc, kc, vc, gbc, pc = _inproj_even(ctx, n1g, csc1, csh1, w_in, qg, kg, seg)
            bias = _bias_table(na_rpb[e])
            y1 = _natten(ql, kl, vl, kc, vc, bias)
            y2 = (gbl, pl_, sc_conv_w[e])
            lat_cm = (False, False)
            if ctx_needed:
                y1c = _ctx_attention(qc, kc, vc)
                y2c = (gbc, pc, sc_conv_w[e])
        else:
            o = l // 2
            wht = w_in_odd[o][:, :3 * HY_WIDTH].T.astype(BF16)
            wag = w_in_odd[o][:, 3 * HY_WIDTH:].astype(BF16)
            x, hyt, ag = _inproj_odd(x, n1g, sc1, sh1, wht, wag)
            taps = _hyena_taps(seq, hy_w1[o], hy_b1[o], hy_w2[o], hy_b2[o], hy_w3[o], hy_freq[o])
            h_spec = _filter_fft(taps, consts)
            y1 = _hyena_latent(hyt, hy_short_w[o], h_spec, hy_bias[o], consts)
            cf_args = (cf_conv_w[o], cf_conv_b[o], cf_ln_g[o], cf_ln_b[o])
            y2 = _conformer(ag, *cf_args)
            lat_cm = (True, False)
            if ctx_needed:
                ctx, hytc, agc = _inproj_odd(ctx, n1g, csc1, csh1, wht, wag)
                taps_c = _hyena_taps(lc, hy_w1[o], hy_b1[o], hy_w2[o], hy_b2[o], hy_w3[o], hy_freq[o])
                y1c = _hyena_context(hytc, hy_short_w[o], taps_c, hy_bias[o])
                y2c = _conformer(agc, *cf_args)

        x1, cmb, hs, cs = _outproj(y1, y2, w_out_bf, x, g1, n2g, sc2, sh2, wr2, br, lat_cm)
        x = _moe(hs, cs, cmb, x1, g2, wg, wu, wd, l)
        if ctx_needed:
            c1, cmbc, hsc, csc = _outproj(y1c, y2c, w_out_bf, ctx, cg1, n2g, csc2, csh2, wr2, br, lat_cm)
            ctx = _moe(hsc, csc, cmbc, c1, cg2, wg, wu, wd, l)
    return _moe_combine(x)
```

```python
import functools
import math
from typing import NamedTuple

import numpy as np
import jax
import jax.numpy as jnp
from jax import lax
from jax.experimental import pallas as pl
from jax.experimental.pallas import tpu as pltpu

F32 = jnp.float32
BF16 = jnp.bfloat16

D_MODEL = 1024
GRID_W = 64
NA_HEADS = 8
HEAD_DIM = 64
NA_WIDTH = 512
NA_WIN_H = 8
NA_WIN_W = 16
SC_WIDTH = 512
HY_WIDTH = 512
HY_BANDS = 16
HY_EMB = 1 + 2 * HY_BANDS
HY_FFN = 64
HY_MAX_DECAY = math.log(1e-2) / 0.3
HY_MIN_DECAY = math.log(1e-2) / 1.5
CF_WIDTH = 512
CF_TAPS = 31
N_GROUPS = 4
EXPERTS_PER_GROUP = 4
N_EXPERTS = 16
D_EXPERT = 256
RMS_EPS = 1e-6
LN_EPS = 1e-5
NEG_INF = -1e30

VMEM_LIMIT_BYTES = 56 * 1024 * 1024
LANES = 128

FFT_NA = 64
FFT_NB = 128
FFT_N = FFT_NA * FFT_NB


def _params(*sem):
    return pltpu.CompilerParams(dimension_semantics=tuple(sem), vmem_limit_bytes=VMEM_LIMIT_BYTES)


def _dot(a, b):
    return jnp.dot(a, b, preferred_element_type=F32)


def _dot_nt(a, b):
    return lax.dot_general(a, b, (((1,), (1,)), ((), ())), preferred_element_type=F32)


def _dot_tn(a, b):
    return lax.dot_general(a, b, (((0,), (0,)), ((), ())), preferred_element_type=F32)


def _dot_f32(a, b):
    return jnp.dot(a, b, preferred_element_type=F32, precision=lax.Precision.HIGHEST)


def _silu(x):
    return x * jax.nn.sigmoid(x)


def _ada_kernel(ct_ref, w_ref, b_ref, o_ref, *, n_cond):
    ct = ct_ref[...]
    s = _silu(ct)
    w = w_ref[0]
    rows = [jnp.sum(w * s[:, r:r + 1], axis=0, keepdims=True) for r in range(n_cond)]
    rows.append(jnp.zeros((8 - n_cond, w.shape[1]), F32))
    o_ref[0] = jnp.concatenate(rows, axis=0) + b_ref[0]


def _ada_modulation(cond, ada_w, ada_b):
    n_cond, d = cond.shape
    depth, _, n6 = ada_w.shape
    tn = 1536
    ct = jnp.zeros((d, 8), F32).at[:, :n_cond].set(cond.T)
    return pl.pallas_call(
        functools.partial(_ada_kernel, n_cond=n_cond),
        out_shape=jax.ShapeDtypeStruct((depth, 8, n6), F32),
        grid=(depth, n6 // tn),
        in_specs=[pl.BlockSpec((d, 8), lambda l, j: (0, 0)),
                  pl.BlockSpec((1, d, tn), lambda l, j: (l, 0, j)),
                  pl.BlockSpec((1, 1, tn), lambda l, j: (l, 0, j))],
        out_specs=pl.BlockSpec((1, 8, tn), lambda l, j: (l, 0, j)),
        compiler_params=_params("parallel", "parallel"),
        name="ada_modulation",
    )(ct, ada_w, ada_b.reshape(depth, 1, n6))


def _norm_modulate(x, g, sc, sh):
    ms = jnp.mean(x * x, axis=-1, keepdims=True)
    return x * lax.rsqrt(ms + RMS_EPS) * g * (1.0 + sc) + sh


def _head_rmsnorm(t, seg, gain):
    ss = _dot((t * t).astype(BF16), seg)
    return t * lax.rsqrt(ss * (1.0 / HEAD_DIM) + RMS_EPS) * gain


class _PendingMoe(NamedTuple):
    ys: jax.Array
    chunk_pos: jax.Array
    cmb: jax.Array
    x1: jax.Array
    g2: jax.Array


def _token_source(xsrc, tm, nt, d):
    tok = lambda i, j, pos: (i, j, 0)
    if not isinstance(xsrc, _PendingMoe):
        return 0, jnp.zeros((1,), jnp.int32), [xsrc], [pl.BlockSpec((1, tm, d), tok)]
    ntc = _tile_chunks(tm)
    chunk = lambda k: pl.BlockSpec((1, MOE_CHUNK, d), lambda i, j, pos: (pos[(i * nt + j) * ntc + k], 0, 0))
    specs = [chunk(k) for k in range(ntc)] + [pl.BlockSpec((1, tm, LANES), tok), pl.BlockSpec((1, tm, d), tok),
                                              pl.BlockSpec((1, 1, d), lambda i, j, pos: (i, 0, 0))]
    return ntc, xsrc.chunk_pos, [xsrc.ys] * ntc + [xsrc.cmb, xsrc.x1, xsrc.g2], specs


def _load_tokens(refs, n_chunks):
    if n_chunks == 0:
        return refs[0][0], refs[1:]
    cmb_ref, x1_ref, g2_ref = refs[n_chunks:n_chunks + 3]
    return _moe_unpermute(refs[:n_chunks], cmb_ref, x1_ref, g2_ref), refs[n_chunks + 3:]


def _inproj_even_kernel(pos_ref, *refs, n_chunks):
    x, refs = _load_tokens(refs, n_chunks)
    g_ref, sc_ref, sh_ref, w_ref, qg_ref, kg_ref, seg_ref = refs[:7]
    outs = refs[7:]
    if n_chunks:
        outs[0][0] = x
        outs = outs[1:]
    q_ref, k_ref, v_ref, gb_ref, p_ref = outs
    u = _norm_modulate(x, g_ref[...], sc_ref[0], sh_ref[0]).astype(BF16)
    seg = seg_ref[...]
    w = NA_WIDTH
    q = _dot(u, w_ref[:, 0 * w:1 * w])
    q_ref[0] = (_head_rmsnorm(q, seg, qg_ref[...]) * (HEAD_DIM ** -0.5)).astype(BF16)
    k = _dot(u, w_ref[:, 1 * w:2 * w])
    k_ref[0] = _head_rmsnorm(k, seg, kg_ref[...]).astype(BF16)
    v_ref[0] = _dot(u, w_ref[:, 2 * w:3 * w]).astype(BF16)
    gb_ref[0] = _dot(u, w_ref[:, 3 * w:4 * w]).astype(BF16)
    gc = _dot(u, w_ref[:, 4 * w:5 * w])
    hv = _dot(u, w_ref[:, 5 * w:6 * w])
    p_ref[0] = (gc * hv).astype(BF16)


def _inproj_even(xsrc, g, sc, sh, w_bf, qg, kg, seg):
    b, n, d = (xsrc.x1 if isinstance(xsrc, _PendingMoe) else xsrc).shape
    tm = min(512, n)
    nt = n // tm
    n_chunks, pos, x_ops, x_specs = _token_source(xsrc, tm, nt, d)
    tok = lambda i, j, pos: (i, j, 0)
    mod = lambda i, j, pos: (i, 0, 0)
    const = lambda i, j, pos: (0, 0)
    out = jax.ShapeDtypeStruct((b, n, NA_WIDTH), BF16)
    x_out = [jax.ShapeDtypeStruct((b, n, d), F32)] if n_chunks else []
    x_out_spec = [pl.BlockSpec((1, tm, d), tok)] if n_chunks else []
    res = pl.pallas_call(
        functools.partial(_inproj_even_kernel, n_chunks=n_chunks),
        out_shape=tuple(x_out) + (out,) * 5,
        grid_spec=pltpu.PrefetchScalarGridSpec(
            num_scalar_prefetch=1,
            grid=(b, nt),
            in_specs=x_specs + [pl.BlockSpec((1, d), const),
                                pl.BlockSpec((1, 1, d), mod),
                                pl.BlockSpec((1, 1, d), mod),
                                pl.BlockSpec(w_bf.shape, const),
                                pl.BlockSpec((1, NA_WIDTH), const),
                                pl.BlockSpec((1, NA_WIDTH), const),
                                pl.BlockSpec((NA_WIDTH, NA_WIDTH), const)],
            out_specs=tuple(x_out_spec) + (pl.BlockSpec((1, tm, NA_WIDTH), tok),) * 5),
        compiler_params=_params("parallel", "parallel"),
        name="inproj_even",
    )(pos, *x_ops, g, sc, sh, w_bf, qg, kg, seg)
    return tuple(res) if n_chunks else (xsrc,) + tuple(res)


def _bias_kernel(rpb_ref, o_ref):
    h = pl.program_id(0)
    qi = lax.broadcasted_iota(jnp.int32, (GRID_W, GRID_W), 0)
    ki = lax.broadcasted_iota(jnp.int32, (GRID_W, GRID_W), 1)
    start = jnp.clip(qi - NA_WIN_W // 2, 0, GRID_W - NA_WIN_W)
    valid = jnp.logical_and(ki >= start, ki < start + NA_WIN_W)
    cidx = jnp.clip(ki - qi, -(NA_WIN_W - 1), NA_WIN_W - 1) + (NA_WIN_W - 1)
    n_dr = 2 * NA_WIN_H - 1
    n_dc = 2 * NA_WIN_W - 1

    def body(j, accs):
        m = cidx == j
        return tuple(jnp.where(m, rpb_ref[(h * n_dr + d) * n_dc + j], a) for d, a in enumerate(accs))

    accs = lax.fori_loop(0, n_dc, body, tuple(jnp.zeros((GRID_W, GRID_W), F32) for _ in range(n_dr)))
    tiles = [jnp.where(valid, a, NEG_INF) for a in accs]
    for d0 in range(NA_WIN_H):
        o_ref[0, d0] = jnp.concatenate(tiles[d0:d0 + NA_WIN_H], axis=1)


def _bias_table(rpb):
    return pl.pallas_call(
        _bias_kernel,
        out_shape=jax.ShapeDtypeStruct((NA_HEADS, NA_WIN_H, GRID_W, NA_WIN_H * GRID_W), F32),
        grid=(NA_HEADS,),
        in_specs=[pl.BlockSpec(memory_space=pltpu.SMEM)],
        out_specs=pl.BlockSpec((1, NA_WIN_H, GRID_W, NA_WIN_H * GRID_W), lambda h: (h, 0, 0, 0)),
        compiler_params=_params("arbitrary"),
        name="rpb_bias_table",
    )(rpb.reshape(-1))


def _pair_attention(q2, kw, vw, bias, kc, vc, first_half):
    s_c = _dot_nt(q2, kc)
    m = jnp.max(s_c, axis=-1, keepdims=True)
    if kw is not None:
        s_w = _dot_nt(q2, kw) + bias
        m = jnp.maximum(m, jnp.max(s_w, axis=-1, keepdims=True))
        p_w = jnp.exp(s_w - m)
    p_c = jnp.exp(s_c - m)
    den = jnp.sum(p_c, axis=-1, keepdims=True)
    o = _dot(p_c.astype(BF16), vc)
    if kw is not None:
        den = den + jnp.sum(p_w, axis=-1, keepdims=True)
        o = o + _dot(p_w.astype(BF16), vw)
    o = o / den
    half = o.shape[0] // 2
    return jnp.where(first_half, o[:half], o[half:])


def _stack_heads(qp, first_half):
    zero = jnp.zeros_like(qp)
    return jnp.concatenate([jnp.where(first_half, qp, zero), jnp.where(first_half, zero, qp)], axis=0)


def _natten_kernel(q_ref, k_ref, v_ref, kc_ref, vc_ref, bias_ref, o_ref, sc_ref, ow_ref, pc_ref,
                   *, rows_per_step, n_rows):
    blk = pl.program_id(1)
    lane = lax.broadcasted_iota(jnp.int32, (GRID_W, LANES), 1)
    first_half = lane < HEAD_DIM
    band = NA_WIN_H * GRID_W
    n_pairs = NA_HEADS // 2
    stacked = 2 * GRID_W
    cols = [slice(hp * LANES, (hp + 1) * LANES) for hp in range(n_pairs)]
    lane3 = lax.broadcasted_iota(jnp.int32, (rows_per_step, GRID_W, LANES), 2)

    for hp in range(n_pairs):
        q3 = q_ref[0, :, cols[hp]].reshape(rows_per_step, GRID_W, LANES)
        zero = jnp.zeros_like(q3)
        q_all = jnp.concatenate([jnp.where(lane3 < HEAD_DIM, q3, zero), jnp.where(lane3 < HEAD_DIM, zero, q3)], axis=1)
        sc_ref[hp] = _dot_nt(q_all.reshape(rows_per_step * stacked, LANES), kc_ref[0, :, cols[hp]])

    def row_body(j, carry):
        r = blk * rows_per_step + j
        start = jnp.clip(r - NA_WIN_H // 2, 0, n_rows - NA_WIN_H)
        d0 = start - r + (NA_WIN_H - 1)
        koff = pl.multiple_of(start * GRID_W, GRID_W)
        qoff = pl.multiple_of(j * GRID_W, GRID_W)
        soff = pl.multiple_of(j * stacked, stacked)
        q2 = [_stack_heads(q_ref[0, pl.ds(qoff, GRID_W), cs], first_half) for cs in cols]
        s_w = [_dot_nt(q2[hp], k_ref[0, pl.ds(koff, band), cols[hp]])
               + jnp.concatenate([bias_ref[2 * hp, d0], bias_ref[2 * hp + 1, d0]], axis=0) for hp in range(n_pairs)]
        s_c = [sc_ref[hp, pl.ds(soff, stacked), :] for hp in range(n_pairs)]
        m = [jnp.maximum(jnp.max(s_w[hp], axis=-1, keepdims=True), jnp.max(s_c[hp], axis=-1, keepdims=True))
             for hp in range(n_pairs)]
        p_w = [jnp.exp(s_w[hp] - m[hp]) for hp in range(n_pairs)]
        p_c = [jnp.exp(s_c[hp] - m[hp]) for hp in range(n_pairs)]
        rden = [1.0 / (jnp.sum(p_w[hp], axis=-1, keepdims=True) + jnp.sum(p_c[hp], axis=-1, keepdims=True))
                for hp in range(n_pairs)]
        for hp in range(n_pairs):
            ow_ref[hp, pl.ds(soff, stacked), :] = _dot(p_w[hp].astype(BF16),
                                                       v_ref[0, pl.ds(koff, band), cols[hp]]) * rden[hp]
            pc_ref[hp, pl.ds(soff, stacked), :] = (p_c[hp] * rden[hp]).astype(BF16)
        return carry

    lax.fori_loop(0, rows_per_step, row_body, 0, unroll=4)

    for hp in range(n_pairs):
        o = ow_ref[hp] + _dot(pc_ref[hp], vc_ref[0, :, cols[hp]])
        o = o.reshape(rows_per_step, stacked, LANES)
        o = jnp.where(lane3 < HEAD_DIM, o[:, :GRID_W, :], o[:, GRID_W:, :])
        o_ref[0, :, cols[hp]] = o.reshape(rows_per_step * GRID_W, LANES).astype(o_ref.dtype)


def _natten(q, k, v, kc, vc, bias):
    b, n, w = q.shape
    n_rows = n // GRID_W
    rows_per_step = 8
    tq = rows_per_step * GRID_W
    lc = kc.shape[1]
    return pl.pallas_call(
        functools.partial(_natten_kernel, rows_per_step=rows_per_step, n_rows=n_rows),
        out_shape=jax.ShapeDtypeStruct((b, n, w), BF16),
        grid=(b, n_rows // rows_per_step),
        in_specs=[pl.BlockSpec((1, tq, w), lambda i, j: (i, j, 0)),
                  pl.BlockSpec((1, n, w), lambda i, j: (i, 0, 0)),
                  pl.BlockSpec((1, n, w), lambda i, j: (i, 0, 0)),
                  pl.BlockSpec((1, lc, w), lambda i, j: (i, 0, 0)),
                  pl.BlockSpec((1, lc, w), lambda i, j: (i, 0, 0)),
                  pl.BlockSpec(bias.shape, lambda i, j: (0, 0, 0, 0))],
        out_specs=pl.BlockSpec((1, tq, w), lambda i, j: (i, j, 0)),
        scratch_shapes=[pltpu.VMEM((NA_HEADS // 2, 2 * tq, lc), F32),
                        pltpu.VMEM((NA_HEADS // 2, 2 * tq, LANES), F32),
                        pltpu.VMEM((NA_HEADS // 2, 2 * tq, lc), BF16)],
        compiler_params=_params("parallel", "arbitrary"),
        name="neighbourhood_attention",
    )(q, k, v, kc, vc, bias)


def _ctx_attn_kernel(q_ref, k_ref, v_ref, o_ref):
    lc = q_ref.shape[1]
    lane = lax.broadcasted_iota(jnp.int32, (lc, LANES), 1)
    first_half = lane < HEAD_DIM
    outs = []
    for hp in range(NA_HEADS // 2):
        cs = slice(hp * LANES, (hp + 1) * LANES)
        q2 = _stack_heads(q_ref[0, :, cs], first_half)
        outs.append(_pair_attention(q2, None, None, None, k_ref[0, :, cs], v_ref[0, :, cs], first_half))
    o_ref[0] = jnp.concatenate(outs, axis=1).astype(o_ref.dtype)


def _ctx_attention(q, k, v):
    b, lc, w = q.shape
    spec = pl.BlockSpec((1, lc, w), lambda i: (i, 0, 0))
    return pl.pallas_call(
        _ctx_attn_kernel,
        out_shape=jax.ShapeDtypeStruct((b, lc, w), BF16),
        grid=(b,),
        in_specs=[spec, spec, spec],
        out_specs=spec,
        compiler_params=_params("parallel"),
        name="context_attention",
    )(q, k, v)


def _inproj_odd_kernel(pos_ref, *refs, n_chunks):
    x, refs = _load_tokens(refs, n_chunks)
    g_ref, sc_ref, sh_ref, wht_ref, wag_ref = refs[:5]
    outs = refs[5:]
    if n_chunks:
        outs[0][0] = x
        outs = outs[1:]
    hy_ref, ag_ref = outs
    u = _norm_modulate(x, g_ref[...], sc_ref[0], sh_ref[0]).astype(BF16)
    hy_ref[0] = _dot_nt(wht_ref[...], u).astype(BF16)
    a = _dot(u, wag_ref[:, :CF_WIDTH])
    g = _dot(u, wag_ref[:, CF_WIDTH:])
    ag_ref[0] = (a * jax.nn.sigmoid(g)).astype(BF16)


def _inproj_odd(xsrc, g, sc, sh, wht_bf, wag_bf):
    b, n, d = (xsrc.x1 if isinstance(xsrc, _PendingMoe) else xsrc).shape
    tm = min(512, n)
    nt = n // tm
    hw = wht_bf.shape[0]
    n_chunks, pos, x_ops, x_specs = _token_source(xsrc, tm, nt, d)
    tok = lambda i, j, pos: (i, j, 0)
    mod = lambda i, j, pos: (i, 0, 0)
    const = lambda i, j, pos: (0, 0)
    x_out = [jax.ShapeDtypeStruct((b, n, d), F32)] if n_chunks else []
    x_out_spec = [pl.BlockSpec((1, tm, d), tok)] if n_chunks else []
    res = pl.pallas_call(
        functools.partial(_inproj_odd_kernel, n_chunks=n_chunks),
        out_shape=tuple(x_out) + (jax.ShapeDtypeStruct((b, hw, n), BF16),
                                  jax.ShapeDtypeStruct((b, n, CF_WIDTH), BF16)),
        grid_spec=pltpu.PrefetchScalarGridSpec(
            num_scalar_prefetch=1,
            grid=(b, nt),
            in_specs=x_specs + [pl.BlockSpec((1, d), const),
                                pl.BlockSpec((1, 1, d), mod),
                                pl.BlockSpec((1, 1, d), mod),
                                pl.BlockSpec(wht_bf.shape, const),
                                pl.BlockSpec(wag_bf.shape, const)],
            out_specs=tuple(x_out_spec) + (pl.BlockSpec((1, hw, tm), lambda i, j, pos: (i, 0, j)),
                                           pl.BlockSpec((1, tm, CF_WIDTH), tok))),
        compiler_params=_params("parallel", "parallel"),
        name="inproj_odd",
    )(pos, *x_ops, g, sc, sh, wht_bf, wag_bf)
    return tuple(res) if n_chunks else (xsrc,) + tuple(res)


SUBLANES = 8
CF_PAD = 2 * SUBLANES
CF_ROWS = 256


def _conformer_kernel(ag_ref, w_ref, cb_ref, lg_ref, lb_ref, o_ref, pad_ref, *, seq):
    zeros = jnp.zeros((CF_PAD, CF_WIDTH), F32)
    pad_ref[0:CF_PAD, :] = zeros
    pad_ref[CF_PAD + seq:2 * CF_PAD + seq, :] = zeros
    pad_ref[CF_PAD:CF_PAD + seq, :] = ag_ref[0].astype(F32)
    shift0 = CF_PAD - CF_TAPS // 2
    n_groups = (shift0 + CF_TAPS - 1) // SUBLANES + 1

    def conv_rows(i, carry):
        n0 = pl.multiple_of(i * CF_ROWS, CF_ROWS)
        wins = [pad_ref[pl.ds(n0 + SUBLANES * a, CF_ROWS + SUBLANES), :] for a in range(n_groups)]
        acc = None
        for b in range(SUBLANES):
            part = None
            for a in range(n_groups):
                j = SUBLANES * a + b - shift0
                if 0 <= j < CF_TAPS:
                    term = w_ref[j:j + 1, :] * wins[a]
                    part = term if part is None else part + term
            part = part[b:b + CF_ROWS, :]
            acc = part if acc is None else acc + part
        y = acc + cb_ref[...]
        mu = jnp.mean(y, axis=-1, keepdims=True)
        yc = y - mu
        var = jnp.mean(yc * yc, axis=-1, keepdims=True)
        z = yc * lax.rsqrt(var + LN_EPS) * lg_ref[...] + lb_ref[...]
        o_ref[0, pl.ds(n0, CF_ROWS), :] = _silu(z).astype(o_ref.dtype)
        return carry

    lax.fori_loop(0, seq // CF_ROWS, conv_rows, 0)


def _conformer(ag, w, cb, lg, lb):
    b, n, c = ag.shape
    spec = pl.BlockSpec((1, n, c), lambda i: (i, 0, 0))
    vec = pl.BlockSpec((1, c), lambda i: (0, 0))
    return pl.pallas_call(
        functools.partial(_conformer_kernel, seq=n),
        out_shape=jax.ShapeDtypeStruct((b, n, c), BF16),
        grid=(b,),
        in_specs=[spec, pl.BlockSpec((CF_TAPS, c), lambda i: (0, 0)), vec, vec, vec],
        out_specs=spec,
        scratch_shapes=[pltpu.VMEM((n + 2 * CF_PAD, c), F32)],
        compiler_params=_params("parallel"),
        name="conformer_conv",
    )(ag, w, cb[None, :], lg[None, :], lb[None, :])


def _hyena_features(length):
    t = np.linspace(0.0, 1.0, length, dtype=np.float32)
    w = (2.0 * math.pi * np.arange(length, dtype=np.float32) / length).astype(np.float32)
    bands = np.linspace(1e-4, HY_BANDS - 1, HY_BANDS, dtype=np.float32)
    ang = (bands[:, None] * w[None, :]).astype(np.float32)
    zt = np.concatenate([t[None, :], np.cos(ang), -np.sin(ang)], axis=0).astype(np.float32)
    deltas = np.abs(np.linspace(HY_MIN_DECAY, HY_MAX_DECAY, HY_WIDTH, dtype=np.float32))
    rev = (length - np.arange(length)) % length
    zt2 = np.concatenate([zt, zt[:, rev]], axis=1)
    t2 = np.concatenate([t, t[rev]])[None, :]
    return zt2, t2, deltas[:, None]


def _taps_kernel(zt_ref, t_ref, dl_ref, w1t_ref, b1_ref, f0_ref, w2t_ref, b2_ref, f1_ref, w3t_ref,
                 o_ref, hid_ref):
    first = jnp.logical_and(pl.program_id(0) == 0, pl.program_id(1) == 0)
    length = t_ref.shape[1] // 2

    @pl.when(first)
    def _():
        h1 = jnp.sin(f0_ref[...] * (_dot_f32(w1t_ref[...], zt_ref[...]) + b1_ref[...]))
        hid_ref[...] = jnp.sin(f1_ref[...] * (_dot_f32(w2t_ref[...], h1) + b2_ref[...]))

    decay = jnp.exp(-(dl_ref[...] * t_ref[...]))
    fwd = _dot_f32(w3t_ref[0, 0], hid_ref[:, :length])
    bwd = _dot_f32(w3t_ref[0, 1], hid_ref[:, length:])
    taps = jnp.concatenate([fwd, bwd], axis=1) * decay
    nrm = jnp.sum(jnp.abs(taps), axis=-1, keepdims=True)
    lane = lax.broadcasted_iota(jnp.int32, taps.shape, 1)
    o_ref[0] = jnp.where(lane == length, 0.0, taps / nrm)


def _hyena_taps(length, w1, b1, w2, b2, w3, freq):
    zt2, t2, deltas = _hyena_features(length)
    cb = 128
    w3t = w3.T.reshape(2, 2, HY_WIDTH, HY_FFN)
    col = lambda v: v.reshape(HY_FFN, 1)
    const = lambda o, j: (0, 0)
    return pl.pallas_call(
        _taps_kernel,
        out_shape=jax.ShapeDtypeStruct((2, HY_WIDTH, 2 * length), F32),
        grid=(2, HY_WIDTH // cb),
        in_specs=[pl.BlockSpec((HY_EMB, 2 * length), const),
                  pl.BlockSpec((1, 2 * length), const),
                  pl.BlockSpec((cb, 1), lambda o, j: (j, 0)),
                  pl.BlockSpec((HY_FFN, HY_EMB), const),
                  pl.BlockSpec((HY_FFN, 1), const),
                  pl.BlockSpec((HY_FFN, 1), const),
                  pl.BlockSpec((HY_FFN, HY_FFN), const),
                  pl.BlockSpec((HY_FFN, 1), const),
                  pl.BlockSpec((HY_FFN, 1), const),
                  pl.BlockSpec((1, 2, cb, HY_FFN), lambda o, j: (o, 0, j, 0))],
        out_specs=pl.BlockSpec((1, cb, 2 * length), lambda o, j: (o, j, 0)),
        scratch_shapes=[pltpu.VMEM((HY_FFN, 2 * length), F32)],
        compiler_params=_params("arbitrary", "arbitrary"),
        name="hyena_filter_taps",
    )(jnp.asarray(zt2), jnp.asarray(t2), jnp.asarray(deltas), w1.T, col(b1), col(freq[0]),
      w2.T, col(b2), col(freq[1]), w3t)


def _dft_constants():
    na, nb, n = FFT_NA, FFT_NB, FFT_N
    half = na // 2
    ka = np.arange(na)
    ang_a = 2.0 * np.pi * np.outer(ka, ka) / na
    ca, sa = np.cos(ang_a), np.sin(ang_a)
    fa = np.block([[ca[:half], -sa[:half]], [sa[:half], ca[:half]]])
    fai = np.block([[ca[:, :half], sa[:, :half]], [-sa[:, :half], ca[:, :half]]])
    kb = np.arange(nb)
    ang_b = 2.0 * np.pi * np.outer(kb, kb) / nb
    cbm, sbm = np.cos(ang_b), np.sin(ang_b)
    fb = np.block([[cbm, -sbm], [sbm, cbm]])
    fbi = np.block([[cbm, sbm], [-sbm, cbm]])
    ang_t = 2.0 * np.pi * np.outer(kb, ka) / n
    ct, st = np.cos(ang_t), np.sin(ang_t)
    tw_fc = np.concatenate([ct, ct], axis=1)
    tw_fs = np.concatenate([st, -st], axis=1)
    tw_ic, tw_is = ct.T.copy(), st.T.copy()
    bf = lambda a: jnp.asarray(a, dtype=F32).astype(BF16)
    f32 = lambda a: jnp.asarray(a, dtype=F32)
    fa_real = np.concatenate([ca, -sa], axis=1)
    return dict(fa=bf(fa), fa_real=bf(fa_real), fai=bf(fai), fb=bf(fb), fbi=bf(fbi),
                tw_fc=f32(tw_fc), tw_fs=f32(tw_fs), tw_ic=f32(tw_ic), tw_is=f32(tw_is))


def _fft_forward(zr, zi, fa, tw_fc, tw_fs, fb):
    c, _, nb = zr.shape
    tr = jnp.swapaxes(zr, 1, 2)
    lhs = tr if zi is None else jnp.concatenate([tr, jnp.swapaxes(zi, 1, 2)], axis=2)
    a = _dot(lhs.reshape(c * nb, lhs.shape[2]).astype(BF16), fa).reshape(c, nb, 2 * FFT_NA)
    a = a * tw_fc + pltpu.roll(a, FFT_NA, axis=2) * tw_fs
    t = jnp.swapaxes(a, 1, 2)
    lhs2 = jnp.concatenate([t[:, :FFT_NA, :], t[:, FFT_NA:, :]], axis=2)
    x = _dot(lhs2.reshape(c * FFT_NA, 2 * nb).astype(BF16), fb)
    return x.reshape(c, FFT_NA, 2 * nb)


def _fft_inverse(y, fbi, tw_ic, tw_is, fai):
    c = y.shape[0]
    nb = FFT_NB
    b = _dot(y.reshape(c * FFT_NA, 2 * nb).astype(BF16), fbi).reshape(c, FFT_NA, 2 * nb)
    br, bi = b[:, :, :nb], b[:, :, nb:]
    rr = br * tw_ic - bi * tw_is
    ii = bi * tw_ic + br * tw_is
    t = jnp.swapaxes(jnp.concatenate([rr, ii], axis=1), 1, 2)
    o = _dot(t.reshape(c * nb, 2 * FFT_NA).astype(BF16), fai).reshape(c, nb, FFT_NA)
    o = jnp.swapaxes(o, 1, 2)
    return o[:, :FFT_NA // 2, :], o[:, FFT_NA // 2:, :]


def _filter_fft_kernel(taps_ref, fa_ref, twc_ref, tws_ref, fb_ref, o_ref):
    h = _fft_forward(taps_ref[0], None, fa_ref[...], twc_ref[...], tws_ref[...], fb_ref[...])
    o_ref[0] = h * (1.0 / FFT_N)


def _filter_fft(taps, consts):
    _, c, n = taps.shape
    cb = 32
    taps4 = taps.reshape(2, c, FFT_NA, FFT_NB)
    cm = lambda o, j: (0, 0)
    return pl.pallas_call(
        _filter_fft_kernel,
        out_shape=jax.ShapeDtypeStruct((2, c, FFT_NA, 2 * FFT_NB), F32),
        grid=(2, c // cb),
        in_specs=[pl.BlockSpec((1, cb, FFT_NA, FFT_NB), lambda o, j: (o, j, 0, 0)),
                  pl.BlockSpec(consts["fa_real"].shape, cm),
                  pl.BlockSpec(consts["tw_fc"].shape, cm),
                  pl.BlockSpec(consts["tw_fs"].shape, cm),
                  pl.BlockSpec(consts["fb"].shape, cm)],
        out_specs=pl.BlockSpec((1, cb, FFT_NA, 2 * FFT_NB), lambda o, j: (o, j, 0, 0)),
        compiler_params=_params("parallel", "parallel"),
        name="hyena_filter_fft",
    )(taps4, consts["fa_real"], consts["tw_fc"], consts["tw_fs"], consts["fb"])


def _shift_tokens(a, direction):
    rows = a.shape[-2]
    lane = lax.broadcasted_iota(jnp.int32, a.shape, a.ndim - 1)
    row = lax.broadcasted_iota(jnp.int32, a.shape, a.ndim - 2)
    if direction == 1:
        l = pltpu.roll(a, 1, axis=a.ndim - 1)
        ls = pltpu.roll(l, 1, axis=a.ndim - 2)
        out = jnp.where(lane == 0, ls, l)
        edge = jnp.logical_and(lane == 0, row == 0)
    else:
        l = pltpu.roll(a, LANES - 1, axis=a.ndim - 1)
        ls = pltpu.roll(l, rows - 1, axis=a.ndim - 2)
        out = jnp.where(lane == LANES - 1, ls, l)
        edge = jnp.logical_and(lane == LANES - 1, row == rows - 1)
    return jnp.where(edge, 0.0, out)


def _short_conv3(a, w_ref):
    return w_ref[0] * _shift_tokens(a, 1) + w_ref[1] * a + w_ref[2] * _shift_tokens(a, -1)


def _hyena_kernel(v_ref, x1_ref, x2_ref, wv_ref, w1_ref, w2_ref, h_ref, hb_ref,
                  fa_ref, twfc_ref, twfs_ref, fb_ref, fbi_ref, twic_ref, twis_ref, fai_ref, o_ref):
    fwd_c = (fa_ref[...], twfc_ref[...], twfs_ref[...], fb_ref[...])
    inv_c = (fbi_ref[...], twic_ref[...], twis_ref[...], fai_ref[...])
    z = _short_conv3(v_ref[...].astype(F32), wv_ref)
    zr, zi = z[0], z[1]
    nb = FFT_NB
    for o, (g_ref, gw_ref) in enumerate(((x1_ref, w1_ref), (x2_ref, w2_ref))):
        x = _fft_forward(zr, zi, *fwd_c)
        h = h_ref[o]
        xr, xi, hr, hi = x[:, :, :nb], x[:, :, nb:], h[:, :, :nb], h[:, :, nb:]
        y = jnp.concatenate([xr * hr - xi * hi, xr * hi + xi * hr], axis=2)
        yr, yi = _fft_inverse(y, *inv_c)
        gate = _short_conv3(g_ref[...].astype(F32), gw_ref)
        bias = hb_ref[o]
        zr = gate[0] * (yr + zr * bias)
        zi = gate[1] * (yi + zi * bias)
    o_ref[0] = zr.astype(o_ref.dtype)
    o_ref[1] = zi.astype(o_ref.dtype)


def _hyena_latent(hyt, short_w, h_spec, hy_bias, consts):
    b, c3, length = hyt.shape
    c = c3 // 3
    rows = length // FFT_NB
    cb = 32
    nblk = c // cb
    hy4 = hyt.reshape(b, c3, rows, FFT_NB)
    w4 = jnp.broadcast_to(short_w.reshape(3, c3, 1, 1), (3, c3, 1, FFT_NB))
    hb4 = jnp.broadcast_to(hy_bias.reshape(2, c, 1, 1), (2, c, 1, FFT_NB))
    sig = lambda g: pl.BlockSpec((2, cb, rows, FFT_NB), lambda j, p: (p, g * nblk + j, 0, 0))
    wsp = lambda g: pl.BlockSpec((3, cb, 1, FFT_NB), lambda j, p: (0, g * nblk + j, 0, 0))
    cm = lambda j, p: (0, 0)
    names = ("fa", "tw_fc", "tw_fs", "fb", "fbi", "tw_ic", "tw_is", "fai")
    out = pl.pallas_call(
        _hyena_kernel,
        out_shape=jax.ShapeDtypeStruct((b, c, rows, FFT_NB), BF16),
        grid=(nblk, b // 2),
        in_specs=[sig(0), sig(1), sig(2), wsp(0), wsp(1), wsp(2),
                  pl.BlockSpec((2, cb, FFT_NA, 2 * FFT_NB), lambda j, p: (0, j, 0, 0)),
                  pl.BlockSpec((2, cb, 1, FFT_NB), lambda j, p: (0, j, 0, 0))]
                 + [pl.BlockSpec(consts[k].shape, cm) for k in names],
        out_specs=pl.BlockSpec((2, cb, rows, FFT_NB), lambda j, p: (p, j, 0, 0)),
        compiler_params=_params("parallel", "arbitrary"),
        name="hyena_long_conv",
    )(hy4, hy4, hy4, w4, w4, w4, h_spec, hb4, *[consts[k] for k in names])
    return out.reshape(b, c, length)


def _dense_dft_constants(length):
    n = 2 * length
    k = np.arange(n)
    ang = 2.0 * np.pi * np.outer(k, k) / n
    fwd = np.concatenate([np.cos(ang), -np.sin(ang)], axis=1)
    inv = np.concatenate([np.cos(ang[:length]).T, -np.sin(ang[:length]).T], axis=0) / n
    bf = lambda a: jnp.asarray(a, dtype=F32).astype(BF16)
    return bf(fwd), bf(inv)


def _shift_lanes(a, direction):
    n = a.shape[-1]
    lane = lax.broadcasted_iota(jnp.int32, a.shape, a.ndim - 1)
    if direction == 1:
        return jnp.where(lane == 0, 0.0, pltpu.roll(a, 1, axis=a.ndim - 1))
    return jnp.where(lane == n - 1, 0.0, pltpu.roll(a, n - 1, axis=a.ndim - 1))


def _hyena_ctx_kernel(v_ref, x1_ref, x2_ref, wv_ref, w1_ref, w2_ref, taps_ref, hb_ref, fwd_ref, inv_ref, o_ref):
    bsz, cb, length = v_ref.shape
    n = 2 * length
    fwd_m, inv_m = fwd_ref[0:length, :], inv_ref[...]

    def conv3(ref, w_ref):
        a = ref[...].astype(F32)
        return w_ref[0] * _shift_lanes(a, 1) + w_ref[1] * a + w_ref[2] * _shift_lanes(a, -1)

    z = conv3(v_ref, wv_ref)
    for o, (g_ref, gw_ref) in enumerate(((x1_ref, w1_ref), (x2_ref, w2_ref))):
        h = _dot(taps_ref[o].astype(BF16), fwd_ref[...])
        hr, hi = h[:, :n], h[:, n:]
        x = _dot(z.reshape(bsz * cb, length).astype(BF16), fwd_m).reshape(bsz, cb, 2 * n)
        xr, xi = x[:, :, :n], x[:, :, n:]
        y = jnp.concatenate([xr * hr - xi * hi, xr * hi + xi * hr], axis=2)
        yt = _dot(y.reshape(bsz * cb, 2 * n).astype(BF16), inv_m).reshape(bsz, cb, length)
        z = conv3(g_ref, gw_ref) * (yt + z * hb_ref[o])
    o_ref[...] = z.astype(o_ref.dtype)


def _hyena_context(hyt, short_w, taps, hy_bias):
    b, c3, length = hyt.shape
    c = c3 // 3
    cb = 128
    nblk = c // cb
    fwd_m, inv_m = _dense_dft_constants(length)
    w4 = jnp.broadcast_to(short_w.reshape(3, c3, 1), (3, c3, length))
    hb = jnp.broadcast_to(hy_bias.reshape(2, c, 1), (2, c, length))
    sig = lambda g: pl.BlockSpec((b, cb, length), lambda j: (0, g * nblk + j, 0))
    wsp = lambda g: pl.BlockSpec((3, cb, length), lambda j: (0, g * nblk + j, 0))
    return pl.pallas_call(
        _hyena_ctx_kernel,
        out_shape=jax.ShapeDtypeStruct((b, c, length), BF16),
        grid=(nblk,),
        in_specs=[sig(0), sig(1), sig(2), wsp(0), wsp(1), wsp(2),
                  pl.BlockSpec((2, cb, 2 * length), lambda j: (0, j, 0)),
                  pl.BlockSpec((2, cb, length), lambda j: (0, j, 0)),
                  pl.BlockSpec(fwd_m.shape, lambda j: (0, 0)),
                  pl.BlockSpec(inv_m.shape, lambda j: (0, 0))],
        out_specs=pl.BlockSpec((b, cb, length), lambda j: (0, j, 0)),
        compiler_params=_params("parallel"),
        name="hyena_context_conv",
    )(hyt, hyt, hyt, w4, w4, w4, taps, hb, fwd_m, inv_m)


def _route(gl, el):
    row = lax.broadcasted_iota(jnp.int32, gl.shape, 0).astype(F32)
    grp = jnp.floor(row * (1.0 / EXPERTS_PER_GROUP))
    big = float(N_EXPERTS)
    gmax = jnp.max(gl, axis=0, keepdims=True)
    gidx = jnp.min(jnp.where(gl == gmax, grp, big), axis=0, keepdims=True)
    gsum = jnp.sum(jnp.exp(gl - gmax), axis=0, keepdims=True) * (1.0 / EXPERTS_PER_GROUP)
    g_w = 1.0 / gsum
    em = jnp.where(grp == gidx, el, NEG_INF)
    t1 = jnp.max(em, axis=0, keepdims=True)
    i1 = jnp.min(jnp.where(em == t1, row, big), axis=0, keepdims=True)
    em2 = jnp.where(row == i1, 2.0 * NEG_INF, em)
    t2 = jnp.max(em2, axis=0, keepdims=True)
    i2 = jnp.min(jnp.where(em2 == t2, row, big), axis=0, keepdims=True)
    e2 = jnp.exp(t2 - t1)
    den = 1.0 + e2
    w1 = g_w / den
    w2 = g_w * e2 / den
    return jnp.where(row == i1, w1, 0.0) + jnp.where(row == i2, w2, 0.0), gidx


SLOT_LANE = N_EXPERTS
CHUNKS_LANE = N_EXPERTS + 1
MOE_CHUNK = 64


def _tile_chunks(tm):
    return tm // MOE_CHUNK + N_GROUPS


def _dispatch_slots(gidx, tri):
    tm = gidx.shape[1]
    sub = lax.broadcasted_iota(jnp.int32, (SUBLANES, tm), 0)
    grp = sub.astype(F32)
    member = jnp.where(grp == gidx, 1.0, 0.0)
    rank = _dot(member.astype(BF16), tri)
    count = jnp.sum(member, axis=1, keepdims=True)
    chunks = jnp.floor((count + float(MOE_CHUNK - 1)) * (1.0 / MOE_CHUNK))
    first = jnp.zeros_like(chunks)
    for g in range(1, N_GROUPS):
        first = first + jnp.where(sub[:, 0:1] >= g, chunks[g - 1:g, :], 0.0)
    slot = jnp.sum(member * (first * float(MOE_CHUNK) + rank - 1.0), axis=0, keepdims=True)
    out = jnp.where(sub == 0, slot, 0.0)
    for g in range(N_GROUPS):
        out = out + jnp.where(sub == g + 1, chunks[g:g + 1, :], 0.0)
    return out


def _gated_conv_tile(gb_ref, p_ref, cw_ref, tm):
    j = pl.program_id(1)
    n = p_ref.shape[1]
    halo = 2 * SUBLANES
    t0 = pl.multiple_of(j * tm, tm)
    p = p_ref[0, pl.ds(t0, tm), :].astype(F32)
    lo = pl.multiple_of(jnp.maximum(t0 - halo, 0), halo)
    hi = pl.multiple_of(jnp.minimum(t0 + tm, n - halo), halo)
    before = p_ref[0, pl.ds(lo, halo), :].astype(F32)[halo - 1:halo, :]
    after = p_ref[0, pl.ds(hi, halo), :].astype(F32)[0:1, :]
    before = jnp.where(t0 == 0, 0.0, before)
    after = jnp.where(t0 + tm >= n, 0.0, after)
    row = lax.broadcasted_iota(jnp.int32, p.shape, 0)
    prev = jnp.where(row == 0, before, pltpu.roll(p, 1, axis=0))
    nxt = jnp.where(row == tm - 1, after, pltpu.roll(p, tm - 1, axis=0))
    y = cw_ref[0:1, :] * prev + cw_ref[1:2, :] * p + cw_ref[2:3, :] * nxt
    return (gb_ref[0].astype(F32) * y).astype(BF16)


def _outproj_kernel(*refs, channel_major, gated_conv):
    y1_ref = refs[0]
    n_y2 = 3 if gated_conv else 1
    (w_ref, x_ref, g1_ref, n2_ref, sc2_ref, sh2_ref, wr_ref, br_ref, tri_ref,
     xo_ref, cmb_ref, hs_ref, cs_ref) = refs[1 + n_y2:]
    y2 = _gated_conv_tile(*refs[1:4], x_ref.shape[1]) if gated_conv else refs[1][0]
    half = w_ref.shape[0] // 2
    dots = [_dot_tn if cm else _dot for cm in channel_major]
    acc = dots[0](y1_ref[0], w_ref[:half, :]) + dots[1](y2, w_ref[half:, :])
    xn = x_ref[0] + g1_ref[0] * acc
    xo_ref[0] = xn
    h = _norm_modulate(xn, n2_ref[...], sc2_ref[0], sh2_ref[0])
    hi = h.astype(BF16)
    lo = (h - hi.astype(F32)).astype(BF16)
    p = _dot(hi, wr_ref[...])
    logits = p[:, :LANES] + p[:, LANES:] + _dot(lo, wr_ref[:, :LANES]) + br_ref[...]
    lt = logits.T
    cmb, gidx = _route(lt[0:N_EXPERTS], lt[N_EXPERTS:2 * N_EXPERTS])
    disp = _dispatch_slots(gidx, tri_ref[...])
    pad = jnp.zeros((LANES - N_EXPERTS - SUBLANES, cmb.shape[1]), F32)
    rows = jnp.concatenate([cmb, disp, pad], axis=0).T
    cmb_ref[0] = rows
    n_chunks = hs_ref.shape[0]
    srow = lax.broadcasted_iota(jnp.int32, (n_chunks * MOE_CHUNK, rows.shape[0]), 0).astype(F32)
    gather = jnp.where(srow == disp[0:1, :], 1.0, 0.0).astype(BF16)
    hs = _dot(gather, hi).astype(BF16)
    r_hi = rows.astype(BF16)
    r_lo = (rows - r_hi.astype(F32)).astype(BF16)
    cs2 = _dot(gather, jnp.concatenate([r_hi, r_lo], axis=1))
    cs = cs2[:, :LANES] + cs2[:, LANES:]
    for k in range(n_chunks):
        hs_ref[k] = hs[k * MOE_CHUNK:(k + 1) * MOE_CHUNK]
        cs_ref[k] = cs[k * MOE_CHUNK:(k + 1) * MOE_CHUNK]


def _outproj(y1, y2, w_bf, x, g1, n2g, sc2, sh2, wr2, br, channel_major):
    b, n, d = x.shape
    tm = min(512, n)
    nt = n // tm
    ntc = _tile_chunks(tm)
    half = d // 2
    tok = lambda i, j: (i, j, 0)
    mod = lambda i, j: (i, 0, 0)
    const = lambda i, j: (0, 0)
    srt = lambda i, j: (i * nt + j, 0, 0)
    y_spec = lambda cm: (pl.BlockSpec((1, half, tm), lambda i, j: (i, 0, j)) if cm
                         else pl.BlockSpec((1, tm, half), tok))
    gated_conv = isinstance(y2, tuple)
    if gated_conv:
        y2_ops = list(y2)
        y2_specs = [pl.BlockSpec((1, tm, half), tok), pl.BlockSpec((1, n, half), mod),
                    pl.BlockSpec(y2[2].shape, const)]
    else:
        y2_ops, y2_specs = [y2], [y_spec(channel_major[1])]
    return pl.pallas_call(
        functools.partial(_outproj_kernel, channel_major=tuple(channel_major), gated_conv=gated_conv),
        out_shape=(jax.ShapeDtypeStruct((b, n, d), F32), jax.ShapeDtypeStruct((b, n, LANES), F32),
                   jax.ShapeDtypeStruct((b * nt * ntc, MOE_CHUNK, d), BF16),
                   jax.ShapeDtypeStruct((b * nt * ntc, MOE_CHUNK, LANES), F32)),
        grid=(b, nt),
        in_specs=[y_spec(channel_major[0])] + y2_specs + [
                  pl.BlockSpec((d, d), const),
                  pl.BlockSpec((1, tm, d), tok),
                  pl.BlockSpec((1, 1, d), mod),
                  pl.BlockSpec((1, d), const),
                  pl.BlockSpec((1, 1, d), mod),
                  pl.BlockSpec((1, 1, d), mod),
                  pl.BlockSpec(wr2.shape, const),
                  pl.BlockSpec(br.shape, const),
                  pl.BlockSpec((tm, tm), const)],
        out_specs=(pl.BlockSpec((1, tm, d), tok), pl.BlockSpec((1, tm, LANES), tok),
                   pl.BlockSpec((ntc, MOE_CHUNK, d), srt), pl.BlockSpec((ntc, MOE_CHUNK, LANES), srt)),
        compiler_params=_params("parallel", "parallel"),
        name="outproj_router",
    )(y1, *y2_ops, w_bf, x, g1, n2g, sc2, sh2, wr2, br,
      jnp.asarray(np.triu(np.ones((tm, tm), np.float32)), dtype=BF16))


def _swiglu_group(h, w_rows, j, wg_ref, wu_ref, wd_ref):
    lane = lax.broadcasted_iota(jnp.int32, w_rows.shape, 1)
    experts = range(EXPERTS_PER_GROUP)
    w_e = [jnp.sum(jnp.where(lane == j * EXPERTS_PER_GROUP + e, w_rows, 0.0), axis=1, keepdims=True) for e in experts]
    a = [_dot(h, wg_ref[e]) for e in experts]
    u = [_dot(h, wu_ref[e]) for e in experts]
    act = [(_silu(a[e]) * u[e] * w_e[e]).astype(BF16) for e in experts]
    out = _dot(act[0], wd_ref[0])
    for e in experts[1:]:
        out = out + _dot(act[e], wd_ref[e])
    return out


MOE_STEP_CHUNKS = 8


def _moe_sorted_kernel(group_ref, used_ref, fresh_ref, src_ref, *refs):
    n = MOE_STEP_CHUNKS
    hs_refs, cs_refs = refs[:n], refs[n:2 * n]
    wg_ref, wu_ref, wd_ref, ys_ref, wg_bf, wu_bf, wd_bf = refs[2 * n:]
    s = pl.program_id(0)

    @pl.when(fresh_ref[s] > 0)
    def _():
        wg_bf[...] = wg_ref[...].astype(BF16)
        wu_bf[...] = wu_ref[...].astype(BF16)
        wd_bf[...] = wd_ref[...].astype(BF16)

    @pl.when(used_ref[s] > 0)
    def _():
        h = jnp.concatenate([r[0] for r in hs_refs], axis=0)
        w_rows = jnp.concatenate([r[0] for r in cs_refs], axis=0)
        y = _swiglu_group(h, w_rows, group_ref[s], wg_bf, wu_bf, wd_bf).astype(BF16)
        for k in range(n):
            ys_ref[k] = y[k * MOE_CHUNK:(k + 1) * MOE_CHUNK]

    @pl.when(used_ref[s] == 0)
    def _():
        ys_ref[...] = jnp.zeros(ys_ref.shape, ys_ref.dtype)


def _moe_sorted(hs, cs, step_group, step_used, step_fresh, chunk_src, wg, wu, wd, layer):
    _, _, d = hs.shape
    n = MOE_STEP_CHUNKS
    steps = step_group.shape[0]
    epg = EXPERTS_PER_GROUP
    chunk = lambda k, width: pl.BlockSpec((1, MOE_CHUNK, width),
                                          lambda s, grp, used, fresh, src: (src[s * n + k], 0, 0))
    wmap = lambda s, grp, used, fresh, src: (layer * N_GROUPS + grp[s], 0, 0)
    return pl.pallas_call(
        _moe_sorted_kernel,
        out_shape=jax.ShapeDtypeStruct((steps * n, MOE_CHUNK, d), BF16),
        grid_spec=pltpu.PrefetchScalarGridSpec(
            num_scalar_prefetch=4,
            grid=(steps,),
            in_specs=[chunk(k, d) for k in range(n)] + [chunk(k, LANES) for k in range(n)]
                     + [pl.BlockSpec((epg, d, D_EXPERT), wmap),
                        pl.BlockSpec((epg, d, D_EXPERT), wmap),
                        pl.BlockSpec((epg, D_EXPERT, d), wmap)],
            out_specs=pl.BlockSpec((n, MOE_CHUNK, d), lambda s, grp, used, fresh, src: (s, 0, 0)),
            scratch_shapes=[pltpu.VMEM((epg, d, D_EXPERT), BF16), pltpu.VMEM((epg, d, D_EXPERT), BF16),
                            pltpu.VMEM((epg, D_EXPERT, d), BF16)]),
        compiler_params=_params("arbitrary"),
        name="moe_sorted_experts",
    )(step_group, step_used, step_fresh, chunk_src, *([hs] * n), *([cs] * n), wg, wu, wd)


def _moe_unpermute(ys_refs, cmb_ref, x1_ref, g2_ref):
    cmb = cmb_ref[0]
    lane = lax.broadcasted_iota(jnp.int32, cmb.shape, 1)
    slot = jnp.sum(jnp.where(lane == SLOT_LANE, cmb, 0.0), axis=1, keepdims=True)
    rows = lax.broadcasted_iota(jnp.int32, (cmb.shape[0], len(ys_refs) * MOE_CHUNK), 1).astype(F32)
    scatter = jnp.where(slot == rows, 1.0, 0.0).astype(BF16)
    ys = jnp.concatenate([r[0] for r in ys_refs], axis=0)
    return x1_ref[0] + g2_ref[0] * _dot(scatter, ys)


def _moe_combine_kernel(pos_ref, *refs):
    x, (o_ref,) = _load_tokens(refs, len(refs) - 4)
    o_ref[0] = x


def _moe_combine(pending):
    b, n, d = pending.x1.shape
    tm = min(512, n)
    nt = n // tm
    n_chunks, pos, x_ops, x_specs = _token_source(pending, tm, nt, d)
    return pl.pallas_call(
        _moe_combine_kernel,
        out_shape=jax.ShapeDtypeStruct((b, n, d), F32),
        grid_spec=pltpu.PrefetchScalarGridSpec(
            num_scalar_prefetch=1,
            grid=(b, nt),
            in_specs=x_specs,
            out_specs=pl.BlockSpec((1, tm, d), lambda i, t, pos: (i, t, 0))),
        compiler_params=_params("parallel", "parallel"),
        name="moe_combine",
    )(pos, *x_ops)


def _moe_chunk_schedule(cmb):
    b, n, _ = cmb.shape
    tm = min(512, n)
    nt = b * (n // tm)
    ntc, nsc = _tile_chunks(tm), MOE_STEP_CHUNKS
    steps = nt * ntc // nsc + N_GROUPS
    i32 = jnp.int32
    cnt = cmb[:, ::tm, CHUNKS_LANE:CHUNKS_LANE + N_GROUPS].reshape(nt, N_GROUPS).astype(i32)
    gi = jnp.arange(N_GROUPS, dtype=i32)
    ti = jnp.arange(nt, dtype=i32)
    earlier_g = (gi[:, None] < gi[None, :]).astype(i32)
    in_tile = jnp.sum(cnt[:, :, None] * earlier_g[None], axis=1)
    before = jnp.sum(cnt[:, None, :] * (ti[:, None] < ti[None, :]).astype(i32)[:, :, None], axis=0)
    total = jnp.sum(cnt, axis=0)
    padded = (total + nsc - 1) // nsc * nsc
    gstart = jnp.sum(padded[:, None] * earlier_g, axis=0)
    gend = gstart + padded
    c = jnp.arange(tm // MOE_CHUNK, dtype=i32)
    dst = gstart[None, :, None] + before[:, :, None] + c[None, None, :]
    src = (ti * ntc)[:, None, None] + in_tile[:, :, None] + c[None, None, :]
    dst = jnp.where(c[None, None, :] < cnt[:, :, None], dst, -1).reshape(-1)
    p = jnp.arange(steps * nsc, dtype=i32)
    chunk_src = jnp.sum(jnp.where(dst[None, :] == p[:, None], src.reshape(-1)[None, :], 0), axis=1)
    first_chunk = jnp.arange(steps, dtype=i32) * nsc
    step_group = jnp.sum((first_chunk[:, None] >= gend[None, :-1]).astype(i32), axis=1)
    prev_group = jnp.sum((first_chunk[:, None] - nsc >= gend[None, :-1]).astype(i32), axis=1)
    step_used = (first_chunk < gend[-1]).astype(i32)
    step_fresh = jnp.logical_or(first_chunk == 0, step_group != prev_group).astype(i32)
    k = jnp.arange(ntc, dtype=i32)
    ends = in_tile + cnt
    grp_k = jnp.minimum(jnp.sum((k[None, :, None] >= ends[:, None, :]).astype(i32), axis=2), N_GROUPS - 1)
    base = gstart[None, :] + before - in_tile
    pos = k[None, :] + jnp.sum(jnp.where(grp_k[:, :, None] == gi[None, None, :], base[:, None, :], 0), axis=2)
    chunk_pos = jnp.where(k[None, :] < ends[:, -1:], pos, 0).reshape(-1).astype(i32)
    return step_group, step_used, step_fresh, chunk_src, chunk_pos


def _moe(hs, cs, cmb, x, g2, wg, wu, wd, layer):
    step_group, step_used, step_fresh, chunk_src, chunk_pos = _moe_chunk_schedule(cmb)
    ys = _moe_sorted(hs, cs, step_group, step_used, step_fresh, chunk_src, wg, wu, wd, layer)
    return _PendingMoe(ys, chunk_pos, cmb, x, g2)


def kernel(x, c, ctx, c_ctx, ada_w, ada_b, norm1_g, norm2_g, w_in_even, qn_g, kn_g, na_rpb, sc_conv_w, w_in_odd, hy_short_w, hy_w1, hy_b1, hy_w2, hy_b2, hy_w3, hy_freq, hy_bias, cf_conv_w, cf_conv_b, cf_ln_g, cf_ln_b, w_out, moe_w_group, moe_b_group, moe_w_router, moe_b_router, moe_w_gate, moe_w_up, moe_w_down):
    depth = ada_w.shape[0]
    bsz, seq, d = x.shape
    lc = ctx.shape[1]
    assert 2 * seq == FFT_N and d == D_MODEL and bsz % 2 == 0

    mods = _ada_modulation(jnp.concatenate([c, c_ctx[None, :]], axis=0), ada_w, ada_b)
    seg = jnp.asarray(np.kron(np.eye(NA_HEADS), np.ones((HEAD_DIM, HEAD_DIM))), dtype=BF16)
    consts = _dft_constants()
    wg = moe_w_gate.reshape((-1,) + moe_w_gate.shape[2:])
    wu = moe_w_up.reshape((-1,) + moe_w_up.shape[2:])
    wd = moe_w_down.reshape((-1,) + moe_w_down.shape[2:])

    for l in range(depth):
        ctx_needed = any(j % 2 == 0 for j in range(l + 1, depth))
        lat_mod = [m[:, None, :] for m in jnp.split(mods[l, :bsz], 6, axis=-1)]
        ctx_mod = [jnp.broadcast_to(m[None, :, :], (bsz, 1, d)) for m in jnp.split(mods[l, bsz:bsz + 1], 6, axis=-1)]
        sh1, sc1, g1, sh2, sc2, g2 = lat_mod
        csh1, csc1, cg1, csh2, csc2, cg2 = ctx_mod
        n1g = norm1_g[l][None, :]
        n2g = norm2_g[l][None, :]
        w_out_bf = w_out[l].astype(BF16)
        wr = jnp.concatenate([jnp.repeat(moe_w_group[l], EXPERTS_PER_GROUP, axis=1), moe_w_router[l],
                              jnp.zeros((d, LANES - 2 * N_EXPERTS), F32)], axis=1)
        br = jnp.concatenate([jnp.repeat(moe_b_group[l], EXPERTS_PER_GROUP), moe_b_router[l],
                              jnp.zeros((LANES - 2 * N_EXPERTS,), F32)])[None, :]
        wrh = wr.astype(BF16)
        wr2 = jnp.concatenate([wrh, (wr - wrh.astype(F32)).astype(BF16)], axis=1)

        if l % 2 == 0:
            e = l // 2
            w_in = w_in_even[e].astype(BF16)
            qg = jnp.tile(qn_g[e], NA_HEADS)[None, :]
            kg = jnp.tile(kn_g[e], NA_HEADS)[None, :]
            x, ql, kl, vl, gbl, pl_ = _inproj_even(x, n1g, sc1, sh1, w_in, qg, kg, seg)
            ctx, qc, kc, vc, gbc, pc = _inproj_even(ctx, n1g, csc1, csh1, w_in, qg, kg, seg)
            bias = _bias_table(na_rpb[e])
            y1 = _natten(ql, kl, vl, kc, vc, bias)
            y2 = (gbl, pl_, sc_conv_w[e])
            lat_cm = (False, False)
            if ctx_needed:
                y1c = _ctx_attention(qc, kc, vc)
                y2c = (gbc, pc, sc_conv_w[e])
        else:
            o = l // 2
            wht = w_in_odd[o][:, :3 * HY_WIDTH].T.astype(BF16)
            wag = w_in_odd[o][:, 3 * HY_WIDTH:].astype(BF16)
            x, hyt, ag = _inproj_odd(x, n1g, sc1, sh1, wht, wag)
            taps = _hyena_taps(seq, hy_w1[o], hy_b1[o], hy_w2[o], hy_b2[o], hy_w3[o], hy_freq[o])
            h_spec = _filter_fft(taps, consts)
            y1 = _hyena_latent(hyt, hy_short_w[o], h_spec, hy_bias[o], consts)
            cf_args = (cf_conv_w[o], cf_conv_b[o], cf_ln_g[o], cf_ln_b[o])
            y2 = _conformer(ag, *cf_args)
            lat_cm = (True, False)
            if ctx_needed:
                ctx, hytc, agc = _inproj_odd(ctx, n1g, csc1, csh1, wht, wag)
                taps_c = _hyena_taps(lc, hy_w1[o], hy_b1[o], hy_w2[o], hy_b2[o], hy_w3[o], hy_freq[o])
                y1c = _hyena_context(hytc, hy_short_w[o], taps_c, hy_bias[o])
                y2c = _conformer(agc, *cf_args)

        x1, cmb, hs, cs = _outproj(y1, y2, w_out_bf, x, g1, n2g, sc2, sh2, wr2, br, lat_cm)
        x = _moe(hs, cs, cmb, x1, g2, wg, wu, wd, l)
        if ctx_needed:
            c1, cmbc, hsc, csc = _outproj(y1c, y2c, w_out_bf, ctx, cg1, n2g, csc2, csh2, wr2, br, lat_cm)
            ctx = _moe(hsc, csc, cmbc, c1, cg2, wg, wu, wd, l)
    return _moe_combine(x)
```

```python
import functools
import math
from typing import NamedTuple

import numpy as np
import jax
import jax.numpy as jnp
from jax import lax
from jax.experimental import pallas as pl
from jax.experimental.pallas import tpu as pltpu

F32 = jnp.float32
BF16 = jnp.bfloat16

D_MODEL = 1024
GRID_W = 64
NA_HEADS = 8
HEAD_DIM = 64
NA_WIDTH = 512
NA_WIN_H = 8
NA_WIN_W = 16
SC_WIDTH = 512
HY_WIDTH = 512
HY_BANDS = 16
HY_EMB = 1 + 2 * HY_BANDS
HY_FFN = 64
HY_MAX_DECAY = math.log(1e-2) / 0.3
HY_MIN_DECAY = math.log(1e-2) / 1.5
CF_WIDTH = 512
CF_TAPS = 31
N_GROUPS = 4
EXPERTS_PER_GROUP = 4
N_EXPERTS = 16
D_EXPERT = 256
RMS_EPS = 1e-6
LN_EPS = 1e-5
NEG_INF = -1e30

VMEM_LIMIT_BYTES = 56 * 1024 * 1024
LANES = 128

FFT_NA = 64
FFT_NB = 128
FFT_N = FFT_NA * FFT_NB


def _params(*sem):
    return pltpu.CompilerParams(dimension_semantics=tuple(sem), vmem_limit_bytes=VMEM_LIMIT_BYTES)


def _dot(a, b):
    return jnp.dot(a, b, preferred_element_type=F32)


def _dot_nt(a, b):
    return lax.dot_general(a, b, (((1,), (1,)), ((), ())), preferred_element_type=F32)


def _dot_tn(a, b):
    return lax.dot_general(a, b, (((0,), (0,)), ((), ())), preferred_element_type=F32)


def _dot_f32(a, b):
    return jnp.dot(a, b, preferred_element_type=F32, precision=lax.Precision.HIGHEST)


def _silu(x):
    return x * jax.nn.sigmoid(x)


def _ada_kernel(ct_ref, w_ref, b_ref, o_ref, *, n_cond):
    ct = ct_ref[...]
    s = _silu(ct)
    w = w_ref[0]
    rows = [jnp.sum(w * s[:, r:r + 1], axis=0, keepdims=True) for r in range(n_cond)]
    rows.append(jnp.zeros((8 - n_cond, w.shape[1]), F32))
    o_ref[0] = jnp.concatenate(rows, axis=0) + b_ref[0]


def _ada_modulation(cond, ada_w, ada_b):
    n_cond, d = cond.shape
    depth, _, n6 = ada_w.shape
    tn = 1536
    ct = jnp.zeros((d, 8), F32).at[:, :n_cond].set(cond.T)
    return pl.pallas_call(
        functools.partial(_ada_kernel, n_cond=n_cond),
        out_shape=jax.ShapeDtypeStruct((depth, 8, n6), F32),
        grid=(depth, n6 // tn),
        in_specs=[pl.BlockSpec((d, 8), lambda l, j: (0, 0)),
                  pl.BlockSpec((1, d, tn), lambda l, j: (l, 0, j)),
                  pl.BlockSpec((1, 1, tn), lambda l, j: (l, 0, j))],
        out_specs=pl.BlockSpec((1, 8, tn), lambda l, j: (l, 0, j)),
        compiler_params=_params("parallel", "parallel"),
        name="ada_modulation",
    )(ct, ada_w, ada_b.reshape(depth, 1, n6))


def _norm_modulate(x, g, sc, sh):
    ms = jnp.mean(x * x, axis=-1, keepdims=True)
    return x * lax.rsqrt(ms + RMS_EPS) * g * (1.0 + sc) + sh


def _head_rmsnorm(t, seg, gain):
    ss = _dot((t * t).astype(BF16), seg)
    return t * lax.rsqrt(ss * (1.0 / HEAD_DIM) + RMS_EPS) * gain


class _PendingMoe(NamedTuple):
    ys: jax.Array
    chunk_pos: jax.Array
    cmb: jax.Array
    x1: jax.Array
    g2: jax.Array


def _token_source(xsrc, tm, nt, d):
    tok = lambda i, j, pos: (i, j, 0)
    if not isinstance(xsrc, _PendingMoe):
        return 0, jnp.zeros((1,), jnp.int32), [xsrc], [pl.BlockSpec((1, tm, d), tok)]
    ntc = _tile_chunks(tm)
    chunk = lambda k: pl.BlockSpec((1, MOE_CHUNK, d), lambda i, j, pos: (pos[(i * nt + j) * ntc + k], 0, 0))
    specs = [chunk(k) for k in range(ntc)] + [pl.BlockSpec((1, tm, LANES), tok), pl.BlockSpec((1, tm, d), tok),
                                              pl.BlockSpec((1, 1, d), lambda i, j, pos: (i, 0, 0))]
    return ntc, xsrc.chunk_pos, [xsrc.ys] * ntc + [xsrc.cmb, xsrc.x1, xsrc.g2], specs


def _load_tokens(refs, n_chunks):
    if n_chunks == 0:
        return refs[0][0], refs[1:]
    cmb_ref, x1_ref, g2_ref = refs[n_chunks:n_chunks + 3]
    return _moe_unpermute(refs[:n_chunks], cmb_ref, x1_ref, g2_ref), refs[n_chunks + 3:]


def _inproj_even_kernel(pos_ref, *refs, n_chunks):
    x, refs = _load_tokens(refs, n_chunks)
    g_ref, sc_ref, sh_ref, w_ref, qg_ref, kg_ref, seg_ref = refs[:7]
    outs = refs[7:]
    if n_chunks:
        outs[0][0] = x
        outs = outs[1:]
    q_ref, k_ref, v_ref, gb_ref, p_ref = outs
    u = _norm_modulate(x, g_ref[...], sc_ref[0], sh_ref[0]).astype(BF16)
    seg = seg_ref[...]
    w = NA_WIDTH
    q = _dot(u, w_ref[:, 0 * w:1 * w])
    q_ref[0] = (_head_rmsnorm(q, seg, qg_ref[...]) * (HEAD_DIM ** -0.5)).astype(BF16)
    k = _dot(u, w_ref[:, 1 * w:2 * w])
    k_ref[0] = _head_rmsnorm(k, seg, kg_ref[...]).astype(BF16)
    v_ref[0] = _dot(u, w_ref[:, 2 * w:3 * w]).astype(BF16)
    gb_ref[0] = _dot(u, w_ref[:, 3 * w:4 * w]).astype(BF16)
    gc = _dot(u, w_ref[:, 4 * w:5 * w])
    hv = _dot(u, w_ref[:, 5 * w:6 * w])
    p_ref[0] = (gc * hv).astype(BF16)


def _inproj_even(xsrc, g, sc, sh, w_bf, qg, kg, seg):
    b, n, d = (xsrc.x1 if isinstance(xsrc, _PendingMoe) else xsrc).shape
    tm = min(512, n)
    nt = n // tm
    n_chunks, pos, x_ops, x_specs = _token_source(xsrc, tm, nt, d)
    tok = lambda i, j, pos: (i, j, 0)
    mod = lambda i, j, pos: (i, 0, 0)
    const = lambda i, j, pos: (0, 0)
    out = jax.ShapeDtypeStruct((b, n, NA_WIDTH), BF16)
    x_out = [jax.ShapeDtypeStruct((b, n, d), F32)] if n_chunks else []
    x_out_spec = [pl.BlockSpec((1, tm, d), tok)] if n_chunks else []
    res = pl.pallas_call(
        functools.partial(_inproj_even_kernel, n_chunks=n_chunks),
        out_shape=tuple(x_out) + (out,) * 5,
        grid_spec=pltpu.PrefetchScalarGridSpec(
            num_scalar_prefetch=1,
            grid=(b, nt),
            in_specs=x_specs + [pl.BlockSpec((1, d), const),
                                pl.BlockSpec((1, 1, d), mod),
                                pl.BlockSpec((1, 1, d), mod),
                                pl.BlockSpec(w_bf.shape, const),
                                pl.BlockSpec((1, NA_WIDTH), const),
                                pl.BlockSpec((1, NA_WIDTH), const),
                                pl.BlockSpec((NA_WIDTH, NA_WIDTH), const)],
            out_specs=tuple(x_out_spec) + (pl.BlockSpec((1, tm, NA_WIDTH), tok),) * 5),
        compiler_params=_params("parallel", "parallel"),
        name="inproj_even",
    )(pos, *x_ops, g, sc, sh, w_bf, qg, kg, seg)
    return tuple(res) if n_chunks else (xsrc,) + tuple(res)


def _bias_kernel(rpb_ref, o_ref):
    h = pl.program_id(0)
    qi = lax.broadcasted_iota(jnp.int32, (GRID_W, GRID_W), 0)
    ki = lax.broadcasted_iota(jnp.int32, (GRID_W, GRID_W), 1)
    start = jnp.clip(qi - NA_WIN_W // 2, 0, GRID_W - NA_WIN_W)
    valid = jnp.logical_and(ki >= start, ki < start + NA_WIN_W)
    cidx = jnp.clip(ki - qi, -(NA_WIN_W - 1), NA_WIN_W - 1) + (NA_WIN_W - 1)
    n_dr = 2 * NA_WIN_H - 1
    n_dc = 2 * NA_WIN_W - 1

    def body(j, accs):
        m = cidx == j
        return tuple(jnp.where(m, rpb_ref[(h * n_dr + d) * n_dc + j], a) for d, a in enumerate(accs))

    accs = lax.fori_loop(0, n_dc, body, tuple(jnp.zeros((GRID_W, GRID_W), F32) for _ in range(n_dr)))
    tiles = [jnp.where(valid, a, NEG_INF) for a in accs]
    for d0 in range(NA_WIN_H):
        o_ref[0, d0] = jnp.concatenate(tiles[d0:d0 + NA_WIN_H], axis=1)


def _bias_table(rpb):
    return pl.pallas_call(
        _bias_kernel,
        out_shape=jax.ShapeDtypeStruct((NA_HEADS, NA_WIN_H, GRID_W, NA_WIN_H * GRID_W), F32),
        grid=(NA_HEADS,),
        in_specs=[pl.BlockSpec(memory_space=pltpu.SMEM)],
        out_specs=pl.BlockSpec((1, NA_WIN_H, GRID_W, NA_WIN_H * GRID_W), lambda h: (h, 0, 0, 0)),
        compiler_params=_params("arbitrary"),
        name="rpb_bias_table",
    )(rpb.reshape(-1))


def _pair_attention(q2, kw, vw, bias, kc, vc, first_half):
    s_c = _dot_nt(q2, kc)
    m = jnp.max(s_c, axis=-1, keepdims=True)
    if kw is not None:
        s_w = _dot_nt(q2, kw) + bias
        m = jnp.maximum(m, jnp.max(s_w, axis=-1, keepdims=True))
        p_w = jnp.exp(s_w - m)
    p_c = jnp.exp(s_c - m)
    den = jnp.sum(p_c, axis=-1, keepdims=True)
    o = _dot(p_c.astype(BF16), vc)
    if kw is not None:
        den = den + jnp.sum(p_w, axis=-1, keepdims=True)
        o = o + _dot(p_w.astype(BF16), vw)
    o = o / den
    half = o.shape[0] // 2
    return jnp.where(first_half, o[:half], o[half:])


def _stack_heads(qp, first_half):
    zero = jnp.zeros_like(qp)
    return jnp.concatenate([jnp.where(first_half, qp, zero), jnp.where(first_half, zero, qp)], axis=0)


def _natten_kernel(q_ref, k_ref, v_ref, kc_ref, vc_ref, bias_ref, o_ref, sc_ref, ow_ref, pc_ref,
                   *, rows_per_step, n_rows):
    blk = pl.program_id(1)
    lane = lax.broadcasted_iota(jnp.int32, (GRID_W, LANES), 1)
    first_half = lane < HEAD_DIM
    band = NA_WIN_H * GRID_W
    n_pairs = NA_HEADS // 2
    stacked = 2 * GRID_W
    cols = [slice(hp * LANES, (hp + 1) * LANES) for hp in range(n_pairs)]
    lane3 = lax.broadcasted_iota(jnp.int32, (rows_per_step, GRID_W, LANES), 2)

    for hp in range(n_pairs):
        q3 = q_ref[0, :, cols[hp]].reshape(rows_per_step, GRID_W, LANES)
        zero = jnp.zeros_like(q3)
        q_all = jnp.concatenate([jnp.where(lane3 < HEAD_DIM, q3, zero), jnp.where(lane3 < HEAD_DIM, zero, q3)], axis=1)
        sc_ref[hp] = _dot_nt(q_all.reshape(rows_per_step * stacked, LANES), kc_ref[0, :, cols[hp]])

    def row_body(j, carry):
        r = blk * rows_per_step + j
        start = jnp.clip(r - NA_WIN_H // 2, 0, n_rows - NA_WIN_H)
        d0 = start - r + (NA_WIN_H - 1)
        koff = pl.multiple_of(start * GRID_W, GRID_W)
        qoff = pl.multiple_of(j * GRID_W, GRID_W)
        soff = pl.multiple_of(j * stacked, stacked)
        q2 = [_stack_heads(q_ref[0, pl.ds(qoff, GRID_W), cs], first_half) for cs in cols]
        s_w = [_dot_nt(q2[hp], k_ref[0, pl.ds(koff, band), cols[hp]])
               + jnp.concatenate([bias_ref[2 * hp, d0], bias_ref[2 * hp + 1, d0]], axis=0) for hp in range(n_pairs)]
        s_c = [sc_ref[hp, pl.ds(soff, stacked), :] for hp in range(n_pairs)]
        m = [jnp.maximum(jnp.max(s_w[hp], axis=-1, keepdims=True), jnp.max(s_c[hp], axis=-1, keepdims=True))
             for hp in range(n_pairs)]
        p_w = [jnp.exp(s_w[hp] - m[hp]) for hp in range(n_pairs)]
        p_c = [jnp.exp(s_c[hp] - m[hp]) for hp in range(n_pairs)]
        rden = [1.0 / (jnp.sum(p_w[hp], axis=-1, keepdims=True) + jnp.sum(p_c[hp], axis=-1, keepdims=True))
                for hp in range(n_pairs)]
        for hp in range(n_pairs):
            ow_ref[hp, pl.ds(soff, stacked), :] = _dot(p_w[hp].astype(BF16),
                                                       v_ref[0, pl.ds(koff, band), cols[hp]]) * rden[hp]
            pc_ref[hp, pl.ds(soff, stacked), :] = (p_c[hp] * rden[hp]).astype(BF16)
        return carry

    lax.fori_loop(0, rows_per_step, row_body, 0, unroll=8)

    for hp in range(n_pairs):
        o = ow_ref[hp] + _dot(pc_ref[hp], vc_ref[0, :, cols[hp]])
        o = o.reshape(rows_per_step, stacked, LANES)
        o = jnp.where(lane3 < HEAD_DIM, o[:, :GRID_W, :], o[:, GRID_W:, :])
        o_ref[0, :, cols[hp]] = o.reshape(rows_per_step * GRID_W, LANES).astype(o_ref.dtype)


def _natten(q, k, v, kc, vc, bias):
    b, n, w = q.shape
    n_rows = n // GRID_W
    rows_per_step = 8
    tq = rows_per_step * GRID_W
    lc = kc.shape[1]
    return pl.pallas_call(
        functools.partial(_natten_kernel, rows_per_step=rows_per_step, n_rows=n_rows),
        out_shape=jax.ShapeDtypeStruct((b, n, w), BF16),
        grid=(b, n_rows // rows_per_step),
        in_specs=[pl.BlockSpec((1, tq, w), lambda i, j: (i, j, 0)),
                  pl.BlockSpec((1, n, w), lambda i, j: (i, 0, 0)),
                  pl.BlockSpec((1, n, w), lambda i, j: (i, 0, 0)),
                  pl.BlockSpec((1, lc, w), lambda i, j: (i, 0, 0)),
                  pl.BlockSpec((1, lc, w), lambda i, j: (i, 0, 0)),
                  pl.BlockSpec(bias.shape, lambda i, j: (0, 0, 0, 0))],
        out_specs=pl.BlockSpec((1, tq, w), lambda i, j: (i, j, 0)),
        scratch_shapes=[pltpu.VMEM((NA_HEADS // 2, 2 * tq, lc), F32),
                        pltpu.VMEM((NA_HEADS // 2, 2 * tq, LANES), F32),
                        pltpu.VMEM((NA_HEADS // 2, 2 * tq, lc), BF16)],
        compiler_params=_params("parallel", "arbitrary"),
        name="neighbourhood_attention",
    )(q, k, v, kc, vc, bias)


def _ctx_attn_kernel(q_ref, k_ref, v_ref, o_ref):
    lc = q_ref.shape[1]
    lane = lax.broadcasted_iota(jnp.int32, (lc, LANES), 1)
    first_half = lane < HEAD_DIM
    outs = []
    for hp in range(NA_HEADS // 2):
        cs = slice(hp * LANES, (hp + 1) * LANES)
        q2 = _stack_heads(q_ref[0, :, cs], first_half)
        outs.append(_pair_attention(q2, None, None, None, k_ref[0, :, cs], v_ref[0, :, cs], first_half))
    o_ref[0] = jnp.concatenate(outs, axis=1).astype(o_ref.dtype)


def _ctx_attention(q, k, v):
    b, lc, w = q.shape
    spec = pl.BlockSpec((1, lc, w), lambda i: (i, 0, 0))
    return pl.pallas_call(
        _ctx_attn_kernel,
        out_shape=jax.ShapeDtypeStruct((b, lc, w), BF16),
        grid=(b,),
        in_specs=[spec, spec, spec],
        out_specs=spec,
        compiler_params=_params("parallel"),
        name="context_attention",
    )(q, k, v)


def _inproj_odd_kernel(pos_ref, *refs, n_chunks):
    x, refs = _load_tokens(refs, n_chunks)
    g_ref, sc_ref, sh_ref, wht_ref, wag_ref = refs[:5]
    outs = refs[5:]
    if n_chunks:
        outs[0][0] = x
        outs = outs[1:]
    hy_ref, ag_ref = outs
    u = _norm_modulate(x, g_ref[...], sc_ref[0], sh_ref[0]).astype(BF16)
    hy_ref[0] = _dot_nt(wht_ref[...], u).astype(BF16)
    a = _dot(u, wag_ref[:, :CF_WIDTH])
    g = _dot(u, wag_ref[:, CF_WIDTH:])
    ag_ref[0] = (a * jax.nn.sigmoid(g)).astype(BF16)


def _inproj_odd(xsrc, g, sc, sh, wht_bf, wag_bf):
    b, n, d = (xsrc.x1 if isinstance(xsrc, _PendingMoe) else xsrc).shape
    tm = min(512, n)
    nt = n // tm
    hw = wht_bf.shape[0]
    n_chunks, pos, x_ops, x_specs = _token_source(xsrc, tm, nt, d)
    tok = lambda i, j, pos: (i, j, 0)
    mod = lambda i, j, pos: (i, 0, 0)
    const = lambda i, j, pos: (0, 0)
    x_out = [jax.ShapeDtypeStruct((b, n, d), F32)] if n_chunks else []
    x_out_spec = [pl.BlockSpec((1, tm, d), tok)] if n_chunks else []
    res = pl.pallas_call(
        functools.partial(_inproj_odd_kernel, n_chunks=n_chunks),
        out_shape=tuple(x_out) + (jax.ShapeDtypeStruct((b, hw, n), BF16),
                                  jax.ShapeDtypeStruct((b, n, CF_WIDTH), BF16)),
        grid_spec=pltpu.PrefetchScalarGridSpec(
            num_scalar_prefetch=1,
            grid=(b, nt),
            in_specs=x_specs + [pl.BlockSpec((1, d), const),
                                pl.BlockSpec((1, 1, d), mod),
                                pl.BlockSpec((1, 1, d), mod),
                                pl.BlockSpec(wht_bf.shape, const),
                                pl.BlockSpec(wag_bf.shape, const)],
            out_specs=tuple(x_out_spec) + (pl.BlockSpec((1, hw, tm), lambda i, j, pos: (i, 0, j)),
                                           pl.BlockSpec((1, tm, CF_WIDTH), tok))),
        compiler_params=_params("parallel", "parallel"),
        name="inproj_odd",
    )(pos, *x_ops, g, sc, sh, wht_bf, wag_bf)
    return tuple(res) if n_chunks else (xsrc,) + tuple(res)


SUBLANES = 8
CF_PAD = 2 * SUBLANES
CF_ROWS = 256


def _conformer_kernel(ag_ref, w_ref, cb_ref, lg_ref, lb_ref, o_ref, pad_ref, *, seq):
    zeros = jnp.zeros((CF_PAD, CF_WIDTH), F32)
    pad_ref[0:CF_PAD, :] = zeros
    pad_ref[CF_PAD + seq:2 * CF_PAD + seq, :] = zeros
    pad_ref[CF_PAD:CF_PAD + seq, :] = ag_ref[0].astype(F32)
    shift0 = CF_PAD - CF_TAPS // 2
    n_groups = (shift0 + CF_TAPS - 1) // SUBLANES + 1

    def conv_rows(i, carry):
        n0 = pl.multiple_of(i * CF_ROWS, CF_ROWS)
        wins = [pad_ref[pl.ds(n0 + SUBLANES * a, CF_ROWS + SUBLANES), :] for a in range(n_groups)]
        acc = None
        for b in range(SUBLANES):
            part = None
            for a in range(n_groups):
                j = SUBLANES * a + b - shift0
                if 0 <= j < CF_TAPS:
                    term = w_ref[j:j + 1, :] * wins[a]
                    part = term if part is None else part + term
            part = part[b:b + CF_ROWS, :]
            acc = part if acc is None else acc + part
        y = acc + cb_ref[...]
        mu = jnp.mean(y, axis=-1, keepdims=True)
        yc = y - mu
        var = jnp.mean(yc * yc, axis=-1, keepdims=True)
        z = yc * lax.rsqrt(var + LN_EPS) * lg_ref[...] + lb_ref[...]
        o_ref[0, pl.ds(n0, CF_ROWS), :] = _silu(z).astype(o_ref.dtype)
        return carry

    lax.fori_loop(0, seq // CF_ROWS, conv_rows, 0)


def _conformer(ag, w, cb, lg, lb):
    b, n, c = ag.shape
    spec = pl.BlockSpec((1, n, c), lambda i: (i, 0, 0))
    vec = pl.BlockSpec((1, c), lambda i: (0, 0))
    return pl.pallas_call(
        functools.partial(_conformer_kernel, seq=n),
        out_shape=jax.ShapeDtypeStruct((b, n, c), BF16),
        grid=(b,),
        in_specs=[spec, pl.BlockSpec((CF_TAPS, c), lambda i: (0, 0)), vec, vec, vec],
        out_specs=spec,
        scratch_shapes=[pltpu.VMEM((n + 2 * CF_PAD, c), F32)],
        compiler_params=_params("parallel"),
        name="conformer_conv",
    )(ag, w, cb[None, :], lg[None, :], lb[None, :])


def _hyena_features(length):
    t = np.linspace(0.0, 1.0, length, dtype=np.float32)
    w = (2.0 * math.pi * np.arange(length, dtype=np.float32) / length).astype(np.float32)
    bands = np.linspace(1e-4, HY_BANDS - 1, HY_BANDS, dtype=np.float32)
    ang = (bands[:, None] * w[None, :]).astype(np.float32)
    zt = np.concatenate([t[None, :], np.cos(ang), -np.sin(ang)], axis=0).astype(np.float32)
    deltas = np.abs(np.linspace(HY_MIN_DECAY, HY_MAX_DECAY, HY_WIDTH, dtype=np.float32))
    rev = (length - np.arange(length)) % length
    zt2 = np.concatenate([zt, zt[:, rev]], axis=1)
    t2 = np.concatenate([t, t[rev]])[None, :]
    return zt2, t2, deltas[:, None]


def _taps_kernel(zt_ref, t_ref, dl_ref, w1t_ref, b1_ref, f0_ref, w2t_ref, b2_ref, f1_ref, w3t_ref,
                 o_ref, hid_ref):
    first = jnp.logical_and(pl.program_id(0) == 0, pl.program_id(1) == 0)
    length = t_ref.shape[1] // 2

    @pl.when(first)
    def _():
        h1 = jnp.sin(f0_ref[...] * (_dot_f32(w1t_ref[...], zt_ref[...]) + b1_ref[...]))
        hid_ref[...] = jnp.sin(f1_ref[...] * (_dot_f32(w2t_ref[...], h1) + b2_ref[...]))

    decay = jnp.exp(-(dl_ref[...] * t_ref[...]))
    fwd = _dot_f32(w3t_ref[0, 0], hid_ref[:, :length])
    bwd = _dot_f32(w3t_ref[0, 1], hid_ref[:, length:])
    taps = jnp.concatenate([fwd, bwd], axis=1) * decay
    nrm = jnp.sum(jnp.abs(taps), axis=-1, keepdims=True)
    lane = lax.broadcasted_iota(jnp.int32, taps.shape, 1)
    o_ref[0] = jnp.where(lane == length, 0.0, taps / nrm)


def _hyena_taps(length, w1, b1, w2, b2, w3, freq):
    zt2, t2, deltas = _hyena_features(length)
    cb = 128
    w3t = w3.T.reshape(2, 2, HY_WIDTH, HY_FFN)
    col = lambda v: v.reshape(HY_FFN, 1)
    const = lambda o, j: (0, 0)
    return pl.pallas_call(
        _taps_kernel,
        out_shape=jax.ShapeDtypeStruct((2, HY_WIDTH, 2 * length), F32),
        grid=(2, HY_WIDTH // cb),
        in_specs=[pl.BlockSpec((HY_EMB, 2 * length), const),
                  pl.BlockSpec((1, 2 * length), const),
                  pl.BlockSpec((cb, 1), lambda o, j: (j, 0)),
                  pl.BlockSpec((HY_FFN, HY_EMB), const),
                  pl.BlockSpec((HY_FFN, 1), const),
                  pl.BlockSpec((HY_FFN, 1), const),
                  pl.BlockSpec((HY_FFN, HY_FFN), const),
                  pl.BlockSpec((HY_FFN, 1), const),
                  pl.BlockSpec((HY_FFN, 1), const),
                  pl.BlockSpec((1, 2, cb, HY_FFN), lambda o, j: (o, 0, j, 0))],
        out_specs=pl.BlockSpec((1, cb, 2 * length), lambda o, j: (o, j, 0)),
        scratch_shapes=[pltpu.VMEM((HY_FFN, 2 * length), F32)],
        compiler_params=_params("arbitrary", "arbitrary"),
        name="hyena_filter_taps",
    )(jnp.asarray(zt2), jnp.asarray(t2), jnp.asarray(deltas), w1.T, col(b1), col(freq[0]),
      w2.T, col(b2), col(freq[1]), w3t)


def _dft_constants():
    na, nb, n = FFT_NA, FFT_NB, FFT_N
    half = na // 2
    ka = np.arange(na)
    ang_a = 2.0 * np.pi * np.outer(ka, ka) / na
    ca, sa = np.cos(ang_a), np.sin(ang_a)
    fa = np.block([[ca[:half], -sa[:half]], [sa[:half], ca[:half]]])
    fai = np.block([[ca[:, :half], sa[:, :half]], [-sa[:, :half], ca[:, :half]]])
    kb = np.arange(nb)
    ang_b = 2.0 * np.pi * np.outer(kb, kb) / nb
    cbm, sbm = np.cos(ang_b), np.sin(ang_b)
    fb = np.block([[cbm, -sbm], [sbm, cbm]])
    fbi = np.block([[cbm, sbm], [-sbm, cbm]])
    ang_t = 2.0 * np.pi * np.outer(kb, ka) / n
    ct, st = np.cos(ang_t), np.sin(ang_t)
    tw_fc = np.concatenate([ct, ct], axis=1)
    tw_fs = np.concatenate([st, -st], axis=1)
    tw_ic, tw_is = ct.T.copy(), st.T.copy()
    bf = lambda a: jnp.asarray(a, dtype=F32).astype(BF16)
    f32 = lambda a: jnp.asarray(a, dtype=F32)
    fa_real = np.concatenate([ca, -sa], axis=1)
    return dict(fa=bf(fa), fa_real=bf(fa_real), fai=bf(fai), fb=bf(fb), fbi=bf(fbi),
                tw_fc=f32(tw_fc), tw_fs=f32(tw_fs), tw_ic=f32(tw_ic), tw_is=f32(tw_is))


def _fft_forward(zr, zi, fa, tw_fc, tw_fs, fb):
    c, _, nb = zr.shape
    tr = jnp.swapaxes(zr, 1, 2)
    lhs = tr if zi is None else jnp.concatenate([tr, jnp.swapaxes(zi, 1, 2)], axis=2)
    a = _dot(lhs.reshape(c * nb, lhs.shape[2]).astype(BF16), fa).reshape(c, nb, 2 * FFT_NA)
    a = a * tw_fc + pltpu.roll(a, FFT_NA, axis=2) * tw_fs
    t = jnp.swapaxes(a, 1, 2)
    lhs2 = jnp.concatenate([t[:, :FFT_NA, :], t[:, FFT_NA:, :]], axis=2)
    x = _dot(lhs2.reshape(c * FFT_NA, 2 * nb).astype(BF16), fb)
    return x.reshape(c, FFT_NA, 2 * nb)


def _fft_inverse(y, fbi, tw_ic, tw_is, fai):
    c = y.shape[0]
    nb = FFT_NB
    b = _dot(y.reshape(c * FFT_NA, 2 * nb).astype(BF16), fbi).reshape(c, FFT_NA, 2 * nb)
    br, bi = b[:, :, :nb], b[:, :, nb:]
    rr = br * tw_ic - bi * tw_is
    ii = bi * tw_ic + br * tw_is
    t = jnp.swapaxes(jnp.concatenate([rr, ii], axis=1), 1, 2)
    o = _dot(t.reshape(c * nb, 2 * FFT_NA).astype(BF16), fai).reshape(c, nb, FFT_NA)
    o = jnp.swapaxes(o, 1, 2)
    return o[:, :FFT_NA // 2, :], o[:, FFT_NA // 2:, :]


def _filter_fft_kernel(taps_ref, fa_ref, twc_ref, tws_ref, fb_ref, o_ref):
    h = _fft_forward(taps_ref[0], None, fa_ref[...], twc_ref[...], tws_ref[...], fb_ref[...])
    o_ref[0] = h * (1.0 / FFT_N)


def _filter_fft(taps, consts):
    _, c, n = taps.shape
    cb = 32
    taps4 = taps.reshape(2, c, FFT_NA, FFT_NB)
    cm = lambda o, j: (0, 0)
    return pl.pallas_call(
        _filter_fft_kernel,
        out_shape=jax.ShapeDtypeStruct((2, c, FFT_NA, 2 * FFT_NB), F32),
        grid=(2, c // cb),
        in_specs=[pl.BlockSpec((1, cb, FFT_NA, FFT_NB), lambda o, j: (o, j, 0, 0)),
                  pl.BlockSpec(consts["fa_real"].shape, cm),
                  pl.BlockSpec(consts["tw_fc"].shape, cm),
                  pl.BlockSpec(consts["tw_fs"].shape, cm),
                  pl.BlockSpec(consts["fb"].shape, cm)],
        out_specs=pl.BlockSpec((1, cb, FFT_NA, 2 * FFT_NB), lambda o, j: (o, j, 0, 0)),
        compiler_params=_params("parallel", "parallel"),
        name="hyena_filter_fft",
    )(taps4, consts["fa_real"], consts["tw_fc"], consts["tw_fs"], consts["fb"])


def _shift_tokens(a, direction):
    rows = a.shape[-2]
    lane = lax.broadcasted_iota(jnp.int32, a.shape, a.ndim - 1)
    row = lax.broadcasted_iota(jnp.int32, a.shape, a.ndim - 2)
    if direction == 1:
        l = pltpu.roll(a, 1, axis=a.ndim - 1)
        ls = pltpu.roll(l, 1, axis=a.ndim - 2)
        out = jnp.where(lane == 0, ls, l)
        edge = jnp.logical_and(lane == 0, row == 0)
    else:
        l = pltpu.roll(a, LANES - 1, axis=a.ndim - 1)
        ls = pltpu.roll(l, rows - 1, axis=a.ndim - 2)
        out = jnp.where(lane == LANES - 1, ls, l)
        edge = jnp.logical_and(lane == LANES - 1, row == rows - 1)
    return jnp.where(edge, 0.0, out)


def _short_conv3(a, w_ref):
    return w_ref[0] * _shift_tokens(a, 1) + w_ref[1] * a + w_ref[2] * _shift_tokens(a, -1)


def _hyena_kernel(v_ref, x1_ref, x2_ref, wv_ref, w1_ref, w2_ref, h_ref, hb_ref,
                  fa_ref, twfc_ref, twfs_ref, fb_ref, fbi_ref, twic_ref, twis_ref, fai_ref, o_ref):
    fwd_c = (fa_ref[...], twfc_ref[...], twfs_ref[...], fb_ref[...])
    inv_c = (fbi_ref[...], twic_ref[...], twis_ref[...], fai_ref[...])
    z = _short_conv3(v_ref[...].astype(F32), wv_ref)
    zr, zi = z[0], z[1]
    nb = FFT_NB
    for o, (g_ref, gw_ref) in enumerate(((x1_ref, w1_ref), (x2_ref, w2_ref))):
        x = _fft_forward(zr, zi, *fwd_c)
        h = h_ref[o]
        xr, xi, hr, hi = x[:, :, :nb], x[:, :, nb:], h[:, :, :nb], h[:, :, nb:]
        y = jnp.concatenate([xr * hr - xi * hi, xr * hi + xi * hr], axis=2)
        yr, yi = _fft_inverse(y, *inv_c)
        gate = _short_conv3(g_ref[...].astype(F32), gw_ref)
        bias = hb_ref[o]
        zr = gate[0] * (yr + zr * bias)
        zi = gate[1] * (yi + zi * bias)
    o_ref[0] = zr.astype(o_ref.dtype)
    o_ref[1] = zi.astype(o_ref.dtype)


def _hyena_latent(hyt, short_w, h_spec, hy_bias, consts):
    b, c3, length = hyt.shape
    c = c3 // 3
    rows = length // FFT_NB
    cb = 32
    nblk = c // cb
    hy4 = hyt.reshape(b, c3, rows, FFT_NB)
    w4 = jnp.broadcast_to(short_w.reshape(3, c3, 1, 1), (3, c3, 1, FFT_NB))
    hb4 = jnp.broadcast_to(hy_bias.reshape(2, c, 1, 1), (2, c, 1, FFT_NB))
    sig = lambda g: pl.BlockSpec((2, cb, rows, FFT_NB), lambda j, p: (p, g * nblk + j, 0, 0))
    wsp = lambda g: pl.BlockSpec((3, cb, 1, FFT_NB), lambda j, p: (0, g * nblk + j, 0, 0))
    cm = lambda j, p: (0, 0)
    names = ("fa", "tw_fc", "tw_fs", "fb", "fbi", "tw_ic", "tw_is", "fai")
    out = pl.pallas_call(
        _hyena_kernel,
        out_shape=jax.ShapeDtypeStruct((b, c, rows, FFT_NB), BF16),
        grid=(nblk, b // 2),
        in_specs=[sig(0), sig(1), sig(2), wsp(0), wsp(1), wsp(2),
                  pl.BlockSpec((2, cb, FFT_NA, 2 * FFT_NB), lambda j, p: (0, j, 0, 0)),
                  pl.BlockSpec((2, cb, 1, FFT_NB), lambda j, p: (0, j, 0, 0))]
                 + [pl.BlockSpec(consts[k].shape, cm) for k in names],
        out_specs=pl.BlockSpec((2, cb, rows, FFT_NB), lambda j, p: (p, j, 0, 0)),
        compiler_params=_params("parallel", "arbitrary"),
        name="hyena_long_conv",
    )(hy4, hy4, hy4, w4, w4, w4, h_spec, hb4, *[consts[k] for k in names])
    return out.reshape(b, c, length)


def _dense_dft_constants(length):
    n = 2 * length
    k = np.arange(n)
    ang = 2.0 * np.pi * np.outer(k, k) / n
    fwd = np.concatenate([np.cos(ang), -np.sin(ang)], axis=1)
    inv = np.concatenate([np.cos(ang[:length]).T, -np.sin(ang[:length]).T], axis=0) / n
    bf = lambda a: jnp.asarray(a, dtype=F32).astype(BF16)
    return bf(fwd), bf(inv)


def _shift_lanes(a, direction):
    n = a.shape[-1]
    lane = lax.broadcasted_iota(jnp.int32, a.shape, a.ndim - 1)
    if direction == 1:
        return jnp.where(lane == 0, 0.0, pltpu.roll(a, 1, axis=a.ndim - 1))
    return jnp.where(lane == n - 1, 0.0, pltpu.roll(a, n - 1, axis=a.ndim - 1))


def _hyena_ctx_kernel(v_ref, x1_ref, x2_ref, wv_ref, w1_ref, w2_ref, taps_ref, hb_ref, fwd_ref, inv_ref, o_ref):
    bsz, cb, length = v_ref.shape
    n = 2 * length
    fwd_m, inv_m = fwd_ref[0:length, :], inv_ref[...]

    def conv3(ref, w_ref):
        a = ref[...].astype(F32)
        return w_ref[0] * _shift_lanes(a, 1) + w_ref[1] * a + w_ref[2] * _shift_lanes(a, -1)

    z = conv3(v_ref, wv_ref)
    for o, (g_ref, gw_ref) in enumerate(((x1_ref, w1_ref), (x2_ref, w2_ref))):
        h = _dot(taps_ref[o].astype(BF16), fwd_ref[...])
        hr, hi = h[:, :n], h[:, n:]
        x = _dot(z.reshape(bsz * cb, length).astype(BF16), fwd_m).reshape(bsz, cb, 2 * n)
        xr, xi = x[:, :, :n], x[:, :, n:]
        y = jnp.concatenate([xr * hr - xi * hi, xr * hi + xi * hr], axis=2)
        yt = _dot(y.reshape(bsz * cb, 2 * n).astype(BF16), inv_m).reshape(bsz, cb, length)
        z = conv3(g_ref, gw_ref) * (yt + z * hb_ref[o])
    o_ref[...] = z.astype(o_ref.dtype)


def _hyena_context(hyt, short_w, taps, hy_bias):
    b, c3, length = hyt.shape
    c = c3 // 3
    cb = 128
    nblk = c // cb
    fwd_m, inv_m = _dense_dft_constants(length)
    w4 = jnp.broadcast_to(short_w.reshape(3, c3, 1), (3, c3, length))
    hb = jnp.broadcast_to(hy_bias.reshape(2, c, 1), (2, c, length))
    sig = lambda g: pl.BlockSpec((b, cb, length), lambda j: (0, g * nblk + j, 0))
    wsp = lambda g: pl.BlockSpec((3, cb, length), lambda j: (0, g * nblk + j, 0))
    return pl.pallas_call(
        _hyena_ctx_kernel,
        out_shape=jax.ShapeDtypeStruct((b, c, length), BF16),
        grid=(nblk,),
        in_specs=[sig(0), sig(1), sig(2), wsp(0), wsp(1), wsp(2),
                  pl.BlockSpec((2, cb, 2 * length), lambda j: (0, j, 0)),
                  pl.BlockSpec((2, cb, length), lambda j: (0, j, 0)),
                  pl.BlockSpec(fwd_m.shape, lambda j: (0, 0)),
                  pl.BlockSpec(inv_m.shape, lambda j: (0, 0))],
        out_specs=pl.BlockSpec((b, cb, length), lambda j: (0, j, 0)),
        compiler_params=_params("parallel"),
        name="hyena_context_conv",
    )(hyt, hyt, hyt, w4, w4, w4, taps, hb, fwd_m, inv_m)


def _route(gl, el):
    row = lax.broadcasted_iota(jnp.int32, gl.shape, 0).astype(F32)
    grp = jnp.floor(row * (1.0 / EXPERTS_PER_GROUP))
    big = float(N_EXPERTS)
    gmax = jnp.max(gl, axis=0, keepdims=True)
    gidx = jnp.min(jnp.where(gl == gmax, grp, big), axis=0, keepdims=True)
    gsum = jnp.sum(jnp.exp(gl - gmax), axis=0, keepdims=True) * (1.0 / EXPERTS_PER_GROUP)
    g_w = 1.0 / gsum
    em = jnp.where(grp == gidx, el, NEG_INF)
    t1 = jnp.max(em, axis=0, keepdims=True)
    i1 = jnp.min(jnp.where(em == t1, row, big), axis=0, keepdims=True)
    em2 = jnp.where(row == i1, 2.0 * NEG_INF, em)
    t2 = jnp.max(em2, axis=0, keepdims=True)
    i2 = jnp.min(jnp.where(em2 == t2, row, big), axis=0, keepdims=True)
    e2 = jnp.exp(t2 - t1)
    den = 1.0 + e2
    w1 = g_w / den
    w2 = g_w * e2 / den
    return jnp.where(row == i1, w1, 0.0) + jnp.where(row == i2, w2, 0.0), gidx


SLOT_LANE = N_EXPERTS
CHUNKS_LANE = N_EXPERTS + 1
MOE_CHUNK = 64


def _tile_chunks(tm):
    return tm // MOE_CHUNK + N_GROUPS


def _dispatch_slots(gidx, tri):
    tm = gidx.shape[1]
    sub = lax.broadcasted_iota(jnp.int32, (SUBLANES, tm), 0)
    grp = sub.astype(F32)
    member = jnp.where(grp == gidx, 1.0, 0.0)
    rank = _dot(member.astype(BF16), tri)
    count = jnp.sum(member, axis=1, keepdims=True)
    chunks = jnp.floor((count + float(MOE_CHUNK - 1)) * (1.0 / MOE_CHUNK))
    first = jnp.zeros_like(chunks)
    for g in range(1, N_GROUPS):
        first = first + jnp.where(sub[:, 0:1] >= g, chunks[g - 1:g, :], 0.0)
    slot = jnp.sum(member * (first * float(MOE_CHUNK) + rank - 1.0), axis=0, keepdims=True)
    out = jnp.where(sub == 0, slot, 0.0)
    for g in range(N_GROUPS):
        out = out + jnp.where(sub == g + 1, chunks[g:g + 1, :], 0.0)
    return out


def _gated_conv_tile(gb_ref, p_ref, cw_ref, tm):
    j = pl.program_id(1)
    n = p_ref.shape[1]
    halo = 2 * SUBLANES
    t0 = pl.multiple_of(j * tm, tm)
    p = p_ref[0, pl.ds(t0, tm), :].astype(F32)
    lo = pl.multiple_of(jnp.maximum(t0 - halo, 0), halo)
    hi = pl.multiple_of(jnp.minimum(t0 + tm, n - halo), halo)
    before = p_ref[0, pl.ds(lo, halo), :].astype(F32)[halo - 1:halo, :]
    after = p_ref[0, pl.ds(hi, halo), :].astype(F32)[0:1, :]
    before = jnp.where(t0 == 0, 0.0, before)
    after = jnp.where(t0 + tm >= n, 0.0, after)
    row = lax.broadcasted_iota(jnp.int32, p.shape, 0)
    prev = jnp.where(row == 0, before, pltpu.roll(p, 1, axis=0))
    nxt = jnp.where(row == tm - 1, after, pltpu.roll(p, tm - 1, axis=0))
    y = cw_ref[0:1, :] * prev + cw_ref[1:2, :] * p + cw_ref[2:3, :] * nxt
    return (gb_ref[0].astype(F32) * y).astype(BF16)


def _outproj_kernel(*refs, channel_major, gated_conv):
    y1_ref = refs[0]
    n_y2 = 3 if gated_conv else 1
    (w_ref, x_ref, g1_ref, n2_ref, sc2_ref, sh2_ref, wr_ref, br_ref, tri_ref,
     xo_ref, cmb_ref, hs_ref, cs_ref) = refs[1 + n_y2:]
    y2 = _gated_conv_tile(*refs[1:4], x_ref.shape[1]) if gated_conv else refs[1][0]
    half = w_ref.shape[0] // 2
    dots = [_dot_tn if cm else _dot for cm in channel_major]
    acc = dots[0](y1_ref[0], w_ref[:half, :]) + dots[1](y2, w_ref[half:, :])
    xn = x_ref[0] + g1_ref[0] * acc
    xo_ref[0] = xn
    h = _norm_modulate(xn, n2_ref[...], sc2_ref[0], sh2_ref[0])
    hi = h.astype(BF16)
    lo = (h - hi.astype(F32)).astype(BF16)
    p = _dot(hi, wr_ref[...])
    logits = p[:, :LANES] + p[:, LANES:] + _dot(lo, wr_ref[:, :LANES]) + br_ref[...]
    lt = logits.T
    cmb, gidx = _route(lt[0:N_EXPERTS], lt[N_EXPERTS:2 * N_EXPERTS])
    disp = _dispatch_slots(gidx, tri_ref[...])
    pad = jnp.zeros((LANES - N_EXPERTS - SUBLANES, cmb.shape[1]), F32)
    rows = jnp.concatenate([cmb, disp, pad], axis=0).T
    cmb_ref[0] = rows
    n_chunks = hs_ref.shape[0]
    srow = lax.broadcasted_iota(jnp.int32, (n_chunks * MOE_CHUNK, rows.shape[0]), 0).astype(F32)
    gather = jnp.where(srow == disp[0:1, :], 1.0, 0.0).astype(BF16)
    hs = _dot(gather, hi).astype(BF16)
    r_hi = rows.astype(BF16)
    r_lo = (rows - r_hi.astype(F32)).astype(BF16)
    cs2 = _dot(gather, jnp.concatenate([r_hi, r_lo], axis=1))
    cs = cs2[:, :LANES] + cs2[:, LANES:]
    for k in range(n_chunks):
        hs_ref[k] = hs[k * MOE_CHUNK:(k + 1) * MOE_CHUNK]
        cs_ref[k] = cs[k * MOE_CHUNK:(k + 1) * MOE_CHUNK]


def _outproj(y1, y2, w_bf, x, g1, n2g, sc2, sh2, wr2, br, channel_major):
    b, n, d = x.shape
    tm = min(512, n)
    nt = n // tm
    ntc = _tile_chunks(tm)
    half = d // 2
    tok = lambda i, j: (i, j, 0)
    mod = lambda i, j: (i, 0, 0)
    const = lambda i, j: (0, 0)
    srt = lambda i, j: (i * nt + j, 0, 0)
    y_spec = lambda cm: (pl.BlockSpec((1, half, tm), lambda i, j: (i, 0, j)) if cm
                         else pl.BlockSpec((1, tm, half), tok))
    gated_conv = isinstance(y2, tuple)
    if gated_conv:
        y2_ops = list(y2)
        y2_specs = [pl.BlockSpec((1, tm, half), tok), pl.BlockSpec((1, n, half), mod),
                    pl.BlockSpec(y2[2].shape, const)]
    else:
        y2_ops, y2_specs = [y2], [y_spec(channel_major[1])]
    return pl.pallas_call(
        functools.partial(_outproj_kernel, channel_major=tuple(channel_major), gated_conv=gated_conv),
        out_shape=(jax.ShapeDtypeStruct((b, n, d), F32), jax.ShapeDtypeStruct((b, n, LANES), F32),
                   jax.ShapeDtypeStruct((b * nt * ntc, MOE_CHUNK, d), BF16),
                   jax.ShapeDtypeStruct((b * nt * ntc, MOE_CHUNK, LANES), F32)),
        grid=(b, nt),
        in_specs=[y_spec(channel_major[0])] + y2_specs + [
                  pl.BlockSpec((d, d), const),
                  pl.BlockSpec((1, tm, d), tok),
                  pl.BlockSpec((1, 1, d), mod),
                  pl.BlockSpec((1, d), const),
                  pl.BlockSpec((1, 1, d), mod),
                  pl.BlockSpec((1, 1, d), mod),
                  pl.BlockSpec(wr2.shape, const),
                  pl.BlockSpec(br.shape, const),
                  pl.BlockSpec((tm, tm), const)],
        out_specs=(pl.BlockSpec((1, tm, d), tok), pl.BlockSpec((1, tm, LANES), tok),
                   pl.BlockSpec((ntc, MOE_CHUNK, d), srt), pl.BlockSpec((ntc, MOE_CHUNK, LANES), srt)),
        compiler_params=_params("parallel", "parallel"),
        name="outproj_router",
    )(y1, *y2_ops, w_bf, x, g1, n2g, sc2, sh2, wr2, br,
      jnp.asarray(np.triu(np.ones((tm, tm), np.float32)), dtype=BF16))


def _swiglu_group(h, w_rows, j, wg_ref, wu_ref, wd_ref):
    lane = lax.broadcasted_iota(jnp.int32, w_rows.shape, 1)
    experts = range(EXPERTS_PER_GROUP)
    w_e = [jnp.sum(jnp.where(lane == j * EXPERTS_PER_GROUP + e, w_rows, 0.0), axis=1, keepdims=True) for e in experts]
    a = [_dot(h, wg_ref[e]) for e in experts]
    u = [_dot(h, wu_ref[e]) for e in experts]
    act = [(_silu(a[e]) * u[e] * w_e[e]).astype(BF16) for e in experts]
    out = _dot(act[0], wd_ref[0])
    for e in experts[1:]:
        out = out + _dot(act[e], wd_ref[e])
    return out


MOE_STEP_CHUNKS = 8


def _moe_sorted_kernel(group_ref, used_ref, fresh_ref, src_ref, *refs):
    n = MOE_STEP_CHUNKS
    hs_refs, cs_refs = refs[:n], refs[n:2 * n]
    wg_ref, wu_ref, wd_ref, ys_ref, wg_bf, wu_bf, wd_bf = refs[2 * n:]
    s = pl.program_id(0)

    @pl.when(fresh_ref[s] > 0)
    def _():
        wg_bf[...] = wg_ref[...].astype(BF16)
        wu_bf[...] = wu_ref[...].astype(BF16)
        wd_bf[...] = wd_ref[...].astype(BF16)

    @pl.when(used_ref[s] > 0)
    def _():
        h = jnp.concatenate([r[0] for r in hs_refs], axis=0)
        w_rows = jnp.concatenate([r[0] for r in cs_refs], axis=0)
        y = _swiglu_group(h, w_rows, group_ref[s], wg_bf, wu_bf, wd_bf).astype(BF16)
        for k in range(n):
            ys_ref[k] = y[k * MOE_CHUNK:(k + 1) * MOE_CHUNK]

    @pl.when(used_ref[s] == 0)
    def _():
        ys_ref[...] = jnp.zeros(ys_ref.shape, ys_ref.dtype)


def _moe_sorted(hs, cs, step_group, step_used, step_fresh, chunk_src, wg, wu, wd, layer):
    _, _, d = hs.shape
    n = MOE_STEP_CHUNKS
    steps = step_group.shape[0]
    epg = EXPERTS_PER_GROUP
    chunk = lambda k, width: pl.BlockSpec((1, MOE_CHUNK, width),
                                          lambda s, grp, used, fresh, src: (src[s * n + k], 0, 0))
    wmap = lambda s, grp, used, fresh, src: (layer * N_GROUPS + grp[s], 0, 0)
    return pl.pallas_call(
        _moe_sorted_kernel,
        out_shape=jax.ShapeDtypeStruct((steps * n, MOE_CHUNK, d), BF16),
        grid_spec=pltpu.PrefetchScalarGridSpec(
            num_scalar_prefetch=4,
            grid=(steps,),
            in_specs=[chunk(k, d) for k in range(n)] + [chunk(k, LANES) for k in range(n)]
                     + [pl.BlockSpec((epg, d, D_EXPERT), wmap),
                        pl.BlockSpec((epg, d, D_EXPERT), wmap),
                        pl.BlockSpec((epg, D_EXPERT, d), wmap)],
            out_specs=pl.BlockSpec((n, MOE_CHUNK, d), lambda s, grp, used, fresh, src: (s, 0, 0)),
            scratch_shapes=[pltpu.VMEM((epg, d, D_EXPERT), BF16), pltpu.VMEM((epg, d, D_EXPERT), BF16),
                            pltpu.VMEM((epg, D_EXPERT, d), BF16)]),
        compiler_params=_params("arbitrary"),
        name="moe_sorted_experts",
    )(step_group, step_used, step_fresh, chunk_src, *([hs] * n), *([cs] * n), wg, wu, wd)


def _moe_unpermute(ys_refs, cmb_ref, x1_ref, g2_ref):
    cmb = cmb_ref[0]
    lane = lax.broadcasted_iota(jnp.int32, cmb.shape, 1)
    slot = jnp.sum(jnp.where(lane == SLOT_LANE, cmb, 0.0), axis=1, keepdims=True)
    rows = lax.broadcasted_iota(jnp.int32, (cmb.shape[0], len(ys_refs) * MOE_CHUNK), 1).astype(F32)
    scatter = jnp.where(slot == rows, 1.0, 0.0).astype(BF16)
    ys = jnp.concatenate([r[0] for r in ys_refs], axis=0)
    return x1_ref[0] + g2_ref[0] * _dot(scatter, ys)


def _moe_combine_kernel(pos_ref, *refs):
    x, (o_ref,) = _load_tokens(refs, len(refs) - 4)
    o_ref[0] = x


def _moe_combine(pending):
    b, n, d = pending.x1.shape
    tm = min(512, n)
    nt = n // tm
    n_chunks, pos, x_ops, x_specs = _token_source(pending, tm, nt, d)
    return pl.pallas_call(
        _moe_combine_kernel,
        out_shape=jax.ShapeDtypeStruct((b, n, d), F32),
        grid_spec=pltpu.PrefetchScalarGridSpec(
            num_scalar_prefetch=1,
            grid=(b, nt),
            in_specs=x_specs,
            out_specs=pl.BlockSpec((1, tm, d), lambda i, t, pos: (i, t, 0))),
        compiler_params=_params("parallel", "parallel"),
        name="moe_combine",
    )(pos, *x_ops)


def _moe_chunk_schedule(cmb):
    b, n, _ = cmb.shape
    tm = min(512, n)
    nt = b * (n // tm)
    ntc, nsc = _tile_chunks(tm), MOE_STEP_CHUNKS
    steps = nt * ntc // nsc + N_GROUPS
    i32 = jnp.int32
    cnt = cmb[:, ::tm, CHUNKS_LANE:CHUNKS_LANE + N_GROUPS].reshape(nt, N_GROUPS).astype(i32)
    gi = jnp.arange(N_GROUPS, dtype=i32)
    ti = jnp.arange(nt, dtype=i32)
    earlier_g = (gi[:, None] < gi[None, :]).astype(i32)
    in_tile = jnp.sum(cnt[:, :, None] * earlier_g[None], axis=1)
    before = jnp.sum(cnt[:, None, :] * (ti[:, None] < ti[None, :]).astype(i32)[:, :, None], axis=0)
    total = jnp.sum(cnt, axis=0)
    padded = (total + nsc - 1) // nsc * nsc
    gstart = jnp.sum(padded[:, None] * earlier_g, axis=0)
    gend = gstart + padded
    c = jnp.arange(tm // MOE_CHUNK, dtype=i32)
    dst = gstart[None, :, None] + before[:, :, None] + c[None, None, :]
    src = (ti * ntc)[:, None, None] + in_tile[:, :, None] + c[None, None, :]
    dst = jnp.where(c[None, None, :] < cnt[:, :, None], dst, -1).reshape(-1)
    p = jnp.arange(steps * nsc, dtype=i32)
    chunk_src = jnp.sum(jnp.where(dst[None, :] == p[:, None], src.reshape(-1)[None, :], 0), axis=1)
    first_chunk = jnp.arange(steps, dtype=i32) * nsc
    step_group = jnp.sum((first_chunk[:, None] >= gend[None, :-1]).astype(i32), axis=1)
    prev_group = jnp.sum((first_chunk[:, None] - nsc >= gend[None, :-1]).astype(i32), axis=1)
    step_used = (first_chunk < gend[-1]).astype(i32)
    step_fresh = jnp.logical_or(first_chunk == 0, step_group != prev_group).astype(i32)
    k = jnp.arange(ntc, dtype=i32)
    ends = in_tile + cnt
    grp_k = jnp.minimum(jnp.sum((k[None, :, None] >= ends[:, None, :]).astype(i32), axis=2), N_GROUPS - 1)
    base = gstart[None, :] + before - in_tile
    pos = k[None, :] + jnp.sum(jnp.where(grp_k[:, :, None] == gi[None, None, :], base[:, None, :], 0), axis=2)
    chunk_pos = jnp.where(k[None, :] < ends[:, -1:], pos, 0).reshape(-1).astype(i32)
    return step_group, step_used, step_fresh, chunk_src, chunk_pos


def _moe(hs, cs, cmb, x, g2, wg, wu, wd, layer):
    step_group, step_used, step_fresh, chunk_src, chunk_pos = _moe_chunk_schedule(cmb)
    ys = _moe_sorted(hs, cs, step_group, step_used, step_fresh, chunk_src, wg, wu, wd, layer)
    return _PendingMoe(ys, chunk_pos, cmb, x, g2)


def kernel(x, c, ctx, c_ctx, ada_w, ada_b, norm1_g, norm2_g, w_in_even, qn_g, kn_g, na_rpb, sc_conv_w, w_in_odd, hy_short_w, hy_w1, hy_b1, hy_w2, hy_b2, hy_w3, hy_freq, hy_bias, cf_conv_w, cf_conv_b, cf_ln_g, cf_ln_b, w_out, moe_w_group, moe_b_group, moe_w_router, moe_b_router, moe_w_gate, moe_w_up, moe_w_down):
    depth = ada_w.shape[0]
    bsz, seq, d = x.shape
    lc = ctx.shape[1]
    assert 2 * seq == FFT_N and d == D_MODEL and bsz % 2 == 0

    mods = _ada_modulation(jnp.concatenate([c, c_ctx[None, :]], axis=0), ada_w, ada_b)
    seg = jnp.asarray(np.kron(np.eye(NA_HEADS), np.ones((HEAD_DIM, HEAD_DIM))), dtype=BF16)
    consts = _dft_constants()
    wg = moe_w_gate.reshape((-1,) + moe_w_gate.shape[2:])
    wu = moe_w_up.reshape((-1,) + moe_w_up.shape[2:])
    wd = moe_w_down.reshape((-1,) + moe_w_down.shape[2:])

    for l in range(depth):
        ctx_needed = any(j % 2 == 0 for j in range(l + 1, depth))
        lat_mod = [m[:, None, :] for m in jnp.split(mods[l, :bsz], 6, axis=-1)]
        ctx_mod = [jnp.broadcast_to(m[None, :, :], (bsz, 1, d)) for m in jnp.split(mods[l, bsz:bsz + 1], 6, axis=-1)]
        sh1, sc1, g1, sh2, sc2, g2 = lat_mod
        csh1, csc1, cg1, csh2, csc2, cg2 = ctx_mod
        n1g = norm1_g[l][None, :]
        n2g = norm2_g[l][None, :]
        w_out_bf = w_out[l].astype(BF16)
        wr = jnp.concatenate([jnp.repeat(moe_w_group[l], EXPERTS_PER_GROUP, axis=1), moe_w_router[l],
                              jnp.zeros((d, LANES - 2 * N_EXPERTS), F32)], axis=1)
        br = jnp.concatenate([jnp.repeat(moe_b_group[l], EXPERTS_PER_GROUP), moe_b_router[l],
                              jnp.zeros((LANES - 2 * N_EXPERTS,), F32)])[None, :]
        wrh = wr.astype(BF16)
        wr2 = jnp.concatenate([wrh, (wr - wrh.astype(F32)).astype(BF16)], axis=1)

        if l % 2 == 0:
            e = l // 2
            w_in = w_in_even[e].astype(BF16)
            qg = jnp.tile(qn_g[e], NA_HEADS)[None, :]
            kg = jnp.tile(kn_g[e], NA_HEADS)[None, :]
            x, ql, kl, vl, gbl, pl_ = _inproj_even(x, n1g, sc1, sh1, w_in, qg, kg, seg)
            ctx, qc, kc, vc, gbc, pc = _inproj_even(ctx, n1g, csc1, csh1, w_in, qg, kg, seg)
            bias = _bias_table(na_rpb[e])
            y1 = _natten(ql, kl, vl, kc, vc, bias)
            y2 = (gbl, pl_, sc_conv_w[e])
            lat_cm = (False, False)
            if ctx_needed:
                y1c = _ctx_attention(qc, kc, vc)
                y2c = (gbc, pc, sc_conv_w[e])
        else:
            o = l // 2
            wht = w_in_odd[o][:, :3 * HY_WIDTH].T.astype(BF16)
            wag = w_in_odd[o][:, 3 * HY_WIDTH:].astype(BF16)
            x, hyt, ag = _inproj_odd(x, n1g, sc1, sh1, wht, wag)
            taps = _hyena_taps(seq, hy_w1[o], hy_b1[o], hy_w2[o], hy_b2[o], hy_w3[o], hy_freq[o])
            h_spec = _filter_fft(taps, consts)
            y1 = _hyena_latent(hyt, hy_short_w[o], h_spec, hy_bias[o], consts)
            cf_args = (cf_conv_w[o], cf_conv_b[o], cf_ln_g[o], cf_ln_b[o])
            y2 = _conformer(ag, *cf_args)
            lat_cm = (True, False)
            if ctx_needed:
                ctx, hytc, agc = _inproj_odd(ctx, n1g, csc1, csh1, wht, wag)
                taps_c = _hyena_taps(lc, hy_w1[o], hy_b1[o], hy_w2[o], hy_b2[o], hy_w3[o], hy_freq[o])
                y1c = _hyena_context(hytc, hy_short_w[o], taps_c, hy_bias[o])
                y2c = _conformer(agc, *cf_args)

        x1, cmb, hs, cs = _outproj(y1, y2, w_out_bf, x, g1, n2g, sc2, sh2, wr2, br, lat_cm)
        x = _moe(hs, cs, cmb, x1, g2, wg, wu, wd, l)
        if ctx_needed:
            c1, cmbc, hsc, csc = _outproj(y1c, y2c, w_out_bf, ctx, cg1, n2g, csc2, csh2, wr2, br, lat_cm)
            ctx = _moe(hsc, csc, cmbc, c1, cg2, wg, wu, wd, l)
    return _moe_combine(x)
```
